```python
import math
import jax, jax.numpy as jnp
from jax import lax
import numpy as np

D_MODEL = 1024
BATCH = 8
SEQ = 2048
DEPTH = 4
DEC_BATCH = 128
DEC_SEQ = 1
PAST_LEN = 16384
PAGE_SIZE = 128

N_EVEN = (DEPTH + 1) // 2
N_ODD = DEPTH // 2
EPS = 1e-5

D_FF = 2816

SSD_HEADS = 16
SSD_HEAD_DIM = 64
SSD_INNER = SSD_HEADS * SSD_HEAD_DIM
SSD_GROUPS = 2
SSD_D_STATE = 128
SSD_CONV = 4
SSD_CONV_DIM = SSD_INNER + 2 * SSD_GROUPS * SSD_D_STATE
SSD_CHUNK = 64

GLA_HEADS = 4
GLA_DK = D_MODEL // 2 // GLA_HEADS
GLA_DV = D_MODEL // GLA_HEADS
GLA_KEY = GLA_HEADS * GLA_DK
GLA_VAL = GLA_HEADS * GLA_DV
GLA_GATE_RANK = 16
GLA_GATE_NORM = 16.0
GLA_CHUNK = 16

IN_EVEN = SSD_INNER + SSD_CONV_DIM + SSD_HEADS + 2 * GLA_KEY + 2 * GLA_VAL + GLA_GATE_RANK
MIX_EVEN = SSD_INNER + GLA_VAL

RWKV_HEAD = 64
RWKV_HEADS = D_MODEL // RWKV_HEAD
RWKV_DECAY_RANK = 64
RWKV_A_RANK = 64
RWKV_V_RANK = 32
RWKV_G_RANK = 160
RWKV_GN_EPS = 64e-5

kernel_name = 'hybrid_ssd_gla_rwkv7_macaron_step'


def rms_norm(x, g):
    xf = x.astype(jnp.float32)
    y = xf * lax.rsqrt(jnp.mean(xf * xf, axis=-1, keepdims=True) + EPS)
    return (y * g.astype(jnp.float32)).astype(x.dtype)


def swiglu(x, w_gu, w_down):
    gate, up = jnp.split(x @ w_gu, 2, axis=-1)
    return (jax.nn.silu(gate) * up) @ w_down


def causal_conv(u, buf, w, b):
    L = u.shape[1]
    full = jnp.concatenate([buf, u], axis=1)
    out = b + full[:, 0:L] * w[0]
    for i in range(1, SSD_CONV):
        out = out + full[:, i:i + L] * w[i]
    return out, full[:, L:]


def ssd_scan(x, dt, A, bm, cm, h0):
    Bsz, L = x.shape[:2]
    c = math.gcd(L, SSD_CHUNK)
    n = L // c
    x = x.reshape(Bsz, n, c, *x.shape[2:])
    dt = dt.reshape(Bsz, n, c, *dt.shape[2:])
    bm = bm.reshape(Bsz, n, c, *bm.shape[2:])
    cm = cm.reshape(Bsz, n, c, *cm.shape[2:])
    a_cum = jnp.cumsum(dt * A, axis=2)
    causal = jnp.tril(jnp.ones((c, c), dtype=bool))
    seg = a_cum[:, :, :, None] - a_cum[:, :, None, :]
    decay = jnp.exp(jnp.where(causal[:, :, None, None], seg, -jnp.inf))
    cb = jnp.einsum('bnigs,bnjgs->bnijg', cm, bm)
    wts = cb[..., None] * decay * dt[:, :, None]
    y = jnp.einsum('bnijgr,bnjgrp->bnigrp', wts, x)
    a_last = a_cum[:, :, -1]
    to_end = jnp.exp(a_last[:, :, None] - a_cum) * dt
    chunk_states = jnp.einsum('bnjgr,bnjgs,bnjgrp->bngrps', to_end, bm, x)

    def step(h, inp):
        s, al = inp
        h_new = (jnp.exp(al)[..., None, None] * h + s).astype(h.dtype)
        return h_new, h

    h_last, h_prev = lax.scan(step, h0, (jnp.moveaxis(chunk_states, 1, 0), jnp.moveaxis(a_last, 1, 0)))
    h_prev = jnp.moveaxis(h_prev, 0, 1)
    y = y + jnp.einsum('bnigs,bngrps->bnigrp', cm, h_prev) * jnp.exp(a_cum)[..., None]
    return y.reshape(Bsz, L, *y.shape[3:]), h_last


def gla_scan(q, k, v, g, S0):
    Bsz, L = q.shape[:2]
    c = math.gcd(L, GLA_CHUNK)
    n = L // c
    rs = lambda t: t.reshape(Bsz, n, c, *t.shape[2:])
    q, k, v, g = rs(q), rs(k), rs(v), rs(g)
    b = jnp.cumsum(g, axis=2)
    qg = q * jnp.exp(b)
    kg = k * jnp.exp(-b)
    causal = jnp.tril(jnp.ones((c, c), dtype=bool))
    att = jnp.where(causal, jnp.einsum('bnihk,bnjhk->bnhij', qg, kg), 0)
    o_intra = jnp.einsum('bnhij,bnjhv->bnihv', att, v)
    b_last = b[:, :, -1]
    kd = k * jnp.exp(b_last[:, :, None] - b)

    def step(S, inp):
        qg_n, kd_n, v_n, bl_n = inp
        o_n = jnp.einsum('bihk,bhkv->bihv', qg_n, S)
        S_new = (jnp.exp(bl_n)[..., None] * S + jnp.einsum('bjhk,bjhv->bhkv', kd_n, v_n)).astype(S.dtype)
        return S_new, o_n

    mv = lambda t: jnp.moveaxis(t, 1, 0)
    S_last, o_inter = lax.scan(step, S0, (mv(qg), mv(kd), mv(v), mv(b_last)))
    o = o_intra + jnp.moveaxis(o_inter, 0, 1)
    return o.reshape(Bsz, L, *o.shape[3:]), S_last


def rwkv7_scan(r, w, k, v, a, b, S0):
    def step(S, inp):
        r_t, w_t, k_t, v_t, a_t, b_t = inp
        sa = jnp.einsum('bhvk,bhk->bhv', S, a_t)
        S_new = (S * w_t[:, :, None, :] + sa[..., None] * b_t[:, :, None, :]
                 + v_t[..., None] * k_t[:, :, None, :]).astype(S.dtype)
        return S_new, jnp.einsum('bhvk,bhk->bhv', S_new, r_t)

    xs = tuple(jnp.moveaxis(t, 1, 0) for t in (r, w, k, v, a, b))
    S_last, o = lax.scan(step, S0, xs)
    return jnp.moveaxis(o, 0, 1), S_last


def ssd_gla_mix(h, conv0, ssm0, gla0, W, i):
    Bsz, L, _ = h.shape
    R = SSD_HEADS // SSD_GROUPS
    sizes = [SSD_INNER, SSD_CONV_DIM, SSD_HEADS, GLA_KEY, GLA_KEY, GLA_VAL, GLA_VAL, GLA_GATE_RANK]
    z, xbc, dt, q, k, v, og, glr = jnp.split(h @ W['ev_w_in'][i], np.cumsum(sizes)[:-1].tolist(), axis=-1)
    xbc, conv_new = causal_conv(xbc, conv0, W['ev_conv_w'][i], W['ev_conv_b'][i])
    xbc = jax.nn.silu(xbc)
    xs, bm, cm = jnp.split(xbc, [SSD_INNER, SSD_INNER + SSD_GROUPS * SSD_D_STATE], axis=-1)
    xs = xs.reshape(Bsz, L, SSD_GROUPS, R, SSD_HEAD_DIM)
    bm = bm.reshape(Bsz, L, SSD_GROUPS, SSD_D_STATE)
    cm = cm.reshape(Bsz, L, SSD_GROUPS, SSD_D_STATE)
    dt = jax.nn.softplus(dt + W['ev_dt_bias'][i]).reshape(Bsz, L, SSD_GROUPS, R)
    A = -jnp.exp(W['ev_a_log'][i]).reshape(SSD_GROUPS, R)
    y, ssm_new = ssd_scan(xs, dt, A, bm, cm, ssm0.reshape(Bsz, SSD_GROUPS, R, SSD_HEAD_DIM, SSD_D_STATE))
    y = y + W['ev_d_skip'][i].reshape(SSD_GROUPS, R)[..., None] * xs
    y = y * jax.nn.silu(z).reshape(Bsz, L, SSD_GROUPS, R, SSD_HEAD_DIM)
    y = rms_norm(y.reshape(Bsz, L, SSD_GROUPS, R * SSD_HEAD_DIM),
                 W['ev_ssd_norm'][i].reshape(SSD_GROUPS, R * SSD_HEAD_DIM)).reshape(Bsz, L, SSD_INNER)
    q = q.reshape(Bsz, L, GLA_HEADS, GLA_DK) * (GLA_DK ** -0.5)
    k = k.reshape(Bsz, L, GLA_HEADS, GLA_DK)
    v = v.reshape(Bsz, L, GLA_HEADS, GLA_DV)
    lg = jax.nn.log_sigmoid(glr @ W['ev_gla_gate_w2'][i] + W['ev_gla_gate_b'][i]) / GLA_GATE_NORM
    lg = lg.reshape(Bsz, L, GLA_HEADS, GLA_DK)
    o, gla_new = gla_scan(q, k, v, lg, gla0)
    o = rms_norm(o, W['ev_gla_norm'][i]) * jax.nn.silu(og.reshape(Bsz, L, GLA_HEADS, GLA_DV))
    o = o.reshape(Bsz, L, GLA_VAL)
    out = jnp.concatenate([y, o], axis=-1) @ W['ev_w_out'][i]
    return out, conv_new, ssm_new.reshape(Bsz, SSD_HEADS, SSD_HEAD_DIM, SSD_D_STATE), gla_new


def rwkv7_mix(h, shift0, wkv0, v_first, W, i):
    Bsz, L, _ = h.shape
    prev = jnp.concatenate([shift0[:, None], h[:, :-1]], axis=1)
    xx = prev - h
    mix = W['od_mix'][i]
    xr, xw, xk, xv, xa, xg = [h + xx * mix[j] for j in range(6)]
    r = xr @ W['od_w_r'][i]
    w = -jax.nn.softplus(-(W['od_w0'][i] + jnp.tanh(xw @ W['od_w1'][i]) @ W['od_w2'][i])) - 0.5
    k = xk @ W['od_w_k'][i]
    v = xv @ W['od_w_v'][i]
    if v_first is None:
        v_first = v
    else:
        j = i - 1
        v = v + (v_first - v) * jax.nn.sigmoid(W['od_v0'][j] + (xv @ W['od_v1'][j]) @ W['od_v2'][j])
    a = jax.nn.sigmoid(W['od_a0'][i] + (xa @ W['od_a1'][i]) @ W['od_a2'][i])
    g = jax.nn.sigmoid(xg @ W['od_g1'][i]) @ W['od_g2'][i]
    heads = lambda t: t.reshape(Bsz, L, RWKV_HEADS, RWKV_HEAD)
    kkf = heads(k * W['od_k_k'][i]).astype(jnp.float32)
    kk = (kkf * lax.rsqrt(jnp.maximum(jnp.sum(kkf * kkf, axis=-1, keepdims=True), 1e-24))).astype(k.dtype)
    k = k * (1 + (a - 1) * W['od_k_a'][i])
    decay = jnp.exp(-jnp.exp(w))
    rh, kh, vh = heads(r), heads(k), heads(v)
    o, wkv_new = rwkv7_scan(rh, heads(decay), kh, vh, -kk, kk * heads(a), wkv0)
    of = o.astype(jnp.float32)
    mu = jnp.mean(of, axis=-1, keepdims=True)
    var = jnp.mean(jnp.square(of - mu), axis=-1, keepdims=True)
    o = ((of - mu) * lax.rsqrt(var + RWKV_GN_EPS)).astype(h.dtype).reshape(Bsz, L, D_MODEL)
    o = o * W['od_gn_w'][i] + W['od_gn_b'][i]
    bonus = jnp.sum(rh * kh * W['od_r_k'][i].reshape(RWKV_HEADS, RWKV_HEAD), axis=-1, keepdims=True) * vh
    o = o + bonus.reshape(Bsz, L, D_MODEL)
    out = (o * g) @ W['od_w_o'][i]
    return out, h[:, -1], wkv_new, v_first


def run_group(x, conv0, ssm0, gla0, shift0, wkv0, W):
    convs, ssms, glas, shifts, wkvs = [], [], [], [], []
    v_first = None
    for layer in range(DEPTH):
        x = x + 0.5 * swiglu(rms_norm(x, W['ffn_norm'][layer, 0]), W['ffn_w_gu'][layer, 0], W['ffn_w_down'][layer, 0])
        h = rms_norm(x, W['mix_norm'][layer])
        i = layer // 2
        if layer % 2 == 0:
            out, c_new, s_new, g_new = ssd_gla_mix(h, conv0[i], ssm0[i], gla0[i], W, i)
            convs.append(c_new)
            ssms.append(s_new)
            glas.append(g_new)
        else:
            out, sh_new, wkv_new, v_first = rwkv7_mix(h, shift0[i], wkv0[i], v_first, W, i)
            shifts.append(sh_new)
            wkvs.append(wkv_new)
        x = x + out
        x = x + 0.5 * swiglu(rms_norm(x, W['ffn_norm'][layer, 1]), W['ffn_w_gu'][layer, 1], W['ffn_w_down'][layer, 1])
    y = rms_norm(x, W['final_norm'])
    return y, jnp.stack(convs), jnp.stack(ssms), jnp.stack(glas), jnp.stack(shifts), jnp.stack(wkvs)


def setup_inputs(seed: int = 0) -> dict:
    key = jax.random.key(seed)
    keys = iter(jax.random.split(key, 64))

    def nrm(shape, scale=1.0):
        return jax.random.normal(next(keys), shape, jnp.float32) * scale

    def unif(shape, lo, hi):
        return jax.random.uniform(next(keys), shape, jnp.float32, lo, hi)

    dt0 = jnp.exp(unif((N_EVEN, SSD_HEADS), math.log(1e-3), math.log(1e-1)))
    nv = N_ODD - 1
    return {
        'x_prompt': nrm((BATCH, SEQ, D_MODEL)),
        'x_sample': nrm((DEC_BATCH, DEC_SEQ, D_MODEL)),
        'state_conv': nrm((N_EVEN, DEC_BATCH, SSD_CONV - 1, SSD_CONV_DIM)),
        'state_ssm': nrm((N_EVEN, DEC_BATCH, SSD_HEADS, SSD_HEAD_DIM, SSD_D_STATE), 0.1),
        'state_gla': nrm((N_EVEN, DEC_BATCH, GLA_HEADS, GLA_DK, GLA_DV), 0.1),
        'state_shift': nrm((N_ODD, DEC_BATCH, D_MODEL)),
        'state_wkv': nrm((N_ODD, DEC_BATCH, RWKV_HEADS, RWKV_HEAD, RWKV_HEAD), 0.1),
        'ffn_norm': 1.0 + nrm((DEPTH, 2, D_MODEL), 0.02),
        'ffn_w_gu': nrm((DEPTH, 2, D_MODEL, 2 * D_FF), D_MODEL ** -0.5),
        'ffn_w_down': nrm((DEPTH, 2, D_FF, D_MODEL), D_FF ** -0.5),
        'mix_norm': 1.0 + nrm((DEPTH, D_MODEL), 0.02),
        'final_norm': 1.0 + nrm((D_MODEL,), 0.02),
        'ev_w_in': nrm((N_EVEN, D_MODEL, IN_EVEN), D_MODEL ** -0.5),
        'ev_conv_w': nrm((N_EVEN, SSD_CONV, SSD_CONV_DIM), SSD_CONV ** -0.5),
        'ev_conv_b': nrm((N_EVEN, SSD_CONV_DIM), 0.02),
        'ev_dt_bias': dt0 + jnp.log(-jnp.expm1(-dt0)),
        'ev_a_log': jnp.log(unif((N_EVEN, SSD_HEADS), 1.0, 16.0)),
        'ev_d_skip': 1.0 + nrm((N_EVEN, SSD_HEADS), 0.1),
        'ev_ssd_norm': 1.0 + nrm((N_EVEN, SSD_INNER), 0.02),
        'ev_gla_gate_w2': nrm((N_EVEN, GLA_GATE_RANK, GLA_KEY), GLA_GATE_RANK ** -0.5),
        'ev_gla_gate_b': nrm((N_EVEN, GLA_KEY), 0.02),
        'ev_gla_norm': 1.0 + nrm((N_EVEN, GLA_DV), 0.02),
        'ev_w_out': nrm((N_EVEN, MIX_EVEN, D_MODEL), MIX_EVEN ** -0.5),
        'od_mix': unif((N_ODD, 6, D_MODEL), 0.0, 1.0),
        'od_w0': unif((N_ODD, D_MODEL), -6.0, -1.0),
        'od_w1': nrm((N_ODD, D_MODEL, RWKV_DECAY_RANK), D_MODEL ** -0.5),
        'od_w2': nrm((N_ODD, RWKV_DECAY_RANK, D_MODEL), 0.1 * RWKV_DECAY_RANK ** -0.5),
        'od_a0': nrm((N_ODD, D_MODEL), 0.1),
        'od_a1': nrm((N_ODD, D_MODEL, RWKV_A_RANK), D_MODEL ** -0.5),
        'od_a2': nrm((N_ODD, RWKV_A_RANK, D_MODEL), RWKV_A_RANK ** -0.5),
        'od_v0': 1.0 + nrm((nv, D_MODEL), 0.1),
        'od_v1': nrm((nv, D_MODEL, RWKV_V_RANK), D_MODEL ** -0.5),
        'od_v2': nrm((nv, RWKV_V_RANK, D_MODEL), RWKV_V_RANK ** -0.5),
        'od_g1': nrm((N_ODD, D_MODEL, RWKV_G_RANK), D_MODEL ** -0.5),
        'od_g2': nrm((N_ODD, RWKV_G_RANK, D_MODEL), RWKV_G_RANK ** -0.5),
        'od_k_k': 0.85 + nrm((N_ODD, D_MODEL), 0.02),
        'od_k_a': 1.0 + nrm((N_ODD, D_MODEL), 0.02),
        'od_r_k': nrm((N_ODD, D_MODEL), 0.1),
        'od_w_r': nrm((N_ODD, D_MODEL, D_MODEL), D_MODEL ** -0.5),
        'od_w_k': nrm((N_ODD, D_MODEL, D_MODEL), D_MODEL ** -0.5),
        'od_w_v': nrm((N_ODD, D_MODEL, D_MODEL), D_MODEL ** -0.5),
        'od_w_o': nrm((N_ODD, D_MODEL, D_MODEL), D_MODEL ** -0.5),
        'od_gn_w': 1.0 + nrm((N_ODD, D_MODEL), 0.02),
        'od_gn_b': nrm((N_ODD, D_MODEL), 0.02),
    }


def reference(x_prompt, x_sample, state_conv, state_ssm, state_gla, state_shift, state_wkv,
              ffn_norm, ffn_w_gu, ffn_w_down, mix_norm, final_norm,
              ev_w_in, ev_conv_w, ev_conv_b, ev_dt_bias, ev_a_log, ev_d_skip, ev_ssd_norm,
              ev_gla_gate_w2, ev_gla_gate_b, ev_gla_norm, ev_w_out,
              od_mix, od_w0, od_w1, od_w2, od_a0, od_a1, od_a2, od_v0, od_v1, od_v2,
              od_g1, od_g2, od_k_k, od_k_a, od_r_k, od_w_r, od_w_k, od_w_v, od_w_o, od_gn_w, od_gn_b):
    W = dict(ffn_norm=ffn_norm, ffn_w_gu=ffn_w_gu, ffn_w_down=ffn_w_down, mix_norm=mix_norm,
             final_norm=final_norm, ev_w_in=ev_w_in, ev_conv_w=ev_conv_w, ev_conv_b=ev_conv_b,
             ev_dt_bias=ev_dt_bias, ev_a_log=ev_a_log, ev_d_skip=ev_d_skip, ev_ssd_norm=ev_ssd_norm,
             ev_gla_gate_w2=ev_gla_gate_w2, ev_gla_gate_b=ev_gla_gate_b, ev_gla_norm=ev_gla_norm,
             ev_w_out=ev_w_out, od_mix=od_mix, od_w0=od_w0, od_w1=od_w1, od_w2=od_w2,
             od_a0=od_a0, od_a1=od_a1, od_a2=od_a2, od_v0=od_v0, od_v1=od_v1, od_v2=od_v2,
             od_g1=od_g1, od_g2=od_g2, od_k_k=od_k_k, od_k_a=od_k_a, od_r_k=od_r_k,
             od_w_r=od_w_r, od_w_k=od_w_k, od_w_v=od_w_v, od_w_o=od_w_o,
             od_gn_w=od_gn_w, od_gn_b=od_gn_b)
    bp = x_prompt.shape[0]
    dty = x_prompt.dtype
    z_conv = jnp.zeros((N_EVEN, bp, SSD_CONV - 1, SSD_CONV_DIM), dty)
    z_ssm = jnp.zeros((N_EVEN, bp, SSD_HEADS, SSD_HEAD_DIM, SSD_D_STATE), dty)
    z_gla = jnp.zeros((N_EVEN, bp, GLA_HEADS, GLA_DK, GLA_DV), dty)
    z_shift = jnp.zeros((N_ODD, bp, D_MODEL), dty)
    z_wkv = jnp.zeros((N_ODD, bp, RWKV_HEADS, RWKV_HEAD, RWKV_HEAD), dty)
    y_prompt, p_conv, p_ssm, p_gla, p_shift, p_wkv = run_group(x_prompt, z_conv, z_ssm, z_gla, z_shift, z_wkv, W)
    y_sample, s_conv, s_ssm, s_gla, s_shift, s_wkv = run_group(x_sample, state_conv, state_ssm, state_gla, state_shift, state_wkv, W)
    return (y_prompt, y_sample, p_conv, p_ssm, p_gla, p_shift, p_wkv, s_conv, s_ssm, s_gla, s_shift, s_wkv)
```

```python
import functools

import jax
import jax.numpy as jnp
from jax import lax
from jax.experimental import pallas as pl
from jax.experimental.pallas import tpu as pltpu

F32 = jnp.float32
BF16 = jnp.bfloat16

D_MODEL = 1024
DEPTH = 4
N_EVEN = 2
N_ODD = 2
EPS = 1e-5
D_FF = 2816

SSD_HEADS = 16
SSD_HEAD_DIM = 64
SSD_INNER = 1024
SSD_GROUPS = 2
SSD_D_STATE = 128
SSD_CONV = 4
SSD_CONV_DIM = 1536
SSD_CHUNK = 64

GLA_HEADS = 4
GLA_DK = 128
GLA_DV = 256
GLA_KEY = 512
GLA_VAL = 1024
GLA_GATE_RANK = 16
GLA_GATE_NORM = 16.0
GLA_CHUNK = 16

RWKV_HEAD = 64
RWKV_HEADS = 16
RWKV_GN_EPS = 64e-5

LANES = 128
SMALL = LANES
IN_PERM = SSD_INNER + SSD_CONV_DIM + 2 * GLA_KEY + 2 * GLA_VAL + SMALL
VMEM_LIMIT = 56 * 1024 * 1024
FF_TILE = 1408
SEQ_BLOCK = 256
STEP_BATCH = 8


def _cp(sem):
    return pltpu.CompilerParams(dimension_semantics=sem, vmem_limit_bytes=VMEM_LIMIT)


def _iota(shape, axis):
    return lax.broadcasted_iota(jnp.int32, shape, axis)


def _onehot(mask):
    return jnp.where(mask, 1.0, 0.0).astype(BF16)


def _dot(a, b):
    return jnp.dot(a, b, preferred_element_type=F32)


def _split3(x):
    hi = x.astype(BF16)
    r = x - hi.astype(F32)
    mid = r.astype(BF16)
    lo = (r - mid.astype(F32)).astype(BF16)
    return hi, mid, lo


def _dot01(x, e):
    hi, mid, lo = _split3(x)
    return _dot(hi, e) + _dot(mid, e) + _dot(lo, e)


def _dot01_l(e, x):
    hi, mid, lo = _split3(x)
    return _dot(e, hi) + _dot(e, mid) + _dot(e, lo)


def _rms(x, g):
    ms = jnp.mean(x * x, axis=-1, keepdims=True)
    return x * lax.rsqrt(ms + EPS) * g


def _silu(x):
    return x * jax.nn.sigmoid(x)


def _softplus(x):
    return jnp.maximum(x, 0.0) + jnp.log1p(jnp.exp(-jnp.abs(x)))


def _head_expand():
    return _onehot((_iota((LANES, D_MODEL), 1) >> 6) == _iota((LANES, D_MODEL), 0))


def _head_reduce():
    return _onehot((_iota((D_MODEL, LANES), 0) >> 6) == _iota((D_MODEL, LANES), 1))


def _ffn_body(x_ref, g_ref, wg_ref, wu_ref, wd_ref, o_ref, xn_ref, acc_ref, *, nff):
    j = pl.program_id(1)

    @pl.when(j == 0)
    def _():
        xn_ref[...] = _rms(x_ref[...], g_ref[...]).astype(BF16)
        acc_ref[...] = jnp.zeros_like(acc_ref)

    xn = xn_ref[...]
    gate = _dot(xn, wg_ref[...])
    up = _dot(xn, wu_ref[...])
    acc_ref[...] += _dot((_silu(gate) * up).astype(BF16), wd_ref[...])

    @pl.when(j == nff - 1)
    def _():
        o_ref[...] = x_ref[...] + 0.5 * acc_ref[...]


def _ffn(x, nrm, wgu, wd, ls, tm):
    T = x.shape[0]
    nff = D_FF // FF_TILE
    return pl.pallas_call(
        functools.partial(_ffn_body, nff=nff),
        grid=(T // tm, nff),
        in_specs=[
            pl.BlockSpec((tm, D_MODEL), lambda i, j: (i, 0)),
            pl.BlockSpec((None, 1, D_MODEL), lambda i, j: (ls, 0, 0)),
            pl.BlockSpec((None, D_MODEL, FF_TILE), lambda i, j: (ls, 0, j)),
            pl.BlockSpec((None, D_MODEL, FF_TILE), lambda i, j: (ls, 0, j + nff)),
            pl.BlockSpec((None, FF_TILE, D_MODEL), lambda i, j: (ls, j, 0)),
        ],
        out_specs=pl.BlockSpec((tm, D_MODEL), lambda i, j: (i, 0)),
        out_shape=jax.ShapeDtypeStruct((T, D_MODEL), F32),
        scratch_shapes=[pltpu.VMEM((tm, D_MODEL), BF16), pltpu.VMEM((tm, D_MODEL), F32)],
        compiler_params=_cp(("parallel", "arbitrary")),
        name="ffn",
    )(x, nrm, wgu, wgu, wd)


def _final_norm_body(x_ref, g_ref, o_ref):
    o_ref[...] = _rms(x_ref[...], g_ref[...])


def _final_norm(x, g, tm):
    T = x.shape[0]
    return pl.pallas_call(
        _final_norm_body,
        grid=(T // tm,),
        in_specs=[pl.BlockSpec((tm, D_MODEL), lambda i: (i, 0)),
                  pl.BlockSpec((1, D_MODEL), lambda i: (0, 0))],
        out_specs=pl.BlockSpec((tm, D_MODEL), lambda i: (i, 0)),
        out_shape=jax.ShapeDtypeStruct((T, D_MODEL), F32),
        compiler_params=_cp(("parallel",)),
        name="final_norm",
    )(x, g)


_EVEN_PIECES = (SSD_INNER, SSD_CONV_DIM, GLA_KEY, GLA_KEY, GLA_VAL, GLA_VAL, SMALL)


def _even_in_body(x_ref, g_ref, w_ref, *out_refs):
    h = _rms(x_ref[...], g_ref[...]).astype(BF16)
    off = 0
    for ref, n in zip(out_refs, _EVEN_PIECES):
        ref[...] = _dot(h, w_ref[:, off:off + n])
        off += n


def _even_in(x, nrm, w, i, tm):
    T = x.shape[0]
    return pl.pallas_call(
        _even_in_body,
        grid=(T // tm,),
        in_specs=[
            pl.BlockSpec((tm, D_MODEL), lambda t: (t, 0)),
            pl.BlockSpec((None, 1, D_MODEL), lambda t: (2 * i, 0, 0)),
            pl.BlockSpec((None, D_MODEL, IN_PERM), lambda t: (i, 0, 0)),
        ],
        out_specs=[pl.BlockSpec((tm, n), lambda t: (t, 0)) for n in _EVEN_PIECES],
        out_shape=[jax.ShapeDtypeStruct((T, n), F32) for n in _EVEN_PIECES],
        compiler_params=_cp(("parallel",)),
        name="even_in",
    )(x, nrm, w)


def _out_proj_body(x_ref, y_ref, o_ref, w_ref, out_ref):
    acc = _dot(y_ref[...].astype(BF16), w_ref[0:SSD_INNER, :])
    acc = acc + _dot(o_ref[...].astype(BF16), w_ref[SSD_INNER:, :])
    out_ref[...] = x_ref[...] + acc


def _out_proj(x, y, o, w, i, tm):
    T = x.shape[0]
    tok = pl.BlockSpec((tm, D_MODEL), lambda t: (t, 0))
    return pl.pallas_call(
        _out_proj_body,
        grid=(T // tm,),
        in_specs=[tok, tok, tok,
                  pl.BlockSpec((None, SSD_INNER + GLA_VAL, D_MODEL), lambda t: (i, 0, 0))],
        out_specs=tok,
        out_shape=jax.ShapeDtypeStruct((T, D_MODEL), F32),
        compiler_params=_cp(("parallel",)),
        name="even_out",
    )(x, y, o, w)


def _ssd_seq_body(xbc_ref, sm_ref, z_ref, cw_ref, cb_ref, dtb_ref, alog_ref, dsk_ref, nw_ref,
                  y_ref, convn_ref, ssmn_ref,
                  cbuf_ref, xact_ref, acx_ref, dtx_ref, hT_ref, *, nblk):
    Lb = SEQ_BLOCK
    C = SSD_CHUNK
    j = pl.program_id(1)

    @pl.when(j == 0)
    def _():
        cbuf_ref[0:8, :] = jnp.zeros((8, SSD_CONV_DIM), F32)
        hT_ref[...] = jnp.zeros_like(hT_ref)

    cbuf_ref[8:8 + Lb, :] = xbc_ref[...]
    acc = cb_ref[...] + cbuf_ref[5:5 + Lb, :] * cw_ref[0:1, :]
    acc = acc + cbuf_ref[6:6 + Lb, :] * cw_ref[1:2, :]
    acc = acc + cbuf_ref[7:7 + Lb, :] * cw_ref[2:3, :]
    acc = acc + cbuf_ref[8:8 + Lb, :] * cw_ref[3:4, :]
    xact_ref[...] = _silu(acc)
    tail = cbuf_ref[Lb:Lb + 8, :]
    cbuf_ref[0:8, :] = tail

    @pl.when(j == nblk - 1)
    def _():
        convn_ref[...] = tail[5:8, :]

    dt = _softplus(sm_ref[...] + dtb_ref[...])
    dta = dt * (-jnp.exp(alog_ref[...]))
    ii = _iota((Lb, Lb), 0)
    jj = _iota((Lb, Lb), 1)
    tril = _onehot(((ii >> 6) == (jj >> 6)) & (ii >= jj))
    acum = _dot01_l(tril, dta)
    expand = _head_expand()
    acx_ref[...] = _dot01(acum, expand)
    dtx_ref[...] = _dot01(dt, expand)

    li = _iota((C, D_MODEL), 0)
    lj = _iota((C, D_MODEL), 1) & (C - 1)
    diag = li == lj
    causal = li >= lj
    ones_c = jnp.ones((C, C), BF16)
    tile8 = _onehot((_iota((C, 8 * C), 1) & (C - 1)) == _iota((C, 8 * C), 0))
    pair_mask = (_iota((LANES, LANES), 0) >> 6) == (_iota((LANES, LANES), 1) >> 6)
    half = SSD_INNER // SSD_GROUPS

    def chunk(c, carry):
        r0 = pl.multiple_of(c * C, C)
        xa = xact_ref[pl.ds(r0, C), :]
        xs = xa[:, :SSD_INNER]
        bms = [xa[:, SSD_INNER + g * SSD_D_STATE:SSD_INNER + (g + 1) * SSD_D_STATE] for g in range(2)]
        cms = [xa[:, SSD_INNER + (2 + g) * SSD_D_STATE:SSD_INNER + (3 + g) * SSD_D_STATE]
               for g in range(2)]
        acx = acx_ref[pl.ds(r0, C), :]
        dtx = dtx_ref[pl.ds(r0, C), :]
        alast = acx[C - 1:C, :]
        arow = _dot01_l(ones_c, jnp.where(diag, acx, 0.0))
        dtrow = _dot01_l(ones_c, jnp.where(diag, dtx, 0.0))
        decay = jnp.exp(jnp.where(causal, acx - arow, -jnp.inf))
        cbt = []
        for g in range(2):
            cb = lax.dot_general(cms[g].astype(BF16), bms[g].astype(BF16),
                                 (((1,), (1,)), ((), ())), preferred_element_type=F32)
            cbt.append(_dot01(cb, tile8))
        wts = (jnp.concatenate(cbt, axis=1) * decay * dtrow).astype(BF16)
        ys = []
        for p in range(SSD_HEADS // 2):
            xp = xs[:, p * LANES:(p + 1) * LANES]
            xbd = jnp.where(pair_mask, jnp.concatenate([xp, xp], axis=0), 0.0).astype(BF16)
            ys.append(_dot(wts[:, p * LANES:(p + 1) * LANES], xbd))
        y = jnp.concatenate(ys, axis=1)
        hT = hT_ref[...]
        hTb = hT.astype(BF16)
        yi = [_dot(cms[g].astype(BF16), hTb[:, g * half:(g + 1) * half]) for g in range(2)]
        y = y + jnp.concatenate(yi, axis=1) * jnp.exp(acx)
        xsc = (jnp.exp(alast - acx) * dtx * xs).astype(BF16)
        st = [lax.dot_general(bms[g].astype(BF16), xsc[:, g * half:(g + 1) * half],
                              (((0,), (0,)), ((), ())), preferred_element_type=F32) for g in range(2)]
        hT_ref[...] = hT * jnp.exp(alast) + jnp.concatenate(st, axis=1)
        y = y + dsk_ref[...] * xs
        y = y * _silu(z_ref[pl.ds(r0, C), :])
        outs = []
        for g in range(2):
            yg = y[:, g * half:(g + 1) * half]
            ms = jnp.mean(yg * yg, axis=-1, keepdims=True)
            outs.append(yg * lax.rsqrt(ms + EPS))
        y_ref[pl.ds(r0, C), :] = jnp.concatenate(outs, axis=1) * nw_ref[...]
        return carry

    lax.fori_loop(0, Lb // C, chunk, 0)

    @pl.when(j == nblk - 1)
    def _():
        ssmn_ref[...] = hT_ref[...].T


def _ssd_seq(xbc, sm, z, P, i, B, L):
    Lb = SEQ_BLOCK
    nblk = L // Lb
    T = B * L
    tok = lambda n: pl.BlockSpec((Lb, n), lambda b, j: (b * nblk + j, 0))
    par = lambda r, n: pl.BlockSpec((None, r, n), lambda b, j: (i, 0, 0))
    y, convn, ssmn = pl.pallas_call(
        functools.partial(_ssd_seq_body, nblk=nblk),
        grid=(B, nblk),
        in_specs=[tok(SSD_CONV_DIM), tok(SMALL), tok(SSD_INNER),
                  par(SSD_CONV, SSD_CONV_DIM), par(1, SSD_CONV_DIM), par(1, SMALL), par(1, SMALL),
                  par(1, SSD_INNER), par(1, SSD_INNER)],
        out_specs=[tok(SSD_INNER),
                   pl.BlockSpec((None, SSD_CONV - 1, SSD_CONV_DIM), lambda b, j: (b, 0, 0)),
                   pl.BlockSpec((None, SSD_INNER, SSD_D_STATE), lambda b, j: (b, 0, 0))],
        out_shape=[jax.ShapeDtypeStruct((T, SSD_INNER), F32),
                   jax.ShapeDtypeStruct((B, SSD_CONV - 1, SSD_CONV_DIM), F32),
                   jax.ShapeDtypeStruct((B, SSD_INNER, SSD_D_STATE), F32)],
        scratch_shapes=[pltpu.VMEM((Lb + 8, SSD_CONV_DIM), F32),
                        pltpu.VMEM((Lb, SSD_CONV_DIM), F32),
                        pltpu.VMEM((Lb, SSD_INNER), F32),
                        pltpu.VMEM((Lb, SSD_INNER), F32),
                        pltpu.VMEM((SSD_D_STATE, SSD_INNER), F32)],
        compiler_params=_cp(("parallel", "arbitrary")),
        name="ssd_seq",
    )(xbc, sm, z, P["conv_w"], P["conv_b"], P["dt_bias"], P["a_log"], P["d_skip_x"], P["ssd_norm"])
    return y, convn, ssmn.reshape(B, SSD_HEADS, SSD_HEAD_DIM, SSD_D_STATE)


def _gla_seq_body(q_ref, k_ref, v_ref, og_ref, sm_ref, w2_ref, gb_ref, nw_ref,
                  o_ref, glan_ref,
                  st_ref, dec_ref, ot_ref, *, nblk):
    Lb = SEQ_BLOCK
    j = pl.program_id(1)

    @pl.when(j == 0)
    def _():
        st_ref[...] = jnp.zeros_like(st_ref)

    gate = _dot(sm_ref[...].astype(BF16), w2_ref[...]) + gb_ref[...]
    lg = -_softplus(-gate) / GLA_GATE_NORM
    ii = _iota((Lb, Lb), 0)
    jj = _iota((Lb, Lb), 1)
    same = (ii >> 4) == (jj >> 4)
    causal = same & (ii >= jj)
    b = _dot01_l(_onehot(causal), lg)
    blast = _dot01_l(_onehot(same), lg)
    qg = q_ref[...] * (GLA_DK ** -0.5) * jnp.exp(b)
    kk = k_ref[...]
    kg = kk * jnp.exp(-b)
    kd = kk * jnp.exp(blast - b)
    dec_ref[...] = jnp.exp(blast)
    lane_chunk = _iota((1, Lb), 1) >> 4

    for h in range(GLA_HEADS):
        ks = slice(h * GLA_DK, (h + 1) * GLA_DK)
        vs = slice(h * GLA_DV, (h + 1) * GLA_DV)
        qg_h = qg[:, ks]
        v_h = v_ref[:, vs]
        att = lax.dot_general(qg_h.astype(BF16), kg[:, ks].astype(BF16),
                              (((1,), (1,)), ((), ())), preferred_element_type=F32)
        att = jnp.where(causal, att, 0.0)
        o_intra = _dot(att.astype(BF16), v_h.astype(BF16))
        qgt = qg_h.T.astype(BF16)
        vt = v_h.T
        kd_h = kd[:, ks].astype(BF16)
        ot_ref[...] = jnp.zeros_like(ot_ref)

        def chunk(c, carry):
            in_c = lane_chunk == c
            st = st_ref[h]
            part = _dot(st.astype(BF16), qgt)
            ot_ref[...] = jnp.where(in_c, part, ot_ref[...])
            vtm = jnp.where(in_c, vt, 0.0).astype(BF16)
            drow = dec_ref[pl.ds(pl.multiple_of(c * GLA_CHUNK, GLA_CHUNK), 8), ks][0:1, :]
            st_ref[h] = st * drow + _dot(vtm, kd_h)
            return carry

        lax.fori_loop(0, Lb // GLA_CHUNK, chunk, 0)
        o_h = o_intra + ot_ref[...].T
        ms = jnp.mean(o_h * o_h, axis=-1, keepdims=True)
        o_h = o_h * lax.rsqrt(ms + EPS) * nw_ref[...]
        o_ref[:, vs] = o_h * _silu(og_ref[:, vs])

    @pl.when(j == nblk - 1)
    def _():
        for h in range(GLA_HEADS):
            glan_ref[h] = st_ref[h].T


def _gla_seq(q, k, v, og, sm, P, i, B, L):
    Lb = SEQ_BLOCK
    nblk = L // Lb
    T = B * L
    tok = lambda n: pl.BlockSpec((Lb, n), lambda b, j: (b * nblk + j, 0))
    par = lambda r, n: pl.BlockSpec((None, r, n), lambda b, j: (i, 0, 0))
    return pl.pallas_call(
        functools.partial(_gla_seq_body, nblk=nblk),
        grid=(B, nblk),
        in_specs=[tok(GLA_KEY), tok(GLA_KEY), tok(GLA_VAL), tok(GLA_VAL), tok(SMALL),
                  par(SMALL, GLA_KEY), par(1, GLA_KEY), par(1, GLA_DV)],
        out_specs=[tok(GLA_VAL),
                   pl.BlockSpec((None, GLA_HEADS, GLA_DK, GLA_DV), lambda b, j: (b, 0, 0, 0))],
        out_shape=[jax.ShapeDtypeStruct((T, GLA_VAL), F32),
                   jax.ShapeDtypeStruct((B, GLA_HEADS, GLA_DK, GLA_DV), F32)],
        scratch_shapes=[pltpu.VMEM((GLA_HEADS, GLA_DV, GLA_DK), F32),
                        pltpu.VMEM((Lb, GLA_KEY), F32),
                        pltpu.VMEM((GLA_DV, Lb), F32)],
        compiler_params=_cp(("parallel", "arbitrary")),
        name="gla_seq",
    )(q, k, v, og, sm, P["gate_w2"], P["gate_b"], P["gla_norm"])


def _even_step_pre_body(xbc_ref, c0_ref, sm_ref, cw_ref, cb_ref, dtb_ref, alog_ref, dsk_ref,
                        w2_ref, gb_ref,
                        xact_ref, convn_ref, dax_ref, c2_ref, ybase_ref, eg_ref):
    u = xbc_ref[...]
    n = SSD_CONV_DIM
    acc = cb_ref[...] + c0_ref[:, 0:n] * cw_ref[0:1, :]
    acc = acc + c0_ref[:, n:2 * n] * cw_ref[1:2, :]
    acc = acc + c0_ref[:, 2 * n:3 * n] * cw_ref[2:3, :]
    acc = acc + u * cw_ref[3:4, :]
    xa = _silu(acc)
    xact_ref[...] = xa
    convn_ref[:, 0:n] = c0_ref[:, n:2 * n]
    convn_ref[:, n:2 * n] = c0_ref[:, 2 * n:3 * n]
    convn_ref[:, 2 * n:3 * n] = u
    dt = _softplus(sm_ref[...] + dtb_ref[...])
    dta = dt * (-jnp.exp(alog_ref[...]))
    expand = _head_expand()
    dtx = _dot01(dt, expand)
    dax_ref[...] = jnp.exp(_dot01(dta, expand))
    xs = xa[:, :SSD_INNER]
    c2_ref[...] = dtx * xs
    half = SSD_INNER // SSD_GROUPS
    cbs = []
    for g in range(2):
        bm = xa[:, SSD_INNER + g * SSD_D_STATE:SSD_INNER + (g + 1) * SSD_D_STATE]
        cm = xa[:, SSD_INNER + (2 + g) * SSD_D_STATE:SSD_INNER + (3 + g) * SSD_D_STATE]
        cb = jnp.sum(cm * bm, axis=-1, keepdims=True)
        cbs.append(jnp.broadcast_to(cb, (cb.shape[0], half)))
    ybase_ref[...] = jnp.concatenate(cbs, axis=1) * dtx * xs + dsk_ref[...] * xs
    gate = _dot(sm_ref[...].astype(BF16), w2_ref[...]) + gb_ref[...]
    eg_ref[...] = jnp.exp(-_softplus(-gate) / GLA_GATE_NORM)


def _even_step_pre(xbc, conv0, sm, P, i):
    Bn = xbc.shape[0]
    full = lambda n: pl.BlockSpec((Bn, n), lambda t: (0, 0))
    par = lambda r, n: pl.BlockSpec((None, r, n), lambda t: (i, 0, 0))
    outs = (SSD_CONV_DIM, 3 * SSD_CONV_DIM, SSD_INNER, SSD_INNER, SSD_INNER, GLA_KEY)
    return pl.pallas_call(
        _even_step_pre_body,
        grid=(1,),
        in_specs=[full(SSD_CONV_DIM), full(3 * SSD_CONV_DIM), full(SMALL),
                  par(SSD_CONV, SSD_CONV_DIM), par(1, SSD_CONV_DIM), par(1, SMALL), par(1, SMALL),
                  par(1, SSD_INNER), par(SMALL, GLA_KEY), par(1, GLA_KEY)],
        out_specs=[full(n) for n in outs],
        out_shape=[jax.ShapeDtypeStruct((Bn, n), F32) for n in outs],
        compiler_params=_cp(("arbitrary",)),
        name="even_step_pre",
    )(xbc, conv0, sm, P["conv_w"], P["conv_b"], P["dt_bias"], P["a_log"], P["d_skip_x"],
      P["gate_w2"], P["gate_b"])


def _to_cols(x):
    Bn, N = x.shape
    return x.reshape(Bn // STEP_BATCH, STEP_BATCH, N).transpose(0, 2, 1)


def _from_cols(x):
    nb, N, bb = x.shape
    return x.transpose(0, 2, 1).reshape(nb * bb, N)


def _ssd_step_body(h_ref, c1_ref, c2_ref, bm_ref, cm_ref, hn_ref, y_ref):
    half = SSD_INNER // SSD_GROUPS
    lane = _iota((SSD_INNER, STEP_BATCH), 1)
    ycols = jnp.zeros((SSD_INNER, STEP_BATCH), F32)
    c1 = c1_ref[...]
    c2 = c2_ref[...]
    for b in range(STEP_BATCH):
        h0 = h_ref[b]
        bmat = jnp.concatenate(
            [jnp.broadcast_to(bm_ref[b:b + 1, g * SSD_D_STATE:(g + 1) * SSD_D_STATE],
                              (half, SSD_D_STATE)) for g in range(2)], axis=0)
        cmat = jnp.concatenate(
            [jnp.broadcast_to(cm_ref[b:b + 1, g * SSD_D_STATE:(g + 1) * SSD_D_STATE],
                              (half, SSD_D_STATE)) for g in range(2)], axis=0)
        ycol = jnp.sum(h0 * cmat, axis=1, keepdims=True)
        ycols = jnp.where(lane == b, ycol, ycols)
        hn_ref[b] = h0 * c1[:, b:b + 1] + c2[:, b:b + 1] * bmat
    y_ref[...] = ycols


def _ssd_step(h0, dax, c2, xact):
    Bn = h0.shape[0]
    bb = STEP_BATCH
    h0 = h0.reshape(Bn, SSD_INNER, SSD_D_STATE)
    bm = xact[:, SSD_INNER:SSD_INNER + 2 * SSD_D_STATE]
    cm = xact[:, SSD_INNER + 2 * SSD_D_STATE:]
    col = pl.BlockSpec((None, SSD_INNER, bb), lambda t: (t, 0, 0))
    st = pl.BlockSpec((bb, SSD_INNER, SSD_D_STATE), lambda t: (t, 0, 0))
    row = pl.BlockSpec((bb, 2 * SSD_D_STATE), lambda t: (t, 0))
    hn, ycols = pl.pallas_call(
        _ssd_step_body,
        grid=(Bn // bb,),
        in_specs=[st, col, col, row, row],
        out_specs=[st, col],
        out_shape=[jax.ShapeDtypeStruct((Bn, SSD_INNER, SSD_D_STATE), F32),
                   jax.ShapeDtypeStruct((Bn // bb, SSD_INNER, bb), F32)],
        compiler_params=_cp(("parallel",)),
        name="ssd_step",
    )(h0, _to_cols(dax), _to_cols(c2), bm, cm)
    return hn.reshape(Bn, SSD_HEADS, SSD_HEAD_DIM, SSD_D_STATE), _from_cols(ycols)


def _gla_step_body(s_ref, eg_ref, kc_ref, qc_ref, v_ref, sn_ref, o_ref):
    eg = eg_ref[...]
    kc = kc_ref[...]
    qc = qc_ref[...] * (GLA_DK ** -0.5)
    for b in range(STEP_BATCH):
        s0 = s_ref[b]
        vmat = jnp.concatenate(
            [jnp.broadcast_to(v_ref[b:b + 1, h * GLA_DV:(h + 1) * GLA_DV], (GLA_DK, GLA_DV))
             for h in range(GLA_HEADS)], axis=0)
        sn = s0 * eg[:, b:b + 1] + kc[:, b:b + 1] * vmat
        sn_ref[b] = sn
        t = sn * qc[:, b:b + 1]
        o_ref[b:b + 1, :] = jnp.concatenate(
            [jnp.sum(t[h * GLA_DK:(h + 1) * GLA_DK], axis=0, keepdims=True)
             for h in range(GLA_HEADS)], axis=1)


def _gla_step(s0, eg, k, q, v):
    Bn = s0.shape[0]
    bb = STEP_BATCH
    s0 = s0.reshape(Bn, GLA_KEY, GLA_DV)
    col = pl.BlockSpec((None, GLA_KEY, bb), lambda t: (t, 0, 0))
    st = pl.BlockSpec((bb, GLA_KEY, GLA_DV), lambda t: (t, 0, 0))
    row = pl.BlockSpec((bb, GLA_VAL), lambda t: (t, 0))
    sn, o = pl.pallas_call(
        _gla_step_body,
        grid=(Bn // bb,),
        in_specs=[st, col, col, col, row],
        out_specs=[st, row],
        out_shape=[jax.ShapeDtypeStruct((Bn, GLA_KEY, GLA_DV), F32),
                   jax.ShapeDtypeStruct((Bn, GLA_VAL), F32)],
        compiler_params=_cp(("parallel",)),
        name="gla_step",
    )(s0, _to_cols(eg), _to_cols(k), _to_cols(q), v)
    return sn.reshape(Bn, GLA_HEADS, GLA_DK, GLA_DV), o


def _even_step_post_body(x_ref, ybase_ref, yi_ref, dax_ref, z_ref, o_ref, og_ref,
                         snw_ref, gnw_ref, w_ref, out_ref):
    y = (ybase_ref[...] + yi_ref[...] * dax_ref[...]) * _silu(z_ref[...])
    half = SSD_INNER // SSD_GROUPS
    acc = None
    for g in range(2):
        yg = y[:, g * half:(g + 1) * half]
        ms = jnp.mean(yg * yg, axis=-1, keepdims=True)
        yn = yg * lax.rsqrt(ms + EPS) * snw_ref[:, g * half:(g + 1) * half]
        part = _dot(yn.astype(BF16), w_ref[g * half:(g + 1) * half, :])
        acc = part if acc is None else acc + part
    for h in range(GLA_HEADS):
        vs = slice(h * GLA_DV, (h + 1) * GLA_DV)
        oh = o_ref[:, vs]
        ms = jnp.mean(oh * oh, axis=-1, keepdims=True)
        on = oh * lax.rsqrt(ms + EPS) * gnw_ref[...] * _silu(og_ref[:, vs])
        acc = acc + _dot(on.astype(BF16), w_ref[SSD_INNER + h * GLA_DV:SSD_INNER + (h + 1) * GLA_DV, :])
    out_ref[...] = x_ref[...] + acc


def _even_step_post(x, ybase, yi, dax, z, o, og, P, w_out, i):
    Bn = x.shape[0]
    full = pl.BlockSpec((Bn, D_MODEL), lambda t: (0, 0))
    return pl.pallas_call(
        _even_step_post_body,
        grid=(1,),
        in_specs=[full] * 7 + [pl.BlockSpec((None, 1, SSD_INNER), lambda t: (i, 0, 0)),
                               pl.BlockSpec((None, 1, GLA_DV), lambda t: (i, 0, 0)),
                               pl.BlockSpec((None, SSD_INNER + GLA_VAL, D_MODEL), lambda t: (i, 0, 0))],
        out_specs=full,
        out_shape=jax.ShapeDtypeStruct((Bn, D_MODEL), F32),
        compiler_params=_cp(("arbitrary",)),
        name="even_step_post",
    )(x, ybase, yi, dax, z, o, og, P["ssd_norm"], P["gla_norm"], w_out)


_ODD_OUTS = 8


def _odd_pre_body(*refs, step, has_vfirst, nblk_seq, tm):
    it = iter(refs)
    x_ref = next(it)
    if step:
        prev_ref = next(it)
    else:
        xp_ref = next(it)
        sh0_ref = next(it)
    g_ref, mix_ref = next(it), next(it)
    wr_ref, wk_ref, wv_ref = next(it), next(it), next(it)
    w0_ref, w1_ref, w2_ref = next(it), next(it), next(it)
    a0_ref, a1_ref, a2_ref = next(it), next(it), next(it)
    if has_vfirst:
        vf_ref, v0_ref, v1_ref, v2_ref = next(it), next(it), next(it), next(it)
    g1_ref, g2_ref = next(it), next(it)
    kkw_ref, kaw_ref = next(it), next(it)
    h_ref, r_ref, dec_ref, k_ref, v_ref, an_ref, b_ref, gg_ref = (next(it) for _ in range(_ODD_OUTS))

    h = _rms(x_ref[...], g_ref[...])
    if step:
        prev = prev_ref[...]
    else:
        i = pl.program_id(0)
        hp = _rms(xp_ref[...], g_ref[...])[7:8, :]
        prow = jnp.where(i % nblk_seq == 0, sh0_ref[...], hp)
        prev = jnp.where(_iota((tm, D_MODEL), 0) == 0, prow, pltpu.roll(h, 1, axis=0))
    h_ref[...] = h
    xx = prev - h
    mixed = lambda r: (h + xx * mix_ref[r:r + 1, :]).astype(BF16)
    xr, xw, xk, xv, xa, xg = (mixed(r) for r in range(6))
    r_ref[...] = _dot(xr, wr_ref[...])
    wl = w0_ref[...] + _dot(jnp.tanh(_dot(xw, w1_ref[...])).astype(BF16), w2_ref[...])
    w = -_softplus(-wl) - 0.5
    dec_ref[...] = jnp.exp(-jnp.exp(w))
    k = _dot(xk, wk_ref[...])
    v = _dot(xv, wv_ref[...])
    if has_vfirst:
        gate = jax.nn.sigmoid(v0_ref[...] + _dot(_dot(xv, v1_ref[...]).astype(BF16), v2_ref[...]))
        v = v + (vf_ref[...] - v) * gate
    v_ref[...] = v
    a = jax.nn.sigmoid(a0_ref[...] + _dot(_dot(xa, a1_ref[...]).astype(BF16), a2_ref[...]))
    gg_ref[...] = _dot(jax.nn.sigmoid(_dot(xg, g1_ref[...])).astype(BF16), g2_ref[...])
    kkf = k * kkw_ref[...]
    ss = _dot01(kkf * kkf, _head_reduce())
    kk = kkf * _dot01(lax.rsqrt(jnp.maximum(ss, 1e-24)), _head_expand())
    k_ref[...] = k * (1.0 + (a - 1.0) * kaw_ref[...])
    an_ref[...] = -kk
    b_ref[...] = kk * a


def _odd_pre(x, prev_or_shift0, vfirst, P, i, tm, L):
    T = x.shape[0]
    step = L == 1
    has_vfirst = vfirst is not None
    nblk_seq = max(L // tm, 1)
    tok = pl.BlockSpec((tm, D_MODEL), lambda t: (t, 0))
    par = lambda r, n, idx=i: pl.BlockSpec((None, r, n), lambda t: (idx, 0, 0))
    args = [x]
    specs = [tok]
    if step:
        args.append(prev_or_shift0)
        specs.append(tok)
    else:
        args += [x, prev_or_shift0]
        specs += [pl.BlockSpec((8, D_MODEL), lambda t: (jnp.maximum(t * (tm // 8) - 1, 0), 0)),
                  pl.BlockSpec((None, 1, D_MODEL), lambda t: (t // nblk_seq, 0, 0))]
    args += [P["mix_norm"], P["mix"], P["w_r"], P["w_k"], P["w_v"], P["w0"], P["w1"], P["w2"],
             P["a0"], P["a1"], P["a2"]]
    specs += [pl.BlockSpec((None, 1, D_MODEL), lambda t: (2 * i + 1, 0, 0)),
              par(8, D_MODEL), par(D_MODEL, D_MODEL), par(D_MODEL, D_MODEL), par(D_MODEL, D_MODEL),
              par(1, D_MODEL), par(D_MODEL, LANES), par(LANES, D_MODEL),
              par(1, D_MODEL), par(D_MODEL, LANES), par(LANES, D_MODEL)]
    if has_vfirst:
        args += [vfirst, P["v0"], P["v1"], P["v2"]]
        specs += [tok, par(1, D_MODEL, i - 1), par(D_MODEL, LANES, i - 1), par(LANES, D_MODEL, i - 1)]
    args += [P["g1"], P["g2"], P["k_k"], P["k_a"]]
    specs += [par(D_MODEL, 2 * LANES), par(2 * LANES, D_MODEL), par(1, D_MODEL), par(1, D_MODEL)]
    return pl.pallas_call(
        functools.partial(_odd_pre_body, step=step, has_vfirst=has_vfirst, nblk_seq=nblk_seq, tm=tm),
        grid=(T // tm,),
        in_specs=specs,
        out_specs=[tok] * _ODD_OUTS,
        out_shape=[jax.ShapeDtypeStruct((T, D_MODEL), F32)] * _ODD_OUTS,
        compiler_params=_cp(("parallel",)),
        name="odd_pre",
    )(*args)


def _odd_post_body(x_ref, o_ref, r_ref, k_ref, v_ref, g_ref, gnw_ref, gnb_ref, rk_ref, wo_ref, out_ref):
    red = _head_reduce()
    expand = _head_expand()
    o = o_ref[...]
    mu = _dot01(_dot01(o, red) * (1.0 / RWKV_HEAD), expand)
    d = o - mu
    var = _dot01(d * d, red) * (1.0 / RWKV_HEAD)
    on = d * _dot01(lax.rsqrt(var + RWKV_GN_EPS), expand) * gnw_ref[...] + gnb_ref[...]
    v = v_ref[...]
    bonus = _dot01(_dot01(r_ref[...] * k_ref[...] * rk_ref[...], red), expand) * v
    out_ref[...] = x_ref[...] + _dot(((on + bonus) * g_ref[...]).astype(BF16), wo_ref[...])


def _odd_post(x, o, r, k, v, g, P, i, tm):
    T = x.shape[0]
    tok = pl.BlockSpec((tm, D_MODEL), lambda t: (t, 0))
    par = lambda r_, n: pl.BlockSpec((None, r_, n), lambda t: (i, 0, 0))
    return pl.pallas_call(
        _odd_post_body,
        grid=(T // tm,),
        in_specs=[tok] * 6 + [par(1, D_MODEL), par(1, D_MODEL), par(1, D_MODEL), par(D_MODEL, D_MODEL)],
        out_specs=tok,
        out_shape=jax.ShapeDtypeStruct((T, D_MODEL), F32),
        compiler_params=_cp(("parallel",)),
        name="odd_post",
    )(x, o, r, k, v, g, P["gn_w"], P["gn_b"], P["r_k"], P["w_o"])


RWKV_TB = 256
RWKV_PAIRS = 2


def _rwkv_seq_body(w_ref, k_ref, a_ref, b_ref, r_ref, v_ref, o_ref, sn_ref, s_ref, src_ref, *, ntb):
    N = RWKV_HEAD
    tbi = pl.program_id(2)

    @pl.when(tbi == 0)
    def _():
        s_ref[...] = jnp.zeros_like(s_ref)

    lane_base = (_iota((N, LANES), 1) >> 6) * N
    for sub in range(RWKV_TB // N):
        base = sub * N
        for qi, ref in enumerate((w_ref, k_ref, a_ref, b_ref, r_ref)):
            for p in range(RWKV_PAIRS):
                bt = ref[base:base + N, p * LANES:(p + 1) * LANES].T
                src_ref[qi, p] = jnp.concatenate([bt[0:N], bt[N:2 * N]], axis=1)

        def step8(t8, states):
            row0 = pl.multiple_of(base + t8 * 8, 8)
            states = list(states)
            for p in range(RWKV_PAIRS):
                ls = slice(p * LANES, (p + 1) * LANES)
                v8 = v_ref[pl.ds(row0, 8), ls]
                s = states[p]
                o_rows = []
                for j in range(8):
                    idx = lane_base + (t8 * 8 + j)
                    col = lambda qi: jnp.take_along_axis(src_ref[qi, p], idx, axis=1)
                    w_c, k_c, a_c, b_c, r_c = (col(qi) for qi in range(5))
                    sa = jnp.sum(s * a_c, axis=0, keepdims=True)
                    s = s * w_c + sa * b_c + k_c * v8[j:j + 1, :]
                    o_rows.append(jnp.sum(s * r_c, axis=0, keepdims=True))
                o_ref[pl.ds(row0, 8), ls] = jnp.concatenate(o_rows, axis=0)
                states[p] = s
            return tuple(states)

        states = lax.fori_loop(0, N // 8, step8, tuple(s_ref[p] for p in range(RWKV_PAIRS)))
        for p in range(RWKV_PAIRS):
            s_ref[p] = states[p]

    @pl.when(tbi == ntb - 1)
    def _():
        for p in range(RWKV_PAIRS):
            sn_ref[p * LANES:(p + 1) * LANES, :] = s_ref[p].T


def _rwkv_seq(w, k, a, b, r, v, B, L):
    T = B * L
    ntb = L // RWKV_TB
    width = RWKV_PAIRS * LANES
    tok = pl.BlockSpec((RWKV_TB, width), lambda bi, pg, t: (bi * ntb + t, pg))
    o, sn = pl.pallas_call(
        functools.partial(_rwkv_seq_body, ntb=ntb),
        grid=(B, D_MODEL // width, ntb),
        in_specs=[tok] * 6,
        out_specs=[tok, pl.BlockSpec((None, width, RWKV_HEAD), lambda bi, pg, t: (bi, pg, 0))],
        out_shape=[jax.ShapeDtypeStruct((T, D_MODEL), F32),
                   jax.ShapeDtypeStruct((B, D_MODEL, RWKV_HEAD), F32)],
        scratch_shapes=[pltpu.VMEM((RWKV_PAIRS, RWKV_HEAD, LANES), F32),
                        pltpu.VMEM((5, RWKV_PAIRS, RWKV_HEAD, LANES), F32)],
        compiler_params=_cp(("parallel", "parallel", "arbitrary")),
        name="rwkv_seq",
    )(w, k, a, b, r, v)
    return o, sn.reshape(B, RWKV_HEADS, RWKV_HEAD, RWKV_HEAD)


def _rwkv_step_body(s_ref, w_ref, k_ref, a_ref, b_ref, r_ref, vc_ref, sn_ref, o_ref):
    N = RWKV_HEAD
    lane = _iota((D_MODEL, STEP_BATCH), 1)
    ocols = jnp.zeros((D_MODEL, STEP_BATCH), F32)
    vc = vc_ref[...]

    def rows(ref, b):
        return jnp.concatenate(
            [jnp.broadcast_to(ref[b:b + 1, h * N:(h + 1) * N], (N, N)) for h in range(RWKV_HEADS)], axis=0)

    for b in range(STEP_BATCH):
        s = s_ref[b]
        sa = jnp.sum(s * rows(a_ref, b), axis=1, keepdims=True)
        s = s * rows(w_ref, b) + sa * rows(b_ref, b) + vc[:, b:b + 1] * rows(k_ref, b)
        sn_ref[b] = s
        ocol = jnp.sum(s * rows(r_ref, b), axis=1, keepdims=True)
        ocols = jnp.where(lane == b, ocol, ocols)
    o_ref[...] = ocols


def _rwkv_step(s0, w, k, a, b, r, v):
    Bn = s0.shape[0]
    bb = STEP_BATCH
    s0 = s0.reshape(Bn, D_MODEL, RWKV_HEAD)
    col = pl.BlockSpec((None, D_MODEL, bb), lambda t: (t, 0, 0))
    st = pl.BlockSpec((bb, D_MODEL, RWKV_HEAD), lambda t: (t, 0, 0))
    row = pl.BlockSpec((bb, D_MODEL), lambda t: (t, 0))
    sn, ocols = pl.pallas_call(
        _rwkv_step_body,
        grid=(Bn // bb,),
        in_specs=[st, row, row, row, row, row, col],
        out_specs=[st, col],
        out_shape=[jax.ShapeDtypeStruct((Bn, D_MODEL, RWKV_HEAD), F32),
                   jax.ShapeDtypeStruct((Bn // bb, D_MODEL, bb), F32)],
        compiler_params=_cp(("parallel",)),
        name="rwkv_step",
    )(s0, w, k, a, b, r, _to_cols(v))
    return _from_cols(ocols), sn.reshape(Bn, RWKV_HEADS, RWKV_HEAD, RWKV_HEAD)


def _pad_to(w, axis, n):
    pad = [(0, 0)] * w.ndim
    pad[axis] = (0, n - w.shape[axis])
    return jnp.pad(w, pad)


def _prepare(W):
    bf = lambda w: w.astype(BF16)
    row = lambda w: w[:, None, :]
    sizes = [SSD_INNER, SSD_CONV_DIM, SSD_HEADS, GLA_KEY, GLA_KEY, GLA_VAL, GLA_VAL, GLA_GATE_RANK]
    offs = [0]
    for s in sizes:
        offs.append(offs[-1] + s)
    piece = lambda n: W["ev_w_in"][:, :, offs[n]:offs[n + 1]]
    w_in = jnp.concatenate([piece(0), piece(1), piece(3), piece(4), piece(5), piece(6),
                            _pad_to(jnp.concatenate([piece(2), piece(7)], axis=-1), 2, SMALL)], axis=-1)
    gate_w2 = jnp.pad(W["ev_gla_gate_w2"], ((0, 0), (SSD_HEADS, SMALL - SSD_HEADS - GLA_GATE_RANK), (0, 0)))
    even = dict(
        w_in=bf(w_in), w_out=bf(W["ev_w_out"]),
        conv_w=W["ev_conv_w"], conv_b=row(W["ev_conv_b"]),
        dt_bias=row(_pad_to(W["ev_dt_bias"], 1, SMALL)), a_log=row(_pad_to(W["ev_a_log"], 1, SMALL)),
        d_skip_x=row(jnp.repeat(W["ev_d_skip"], SSD_HEAD_DIM, axis=1)),
        ssd_norm=row(W["ev_ssd_norm"]), gate_w2=bf(gate_w2), gate_b=row(W["ev_gla_gate_b"]),
        gla_norm=row(W["ev_gla_norm"]),
    )
    odd = dict(
        mix_norm=row(W["mix_norm"]), mix=_pad_to(W["od_mix"], 1, 8),
        w_r=bf(W["od_w_r"]), w_k=bf(W["od_w_k"]), w_v=bf(W["od_w_v"]), w_o=bf(W["od_w_o"]),
        w0=row(W["od_w0"]), w1=bf(_pad_to(W["od_w1"], 2, LANES)), w2=bf(_pad_to(W["od_w2"], 1, LANES)),
        a0=row(W["od_a0"]), a1=bf(_pad_to(W["od_a1"], 2, LANES)), a2=bf(_pad_to(W["od_a2"], 1, LANES)),
        v0=row(W["od_v0"]), v1=bf(_pad_to(W["od_v1"], 2, LANES)), v2=bf(_pad_to(W["od_v2"], 1, LANES)),
        g1=bf(_pad_to(W["od_g1"], 2, 2 * LANES)), g2=bf(_pad_to(W["od_g2"], 1, 2 * LANES)),
        k_k=row(W["od_k_k"]), k_a=row(W["od_k_a"]), r_k=row(W["od_r_k"]),
        gn_w=row(W["od_gn_w"]), gn_b=row(W["od_gn_b"]),
    )
    ffn = dict(
        norm=W["ffn_norm"].reshape(DEPTH * 2, 1, D_MODEL),
        w_gu=bf(W["ffn_w_gu"]).reshape(DEPTH * 2, D_MODEL, 2 * D_FF),
        w_down=bf(W["ffn_w_down"]).reshape(DEPTH * 2, D_FF, D_MODEL),
    )
    return dict(even=even, odd=odd, ffn=ffn, mix_norm=row(W["mix_norm"]),
                final_norm=W["final_norm"][None, :])


def _run_group(x3, states, Wp):
    B, L, _ = x3.shape
    T = B * L
    step = L == 1
    x = x3.reshape(T, D_MODEL)
    tm_ffn = min(512, T)
    tm_proj = min(256, T)
    ffn, even, odd = Wp["ffn"], Wp["even"], Wp["odd"]
    convs, ssms, glas, shifts, wkvs = [], [], [], [], []
    v_first = None
    for layer in range(DEPTH):
        i = layer // 2
        x = _ffn(x, ffn["norm"], ffn["w_gu"], ffn["w_down"], 2 * layer, tm_ffn)
        if layer % 2 == 0:
            z, xbc, q, k, v, og, sm = _even_in(x, Wp["mix_norm"], even["w_in"], i, tm_proj)
            if step:
                conv0, ssm0, gla0 = states[0][i], states[1][i], states[2][i]
                xact, convn, dax, c2, ybase, eg = _even_step_pre(
                    xbc, conv0.reshape(B, 3 * SSD_CONV_DIM), sm, even, i)
                ssmn, yi = _ssd_step(ssm0, dax, c2, xact)
                glan, o = _gla_step(gla0, eg, k, q, v)
                x = _even_step_post(x, ybase, yi, dax, z, o, og, even, even["w_out"], i)
                convn = convn.reshape(B, SSD_CONV - 1, SSD_CONV_DIM)
            else:
                y, convn, ssmn = _ssd_seq(xbc, sm, z, even, i, B, L)
                o, glan = _gla_seq(q, k, v, og, sm, even, i, B, L)
                x = _out_proj(x, y, o, even["w_out"], i, tm_ffn)
            convs.append(convn)
            ssms.append(ssmn)
            glas.append(glan)
        else:
            if step:
                prev, wkv0 = states[3][i], states[4][i]
            else:
                prev = jnp.zeros((B, 1, D_MODEL), F32)
            h, r, dec, k, v, an, bb, gg = _odd_pre(x, prev, v_first, odd, i, tm_proj, L)
            if v_first is None:
                v_first = v
            if step:
                o, wkvn = _rwkv_step(wkv0, dec, k, an, bb, r, v)
            else:
                o, wkvn = _rwkv_seq(dec, k, an, bb, r, v, B, L)
            x = _odd_post(x, o, r, k, v, gg, odd, i, tm_ffn)
            shifts.append(h.reshape(B, L, D_MODEL)[:, -1])
            wkvs.append(wkvn)
        x = _ffn(x, ffn["norm"], ffn["w_gu"], ffn["w_down"], 2 * layer + 1, tm_ffn)
    y = _final_norm(x, Wp["final_norm"], tm_ffn).reshape(B, L, D_MODEL)
    return y, jnp.stack(convs), jnp.stack(ssms), jnp.stack(glas), jnp.stack(shifts), jnp.stack(wkvs)


def kernel(x_prompt, x_sample, state_conv, state_ssm, state_gla, state_shift, state_wkv, ffn_norm, ffn_w_gu, ffn_w_down, mix_norm, final_norm, ev_w_in, ev_conv_w, ev_conv_b, ev_dt_bias, ev_a_log, ev_d_skip, ev_ssd_norm, ev_gla_gate_w2, ev_gla_gate_b, ev_gla_norm, ev_w_out, od_mix, od_w0, od_w1, od_w2, od_a0, od_a1, od_a2, od_v0, od_v1, od_v2, od_g1, od_g2, od_k_k, od_k_a, od_r_k, od_w_r, od_w_k, od_w_v, od_w_o, od_gn_w, od_gn_b):
    W = dict(ffn_norm=ffn_norm, ffn_w_gu=ffn_w_gu, ffn_w_down=ffn_w_down, mix_norm=mix_norm,
             final_norm=final_norm, ev_w_in=ev_w_in, ev_conv_w=ev_conv_w, ev_conv_b=ev_conv_b,
             ev_dt_bias=ev_dt_bias, ev_a_log=ev_a_log, ev_d_skip=ev_d_skip, ev_ssd_norm=ev_ssd_norm,
             ev_gla_gate_w2=ev_gla_gate_w2, ev_gla_gate_b=ev_gla_gate_b, ev_gla_norm=ev_gla_norm,
             ev_w_out=ev_w_out, od_mix=od_mix, od_w0=od_w0, od_w1=od_w1, od_w2=od_w2,
             od_a0=od_a0, od_a1=od_a1, od_a2=od_a2, od_v0=od_v0, od_v1=od_v1, od_v2=od_v2,
             od_g1=od_g1, od_g2=od_g2, od_k_k=od_k_k, od_k_a=od_k_a, od_r_k=od_r_k,
             od_w_r=od_w_r, od_w_k=od_w_k, od_w_v=od_w_v, od_w_o=od_w_o,
             od_gn_w=od_gn_w, od_gn_b=od_gn_b)
    Wp = _prepare(W)
    prompt = _run_group(x_prompt, None, Wp)
    sample = _run_group(x_sample, (state_conv, state_ssm, state_gla, state_shift, state_wkv), Wp)
    return (prompt[0], sample[0]) + prompt[1:] + sample[1:]
```

```python
import functools

import jax
import jax.numpy as jnp
from jax import lax
from jax.experimental import pallas as pl
from jax.experimental.pallas import tpu as pltpu

F32 = jnp.float32
BF16 = jnp.bfloat16

D_MODEL = 1024
DEPTH = 4
N_EVEN = 2
N_ODD = 2
EPS = 1e-5
D_FF = 2816

SSD_HEADS = 16
SSD_HEAD_DIM = 64
SSD_INNER = 1024
SSD_GROUPS = 2
SSD_D_STATE = 128
SSD_CONV = 4
SSD_CONV_DIM = 1536
SSD_CHUNK = 64

GLA_HEADS = 4
GLA_DK = 128
GLA_DV = 256
GLA_KEY = 512
GLA_VAL = 1024
GLA_GATE_RANK = 16
GLA_GATE_NORM = 16.0
GLA_CHUNK = 16

RWKV_HEAD = 64
RWKV_HEADS = 16
RWKV_GN_EPS = 64e-5

LANES = 128
SMALL = LANES
IN_PERM = SSD_INNER + SSD_CONV_DIM + 2 * GLA_KEY + 2 * GLA_VAL + SMALL
VMEM_LIMIT = 56 * 1024 * 1024
FF_TILE = 1408
SEQ_BLOCK = 256
STEP_BATCH = 8


def _cp(sem):
    return pltpu.CompilerParams(dimension_semantics=sem, vmem_limit_bytes=VMEM_LIMIT)


def _iota(shape, axis):
    return lax.broadcasted_iota(jnp.int32, shape, axis)


def _onehot(mask):
    return jnp.where(mask, 1.0, 0.0).astype(BF16)


def _dot(a, b):
    return jnp.dot(a, b, preferred_element_type=F32)


def _split3(x):
    hi = x.astype(BF16)
    r = x - hi.astype(F32)
    mid = r.astype(BF16)
    lo = (r - mid.astype(F32)).astype(BF16)
    return hi, mid, lo


def _dot01(x, e):
    hi, mid, lo = _split3(x)
    return _dot(hi, e) + _dot(mid, e) + _dot(lo, e)


def _dot01_l(e, x):
    hi, mid, lo = _split3(x)
    return _dot(e, hi) + _dot(e, mid) + _dot(e, lo)


def _rms(x, g):
    ms = jnp.mean(x * x, axis=-1, keepdims=True)
    return x * lax.rsqrt(ms + EPS) * g


def _silu(x):
    return x * jax.nn.sigmoid(x)


def _softplus(x):
    return jnp.maximum(x, 0.0) + jnp.log1p(jnp.exp(-jnp.abs(x)))


def _head_expand():
    return _onehot((_iota((LANES, D_MODEL), 1) >> 6) == _iota((LANES, D_MODEL), 0))


def _head_reduce():
    return _onehot((_iota((D_MODEL, LANES), 0) >> 6) == _iota((D_MODEL, LANES), 1))


def _ffn_body(x_ref, g_ref, wg_ref, wu_ref, wd_ref, o_ref, xn_ref, acc_ref, *, nff):
    j = pl.program_id(1)

    @pl.when(j == 0)
    def _():
        xn_ref[...] = _rms(x_ref[...], g_ref[...]).astype(BF16)
        acc_ref[...] = jnp.zeros_like(acc_ref)

    xn = xn_ref[...]
    gate = _dot(xn, wg_ref[...])
    up = _dot(xn, wu_ref[...])
    acc_ref[...] += _dot((_silu(gate) * up).astype(BF16), wd_ref[...])

    @pl.when(j == nff - 1)
    def _():
        o_ref[...] = x_ref[...] + 0.5 * acc_ref[...]


def _ffn(x, nrm, wgu, wd, ls, tm):
    T = x.shape[0]
    nff = D_FF // FF_TILE
    return pl.pallas_call(
        functools.partial(_ffn_body, nff=nff),
        grid=(T // tm, nff),
        in_specs=[
            pl.BlockSpec((tm, D_MODEL), lambda i, j: (i, 0)),
            pl.BlockSpec((None, 1, D_MODEL), lambda i, j: (ls, 0, 0)),
            pl.BlockSpec((None, D_MODEL, FF_TILE), lambda i, j: (ls, 0, j)),
            pl.BlockSpec((None, D_MODEL, FF_TILE), lambda i, j: (ls, 0, j + nff)),
            pl.BlockSpec((None, FF_TILE, D_MODEL), lambda i, j: (ls, j, 0)),
        ],
        out_specs=pl.BlockSpec((tm, D_MODEL), lambda i, j: (i, 0)),
        out_shape=jax.ShapeDtypeStruct((T, D_MODEL), F32),
        scratch_shapes=[pltpu.VMEM((tm, D_MODEL), BF16), pltpu.VMEM((tm, D_MODEL), F32)],
        compiler_params=_cp(("parallel", "arbitrary")),
        name="ffn",
    )(x, nrm, wgu, wgu, wd)


def _final_norm_body(x_ref, g_ref, o_ref):
    o_ref[...] = _rms(x_ref[...], g_ref[...])


def _final_norm(x, g, tm):
    T = x.shape[0]
    return pl.pallas_call(
        _final_norm_body,
        grid=(T // tm,),
        in_specs=[pl.BlockSpec((tm, D_MODEL), lambda i: (i, 0)),
                  pl.BlockSpec((1, D_MODEL), lambda i: (0, 0))],
        out_specs=pl.BlockSpec((tm, D_MODEL), lambda i: (i, 0)),
        out_shape=jax.ShapeDtypeStruct((T, D_MODEL), F32),
        compiler_params=_cp(("parallel",)),
        name="final_norm",
    )(x, g)


_EVEN_PIECES = (SSD_INNER, SSD_CONV_DIM, GLA_KEY, GLA_KEY, GLA_VAL, GLA_VAL, SMALL)


def _even_in_body(x_ref, g_ref, w_ref, *out_refs):
    h = _rms(x_ref[...], g_ref[...]).astype(BF16)
    off = 0
    for ref, n in zip(out_refs, _EVEN_PIECES):
        ref[...] = _dot(h, w_ref[:, off:off + n])
        off += n


def _even_in(x, nrm, w, i, tm):
    T = x.shape[0]
    return pl.pallas_call(
        _even_in_body,
        grid=(T // tm,),
        in_specs=[
            pl.BlockSpec((tm, D_MODEL), lambda t: (t, 0)),
            pl.BlockSpec((None, 1, D_MODEL), lambda t: (2 * i, 0, 0)),
            pl.BlockSpec((None, D_MODEL, IN_PERM), lambda t: (i, 0, 0)),
        ],
        out_specs=[pl.BlockSpec((tm, n), lambda t: (t, 0)) for n in _EVEN_PIECES],
        out_shape=[jax.ShapeDtypeStruct((T, n), F32) for n in _EVEN_PIECES],
        compiler_params=_cp(("parallel",)),
        name="even_in",
    )(x, nrm, w)


def _out_proj_body(x_ref, y_ref, o_ref, w_ref, out_ref):
    acc = _dot(y_ref[...].astype(BF16), w_ref[0:SSD_INNER, :])
    acc = acc + _dot(o_ref[...].astype(BF16), w_ref[SSD_INNER:, :])
    out_ref[...] = x_ref[...] + acc


def _out_proj(x, y, o, w, i, tm):
    T = x.shape[0]
    tok = pl.BlockSpec((tm, D_MODEL), lambda t: (t, 0))
    return pl.pallas_call(
        _out_proj_body,
        grid=(T // tm,),
        in_specs=[tok, tok, tok,
                  pl.BlockSpec((None, SSD_INNER + GLA_VAL, D_MODEL), lambda t: (i, 0, 0))],
        out_specs=tok,
        out_shape=jax.ShapeDtypeStruct((T, D_MODEL), F32),
        compiler_params=_cp(("parallel",)),
        name="even_out",
    )(x, y, o, w)


def _ssd_seq_body(xbc_ref, sm_ref, z_ref, cw_ref, cb_ref, dtb_ref, alog_ref, dsk_ref, nw_ref,
                  y_ref, convn_ref, ssmn_ref,
                  cbuf_ref, xact_ref, acx_ref, dtx_ref, hT_ref, *, nblk):
    Lb = SEQ_BLOCK
    C = SSD_CHUNK
    j = pl.program_id(1)

    @pl.when(j == 0)
    def _():
        cbuf_ref[0:8, :] = jnp.zeros((8, SSD_CONV_DIM), F32)
        hT_ref[...] = jnp.zeros_like(hT_ref)

    cbuf_ref[8:8 + Lb, :] = xbc_ref[...]
    acc = cb_ref[...] + cbuf_ref[5:5 + Lb, :] * cw_ref[0:1, :]
    acc = acc + cbuf_ref[6:6 + Lb, :] * cw_ref[1:2, :]
    acc = acc + cbuf_ref[7:7 + Lb, :] * cw_ref[2:3, :]
    acc = acc + cbuf_ref[8:8 + Lb, :] * cw_ref[3:4, :]
    xact_ref[...] = _silu(acc)
    tail = cbuf_ref[Lb:Lb + 8, :]
    cbuf_ref[0:8, :] = tail

    @pl.when(j == nblk - 1)
    def _():
        convn_ref[...] = tail[5:8, :]

    dt = _softplus(sm_ref[...] + dtb_ref[...])
    dta = dt * (-jnp.exp(alog_ref[...]))
    ii = _iota((Lb, Lb), 0)
    jj = _iota((Lb, Lb), 1)
    tril = _onehot(((ii >> 6) == (jj >> 6)) & (ii >= jj))
    acum = _dot01_l(tril, dta)
    expand = _head_expand()
    acx_ref[...] = _dot01(acum, expand)
    dtx_ref[...] = _dot01(dt, expand)

    li = _iota((C, D_MODEL), 0)
    lj = _iota((C, D_MODEL), 1) & (C - 1)
    diag = li == lj
    causal = li >= lj
    ones_c = jnp.ones((C, C), BF16)
    tile8 = _onehot((_iota((C, 8 * C), 1) & (C - 1)) == _iota((C, 8 * C), 0))
    pair_mask = (_iota((LANES, LANES), 0) >> 6) == (_iota((LANES, LANES), 1) >> 6)
    half = SSD_INNER // SSD_GROUPS

    def chunk(c, carry):
        r0 = pl.multiple_of(c * C, C)
        xa = xact_ref[pl.ds(r0, C), :]
        xs = xa[:, :SSD_INNER]
        bms = [xa[:, SSD_INNER + g * SSD_D_STATE:SSD_INNER + (g + 1) * SSD_D_STATE] for g in range(2)]
        cms = [xa[:, SSD_INNER + (2 + g) * SSD_D_STATE:SSD_INNER + (3 + g) * SSD_D_STATE]
               for g in range(2)]
        acx = acx_ref[pl.ds(r0, C), :]
        dtx = dtx_ref[pl.ds(r0, C), :]
        alast = acx[C - 1:C, :]
        arow = _dot01_l(ones_c, jnp.where(diag, acx, 0.0))
        dtrow = _dot01_l(ones_c, jnp.where(diag, dtx, 0.0))
        decay = jnp.exp(jnp.where(causal, acx - arow, -jnp.inf))
        cbt = []
        for g in range(2):
            cb = lax.dot_general(cms[g].astype(BF16), bms[g].astype(BF16),
                                 (((1,), (1,)), ((), ())), preferred_element_type=F32)
            cbt.append(_dot01(cb, tile8))
        wts = (jnp.concatenate(cbt, axis=1) * decay * dtrow).astype(BF16)
        ys = []
        for p in range(SSD_HEADS // 2):
            xp = xs[:, p * LANES:(p + 1) * LANES]
            xbd = jnp.where(pair_mask, jnp.concatenate([xp, xp], axis=0), 0.0).astype(BF16)
            ys.append(_dot(wts[:, p * LANES:(p + 1) * LANES], xbd))
        y = jnp.concatenate(ys, axis=1)
        hT = hT_ref[...]
        hTb = hT.astype(BF16)
        yi = [_dot(cms[g].astype(BF16), hTb[:, g * half:(g + 1) * half]) for g in range(2)]
        y = y + jnp.concatenate(yi, axis=1) * jnp.exp(acx)
        xsc = (jnp.exp(alast - acx) * dtx * xs).astype(BF16)
        st = [lax.dot_general(bms[g].astype(BF16), xsc[:, g * half:(g + 1) * half],
                              (((0,), (0,)), ((), ())), preferred_element_type=F32) for g in range(2)]
        hT_ref[...] = hT * jnp.exp(alast) + jnp.concatenate(st, axis=1)
        y = y + dsk_ref[...] * xs
        y = y * _silu(z_ref[pl.ds(r0, C), :])
        outs = []
        for g in range(2):
            yg = y[:, g * half:(g + 1) * half]
            ms = jnp.mean(yg * yg, axis=-1, keepdims=True)
            outs.append(yg * lax.rsqrt(ms + EPS))
        y_ref[pl.ds(r0, C), :] = jnp.concatenate(outs, axis=1) * nw_ref[...]
        return carry

    lax.fori_loop(0, Lb // C, chunk, 0)

    @pl.when(j == nblk - 1)
    def _():
        ssmn_ref[...] = hT_ref[...].T


def _ssd_seq(xbc, sm, z, P, i, B, L):
    Lb = SEQ_BLOCK
    nblk = L // Lb
    T = B * L
    tok = lambda n: pl.BlockSpec((Lb, n), lambda b, j: (b * nblk + j, 0))
    par = lambda r, n: pl.BlockSpec((None, r, n), lambda b, j: (i, 0, 0))
    y, convn, ssmn = pl.pallas_call(
        functools.partial(_ssd_seq_body, nblk=nblk),
        grid=(B, nblk),
        in_specs=[tok(SSD_CONV_DIM), tok(SMALL), tok(SSD_INNER),
                  par(SSD_CONV, SSD_CONV_DIM), par(1, SSD_CONV_DIM), par(1, SMALL), par(1, SMALL),
                  par(1, SSD_INNER), par(1, SSD_INNER)],
        out_specs=[tok(SSD_INNER),
                   pl.BlockSpec((None, SSD_CONV - 1, SSD_CONV_DIM), lambda b, j: (b, 0, 0)),
                   pl.BlockSpec((None, SSD_INNER, SSD_D_STATE), lambda b, j: (b, 0, 0))],
        out_shape=[jax.ShapeDtypeStruct((T, SSD_INNER), F32),
                   jax.ShapeDtypeStruct((B, SSD_CONV - 1, SSD_CONV_DIM), F32),
                   jax.ShapeDtypeStruct((B, SSD_INNER, SSD_D_STATE), F32)],
        scratch_shapes=[pltpu.VMEM((Lb + 8, SSD_CONV_DIM), F32),
                        pltpu.VMEM((Lb, SSD_CONV_DIM), F32),
                        pltpu.VMEM((Lb, SSD_INNER), F32),
                        pltpu.VMEM((Lb, SSD_INNER), F32),
                        pltpu.VMEM((SSD_D_STATE, SSD_INNER), F32)],
        compiler_params=_cp(("parallel", "arbitrary")),
        name="ssd_seq",
    )(xbc, sm, z, P["conv_w"], P["conv_b"], P["dt_bias"], P["a_log"], P["d_skip_x"], P["ssd_norm"])
    return y, convn, ssmn.reshape(B, SSD_HEADS, SSD_HEAD_DIM, SSD_D_STATE)


def _gla_seq_body(q_ref, k_ref, v_ref, og_ref, sm_ref, w2_ref, gb_ref, nw_ref,
                  o_ref, glan_ref,
                  st_ref, dec_ref, qg_ref, kd_ref, oi_ref, *, nblk):
    Lb = SEQ_BLOCK
    j = pl.program_id(1)

    @pl.when(j == 0)
    def _():
        st_ref[...] = jnp.zeros_like(st_ref)

    gate = _dot(sm_ref[...].astype(BF16), w2_ref[...]) + gb_ref[...]
    lg = -_softplus(-gate) / GLA_GATE_NORM
    ii = _iota((Lb, Lb), 0)
    jj = _iota((Lb, Lb), 1)
    same = (ii >> 4) == (jj >> 4)
    causal = same & (ii >= jj)
    b = _dot01_l(_onehot(causal), lg)
    blast = _dot01_l(_onehot(same), lg)
    qg = q_ref[...] * (GLA_DK ** -0.5) * jnp.exp(b)
    kk = k_ref[...]
    kg = kk * jnp.exp(-b)
    kd = kk * jnp.exp(blast - b)
    dec_ref[...] = jnp.exp(blast)
    qg_ref[...] = qg.astype(BF16)
    kd_ref[...] = kd.astype(BF16)

    def chunk(c, carry):
        r0 = pl.multiple_of(c * GLA_CHUNK, GLA_CHUNK)
        for h in range(GLA_HEADS):
            ks = slice(h * GLA_DK, (h + 1) * GLA_DK)
            vs = slice(h * GLA_DV, (h + 1) * GLA_DV)
            st = st_ref[h]
            oi_ref[pl.ds(r0, GLA_CHUNK), vs] = lax.dot_general(
                qg_ref[pl.ds(r0, GLA_CHUNK), ks], st.astype(BF16),
                (((1,), (1,)), ((), ())), preferred_element_type=F32)
            upd = lax.dot_general(v_ref[pl.ds(r0, GLA_CHUNK), vs].astype(BF16),
                                  kd_ref[pl.ds(r0, GLA_CHUNK), ks],
                                  (((0,), (0,)), ((), ())), preferred_element_type=F32)
            st_ref[h] = st * dec_ref[pl.ds(r0, 8), ks][0:1, :] + upd
        return carry

    lax.fori_loop(0, Lb // GLA_CHUNK, chunk, 0)

    for h in range(GLA_HEADS):
        ks = slice(h * GLA_DK, (h + 1) * GLA_DK)
        vs = slice(h * GLA_DV, (h + 1) * GLA_DV)
        att = lax.dot_general(qg[:, ks].astype(BF16), kg[:, ks].astype(BF16),
                              (((1,), (1,)), ((), ())), preferred_element_type=F32)
        att = jnp.where(causal, att, 0.0)
        o_h = _dot(att.astype(BF16), v_ref[:, vs].astype(BF16)) + oi_ref[:, vs]
        ms = jnp.mean(o_h * o_h, axis=-1, keepdims=True)
        o_h = o_h * lax.rsqrt(ms + EPS) * nw_ref[...]
        o_ref[:, vs] = o_h * _silu(og_ref[:, vs])

    @pl.when(j == nblk - 1)
    def _():
        for h in range(GLA_HEADS):
            glan_ref[h] = st_ref[h].T


def _gla_seq(q, k, v, og, sm, P, i, B, L):
    Lb = SEQ_BLOCK
    nblk = L // Lb
    T = B * L
    tok = lambda n: pl.BlockSpec((Lb, n), lambda b, j: (b * nblk + j, 0))
    par = lambda r, n: pl.BlockSpec((None, r, n), lambda b, j: (i, 0, 0))
    return pl.pallas_call(
        functools.partial(_gla_seq_body, nblk=nblk),
        grid=(B, nblk),
        in_specs=[tok(GLA_KEY), tok(GLA_KEY), tok(GLA_VAL), tok(GLA_VAL), tok(SMALL),
                  par(SMALL, GLA_KEY), par(1, GLA_KEY), par(1, GLA_DV)],
        out_specs=[tok(GLA_VAL),
                   pl.BlockSpec((None, GLA_HEADS, GLA_DK, GLA_DV), lambda b, j: (b, 0, 0, 0))],
        out_shape=[jax.ShapeDtypeStruct((T, GLA_VAL), F32),
                   jax.ShapeDtypeStruct((B, GLA_HEADS, GLA_DK, GLA_DV), F32)],
        scratch_shapes=[pltpu.VMEM((GLA_HEADS, GLA_DV, GLA_DK), F32),
                        pltpu.VMEM((Lb, GLA_KEY), F32),
                        pltpu.VMEM((Lb, GLA_KEY), BF16),
                        pltpu.VMEM((Lb, GLA_KEY), BF16),
                        pltpu.VMEM((Lb, GLA_VAL), F32)],
        compiler_params=_cp(("parallel", "arbitrary")),
        name="gla_seq",
    )(q, k, v, og, sm, P["gate_w2"], P["gate_b"], P["gla_norm"])


def _even_step_pre_body(xbc_ref, c0_ref, sm_ref, cw_ref, cb_ref, dtb_ref, alog_ref, dsk_ref,
                        w2_ref, gb_ref,
                        xact_ref, convn_ref, dax_ref, c2_ref, ybase_ref, eg_ref):
    u = xbc_ref[...]
    n = SSD_CONV_DIM
    acc = cb_ref[...] + c0_ref[:, 0:n] * cw_ref[0:1, :]
    acc = acc + c0_ref[:, n:2 * n] * cw_ref[1:2, :]
    acc = acc + c0_ref[:, 2 * n:3 * n] * cw_ref[2:3, :]
    acc = acc + u * cw_ref[3:4, :]
    xa = _silu(acc)
    xact_ref[...] = xa
    convn_ref[:, 0:n] = c0_ref[:, n:2 * n]
    convn_ref[:, n:2 * n] = c0_ref[:, 2 * n:3 * n]
    convn_ref[:, 2 * n:3 * n] = u
    dt = _softplus(sm_ref[...] + dtb_ref[...])
    dta = dt * (-jnp.exp(alog_ref[...]))
    expand = _head_expand()
    dtx = _dot01(dt, expand)
    dax_ref[...] = jnp.exp(_dot01(dta, expand))
    xs = xa[:, :SSD_INNER]
    c2_ref[...] = dtx * xs
    half = SSD_INNER // SSD_GROUPS
    cbs = []
    for g in range(2):
        bm = xa[:, SSD_INNER + g * SSD_D_STATE:SSD_INNER + (g + 1) * SSD_D_STATE]
        cm = xa[:, SSD_INNER + (2 + g) * SSD_D_STATE:SSD_INNER + (3 + g) * SSD_D_STATE]
        cb = jnp.sum(cm * bm, axis=-1, keepdims=True)
        cbs.append(jnp.broadcast_to(cb, (cb.shape[0], half)))
    ybase_ref[...] = jnp.concatenate(cbs, axis=1) * dtx * xs + dsk_ref[...] * xs
    gate = _dot(sm_ref[...].astype(BF16), w2_ref[...]) + gb_ref[...]
    eg_ref[...] = jnp.exp(-_softplus(-gate) / GLA_GATE_NORM)


def _even_step_pre(xbc, conv0, sm, P, i):
    Bn = xbc.shape[0]
    full = lambda n: pl.BlockSpec((Bn, n), lambda t: (0, 0))
    par = lambda r, n: pl.BlockSpec((None, r, n), lambda t: (i, 0, 0))
    outs = (SSD_CONV_DIM, 3 * SSD_CONV_DIM, SSD_INNER, SSD_INNER, SSD_INNER, GLA_KEY)
    return pl.pallas_call(
        _even_step_pre_body,
        grid=(1,),
        in_specs=[full(SSD_CONV_DIM), full(3 * SSD_CONV_DIM), full(SMALL),
                  par(SSD_CONV, SSD_CONV_DIM), par(1, SSD_CONV_DIM), par(1, SMALL), par(1, SMALL),
                  par(1, SSD_INNER), par(SMALL, GLA_KEY), par(1, GLA_KEY)],
        out_specs=[full(n) for n in outs],
        out_shape=[jax.ShapeDtypeStruct((Bn, n), F32) for n in outs],
        compiler_params=_cp(("arbitrary",)),
        name="even_step_pre",
    )(xbc, conv0, sm, P["conv_w"], P["conv_b"], P["dt_bias"], P["a_log"], P["d_skip_x"],
      P["gate_w2"], P["gate_b"])


def _to_cols(x):
    Bn, N = x.shape
    return x.reshape(Bn // STEP_BATCH, STEP_BATCH, N).transpose(0, 2, 1)


def _from_cols(x):
    nb, N, bb = x.shape
    return x.transpose(0, 2, 1).reshape(nb * bb, N)


def _ssd_step_body(h_ref, c1_ref, c2_ref, bm_ref, cm_ref, hn_ref, y_ref):
    half = SSD_INNER // SSD_GROUPS
    lane = _iota((SSD_INNER, STEP_BATCH), 1)
    ycols = jnp.zeros((SSD_INNER, STEP_BATCH), F32)
    c1 = c1_ref[...]
    c2 = c2_ref[...]
    for b in range(STEP_BATCH):
        h0 = h_ref[b]
        bmat = jnp.concatenate(
            [jnp.broadcast_to(bm_ref[b:b + 1, g * SSD_D_STATE:(g + 1) * SSD_D_STATE],
                              (half, SSD_D_STATE)) for g in range(2)], axis=0)
        cmat = jnp.concatenate(
            [jnp.broadcast_to(cm_ref[b:b + 1, g * SSD_D_STATE:(g + 1) * SSD_D_STATE],
                              (half, SSD_D_STATE)) for g in range(2)], axis=0)
        ycol = jnp.sum(h0 * cmat, axis=1, keepdims=True)
        ycols = jnp.where(lane == b, ycol, ycols)
        hn_ref[b] = h0 * c1[:, b:b + 1] + c2[:, b:b + 1] * bmat
    y_ref[...] = ycols


def _ssd_step(h0, dax, c2, xact):
    Bn = h0.shape[0]
    bb = STEP_BATCH
    h0 = h0.reshape(Bn, SSD_INNER, SSD_D_STATE)
    bm = xact[:, SSD_INNER:SSD_INNER + 2 * SSD_D_STATE]
    cm = xact[:, SSD_INNER + 2 * SSD_D_STATE:]
    col = pl.BlockSpec((None, SSD_INNER, bb), lambda t: (t, 0, 0))
    st = pl.BlockSpec((bb, SSD_INNER, SSD_D_STATE), lambda t: (t, 0, 0))
    row = pl.BlockSpec((bb, 2 * SSD_D_STATE), lambda t: (t, 0))
    hn, ycols = pl.pallas_call(
        _ssd_step_body,
        grid=(Bn // bb,),
        in_specs=[st, col, col, row, row],
        out_specs=[st, col],
        out_shape=[jax.ShapeDtypeStruct((Bn, SSD_INNER, SSD_D_STATE), F32),
                   jax.ShapeDtypeStruct((Bn // bb, SSD_INNER, bb), F32)],
        compiler_params=_cp(("parallel",)),
        name="ssd_step",
    )(h0, _to_cols(dax), _to_cols(c2), bm, cm)
    return hn.reshape(Bn, SSD_HEADS, SSD_HEAD_DIM, SSD_D_STATE), _from_cols(ycols)


def _gla_step_body(s_ref, eg_ref, kc_ref, qc_ref, v_ref, sn_ref, o_ref):
    eg = eg_ref[...]
    kc = kc_ref[...]
    qc = qc_ref[...] * (GLA_DK ** -0.5)
    for b in range(STEP_BATCH):
        s0 = s_ref[b]
        vmat = jnp.concatenate(
            [jnp.broadcast_to(v_ref[b:b + 1, h * GLA_DV:(h + 1) * GLA_DV], (GLA_DK, GLA_DV))
             for h in range(GLA_HEADS)], axis=0)
        sn = s0 * eg[:, b:b + 1] + kc[:, b:b + 1] * vmat
        sn_ref[b] = sn
        t = sn * qc[:, b:b + 1]
        o_ref[b:b + 1, :] = jnp.concatenate(
            [jnp.sum(t[h * GLA_DK:(h + 1) * GLA_DK], axis=0, keepdims=True)
             for h in range(GLA_HEADS)], axis=1)


def _gla_step(s0, eg, k, q, v):
    Bn = s0.shape[0]
    bb = STEP_BATCH
    s0 = s0.reshape(Bn, GLA_KEY, GLA_DV)
    col = pl.BlockSpec((None, GLA_KEY, bb), lambda t: (t, 0, 0))
    st = pl.BlockSpec((bb, GLA_KEY, GLA_DV), lambda t: (t, 0, 0))
    row = pl.BlockSpec((bb, GLA_VAL), lambda t: (t, 0))
    sn, o = pl.pallas_call(
        _gla_step_body,
        grid=(Bn // bb,),
        in_specs=[st, col, col, col, row],
        out_specs=[st, row],
        out_shape=[jax.ShapeDtypeStruct((Bn, GLA_KEY, GLA_DV), F32),
                   jax.ShapeDtypeStruct((Bn, GLA_VAL), F32)],
        compiler_params=_cp(("parallel",)),
        name="gla_step",
    )(s0, _to_cols(eg), _to_cols(k), _to_cols(q), v)
    return sn.reshape(Bn, GLA_HEADS, GLA_DK, GLA_DV), o


def _even_step_post_body(x_ref, ybase_ref, yi_ref, dax_ref, z_ref, o_ref, og_ref,
                         snw_ref, gnw_ref, w_ref, out_ref):
    y = (ybase_ref[...] + yi_ref[...] * dax_ref[...]) * _silu(z_ref[...])
    half = SSD_INNER // SSD_GROUPS
    acc = None
    for g in range(2):
        yg = y[:, g * half:(g + 1) * half]
        ms = jnp.mean(yg * yg, axis=-1, keepdims=True)
        yn = yg * lax.rsqrt(ms + EPS) * snw_ref[:, g * half:(g + 1) * half]
        part = _dot(yn.astype(BF16), w_ref[g * half:(g + 1) * half, :])
        acc = part if acc is None else acc + part
    for h in range(GLA_HEADS):
        vs = slice(h * GLA_DV, (h + 1) * GLA_DV)
        oh = o_ref[:, vs]
        ms = jnp.mean(oh * oh, axis=-1, keepdims=True)
        on = oh * lax.rsqrt(ms + EPS) * gnw_ref[...] * _silu(og_ref[:, vs])
        acc = acc + _dot(on.astype(BF16), w_ref[SSD_INNER + h * GLA_DV:SSD_INNER + (h + 1) * GLA_DV, :])
    out_ref[...] = x_ref[...] + acc


def _even_step_post(x, ybase, yi, dax, z, o, og, P, w_out, i):
    Bn = x.shape[0]
    full = pl.BlockSpec((Bn, D_MODEL), lambda t: (0, 0))
    return pl.pallas_call(
        _even_step_post_body,
        grid=(1,),
        in_specs=[full] * 7 + [pl.BlockSpec((None, 1, SSD_INNER), lambda t: (i, 0, 0)),
                               pl.BlockSpec((None, 1, GLA_DV), lambda t: (i, 0, 0)),
                               pl.BlockSpec((None, SSD_INNER + GLA_VAL, D_MODEL), lambda t: (i, 0, 0))],
        out_specs=full,
        out_shape=jax.ShapeDtypeStruct((Bn, D_MODEL), F32),
        compiler_params=_cp(("arbitrary",)),
        name="even_step_post",
    )(x, ybase, yi, dax, z, o, og, P["ssd_norm"], P["gla_norm"], w_out)


_ODD_OUTS = 8


def _odd_pre_body(*refs, step, has_vfirst, nblk_seq, tm):
    it = iter(refs)
    x_ref = next(it)
    if step:
        prev_ref = next(it)
    else:
        xp_ref = next(it)
        sh0_ref = next(it)
    g_ref, mix_ref = next(it), next(it)
    wr_ref, wk_ref, wv_ref = next(it), next(it), next(it)
    w0_ref, w1_ref, w2_ref = next(it), next(it), next(it)
    a0_ref, a1_ref, a2_ref = next(it), next(it), next(it)
    if has_vfirst:
        vf_ref, v0_ref, v1_ref, v2_ref = next(it), next(it), next(it), next(it)
    g1_ref, g2_ref = next(it), next(it)
    kkw_ref, kaw_ref = next(it), next(it)
    h_ref, r_ref, dec_ref, k_ref, v_ref, an_ref, b_ref, gg_ref = (next(it) for _ in range(_ODD_OUTS))

    h = _rms(x_ref[...], g_ref[...])
    if step:
        prev = prev_ref[...]
    else:
        i = pl.program_id(0)
        hp = _rms(xp_ref[...], g_ref[...])[7:8, :]
        prow = jnp.where(i % nblk_seq == 0, sh0_ref[...], hp)
        prev = jnp.where(_iota((tm, D_MODEL), 0) == 0, prow, pltpu.roll(h, 1, axis=0))
    h_ref[...] = h
    xx = prev - h
    mixed = lambda r: (h + xx * mix_ref[r:r + 1, :]).astype(BF16)
    xr, xw, xk, xv, xa, xg = (mixed(r) for r in range(6))
    r_ref[...] = _dot(xr, wr_ref[...])
    wl = w0_ref[...] + _dot(jnp.tanh(_dot(xw, w1_ref[...])).astype(BF16), w2_ref[...])
    w = -_softplus(-wl) - 0.5
    dec_ref[...] = jnp.exp(-jnp.exp(w))
    k = _dot(xk, wk_ref[...])
    v = _dot(xv, wv_ref[...])
    if has_vfirst:
        gate = jax.nn.sigmoid(v0_ref[...] + _dot(_dot(xv, v1_ref[...]).astype(BF16), v2_ref[...]))
        v = v + (vf_ref[...] - v) * gate
    v_ref[...] = v
    a = jax.nn.sigmoid(a0_ref[...] + _dot(_dot(xa, a1_ref[...]).astype(BF16), a2_ref[...]))
    gg_ref[...] = _dot(jax.nn.sigmoid(_dot(xg, g1_ref[...])).astype(BF16), g2_ref[...])
    kkf = k * kkw_ref[...]
    ss = _dot01(kkf * kkf, _head_reduce())
    kk = kkf * _dot01(lax.rsqrt(jnp.maximum(ss, 1e-24)), _head_expand())
    k_ref[...] = k * (1.0 + (a - 1.0) * kaw_ref[...])
    an_ref[...] = -kk
    b_ref[...] = kk * a


def _odd_pre(x, prev_or_shift0, vfirst, P, i, tm, L):
    T = x.shape[0]
    step = L == 1
    has_vfirst = vfirst is not None
    nblk_seq = max(L // tm, 1)
    tok = pl.BlockSpec((tm, D_MODEL), lambda t: (t, 0))
    par = lambda r, n, idx=i: pl.BlockSpec((None, r, n), lambda t: (idx, 0, 0))
    args = [x]
    specs = [tok]
    if step:
        args.append(prev_or_shift0)
        specs.append(tok)
    else:
        args += [x, prev_or_shift0]
        specs += [pl.BlockSpec((8, D_MODEL), lambda t: (jnp.maximum(t * (tm // 8) - 1, 0), 0)),
                  pl.BlockSpec((None, 1, D_MODEL), lambda t: (t // nblk_seq, 0, 0))]
    args += [P["mix_norm"], P["mix"], P["w_r"], P["w_k"], P["w_v"], P["w0"], P["w1"], P["w2"],
             P["a0"], P["a1"], P["a2"]]
    specs += [pl.BlockSpec((None, 1, D_MODEL), lambda t: (2 * i + 1, 0, 0)),
              par(8, D_MODEL), par(D_MODEL, D_MODEL), par(D_MODEL, D_MODEL), par(D_MODEL, D_MODEL),
              par(1, D_MODEL), par(D_MODEL, LANES), par(LANES, D_MODEL),
              par(1, D_MODEL), par(D_MODEL, LANES), par(LANES, D_MODEL)]
    if has_vfirst:
        args += [vfirst, P["v0"], P["v1"], P["v2"]]
        specs += [tok, par(1, D_MODEL, i - 1), par(D_MODEL, LANES, i - 1), par(LANES, D_MODEL, i - 1)]
    args += [P["g1"], P["g2"], P["k_k"], P["k_a"]]
    specs += [par(D_MODEL, 2 * LANES), par(2 * LANES, D_MODEL), par(1, D_MODEL), par(1, D_MODEL)]
    return pl.pallas_call(
        functools.partial(_odd_pre_body, step=step, has_vfirst=has_vfirst, nblk_seq=nblk_seq, tm=tm),
        grid=(T // tm,),
        in_specs=specs,
        out_specs=[tok] * _ODD_OUTS,
        out_shape=[jax.ShapeDtypeStruct((T, D_MODEL), F32)] * _ODD_OUTS,
        compiler_params=_cp(("parallel",)),
        name="odd_pre",
    )(*args)


def _odd_post_body(x_ref, o_ref, r_ref, k_ref, v_ref, g_ref, gnw_ref, gnb_ref, rk_ref, wo_ref, out_ref):
    red = _head_reduce()
    expand = _head_expand()
    o = o_ref[...]
    mu = _dot01(_dot01(o, red) * (1.0 / RWKV_HEAD), expand)
    d = o - mu
    var = _dot01(d * d, red) * (1.0 / RWKV_HEAD)
    on = d * _dot01(lax.rsqrt(var + RWKV_GN_EPS), expand) * gnw_ref[...] + gnb_ref[...]
    v = v_ref[...]
    bonus = _dot01(_dot01(r_ref[...] * k_ref[...] * rk_ref[...], red), expand) * v
    out_ref[...] = x_ref[...] + _dot(((on + bonus) * g_ref[...]).astype(BF16), wo_ref[...])


def _odd_post(x, o, r, k, v, g, P, i, tm):
    T = x.shape[0]
    tok = pl.BlockSpec((tm, D_MODEL), lambda t: (t, 0))
    par = lambda r_, n: pl.BlockSpec((None, r_, n), lambda t: (i, 0, 0))
    return pl.pallas_call(
        _odd_post_body,
        grid=(T // tm,),
        in_specs=[tok] * 6 + [par(1, D_MODEL), par(1, D_MODEL), par(1, D_MODEL), par(D_MODEL, D_MODEL)],
        out_specs=tok,
        out_shape=jax.ShapeDtypeStruct((T, D_MODEL), F32),
        compiler_params=_cp(("parallel",)),
        name="odd_post",
    )(x, o, r, k, v, g, P["gn_w"], P["gn_b"], P["r_k"], P["w_o"])


RWKV_TB = 64
RWKV_NPAIR = RWKV_HEADS // 2


RWKV_SEQS = 2
RWKV_UNROLL = 4


def _rwkv_seq_body(w_ref, k_ref, a_ref, b_ref, r_ref, v_ref, o_ref, sn_ref, s_ref, vt_ref, oacc_ref, *, ntb):
    N = RWKV_HEAD
    R = RWKV_NPAIR * N
    tbi = pl.program_id(1)

    @pl.when(tbi == 0)
    def _():
        s_ref[...] = jnp.zeros_like(s_ref)

    same_head = _onehot((_iota((LANES, LANES), 0) >> 6) == (_iota((LANES, LANES), 1) >> 6))
    seg = jnp.concatenate([same_head, same_head], axis=0)
    for c in range(RWKV_SEQS):
        for p in range(RWKV_NPAIR):
            bt = v_ref[c, :, p * LANES:(p + 1) * LANES].T
            vt_ref[c, p * N:(p + 1) * N, :] = jnp.concatenate([bt[0:N], bt[N:2 * N]], axis=1)
    oacc_ref[...] = jnp.zeros_like(oacc_ref)
    lane_step = _iota((R, LANES), 1) & (N - 1)
    lane_base = (_iota((R, LANES), 1) >> 6) * N

    def rows(ref, c, t):
        return jnp.concatenate(
            [jnp.broadcast_to(ref[c, t, p:p + 1, :], (N, LANES)) for p in range(RWKV_NPAIR)], axis=0)

    def steps(tu, carry):
        for j in range(RWKV_UNROLL):
            t = tu * RWKV_UNROLL + j
            for c in range(RWKV_SEQS):
                s = s_ref[c]
                x = s * rows(a_ref, c, t)
                hi = x.astype(BF16)
                mid = (x - hi.astype(F32)).astype(BF16)
                sa = _dot(jnp.concatenate([hi, mid], axis=1), seg)
                vcol = jnp.take_along_axis(vt_ref[c], lane_base + t, axis=1)
                s = s * rows(w_ref, c, t) + sa * rows(b_ref, c, t) + vcol * rows(k_ref, c, t)
                s_ref[c] = s
                y = s * rows(r_ref, c, t)
                yh = y.astype(BF16)
                ym = (y - yh.astype(F32)).astype(BF16)
                ob = _dot(jnp.concatenate([yh, ym], axis=1), seg)
                oacc_ref[c] = jnp.where(lane_step == t, ob, oacc_ref[c])
        return carry

    lax.fori_loop(0, RWKV_TB // RWKV_UNROLL, steps, 0)

    for c in range(RWKV_SEQS):
        for p in range(RWKV_NPAIR):
            ot = oacc_ref[c, p * N:(p + 1) * N, :].T
            o_ref[c, :, p * LANES:(p + 1) * LANES] = jnp.concatenate([ot[0:N], ot[N:2 * N]], axis=1)

    @pl.when(tbi == ntb - 1)
    def _():
        for c in range(RWKV_SEQS):
            for p in range(RWKV_NPAIR):
                for hh in range(2):
                    sn_ref[c, (2 * p + hh) * N:(2 * p + hh + 1) * N, :] = (
                        s_ref[c, p * N:(p + 1) * N, hh * N:(hh + 1) * N])


def _rwkv_seq(w, k, a, b, r, v, B, L):
    ntb = L // RWKV_TB
    nc = RWKV_SEQS
    as_rows = lambda x: x.reshape(B, L, RWKV_NPAIR, LANES)
    rowspec = pl.BlockSpec((nc, RWKV_TB, RWKV_NPAIR, LANES), lambda bi, t: (bi, t, 0, 0))
    tok = pl.BlockSpec((nc, RWKV_TB, D_MODEL), lambda bi, t: (bi, t, 0))
    state_rows = RWKV_NPAIR * RWKV_HEAD
    o, sn = pl.pallas_call(
        functools.partial(_rwkv_seq_body, ntb=ntb),
        grid=(B // nc, ntb),
        in_specs=[rowspec] * 5 + [tok],
        out_specs=[tok, pl.BlockSpec((nc, D_MODEL, RWKV_HEAD), lambda bi, t: (bi, 0, 0))],
        out_shape=[jax.ShapeDtypeStruct((B, L, D_MODEL), F32),
                   jax.ShapeDtypeStruct((B, D_MODEL, RWKV_HEAD), F32)],
        scratch_shapes=[pltpu.VMEM((nc, state_rows, LANES), F32),
                        pltpu.VMEM((nc, state_rows, LANES), F32),
                        pltpu.VMEM((nc, state_rows, LANES), F32)],
        compiler_params=_cp(("parallel", "arbitrary")),
        name="rwkv_seq",
    )(as_rows(w), as_rows(k), as_rows(a), as_rows(b), as_rows(r), v.reshape(B, L, D_MODEL))
    return o.reshape(B * L, D_MODEL), sn.reshape(B, RWKV_HEADS, RWKV_HEAD, RWKV_HEAD)


def _rwkv_step_body(s_ref, w_ref, k_ref, a_ref, b_ref, r_ref, vc_ref, sn_ref, o_ref):
    N = RWKV_HEAD
    lane = _iota((D_MODEL, STEP_BATCH), 1)
    ocols = jnp.zeros((D_MODEL, STEP_BATCH), F32)
    vc = vc_ref[...]

    def rows(ref, b):
        return jnp.concatenate(
            [jnp.broadcast_to(ref[b:b + 1, h * N:(h + 1) * N], (N, N)) for h in range(RWKV_HEADS)], axis=0)

    for b in range(STEP_BATCH):
        s = s_ref[b]
        sa = jnp.sum(s * rows(a_ref, b), axis=1, keepdims=True)
        s = s * rows(w_ref, b) + sa * rows(b_ref, b) + vc[:, b:b + 1] * rows(k_ref, b)
        sn_ref[b] = s
        ocol = jnp.sum(s * rows(r_ref, b), axis=1, keepdims=True)
        ocols = jnp.where(lane == b, ocol, ocols)
    o_ref[...] = ocols


def _rwkv_step(s0, w, k, a, b, r, v):
    Bn = s0.shape[0]
    bb = STEP_BATCH
    s0 = s0.reshape(Bn, D_MODEL, RWKV_HEAD)
    col = pl.BlockSpec((None, D_MODEL, bb), lambda t: (t, 0, 0))
    st = pl.BlockSpec((bb, D_MODEL, RWKV_HEAD), lambda t: (t, 0, 0))
    row = pl.BlockSpec((bb, D_MODEL), lambda t: (t, 0))
    sn, ocols = pl.pallas_call(
        _rwkv_step_body,
        grid=(Bn // bb,),
        in_specs=[st, row, row, row, row, row, col],
        out_specs=[st, col],
        out_shape=[jax.ShapeDtypeStruct((Bn, D_MODEL, RWKV_HEAD), F32),
                   jax.ShapeDtypeStruct((Bn // bb, D_MODEL, bb), F32)],
        compiler_params=_cp(("parallel",)),
        name="rwkv_step",
    )(s0, w, k, a, b, r, _to_cols(v))
    return _from_cols(ocols), sn.reshape(Bn, RWKV_HEADS, RWKV_HEAD, RWKV_HEAD)


def _pad_to(w, axis, n):
    pad = [(0, 0)] * w.ndim
    pad[axis] = (0, n - w.shape[axis])
    return jnp.pad(w, pad)


def _prepare(W):
    bf = lambda w: w.astype(BF16)
    row = lambda w: w[:, None, :]
    sizes = [SSD_INNER, SSD_CONV_DIM, SSD_HEADS, GLA_KEY, GLA_KEY, GLA_VAL, GLA_VAL, GLA_GATE_RANK]
    offs = [0]
    for s in sizes:
        offs.append(offs[-1] + s)
    piece = lambda n: W["ev_w_in"][:, :, offs[n]:offs[n + 1]]
    w_in = jnp.concatenate([piece(0), piece(1), piece(3), piece(4), piece(5), piece(6),
                            _pad_to(jnp.concatenate([piece(2), piece(7)], axis=-1), 2, SMALL)], axis=-1)
    gate_w2 = jnp.pad(W["ev_gla_gate_w2"], ((0, 0), (SSD_HEADS, SMALL - SSD_HEADS - GLA_GATE_RANK), (0, 0)))
    even = dict(
        w_in=bf(w_in), w_out=bf(W["ev_w_out"]),
        conv_w=W["ev_conv_w"], conv_b=row(W["ev_conv_b"]),
        dt_bias=row(_pad_to(W["ev_dt_bias"], 1, SMALL)), a_log=row(_pad_to(W["ev_a_log"], 1, SMALL)),
        d_skip_x=row(jnp.repeat(W["ev_d_skip"], SSD_HEAD_DIM, axis=1)),
        ssd_norm=row(W["ev_ssd_norm"]), gate_w2=bf(gate_w2), gate_b=row(W["ev_gla_gate_b"]),
        gla_norm=row(W["ev_gla_norm"]),
    )
    odd = dict(
        mix_norm=row(W["mix_norm"]), mix=_pad_to(W["od_mix"], 1, 8),
        w_r=bf(W["od_w_r"]), w_k=bf(W["od_w_k"]), w_v=bf(W["od_w_v"]), w_o=bf(W["od_w_o"]),
        w0=row(W["od_w0"]), w1=bf(_pad_to(W["od_w1"], 2, LANES)), w2=bf(_pad_to(W["od_w2"], 1, LANES)),
        a0=row(W["od_a0"]), a1=bf(_pad_to(W["od_a1"], 2, LANES)), a2=bf(_pad_to(W["od_a2"], 1, LANES)),
        v0=row(W["od_v0"]), v1=bf(_pad_to(W["od_v1"], 2, LANES)), v2=bf(_pad_to(W["od_v2"], 1, LANES)),
        g1=bf(_pad_to(W["od_g1"], 2, 2 * LANES)), g2=bf(_pad_to(W["od_g2"], 1, 2 * LANES)),
        k_k=row(W["od_k_k"]), k_a=row(W["od_k_a"]), r_k=row(W["od_r_k"]),
        gn_w=row(W["od_gn_w"]), gn_b=row(W["od_gn_b"]),
    )
    ffn = dict(
        norm=W["ffn_norm"].reshape(DEPTH * 2, 1, D_MODEL),
        w_gu=bf(W["ffn_w_gu"]).reshape(DEPTH * 2, D_MODEL, 2 * D_FF),
        w_down=bf(W["ffn_w_down"]).reshape(DEPTH * 2, D_FF, D_MODEL),
    )
    return dict(even=even, odd=odd, ffn=ffn, mix_norm=row(W["mix_norm"]),
                final_norm=W["final_norm"][None, :])


def _run_group(x3, states, Wp):
    B, L, _ = x3.shape
    T = B * L
    step = L == 1
    x = x3.reshape(T, D_MODEL)
    tm_ffn = min(512, T)
    tm_proj = min(256, T)
    ffn, even, odd = Wp["ffn"], Wp["even"], Wp["odd"]
    convs, ssms, glas, shifts, wkvs = [], [], [], [], []
    v_first = None
    for layer in range(DEPTH):
        i = layer // 2
        x = _ffn(x, ffn["norm"], ffn["w_gu"], ffn["w_down"], 2 * layer, tm_ffn)
        if layer % 2 == 0:
            z, xbc, q, k, v, og, sm = _even_in(x, Wp["mix_norm"], even["w_in"], i, tm_proj)
            if step:
                conv0, ssm0, gla0 = states[0][i], states[1][i], states[2][i]
                xact, convn, dax, c2, ybase, eg = _even_step_pre(
                    xbc, conv0.reshape(B, 3 * SSD_CONV_DIM), sm, even, i)
                ssmn, yi = _ssd_step(ssm0, dax, c2, xact)
                glan, o = _gla_step(gla0, eg, k, q, v)
                x = _even_step_post(x, ybase, yi, dax, z, o, og, even, even["w_out"], i)
                convn = convn.reshape(B, SSD_CONV - 1, SSD_CONV_DIM)
            else:
                y, convn, ssmn = _ssd_seq(xbc, sm, z, even, i, B, L)
                o, glan = _gla_seq(q, k, v, og, sm, even, i, B, L)
                x = _out_proj(x, y, o, even["w_out"], i, tm_ffn)
            convs.append(convn)
            ssms.append(ssmn)
            glas.append(glan)
        else:
            if step:
                prev, wkv0 = states[3][i], states[4][i]
            else:
                prev = jnp.zeros((B, 1, D_MODEL), F32)
            h, r, dec, k, v, an, bb, gg = _odd_pre(x, prev, v_first, odd, i, tm_proj, L)
            if v_first is None:
                v_first = v
            if step:
                o, wkvn = _rwkv_step(wkv0, dec, k, an, bb, r, v)
            else:
                o, wkvn = _rwkv_seq(dec, k, an, bb, r, v, B, L)
            x = _odd_post(x, o, r, k, v, gg, odd, i, tm_ffn)
            shifts.append(h.reshape(B, L, D_MODEL)[:, -1])
            wkvs.append(wkvn)
        x = _ffn(x, ffn["norm"], ffn["w_gu"], ffn["w_down"], 2 * layer + 1, tm_ffn)
    y = _final_norm(x, Wp["final_norm"], tm_ffn).reshape(B, L, D_MODEL)
    return y, jnp.stack(convs), jnp.stack(ssms), jnp.stack(glas), jnp.stack(shifts), jnp.stack(wkvs)


def kernel(x_prompt, x_sample, state_conv, state_ssm, state_gla, state_shift, state_wkv, ffn_norm, ffn_w_gu, ffn_w_down, mix_norm, final_norm, ev_w_in, ev_conv_w, ev_conv_b, ev_dt_bias, ev_a_log, ev_d_skip, ev_ssd_norm, ev_gla_gate_w2, ev_gla_gate_b, ev_gla_norm, ev_w_out, od_mix, od_w0, od_w1, od_w2, od_a0, od_a1, od_a2, od_v0, od_v1, od_v2, od_g1, od_g2, od_k_k, od_k_a, od_r_k, od_w_r, od_w_k, od_w_v, od_w_o, od_gn_w, od_gn_b):
    W = dict(ffn_norm=ffn_norm, ffn_w_gu=ffn_w_gu, ffn_w_down=ffn_w_down, mix_norm=mix_norm,
             final_norm=final_norm, ev_w_in=ev_w_in, ev_conv_w=ev_conv_w, ev_conv_b=ev_conv_b,
             ev_dt_bias=ev_dt_bias, ev_a_log=ev_a_log, ev_d_skip=ev_d_skip, ev_ssd_norm=ev_ssd_norm,
             ev_gla_gate_w2=ev_gla_gate_w2, ev_gla_gate_b=ev_gla_gate_b, ev_gla_norm=ev_gla_norm,
             ev_w_out=ev_w_out, od_mix=od_mix, od_w0=od_w0, od_w1=od_w1, od_w2=od_w2,
             od_a0=od_a0, od_a1=od_a1, od_a2=od_a2, od_v0=od_v0, od_v1=od_v1, od_v2=od_v2,
             od_g1=od_g1, od_g2=od_g2, od_k_k=od_k_k, od_k_a=od_k_a, od_r_k=od_r_k,
             od_w_r=od_w_r, od_w_k=od_w_k, od_w_v=od_w_v, od_w_o=od_w_o,
             od_gn_w=od_gn_w, od_gn_b=od_gn_b)
    Wp = _prepare(W)
    prompt = _run_group(x_prompt, None, Wp)
    sample = _run_group(x_sample, (state_conv, state_ssm, state_gla, state_shift, state_wkv), Wp)
    return (prompt[0], sample[0]) + prompt[1:] + sample[1:]
```

```python
import functools

import jax
import jax.numpy as jnp
from jax import lax
from jax.experimental import pallas as pl
from jax.experimental.pallas import tpu as pltpu

F32 = jnp.float32
BF16 = jnp.bfloat16

D_MODEL = 1024
DEPTH = 4
N_EVEN = 2
N_ODD = 2
EPS = 1e-5
D_FF = 2816

SSD_HEADS = 16
SSD_HEAD_DIM = 64
SSD_INNER = 1024
SSD_GROUPS = 2
SSD_D_STATE = 128
SSD_CONV = 4
SSD_CONV_DIM = 1536
SSD_CHUNK = 64

GLA_HEADS = 4
GLA_DK = 128
GLA_DV = 256
GLA_KEY = 512
GLA_VAL = 1024
GLA_GATE_RANK = 16
GLA_GATE_NORM = 16.0
GLA_CHUNK = 16

RWKV_HEAD = 64
RWKV_HEADS = 16
RWKV_GN_EPS = 64e-5

LANES = 128
SMALL = LANES
IN_PERM = SSD_INNER + SSD_CONV_DIM + 2 * GLA_KEY + 2 * GLA_VAL + SMALL
VMEM_LIMIT = 56 * 1024 * 1024
FF_TILE = 1408
SEQ_BLOCK = 256
STEP_BATCH = 8


def _cp(sem):
    return pltpu.CompilerParams(dimension_semantics=sem, vmem_limit_bytes=VMEM_LIMIT)


def _iota(shape, axis):
    return lax.broadcasted_iota(jnp.int32, shape, axis)


def _onehot(mask):
    return jnp.where(mask, 1.0, 0.0).astype(BF16)


def _dot(a, b):
    return jnp.dot(a, b, preferred_element_type=F32)


def _split3(x):
    hi = x.astype(BF16)
    r = x - hi.astype(F32)
    mid = r.astype(BF16)
    lo = (r - mid.astype(F32)).astype(BF16)
    return hi, mid, lo


def _dot01(x, e):
    hi, mid, lo = _split3(x)
    return _dot(hi, e) + _dot(mid, e) + _dot(lo, e)


def _dot01_l(e, x):
    hi, mid, lo = _split3(x)
    return _dot(e, hi) + _dot(e, mid) + _dot(e, lo)


def _rms(x, g):
    ms = jnp.mean(x * x, axis=-1, keepdims=True)
    return x * lax.rsqrt(ms + EPS) * g


def _silu(x):
    return x * jax.nn.sigmoid(x)


def _softplus(x):
    return jnp.maximum(x, 0.0) + jnp.log1p(jnp.exp(-jnp.abs(x)))


def _head_expand():
    return _onehot((_iota((LANES, D_MODEL), 1) >> 6) == _iota((LANES, D_MODEL), 0))


def _head_reduce():
    return _onehot((_iota((D_MODEL, LANES), 0) >> 6) == _iota((D_MODEL, LANES), 1))


def _ffn_body(x_ref, g_ref, wg_ref, wu_ref, wd_ref, o_ref, xn_ref, acc_ref, *, nff):
    j = pl.program_id(1)

    @pl.when(j == 0)
    def _():
        xn_ref[...] = _rms(x_ref[...], g_ref[...]).astype(BF16)
        acc_ref[...] = jnp.zeros_like(acc_ref)

    xn = xn_ref[...]
    gate = _dot(xn, wg_ref[...])
    up = _dot(xn, wu_ref[...])
    acc_ref[...] += _dot((_silu(gate) * up).astype(BF16), wd_ref[...])

    @pl.when(j == nff - 1)
    def _():
        o_ref[...] = x_ref[...] + 0.5 * acc_ref[...]


def _ffn(x, nrm, wgu, wd, ls, tm):
    T = x.shape[0]
    nff = D_FF // FF_TILE
    return pl.pallas_call(
        functools.partial(_ffn_body, nff=nff),
        grid=(T // tm, nff),
        in_specs=[
            pl.BlockSpec((tm, D_MODEL), lambda i, j: (i, 0)),
            pl.BlockSpec((None, 1, D_MODEL), lambda i, j: (ls, 0, 0)),
            pl.BlockSpec((None, D_MODEL, FF_TILE), lambda i, j: (ls, 0, j)),
            pl.BlockSpec((None, D_MODEL, FF_TILE), lambda i, j: (ls, 0, j + nff)),
            pl.BlockSpec((None, FF_TILE, D_MODEL), lambda i, j: (ls, j, 0)),
        ],
        out_specs=pl.BlockSpec((tm, D_MODEL), lambda i, j: (i, 0)),
        out_shape=jax.ShapeDtypeStruct((T, D_MODEL), F32),
        scratch_shapes=[pltpu.VMEM((tm, D_MODEL), BF16), pltpu.VMEM((tm, D_MODEL), F32)],
        compiler_params=_cp(("parallel", "arbitrary")),
        name="ffn",
    )(x, nrm, wgu, wgu, wd)


def _final_norm_body(x_ref, g_ref, o_ref):
    o_ref[...] = _rms(x_ref[...], g_ref[...])


def _final_norm(x, g, tm):
    T = x.shape[0]
    return pl.pallas_call(
        _final_norm_body,
        grid=(T // tm,),
        in_specs=[pl.BlockSpec((tm, D_MODEL), lambda i: (i, 0)),
                  pl.BlockSpec((1, D_MODEL), lambda i: (0, 0))],
        out_specs=pl.BlockSpec((tm, D_MODEL), lambda i: (i, 0)),
        out_shape=jax.ShapeDtypeStruct((T, D_MODEL), F32),
        compiler_params=_cp(("parallel",)),
        name="final_norm",
    )(x, g)


_EVEN_PIECES = (SSD_INNER, SSD_CONV_DIM, GLA_KEY, GLA_KEY, GLA_VAL, GLA_VAL, SMALL)


def _even_in_body(x_ref, g_ref, w_ref, *out_refs):
    h = _rms(x_ref[...], g_ref[...]).astype(BF16)
    off = 0
    for ref, n in zip(out_refs, _EVEN_PIECES):
        ref[...] = _dot(h, w_ref[:, off:off + n])
        off += n


def _even_in(x, nrm, w, i, tm):
    T = x.shape[0]
    return pl.pallas_call(
        _even_in_body,
        grid=(T // tm,),
        in_specs=[
            pl.BlockSpec((tm, D_MODEL), lambda t: (t, 0)),
            pl.BlockSpec((None, 1, D_MODEL), lambda t: (2 * i, 0, 0)),
            pl.BlockSpec((None, D_MODEL, IN_PERM), lambda t: (i, 0, 0)),
        ],
        out_specs=[pl.BlockSpec((tm, n), lambda t: (t, 0)) for n in _EVEN_PIECES],
        out_shape=[jax.ShapeDtypeStruct((T, n), F32) for n in _EVEN_PIECES],
        compiler_params=_cp(("parallel",)),
        name="even_in",
    )(x, nrm, w)


def _out_proj_body(x_ref, y_ref, o_ref, w_ref, out_ref):
    acc = _dot(y_ref[...].astype(BF16), w_ref[0:SSD_INNER, :])
    acc = acc + _dot(o_ref[...].astype(BF16), w_ref[SSD_INNER:, :])
    out_ref[...] = x_ref[...] + acc


def _out_proj(x, y, o, w, i, tm):
    T = x.shape[0]
    tok = pl.BlockSpec((tm, D_MODEL), lambda t: (t, 0))
    return pl.pallas_call(
        _out_proj_body,
        grid=(T // tm,),
        in_specs=[tok, tok, tok,
                  pl.BlockSpec((None, SSD_INNER + GLA_VAL, D_MODEL), lambda t: (i, 0, 0))],
        out_specs=tok,
        out_shape=jax.ShapeDtypeStruct((T, D_MODEL), F32),
        compiler_params=_cp(("parallel",)),
        name="even_out",
    )(x, y, o, w)


def _ssd_seq_body(xbc_ref, sm_ref, z_ref, cw_ref, cb_ref, dtb_ref, alog_ref, dsk_ref, nw_ref,
                  y_ref, convn_ref, ssmn_ref,
                  cbuf_ref, xact_ref, acx_ref, dtx_ref, hT_ref, *, nblk):
    Lb = SEQ_BLOCK
    C = SSD_CHUNK
    j = pl.program_id(1)

    @pl.when(j == 0)
    def _():
        cbuf_ref[0:8, :] = jnp.zeros((8, SSD_CONV_DIM), F32)
        hT_ref[...] = jnp.zeros_like(hT_ref)

    cbuf_ref[8:8 + Lb, :] = xbc_ref[...]
    acc = cb_ref[...] + cbuf_ref[5:5 + Lb, :] * cw_ref[0:1, :]
    acc = acc + cbuf_ref[6:6 + Lb, :] * cw_ref[1:2, :]
    acc = acc + cbuf_ref[7:7 + Lb, :] * cw_ref[2:3, :]
    acc = acc + cbuf_ref[8:8 + Lb, :] * cw_ref[3:4, :]
    xact_ref[...] = _silu(acc)
    tail = cbuf_ref[Lb:Lb + 8, :]
    cbuf_ref[0:8, :] = tail

    @pl.when(j == nblk - 1)
    def _():
        convn_ref[...] = tail[5:8, :]

    dt = _softplus(sm_ref[...] + dtb_ref[...])
    dta = dt * (-jnp.exp(alog_ref[...]))
    ii = _iota((Lb, Lb), 0)
    jj = _iota((Lb, Lb), 1)
    tril = _onehot(((ii >> 6) == (jj >> 6)) & (ii >= jj))
    acum = _dot01_l(tril, dta)
    expand = _head_expand()
    acx_ref[...] = _dot01(acum, expand)
    dtx_ref[...] = _dot01(dt, expand)

    li = _iota((C, D_MODEL), 0)
    lj = _iota((C, D_MODEL), 1) & (C - 1)
    diag = li == lj
    causal = li >= lj
    ones_c = jnp.ones((C, C), BF16)
    tile8 = _onehot((_iota((C, 8 * C), 1) & (C - 1)) == _iota((C, 8 * C), 0))
    pair_mask = (_iota((LANES, LANES), 0) >> 6) == (_iota((LANES, LANES), 1) >> 6)
    half = SSD_INNER // SSD_GROUPS

    def chunk(c, carry):
        r0 = pl.multiple_of(c * C, C)
        xa = xact_ref[pl.ds(r0, C), :]
        xs = xa[:, :SSD_INNER]
        bms = [xa[:, SSD_INNER + g * SSD_D_STATE:SSD_INNER + (g + 1) * SSD_D_STATE] for g in range(2)]
        cms = [xa[:, SSD_INNER + (2 + g) * SSD_D_STATE:SSD_INNER + (3 + g) * SSD_D_STATE]
               for g in range(2)]
        acx = acx_ref[pl.ds(r0, C), :]
        dtx = dtx_ref[pl.ds(r0, C), :]
        alast = acx[C - 1:C, :]
        arow = _dot01_l(ones_c, jnp.where(diag, acx, 0.0))
        dtrow = _dot01_l(ones_c, jnp.where(diag, dtx, 0.0))
        decay = jnp.exp(jnp.where(causal, acx - arow, -jnp.inf))
        cbt = []
        for g in range(2):
            cb = lax.dot_general(cms[g].astype(BF16), bms[g].astype(BF16),
                                 (((1,), (1,)), ((), ())), preferred_element_type=F32)
            cbt.append(_dot01(cb, tile8))
        wts = (jnp.concatenate(cbt, axis=1) * decay * dtrow).astype(BF16)
        ys = []
        for p in range(SSD_HEADS // 2):
            xp = xs[:, p * LANES:(p + 1) * LANES]
            xbd = jnp.where(pair_mask, jnp.concatenate([xp, xp], axis=0), 0.0).astype(BF16)
            ys.append(_dot(wts[:, p * LANES:(p + 1) * LANES], xbd))
        y = jnp.concatenate(ys, axis=1)
        hT = hT_ref[...]
        hTb = hT.astype(BF16)
        yi = [_dot(cms[g].astype(BF16), hTb[:, g * half:(g + 1) * half]) for g in range(2)]
        y = y + jnp.concatenate(yi, axis=1) * jnp.exp(acx)
        xsc = (jnp.exp(alast - acx) * dtx * xs).astype(BF16)
        st = [lax.dot_general(bms[g].astype(BF16), xsc[:, g * half:(g + 1) * half],
                              (((0,), (0,)), ((), ())), preferred_element_type=F32) for g in range(2)]
        hT_ref[...] = hT * jnp.exp(alast) + jnp.concatenate(st, axis=1)
        y = y + dsk_ref[...] * xs
        y = y * _silu(z_ref[pl.ds(r0, C), :])
        outs = []
        for g in range(2):
            yg = y[:, g * half:(g + 1) * half]
            ms = jnp.mean(yg * yg, axis=-1, keepdims=True)
            outs.append(yg * lax.rsqrt(ms + EPS))
        y_ref[pl.ds(r0, C), :] = jnp.concatenate(outs, axis=1) * nw_ref[...]
        return carry

    lax.fori_loop(0, Lb // C, chunk, 0)

    @pl.when(j == nblk - 1)
    def _():
        ssmn_ref[...] = hT_ref[...].T


def _ssd_seq(xbc, sm, z, P, i, B, L):
    Lb = SEQ_BLOCK
    nblk = L // Lb
    T = B * L
    tok = lambda n: pl.BlockSpec((Lb, n), lambda b, j: (b * nblk + j, 0))
    par = lambda r, n: pl.BlockSpec((None, r, n), lambda b, j: (i, 0, 0))
    y, convn, ssmn = pl.pallas_call(
        functools.partial(_ssd_seq_body, nblk=nblk),
        grid=(B, nblk),
        in_specs=[tok(SSD_CONV_DIM), tok(SMALL), tok(SSD_INNER),
                  par(SSD_CONV, SSD_CONV_DIM), par(1, SSD_CONV_DIM), par(1, SMALL), par(1, SMALL),
                  par(1, SSD_INNER), par(1, SSD_INNER)],
        out_specs=[tok(SSD_INNER),
                   pl.BlockSpec((None, SSD_CONV - 1, SSD_CONV_DIM), lambda b, j: (b, 0, 0)),
                   pl.BlockSpec((None, SSD_INNER, SSD_D_STATE), lambda b, j: (b, 0, 0))],
        out_shape=[jax.ShapeDtypeStruct((T, SSD_INNER), F32),
                   jax.ShapeDtypeStruct((B, SSD_CONV - 1, SSD_CONV_DIM), F32),
                   jax.ShapeDtypeStruct((B, SSD_INNER, SSD_D_STATE), F32)],
        scratch_shapes=[pltpu.VMEM((Lb + 8, SSD_CONV_DIM), F32),
                        pltpu.VMEM((Lb, SSD_CONV_DIM), F32),
                        pltpu.VMEM((Lb, SSD_INNER), F32),
                        pltpu.VMEM((Lb, SSD_INNER), F32),
                        pltpu.VMEM((SSD_D_STATE, SSD_INNER), F32)],
        compiler_params=_cp(("parallel", "arbitrary")),
        name="ssd_seq",
    )(xbc, sm, z, P["conv_w"], P["conv_b"], P["dt_bias"], P["a_log"], P["d_skip_x"], P["ssd_norm"])
    return y, convn, ssmn.reshape(B, SSD_HEADS, SSD_HEAD_DIM, SSD_D_STATE)


def _gla_seq_body(q_ref, k_ref, v_ref, og_ref, sm_ref, w2_ref, gb_ref, nw_ref,
                  o_ref, glan_ref,
                  st_ref, dec_ref, qg_ref, kd_ref, oi_ref, *, nblk):
    Lb = SEQ_BLOCK
    j = pl.program_id(1)

    @pl.when(j == 0)
    def _():
        st_ref[...] = jnp.zeros_like(st_ref)

    gate = _dot(sm_ref[...].astype(BF16), w2_ref[...]) + gb_ref[...]
    lg = -_softplus(-gate) / GLA_GATE_NORM
    ii = _iota((Lb, Lb), 0)
    jj = _iota((Lb, Lb), 1)
    same = (ii >> 4) == (jj >> 4)
    causal = same & (ii >= jj)
    b = _dot01_l(_onehot(causal), lg)
    blast = _dot01_l(_onehot(same), lg)
    qg = q_ref[...] * (GLA_DK ** -0.5) * jnp.exp(b)
    kk = k_ref[...]
    kg = kk * jnp.exp(-b)
    kd = kk * jnp.exp(blast - b)
    dec_ref[...] = jnp.exp(blast)
    qg_ref[...] = qg.astype(BF16)
    kd_ref[...] = kd.astype(BF16)

    def chunk(c, carry):
        r0 = pl.multiple_of(c * GLA_CHUNK, GLA_CHUNK)
        for h in range(GLA_HEADS):
            ks = slice(h * GLA_DK, (h + 1) * GLA_DK)
            vs = slice(h * GLA_DV, (h + 1) * GLA_DV)
            st = st_ref[h]
            oi_ref[pl.ds(r0, GLA_CHUNK), vs] = lax.dot_general(
                qg_ref[pl.ds(r0, GLA_CHUNK), ks], st.astype(BF16),
                (((1,), (1,)), ((), ())), preferred_element_type=F32)
            upd = lax.dot_general(v_ref[pl.ds(r0, GLA_CHUNK), vs].astype(BF16),
                                  kd_ref[pl.ds(r0, GLA_CHUNK), ks],
                                  (((0,), (0,)), ((), ())), preferred_element_type=F32)
            st_ref[h] = st * dec_ref[pl.ds(r0, 8), ks][0:1, :] + upd
        return carry

    lax.fori_loop(0, Lb // GLA_CHUNK, chunk, 0)

    for h in range(GLA_HEADS):
        ks = slice(h * GLA_DK, (h + 1) * GLA_DK)
        vs = slice(h * GLA_DV, (h + 1) * GLA_DV)
        att = lax.dot_general(qg[:, ks].astype(BF16), kg[:, ks].astype(BF16),
                              (((1,), (1,)), ((), ())), preferred_element_type=F32)
        att = jnp.where(causal, att, 0.0)
        o_h = _dot(att.astype(BF16), v_ref[:, vs].astype(BF16)) + oi_ref[:, vs]
        ms = jnp.mean(o_h * o_h, axis=-1, keepdims=True)
        o_h = o_h * lax.rsqrt(ms + EPS) * nw_ref[...]
        o_ref[:, vs] = o_h * _silu(og_ref[:, vs])

    @pl.when(j == nblk - 1)
    def _():
        for h in range(GLA_HEADS):
            glan_ref[h] = st_ref[h].T


def _gla_seq(q, k, v, og, sm, P, i, B, L):
    Lb = SEQ_BLOCK
    nblk = L // Lb
    T = B * L
    tok = lambda n: pl.BlockSpec((Lb, n), lambda b, j: (b * nblk + j, 0))
    par = lambda r, n: pl.BlockSpec((None, r, n), lambda b, j: (i, 0, 0))
    return pl.pallas_call(
        functools.partial(_gla_seq_body, nblk=nblk),
        grid=(B, nblk),
        in_specs=[tok(GLA_KEY), tok(GLA_KEY), tok(GLA_VAL), tok(GLA_VAL), tok(SMALL),
                  par(SMALL, GLA_KEY), par(1, GLA_KEY), par(1, GLA_DV)],
        out_specs=[tok(GLA_VAL),
                   pl.BlockSpec((None, GLA_HEADS, GLA_DK, GLA_DV), lambda b, j: (b, 0, 0, 0))],
        out_shape=[jax.ShapeDtypeStruct((T, GLA_VAL), F32),
                   jax.ShapeDtypeStruct((B, GLA_HEADS, GLA_DK, GLA_DV), F32)],
        scratch_shapes=[pltpu.VMEM((GLA_HEADS, GLA_DV, GLA_DK), F32),
                        pltpu.VMEM((Lb, GLA_KEY), F32),
                        pltpu.VMEM((Lb, GLA_KEY), BF16),
                        pltpu.VMEM((Lb, GLA_KEY), BF16),
                        pltpu.VMEM((Lb, GLA_VAL), F32)],
        compiler_params=_cp(("parallel", "arbitrary")),
        name="gla_seq",
    )(q, k, v, og, sm, P["gate_w2"], P["gate_b"], P["gla_norm"])


def _even_step_pre_body(xbc_ref, c0_ref, sm_ref, cw_ref, cb_ref, dtb_ref, alog_ref, dsk_ref,
                        w2_ref, gb_ref,
                        xact_ref, convn_ref, dax_ref, c2_ref, ybase_ref, eg_ref):
    u = xbc_ref[...]
    n = SSD_CONV_DIM
    acc = cb_ref[...] + c0_ref[:, 0:n] * cw_ref[0:1, :]
    acc = acc + c0_ref[:, n:2 * n] * cw_ref[1:2, :]
    acc = acc + c0_ref[:, 2 * n:3 * n] * cw_ref[2:3, :]
    acc = acc + u * cw_ref[3:4, :]
    xa = _silu(acc)
    xact_ref[...] = xa
    convn_ref[:, 0:n] = c0_ref[:, n:2 * n]
    convn_ref[:, n:2 * n] = c0_ref[:, 2 * n:3 * n]
    convn_ref[:, 2 * n:3 * n] = u
    dt = _softplus(sm_ref[...] + dtb_ref[...])
    dta = dt * (-jnp.exp(alog_ref[...]))
    expand = _head_expand()
    dtx = _dot01(dt, expand)
    dax_ref[...] = jnp.exp(_dot01(dta, expand))
    xs = xa[:, :SSD_INNER]
    c2_ref[...] = dtx * xs
    half = SSD_INNER // SSD_GROUPS
    cbs = []
    for g in range(2):
        bm = xa[:, SSD_INNER + g * SSD_D_STATE:SSD_INNER + (g + 1) * SSD_D_STATE]
        cm = xa[:, SSD_INNER + (2 + g) * SSD_D_STATE:SSD_INNER + (3 + g) * SSD_D_STATE]
        cb = jnp.sum(cm * bm, axis=-1, keepdims=True)
        cbs.append(jnp.broadcast_to(cb, (cb.shape[0], half)))
    ybase_ref[...] = jnp.concatenate(cbs, axis=1) * dtx * xs + dsk_ref[...] * xs
    gate = _dot(sm_ref[...].astype(BF16), w2_ref[...]) + gb_ref[...]
    eg_ref[...] = jnp.exp(-_softplus(-gate) / GLA_GATE_NORM)


def _even_step_pre(xbc, conv0, sm, P, i):
    Bn = xbc.shape[0]
    full = lambda n: pl.BlockSpec((Bn, n), lambda t: (0, 0))
    par = lambda r, n: pl.BlockSpec((None, r, n), lambda t: (i, 0, 0))
    outs = (SSD_CONV_DIM, 3 * SSD_CONV_DIM, SSD_INNER, SSD_INNER, SSD_INNER, GLA_KEY)
    return pl.pallas_call(
        _even_step_pre_body,
        grid=(1,),
        in_specs=[full(SSD_CONV_DIM), full(3 * SSD_CONV_DIM), full(SMALL),
                  par(SSD_CONV, SSD_CONV_DIM), par(1, SSD_CONV_DIM), par(1, SMALL), par(1, SMALL),
                  par(1, SSD_INNER), par(SMALL, GLA_KEY), par(1, GLA_KEY)],
        out_specs=[full(n) for n in outs],
        out_shape=[jax.ShapeDtypeStruct((Bn, n), F32) for n in outs],
        compiler_params=_cp(("arbitrary",)),
        name="even_step_pre",
    )(xbc, conv0, sm, P["conv_w"], P["conv_b"], P["dt_bias"], P["a_log"], P["d_skip_x"],
      P["gate_w2"], P["gate_b"])


def _to_cols(x):
    Bn, N = x.shape
    return x.reshape(Bn // STEP_BATCH, STEP_BATCH, N).transpose(0, 2, 1)


def _from_cols(x):
    nb, N, bb = x.shape
    return x.transpose(0, 2, 1).reshape(nb * bb, N)


def _layer_state_call(body, name, state_all, acc, i, rows, width, args, specs, out_spec, out_shape):
    nl, Bn = state_all.shape[:2]
    st = pl.BlockSpec((None, STEP_BATCH, rows, width), lambda t: (i, t, 0, 0))
    args = [state_all.reshape(nl, Bn, rows, width)] + list(args)
    specs = [st] + list(specs)
    aliases = {}
    if acc is not None:
        aliases = {len(args): 0}
        args.append(acc)
        specs.append(pl.BlockSpec(memory_space=pl.ANY))
    return pl.pallas_call(
        body,
        grid=(Bn // STEP_BATCH,),
        in_specs=specs,
        out_specs=[st, out_spec],
        out_shape=[jax.ShapeDtypeStruct((nl, Bn, rows, width), F32), out_shape],
        input_output_aliases=aliases,
        compiler_params=_cp(("parallel",)),
        name=name,
    )(*args)


def _ssd_step_body(h_ref, c1_ref, c2_ref, bm_ref, cm_ref, *rest):
    hn_ref, y_ref = rest[-2:]
    half = SSD_INNER // SSD_GROUPS
    lane = _iota((SSD_INNER, STEP_BATCH), 1)
    ycols = jnp.zeros((SSD_INNER, STEP_BATCH), F32)
    c1 = c1_ref[...]
    c2 = c2_ref[...]
    for b in range(STEP_BATCH):
        h0 = h_ref[b]
        bmat = jnp.concatenate(
            [jnp.broadcast_to(bm_ref[b:b + 1, g * SSD_D_STATE:(g + 1) * SSD_D_STATE],
                              (half, SSD_D_STATE)) for g in range(2)], axis=0)
        cmat = jnp.concatenate(
            [jnp.broadcast_to(cm_ref[b:b + 1, g * SSD_D_STATE:(g + 1) * SSD_D_STATE],
                              (half, SSD_D_STATE)) for g in range(2)], axis=0)
        ycol = jnp.sum(h0 * cmat, axis=1, keepdims=True)
        ycols = jnp.where(lane == b, ycol, ycols)
        hn_ref[b] = h0 * c1[:, b:b + 1] + c2[:, b:b + 1] * bmat
    y_ref[...] = ycols


def _ssd_step(state_all, acc, i, dax, c2, xact):
    Bn = state_all.shape[1]
    bb = STEP_BATCH
    bm = xact[:, SSD_INNER:SSD_INNER + 2 * SSD_D_STATE]
    cm = xact[:, SSD_INNER + 2 * SSD_D_STATE:]
    col = pl.BlockSpec((None, SSD_INNER, bb), lambda t: (t, 0, 0))
    row = pl.BlockSpec((bb, 2 * SSD_D_STATE), lambda t: (t, 0))
    acc, ycols = _layer_state_call(
        _ssd_step_body, "ssd_step", state_all, acc, i, SSD_INNER, SSD_D_STATE,
        [_to_cols(dax), _to_cols(c2), bm, cm], [col, col, row, row],
        col, jax.ShapeDtypeStruct((Bn // bb, SSD_INNER, bb), F32))
    return acc, _from_cols(ycols)


def _gla_step_body(s_ref, eg_ref, kc_ref, qc_ref, v_ref, *rest):
    sn_ref, o_ref = rest[-2:]
    eg = eg_ref[...]
    kc = kc_ref[...]
    qc = qc_ref[...] * (GLA_DK ** -0.5)
    for b in range(STEP_BATCH):
        s0 = s_ref[b]
        vmat = jnp.concatenate(
            [jnp.broadcast_to(v_ref[b:b + 1, h * GLA_DV:(h + 1) * GLA_DV], (GLA_DK, GLA_DV))
             for h in range(GLA_HEADS)], axis=0)
        sn = s0 * eg[:, b:b + 1] + kc[:, b:b + 1] * vmat
        sn_ref[b] = sn
        t = sn * qc[:, b:b + 1]
        o_ref[b:b + 1, :] = jnp.concatenate(
            [jnp.sum(t[h * GLA_DK:(h + 1) * GLA_DK], axis=0, keepdims=True)
             for h in range(GLA_HEADS)], axis=1)


def _gla_step(state_all, acc, i, eg, k, q, v):
    Bn = state_all.shape[1]
    bb = STEP_BATCH
    col = pl.BlockSpec((None, GLA_KEY, bb), lambda t: (t, 0, 0))
    row = pl.BlockSpec((bb, GLA_VAL), lambda t: (t, 0))
    return _layer_state_call(
        _gla_step_body, "gla_step", state_all, acc, i, GLA_KEY, GLA_DV,
        [_to_cols(eg), _to_cols(k), _to_cols(q), v], [col, col, col, row],
        row, jax.ShapeDtypeStruct((Bn, GLA_VAL), F32))


def _even_step_post_body(x_ref, ybase_ref, yi_ref, dax_ref, z_ref, o_ref, og_ref,
                         snw_ref, gnw_ref, w_ref, out_ref):
    y = (ybase_ref[...] + yi_ref[...] * dax_ref[...]) * _silu(z_ref[...])
    half = SSD_INNER // SSD_GROUPS
    acc = None
    for g in range(2):
        yg = y[:, g * half:(g + 1) * half]
        ms = jnp.mean(yg * yg, axis=-1, keepdims=True)
        yn = yg * lax.rsqrt(ms + EPS) * snw_ref[:, g * half:(g + 1) * half]
        part = _dot(yn.astype(BF16), w_ref[g * half:(g + 1) * half, :])
        acc = part if acc is None else acc + part
    for h in range(GLA_HEADS):
        vs = slice(h * GLA_DV, (h + 1) * GLA_DV)
        oh = o_ref[:, vs]
        ms = jnp.mean(oh * oh, axis=-1, keepdims=True)
        on = oh * lax.rsqrt(ms + EPS) * gnw_ref[...] * _silu(og_ref[:, vs])
        acc = acc + _dot(on.astype(BF16), w_ref[SSD_INNER + h * GLA_DV:SSD_INNER + (h + 1) * GLA_DV, :])
    out_ref[...] = x_ref[...] + acc


def _even_step_post(x, ybase, yi, dax, z, o, og, P, w_out, i):
    Bn = x.shape[0]
    full = pl.BlockSpec((Bn, D_MODEL), lambda t: (0, 0))
    return pl.pallas_call(
        _even_step_post_body,
        grid=(1,),
        in_specs=[full] * 7 + [pl.BlockSpec((None, 1, SSD_INNER), lambda t: (i, 0, 0)),
                               pl.BlockSpec((None, 1, GLA_DV), lambda t: (i, 0, 0)),
                               pl.BlockSpec((None, SSD_INNER + GLA_VAL, D_MODEL), lambda t: (i, 0, 0))],
        out_specs=full,
        out_shape=jax.ShapeDtypeStruct((Bn, D_MODEL), F32),
        compiler_params=_cp(("arbitrary",)),
        name="even_step_post",
    )(x, ybase, yi, dax, z, o, og, P["ssd_norm"], P["gla_norm"], w_out)


_ODD_OUTS = 8


def _odd_pre_body(*refs, step, has_vfirst, nblk_seq, tm):
    it = iter(refs)
    x_ref = next(it)
    if step:
        prev_ref = next(it)
    else:
        xp_ref = next(it)
        sh0_ref = next(it)
    g_ref, mix_ref = next(it), next(it)
    wr_ref, wk_ref, wv_ref = next(it), next(it), next(it)
    w0_ref, w1_ref, w2_ref = next(it), next(it), next(it)
    a0_ref, a1_ref, a2_ref = next(it), next(it), next(it)
    if has_vfirst:
        vf_ref, v0_ref, v1_ref, v2_ref = next(it), next(it), next(it), next(it)
    g1_ref, g2_ref = next(it), next(it)
    kkw_ref, kaw_ref = next(it), next(it)
    h_ref, r_ref, dec_ref, k_ref, v_ref, an_ref, b_ref, gg_ref = (next(it) for _ in range(_ODD_OUTS))

    h = _rms(x_ref[...], g_ref[...])
    if step:
        prev = prev_ref[...]
    else:
        i = pl.program_id(0)
        hp = _rms(xp_ref[...], g_ref[...])[7:8, :]
        prow = jnp.where(i % nblk_seq == 0, sh0_ref[...], hp)
        prev = jnp.where(_iota((tm, D_MODEL), 0) == 0, prow, pltpu.roll(h, 1, axis=0))
    h_ref[...] = h
    xx = prev - h
    mixed = lambda r: (h + xx * mix_ref[r:r + 1, :]).astype(BF16)
    xr, xw, xk, xv, xa, xg = (mixed(r) for r in range(6))
    r_ref[...] = _dot(xr, wr_ref[...])
    wl = w0_ref[...] + _dot(jnp.tanh(_dot(xw, w1_ref[...])).astype(BF16), w2_ref[...])
    w = -_softplus(-wl) - 0.5
    dec_ref[...] = jnp.exp(-jnp.exp(w))
    k = _dot(xk, wk_ref[...])
    v = _dot(xv, wv_ref[...])
    if has_vfirst:
        gate = jax.nn.sigmoid(v0_ref[...] + _dot(_dot(xv, v1_ref[...]).astype(BF16), v2_ref[...]))
        v = v + (vf_ref[...] - v) * gate
    v_ref[...] = v
    a = jax.nn.sigmoid(a0_ref[...] + _dot(_dot(xa, a1_ref[...]).astype(BF16), a2_ref[...]))
    gg_ref[...] = _dot(jax.nn.sigmoid(_dot(xg, g1_ref[...])).astype(BF16), g2_ref[...])
    kkf = k * kkw_ref[...]
    red = _head_reduce()
    expand = _head_expand()
    ss = _dot01x2(kkf * kkf, jnp.concatenate([red, red], axis=0))
    kk = kkf * _dot01x2(lax.rsqrt(jnp.maximum(ss, 1e-24)), jnp.concatenate([expand, expand], axis=0))
    k_ref[...] = k * (1.0 + (a - 1.0) * kaw_ref[...])
    an_ref[...] = -kk
    b_ref[...] = kk * a


def _odd_pre(x, prev_or_shift0, vfirst, P, i, tm, L):
    T = x.shape[0]
    step = L == 1
    has_vfirst = vfirst is not None
    nblk_seq = max(L // tm, 1)
    tok = pl.BlockSpec((tm, D_MODEL), lambda t: (t, 0))
    par = lambda r, n, idx=i: pl.BlockSpec((None, r, n), lambda t: (idx, 0, 0))
    args = [x]
    specs = [tok]
    if step:
        args.append(prev_or_shift0)
        specs.append(tok)
    else:
        args += [x, prev_or_shift0]
        specs += [pl.BlockSpec((8, D_MODEL), lambda t: (jnp.maximum(t * (tm // 8) - 1, 0), 0)),
                  pl.BlockSpec((None, 1, D_MODEL), lambda t: (t // nblk_seq, 0, 0))]
    args += [P["mix_norm"], P["mix"], P["w_r"], P["w_k"], P["w_v"], P["w0"], P["w1"], P["w2"],
             P["a0"], P["a1"], P["a2"]]
    specs += [pl.BlockSpec((None, 1, D_MODEL), lambda t: (2 * i + 1, 0, 0)),
              par(8, D_MODEL), par(D_MODEL, D_MODEL), par(D_MODEL, D_MODEL), par(D_MODEL, D_MODEL),
              par(1, D_MODEL), par(D_MODEL, LANES), par(LANES, D_MODEL),
              par(1, D_MODEL), par(D_MODEL, LANES), par(LANES, D_MODEL)]
    if has_vfirst:
        args += [vfirst, P["v0"], P["v1"], P["v2"]]
        specs += [tok, par(1, D_MODEL, i - 1), par(D_MODEL, LANES, i - 1), par(LANES, D_MODEL, i - 1)]
    args += [P["g1"], P["g2"], P["k_k"], P["k_a"]]
    specs += [par(D_MODEL, 2 * LANES), par(2 * LANES, D_MODEL), par(1, D_MODEL), par(1, D_MODEL)]
    return pl.pallas_call(
        functools.partial(_odd_pre_body, step=step, has_vfirst=has_vfirst, nblk_seq=nblk_seq, tm=tm),
        grid=(T // tm,),
        in_specs=specs,
        out_specs=[tok] * _ODD_OUTS,
        out_shape=[jax.ShapeDtypeStruct((T, D_MODEL), F32)] * _ODD_OUTS,
        compiler_params=_cp(("parallel",)),
        name="odd_pre",
    )(*args)


def _dot01x2(x, e2):
    hi = x.astype(BF16)
    mid = (x - hi.astype(F32)).astype(BF16)
    return _dot(jnp.concatenate([hi, mid], axis=1), e2)


def _odd_post_body(x_ref, o_ref, r_ref, k_ref, v_ref, g_ref, gnw_ref, gnb_ref, rk_ref, wo_ref, out_ref):
    red = _head_reduce()
    red = jnp.concatenate([red, red], axis=0)
    expand = _head_expand()
    expand = jnp.concatenate([expand, expand], axis=0)
    o = o_ref[...]
    mu = _dot01x2(_dot01x2(o, red) * (1.0 / RWKV_HEAD), expand)
    d = o - mu
    var = _dot01x2(d * d, red) * (1.0 / RWKV_HEAD)
    on = d * _dot01x2(lax.rsqrt(var + RWKV_GN_EPS), expand) * gnw_ref[...] + gnb_ref[...]
    v = v_ref[...]
    bonus = _dot01x2(_dot01x2(r_ref[...] * k_ref[...] * rk_ref[...], red), expand) * v
    out_ref[...] = x_ref[...] + _dot(((on + bonus) * g_ref[...]).astype(BF16), wo_ref[...])


def _odd_post(x, o, r, k, v, g, P, i, tm):
    T = x.shape[0]
    tok = pl.BlockSpec((tm, D_MODEL), lambda t: (t, 0))
    par = lambda r_, n: pl.BlockSpec((None, r_, n), lambda t: (i, 0, 0))
    return pl.pallas_call(
        _odd_post_body,
        grid=(T // tm,),
        in_specs=[tok] * 6 + [par(1, D_MODEL), par(1, D_MODEL), par(1, D_MODEL), par(D_MODEL, D_MODEL)],
        out_specs=tok,
        out_shape=jax.ShapeDtypeStruct((T, D_MODEL), F32),
        compiler_params=_cp(("parallel",)),
        name="odd_post",
    )(x, o, r, k, v, g, P["gn_w"], P["gn_b"], P["r_k"], P["w_o"])


RWKV_TB = 64
RWKV_NPAIR = RWKV_HEADS // 2


RWKV_SEQS = 2
RWKV_UNROLL = 8


def _rwkv_seq_body(w_ref, k_ref, a_ref, b_ref, r_ref, v_ref, o_ref, sn_ref, s_ref, vt_ref, oacc_ref, *, ntb):
    N = RWKV_HEAD
    R = RWKV_NPAIR * N
    tbi = pl.program_id(1)

    @pl.when(tbi == 0)
    def _():
        s_ref[...] = jnp.zeros_like(s_ref)

    same_head = _onehot((_iota((LANES, LANES), 0) >> 6) == (_iota((LANES, LANES), 1) >> 6))
    seg = jnp.concatenate([same_head, same_head], axis=0)
    for c in range(RWKV_SEQS):
        for p in range(RWKV_NPAIR):
            bt = v_ref[c, :, p * LANES:(p + 1) * LANES].T
            vt_ref[c, p * N:(p + 1) * N, :] = jnp.concatenate([bt[0:N], bt[N:2 * N]], axis=1)
    oacc_ref[...] = jnp.zeros_like(oacc_ref)
    lane_step = _iota((R, LANES), 1) & (N - 1)
    lane_base = (_iota((R, LANES), 1) >> 6) * N

    def steps(tu, carry):
        row0 = pl.multiple_of(tu * RWKV_UNROLL, RWKV_UNROLL)
        for j in range(RWKV_UNROLL):
            t = tu * RWKV_UNROLL + j

            def rows(ref, c, t):
                return jnp.concatenate(
                    [jnp.broadcast_to(ref[c, pl.ds(row0, RWKV_UNROLL), p * LANES:(p + 1) * LANES][j:j + 1, :],
                                      (N, LANES)) for p in range(RWKV_NPAIR)], axis=0)

            for c in range(RWKV_SEQS):
                s = s_ref[c]
                x = s * rows(a_ref, c, t)
                hi = x.astype(BF16)
                mid = (x - hi.astype(F32)).astype(BF16)
                sa = _dot(jnp.concatenate([hi, mid], axis=1), seg)
                vcol = jnp.take_along_axis(vt_ref[c], lane_base + t, axis=1)
                s = s * rows(w_ref, c, t) + sa * rows(b_ref, c, t) + vcol * rows(k_ref, c, t)
                s_ref[c] = s
                y = s * rows(r_ref, c, t)
                yh = y.astype(BF16)
                ym = (y - yh.astype(F32)).astype(BF16)
                ob = _dot(jnp.concatenate([yh, ym], axis=1), seg)
                oacc_ref[c] = jnp.where(lane_step == t, ob, oacc_ref[c])
        return carry

    lax.fori_loop(0, RWKV_TB // RWKV_UNROLL, steps, 0)

    for c in range(RWKV_SEQS):
        for p in range(RWKV_NPAIR):
            ot = oacc_ref[c, p * N:(p + 1) * N, :].T
            o_ref[c, :, p * LANES:(p + 1) * LANES] = jnp.concatenate([ot[0:N], ot[N:2 * N]], axis=1)

    @pl.when(tbi == ntb - 1)
    def _():
        for c in range(RWKV_SEQS):
            for p in range(RWKV_NPAIR):
                for hh in range(2):
                    sn_ref[c, (2 * p + hh) * N:(2 * p + hh + 1) * N, :] = (
                        s_ref[c, p * N:(p + 1) * N, hh * N:(hh + 1) * N])


def _rwkv_seq(w, k, a, b, r, v, B, L):
    ntb = L // RWKV_TB
    nc = RWKV_SEQS
    as_seqs = lambda x: x.reshape(B, L, D_MODEL)
    tok = pl.BlockSpec((nc, RWKV_TB, D_MODEL), lambda bi, t: (bi, t, 0))
    state_rows = RWKV_NPAIR * RWKV_HEAD
    o, sn = pl.pallas_call(
        functools.partial(_rwkv_seq_body, ntb=ntb),
        grid=(B // nc, ntb),
        in_specs=[tok] * 6,
        out_specs=[tok, pl.BlockSpec((nc, D_MODEL, RWKV_HEAD), lambda bi, t: (bi, 0, 0))],
        out_shape=[jax.ShapeDtypeStruct((B, L, D_MODEL), F32),
                   jax.ShapeDtypeStruct((B, D_MODEL, RWKV_HEAD), F32)],
        scratch_shapes=[pltpu.VMEM((nc, state_rows, LANES), F32),
                        pltpu.VMEM((nc, state_rows, LANES), F32),
                        pltpu.VMEM((nc, state_rows, LANES), F32)],
        compiler_params=_cp(("parallel", "arbitrary")),
        name="rwkv_seq",
    )(as_seqs(w), as_seqs(k), as_seqs(a), as_seqs(b), as_seqs(r), as_seqs(v))
    return o.reshape(B * L, D_MODEL), sn.reshape(B, RWKV_HEADS, RWKV_HEAD, RWKV_HEAD)


def _rwkv_step_body(s_ref, w_ref, k_ref, a_ref, b_ref, r_ref, vc_ref, *rest):
    sn_ref, o_ref = rest[-2:]
    N = RWKV_HEAD
    lane = _iota((D_MODEL, STEP_BATCH), 1)
    ocols = jnp.zeros((D_MODEL, STEP_BATCH), F32)
    vc = vc_ref[...]

    def rows(ref, b):
        return jnp.concatenate(
            [jnp.broadcast_to(ref[b:b + 1, h * N:(h + 1) * N], (N, N)) for h in range(RWKV_HEADS)], axis=0)

    for b in range(STEP_BATCH):
        s = s_ref[b]
        sa = jnp.sum(s * rows(a_ref, b), axis=1, keepdims=True)
        s = s * rows(w_ref, b) + sa * rows(b_ref, b) + vc[:, b:b + 1] * rows(k_ref, b)
        sn_ref[b] = s
        ocol = jnp.sum(s * rows(r_ref, b), axis=1, keepdims=True)
        ocols = jnp.where(lane == b, ocol, ocols)
    o_ref[...] = ocols


def _rwkv_step(state_all, acc, i, w, k, a, b, r, v):
    Bn = state_all.shape[1]
    bb = STEP_BATCH
    col = pl.BlockSpec((None, D_MODEL, bb), lambda t: (t, 0, 0))
    row = pl.BlockSpec((bb, D_MODEL), lambda t: (t, 0))
    acc, ocols = _layer_state_call(
        _rwkv_step_body, "rwkv_step", state_all, acc, i, D_MODEL, RWKV_HEAD,
        [w, k, a, b, r, _to_cols(v)], [row] * 5 + [col],
        col, jax.ShapeDtypeStruct((Bn // bb, D_MODEL, bb), F32))
    return _from_cols(ocols), acc


def _pad_to(w, axis, n):
    pad = [(0, 0)] * w.ndim
    pad[axis] = (0, n - w.shape[axis])
    return jnp.pad(w, pad)


def _prepare(W):
    bf = lambda w: w.astype(BF16)
    row = lambda w: w[:, None, :]
    sizes = [SSD_INNER, SSD_CONV_DIM, SSD_HEADS, GLA_KEY, GLA_KEY, GLA_VAL, GLA_VAL, GLA_GATE_RANK]
    offs = [0]
    for s in sizes:
        offs.append(offs[-1] + s)
    piece = lambda n: W["ev_w_in"][:, :, offs[n]:offs[n + 1]]
    w_in = jnp.concatenate([piece(0), piece(1), piece(3), piece(4), piece(5), piece(6),
                            _pad_to(jnp.concatenate([piece(2), piece(7)], axis=-1), 2, SMALL)], axis=-1)
    gate_w2 = jnp.pad(W["ev_gla_gate_w2"], ((0, 0), (SSD_HEADS, SMALL - SSD_HEADS - GLA_GATE_RANK), (0, 0)))
    even = dict(
        w_in=bf(w_in), w_out=bf(W["ev_w_out"]),
        conv_w=W["ev_conv_w"], conv_b=row(W["ev_conv_b"]),
        dt_bias=row(_pad_to(W["ev_dt_bias"], 1, SMALL)), a_log=row(_pad_to(W["ev_a_log"], 1, SMALL)),
        d_skip_x=row(jnp.repeat(W["ev_d_skip"], SSD_HEAD_DIM, axis=1)),
        ssd_norm=row(W["ev_ssd_norm"]), gate_w2=bf(gate_w2), gate_b=row(W["ev_gla_gate_b"]),
        gla_norm=row(W["ev_gla_norm"]),
    )
    odd = dict(
        mix_norm=row(W["mix_norm"]), mix=_pad_to(W["od_mix"], 1, 8),
        w_r=bf(W["od_w_r"]), w_k=bf(W["od_w_k"]), w_v=bf(W["od_w_v"]), w_o=bf(W["od_w_o"]),
        w0=row(W["od_w0"]), w1=bf(_pad_to(W["od_w1"], 2, LANES)), w2=bf(_pad_to(W["od_w2"], 1, LANES)),
        a0=row(W["od_a0"]), a1=bf(_pad_to(W["od_a1"], 2, LANES)), a2=bf(_pad_to(W["od_a2"], 1, LANES)),
        v0=row(W["od_v0"]), v1=bf(_pad_to(W["od_v1"], 2, LANES)), v2=bf(_pad_to(W["od_v2"], 1, LANES)),
        g1=bf(_pad_to(W["od_g1"], 2, 2 * LANES)), g2=bf(_pad_to(W["od_g2"], 1, 2 * LANES)),
        k_k=row(W["od_k_k"]), k_a=row(W["od_k_a"]), r_k=row(W["od_r_k"]),
        gn_w=row(W["od_gn_w"]), gn_b=row(W["od_gn_b"]),
    )
    ffn = dict(
        norm=W["ffn_norm"].reshape(DEPTH * 2, 1, D_MODEL),
        w_gu=bf(W["ffn_w_gu"]).reshape(DEPTH * 2, D_MODEL, 2 * D_FF),
        w_down=bf(W["ffn_w_down"]).reshape(DEPTH * 2, D_FF, D_MODEL),
    )
    return dict(even=even, odd=odd, ffn=ffn, mix_norm=row(W["mix_norm"]),
                final_norm=W["final_norm"][None, :])


def _run_group(x3, states, Wp):
    B, L, _ = x3.shape
    T = B * L
    step = L == 1
    x = x3.reshape(T, D_MODEL)
    tm_ffn = min(512, T)
    tm_proj = min(256, T)
    ffn, even, odd = Wp["ffn"], Wp["even"], Wp["odd"]
    convs, ssms, glas, shifts, wkvs = [], [], [], [], []
    ssm_acc = gla_acc = wkv_acc = None
    v_first = None
    for layer in range(DEPTH):
        i = layer // 2
        x = _ffn(x, ffn["norm"], ffn["w_gu"], ffn["w_down"], 2 * layer, tm_ffn)
        if layer % 2 == 0:
            z, xbc, q, k, v, og, sm = _even_in(x, Wp["mix_norm"], even["w_in"], i, tm_proj)
            if step:
                xact, convn, dax, c2, ybase, eg = _even_step_pre(
                    xbc, states[0][i].reshape(B, 3 * SSD_CONV_DIM), sm, even, i)
                ssm_acc, yi = _ssd_step(states[1], ssm_acc, i, dax, c2, xact)
                gla_acc, o = _gla_step(states[2], gla_acc, i, eg, k, q, v)
                x = _even_step_post(x, ybase, yi, dax, z, o, og, even, even["w_out"], i)
                convn = convn.reshape(B, SSD_CONV - 1, SSD_CONV_DIM)
            else:
                y, convn, ssmn = _ssd_seq(xbc, sm, z, even, i, B, L)
                o, glan = _gla_seq(q, k, v, og, sm, even, i, B, L)
                x = _out_proj(x, y, o, even["w_out"], i, tm_ffn)
                ssms.append(ssmn)
                glas.append(glan)
            convs.append(convn)
        else:
            prev = states[3][i] if step else jnp.zeros((B, 1, D_MODEL), F32)
            h, r, dec, k, v, an, bb, gg = _odd_pre(x, prev, v_first, odd, i, tm_proj, L)
            if v_first is None:
                v_first = v
            if step:
                o, wkv_acc = _rwkv_step(states[4], wkv_acc, i, dec, k, an, bb, r, v)
            else:
                o, wkvn = _rwkv_seq(dec, k, an, bb, r, v, B, L)
                wkvs.append(wkvn)
            x = _odd_post(x, o, r, k, v, gg, odd, i, tm_ffn)
            shifts.append(h.reshape(B, L, D_MODEL)[:, -1])
        x = _ffn(x, ffn["norm"], ffn["w_gu"], ffn["w_down"], 2 * layer + 1, tm_ffn)
    y = _final_norm(x, Wp["final_norm"], tm_ffn).reshape(B, L, D_MODEL)
    if step:
        ssm_out = ssm_acc.reshape(N_EVEN, B, SSD_HEADS, SSD_HEAD_DIM, SSD_D_STATE)
        gla_out = gla_acc.reshape(N_EVEN, B, GLA_HEADS, GLA_DK, GLA_DV)
        wkv_out = wkv_acc.reshape(N_ODD, B, RWKV_HEADS, RWKV_HEAD, RWKV_HEAD)
    else:
        ssm_out, gla_out, wkv_out = jnp.stack(ssms), jnp.stack(glas), jnp.stack(wkvs)
    return y, jnp.stack(convs), ssm_out, gla_out, jnp.stack(shifts), wkv_out


def kernel(x_prompt, x_sample, state_conv, state_ssm, state_gla, state_shift, state_wkv, ffn_norm, ffn_w_gu, ffn_w_down, mix_norm, final_norm, ev_w_in, ev_conv_w, ev_conv_b, ev_dt_bias, ev_a_log, ev_d_skip, ev_ssd_norm, ev_gla_gate_w2, ev_gla_gate_b, ev_gla_norm, ev_w_out, od_mix, od_w0, od_w1, od_w2, od_a0, od_a1, od_a2, od_v0, od_v1, od_v2, od_g1, od_g2, od_k_k, od_k_a, od_r_k, od_w_r, od_w_k, od_w_v, od_w_o, od_gn_w, od_gn_b):
    W = dict(ffn_norm=ffn_norm, ffn_w_gu=ffn_w_gu, ffn_w_down=ffn_w_down, mix_norm=mix_norm,
             final_norm=final_norm, ev_w_in=ev_w_in, ev_conv_w=ev_conv_w, ev_conv_b=ev_conv_b,
             ev_dt_bias=ev_dt_bias, ev_a_log=ev_a_log, ev_d_skip=ev_d_skip, ev_ssd_norm=ev_ssd_norm,
             ev_gla_gate_w2=ev_gla_gate_w2, ev_gla_gate_b=ev_gla_gate_b, ev_gla_norm=ev_gla_norm,
             ev_w_out=ev_w_out, od_mix=od_mix, od_w0=od_w0, od_w1=od_w1, od_w2=od_w2,
             od_a0=od_a0, od_a1=od_a1, od_a2=od_a2, od_v0=od_v0, od_v1=od_v1, od_v2=od_v2,
             od_g1=od_g1, od_g2=od_g2, od_k_k=od_k_k, od_k_a=od_k_a, od_r_k=od_r_k,
             od_w_r=od_w_r, od_w_k=od_w_k, od_w_v=od_w_v, od_w_o=od_w_o,
             od_gn_w=od_gn_w, od_gn_b=od_gn_b)
    Wp = _prepare(W)
    prompt = _run_group(x_prompt, None, Wp)
    sample = _run_group(x_sample, (state_conv, state_ssm, state_gla, state_shift, state_wkv), Wp)
    return (prompt[0], sample[0]) + prompt[1:] + sample[1:]
```

```python
import functools

import jax
import jax.numpy as jnp
from jax import lax
from jax.experimental import pallas as pl
from jax.experimental.pallas import tpu as pltpu

F32 = jnp.float32
BF16 = jnp.bfloat16

D_MODEL = 1024
DEPTH = 4
N_EVEN = 2
N_ODD = 2
EPS = 1e-5
D_FF = 2816

SSD_HEADS = 16
SSD_HEAD_DIM = 64
SSD_INNER = 1024
SSD_GROUPS = 2
SSD_D_STATE = 128
SSD_CONV = 4
SSD_CONV_DIM = 1536
SSD_CHUNK = 64

GLA_HEADS = 4
GLA_DK = 128
GLA_DV = 256
GLA_KEY = 512
GLA_VAL = 1024
GLA_GATE_RANK = 16
GLA_GATE_NORM = 16.0
GLA_CHUNK = 16

RWKV_HEAD = 64
RWKV_HEADS = 16
RWKV_GN_EPS = 64e-5

LANES = 128
SMALL = LANES
IN_PERM = SSD_INNER + SSD_CONV_DIM + 2 * GLA_KEY + 2 * GLA_VAL + SMALL
VMEM_LIMIT = 56 * 1024 * 1024
FF_TILE = 1408
SEQ_BLOCK = 256
STEP_BATCH = 8


def _cp(sem):
    return pltpu.CompilerParams(dimension_semantics=sem, vmem_limit_bytes=VMEM_LIMIT)


def _iota(shape, axis):
    return lax.broadcasted_iota(jnp.int32, shape, axis)


def _onehot(mask):
    return jnp.where(mask, 1.0, 0.0).astype(BF16)


def _dot(a, b):
    return jnp.dot(a, b, preferred_element_type=F32)


def _split3(x):
    hi = x.astype(BF16)
    r = x - hi.astype(F32)
    mid = r.astype(BF16)
    lo = (r - mid.astype(F32)).astype(BF16)
    return hi, mid, lo


def _dot01(x, e):
    hi, mid, lo = _split3(x)
    return _dot(hi, e) + _dot(mid, e) + _dot(lo, e)


def _dot01_l(e, x):
    hi, mid, lo = _split3(x)
    return _dot(e, hi) + _dot(e, mid) + _dot(e, lo)


def _rms(x, g):
    ms = jnp.mean(x * x, axis=-1, keepdims=True)
    return x * lax.rsqrt(ms + EPS) * g


def _silu(x):
    return x * jax.nn.sigmoid(x)


def _softplus(x):
    return jnp.maximum(x, 0.0) + jnp.log1p(jnp.exp(-jnp.abs(x)))


def _softplus_abs(x):
    return jnp.maximum(x, 0.0) + jnp.log(1.0 + jnp.exp(-jnp.abs(x)))


def _head_expand():
    return _onehot((_iota((LANES, D_MODEL), 1) >> 6) == _iota((LANES, D_MODEL), 0))


def _head_reduce():
    return _onehot((_iota((D_MODEL, LANES), 0) >> 6) == _iota((D_MODEL, LANES), 1))


def _ffn_body(x_ref, g_ref, wgu_ref, wd_ref, *rest):
    o_ref = rest[-1]
    x = x_ref[...]
    xn = _rms(x, g_ref[...]).astype(BF16)
    acc = None
    for j in range(D_FF // FF_TILE):
        lo = j * FF_TILE
        gate = _dot(xn, wgu_ref[:, lo:lo + FF_TILE])
        up = _dot(xn, wgu_ref[:, D_FF + lo:D_FF + lo + FF_TILE])
        part = _dot((_silu(gate) * up).astype(BF16), wd_ref[lo:lo + FF_TILE, :])
        acc = part if acc is None else acc + part
    y = x + 0.5 * acc
    o_ref[...] = _rms(y, rest[0][...]) if len(rest) == 2 else y


def _ffn(x, nrm, wgu, wd, ls, tm, final_g=None):
    T = x.shape[0]
    resident = pl.Buffered(1)
    args = [x, nrm, wgu, wd]
    specs = [
        pl.BlockSpec((tm, D_MODEL), lambda i: (i, 0)),
        pl.BlockSpec((None, 1, D_MODEL), lambda i: (ls, 0, 0)),
        pl.BlockSpec((None, D_MODEL, 2 * D_FF), lambda i: (ls, 0, 0), pipeline_mode=resident),
        pl.BlockSpec((None, D_FF, D_MODEL), lambda i: (ls, 0, 0), pipeline_mode=resident),
    ]
    if final_g is not None:
        args.append(final_g)
        specs.append(pl.BlockSpec((1, D_MODEL), lambda i: (0, 0)))
    return pl.pallas_call(
        _ffn_body,
        grid=(T // tm,),
        in_specs=specs,
        out_specs=pl.BlockSpec((tm, D_MODEL), lambda i: (i, 0)),
        out_shape=jax.ShapeDtypeStruct((T, D_MODEL), F32),
        compiler_params=_cp(("parallel",)),
        name="ffn",
    )(*args)


_EVEN_PIECES = (SSD_INNER, SSD_CONV_DIM, GLA_KEY, GLA_KEY, GLA_VAL, GLA_VAL, SMALL)


def _even_in_body(x_ref, g_ref, w_ref, *out_refs):
    h = _rms(x_ref[...], g_ref[...]).astype(BF16)
    off = 0
    for ref, n in zip(out_refs, _EVEN_PIECES):
        ref[...] = _dot(h, w_ref[:, off:off + n])
        off += n


def _even_in(x, nrm, w, i, tm):
    T = x.shape[0]
    return pl.pallas_call(
        _even_in_body,
        grid=(T // tm,),
        in_specs=[
            pl.BlockSpec((tm, D_MODEL), lambda t: (t, 0)),
            pl.BlockSpec((None, 1, D_MODEL), lambda t: (2 * i, 0, 0)),
            pl.BlockSpec((None, D_MODEL, IN_PERM), lambda t: (i, 0, 0)),
        ],
        out_specs=[pl.BlockSpec((tm, n), lambda t: (t, 0)) for n in _EVEN_PIECES],
        out_shape=[jax.ShapeDtypeStruct((T, n), F32) for n in _EVEN_PIECES],
        compiler_params=_cp(("parallel",)),
        name="even_in",
    )(x, nrm, w)


def _out_proj_body(x_ref, y_ref, o_ref, w_ref, out_ref):
    acc = _dot(y_ref[...].astype(BF16), w_ref[0:SSD_INNER, :])
    acc = acc + _dot(o_ref[...].astype(BF16), w_ref[SSD_INNER:, :])
    out_ref[...] = x_ref[...] + acc


def _out_proj(x, y, o, w, i, tm):
    T = x.shape[0]
    tok = pl.BlockSpec((tm, D_MODEL), lambda t: (t, 0))
    return pl.pallas_call(
        _out_proj_body,
        grid=(T // tm,),
        in_specs=[tok, tok, tok,
                  pl.BlockSpec((None, SSD_INNER + GLA_VAL, D_MODEL), lambda t: (i, 0, 0))],
        out_specs=tok,
        out_shape=jax.ShapeDtypeStruct((T, D_MODEL), F32),
        compiler_params=_cp(("parallel",)),
        name="even_out",
    )(x, y, o, w)


def _ssd_seq_body(xbc_ref, sm_ref, z_ref, cw_ref, cb_ref, dtb_ref, alog_ref, dsk_ref, nw_ref,
                  y_ref, convn_ref, ssmn_ref,
                  cbuf_ref, xact_ref, acx_ref, dtx_ref, hT_ref, *, nblk):
    Lb = SEQ_BLOCK
    C = SSD_CHUNK
    j = pl.program_id(1)

    @pl.when(j == 0)
    def _():
        cbuf_ref[0:8, :] = jnp.zeros((8, SSD_CONV_DIM), F32)
        hT_ref[...] = jnp.zeros_like(hT_ref)

    x = xbc_ref[...]
    acc = cb_ref[...] + pltpu.roll(x, 3, axis=0) * cw_ref[0:1, :]
    acc = acc + pltpu.roll(x, 2, axis=0) * cw_ref[1:2, :]
    acc = acc + pltpu.roll(x, 1, axis=0) * cw_ref[2:3, :]
    acc = acc + x * cw_ref[3:4, :]
    xact_ref[...] = _silu(acc)
    cbuf_ref[8:16, :] = x[0:8, :]
    head = cb_ref[...] + cbuf_ref[5:13, :] * cw_ref[0:1, :]
    head = head + cbuf_ref[6:14, :] * cw_ref[1:2, :]
    head = head + cbuf_ref[7:15, :] * cw_ref[2:3, :]
    head = head + cbuf_ref[8:16, :] * cw_ref[3:4, :]
    xact_ref[0:8, :] = _silu(head)
    tail = x[Lb - 8:Lb, :]
    cbuf_ref[0:8, :] = tail

    @pl.when(j == nblk - 1)
    def _():
        convn_ref[...] = tail[5:8, :]

    dt = _softplus(sm_ref[...] + dtb_ref[...])
    dta = dt * (-jnp.exp(alog_ref[...]))
    ii = _iota((Lb, Lb), 0)
    jj = _iota((Lb, Lb), 1)
    tril = _onehot(((ii >> 6) == (jj >> 6)) & (ii >= jj))
    acum = _dot01_l(tril, dta)
    expand = _head_expand()
    acx_ref[...] = _dot01(acum, expand)
    dtx_ref[...] = _dot01(dt, expand)

    li = _iota((C, D_MODEL), 0)
    lj = _iota((C, D_MODEL), 1) & (C - 1)
    diag = li == lj
    causal = li >= lj
    ones_c = jnp.ones((C, C), BF16)
    tile8 = _onehot((_iota((C, 8 * C), 1) & (C - 1)) == _iota((C, 8 * C), 0))
    pair_mask = (_iota((LANES, LANES), 0) >> 6) == (_iota((LANES, LANES), 1) >> 6)
    half = SSD_INNER // SSD_GROUPS

    def chunk(c, carry):
        r0 = pl.multiple_of(c * C, C)
        xa = xact_ref[pl.ds(r0, C), :]
        xs = xa[:, :SSD_INNER]
        bms = [xa[:, SSD_INNER + g * SSD_D_STATE:SSD_INNER + (g + 1) * SSD_D_STATE] for g in range(2)]
        cms = [xa[:, SSD_INNER + (2 + g) * SSD_D_STATE:SSD_INNER + (3 + g) * SSD_D_STATE]
               for g in range(2)]
        acx = acx_ref[pl.ds(r0, C), :]
        dtx = dtx_ref[pl.ds(r0, C), :]
        alast = acx[C - 1:C, :]
        arow = _dot01_l(ones_c, jnp.where(diag, acx, 0.0))
        dtrow = _dot01_l(ones_c, jnp.where(diag, dtx, 0.0))
        decay = jnp.exp(jnp.where(causal, acx - arow, -jnp.inf))
        cbt = []
        for g in range(2):
            cb = lax.dot_general(cms[g].astype(BF16), bms[g].astype(BF16),
                                 (((1,), (1,)), ((), ())), preferred_element_type=F32)
            cbt.append(_dot01(cb, tile8))
        wts = (jnp.concatenate(cbt, axis=1) * decay * dtrow).astype(BF16)
        ys = []
        for p in range(SSD_HEADS // 2):
            xp = xs[:, p * LANES:(p + 1) * LANES]
            xbd = jnp.where(pair_mask, jnp.concatenate([xp, xp], axis=0), 0.0).astype(BF16)
            ys.append(_dot(wts[:, p * LANES:(p + 1) * LANES], xbd))
        y = jnp.concatenate(ys, axis=1)
        hT = hT_ref[...]
        hTb = hT.astype(BF16)
        yi = [_dot(cms[g].astype(BF16), hTb[:, g * half:(g + 1) * half]) for g in range(2)]
        y = y + jnp.concatenate(yi, axis=1) * jnp.exp(acx)
        xsc = (jnp.exp(alast - acx) * dtx * xs).astype(BF16)
        st = [lax.dot_general(bms[g].astype(BF16), xsc[:, g * half:(g + 1) * half],
                              (((0,), (0,)), ((), ())), preferred_element_type=F32) for g in range(2)]
        hT_ref[...] = hT * jnp.exp(alast) + jnp.concatenate(st, axis=1)
        y = y + dsk_ref[...] * xs
        y = y * _silu(z_ref[pl.ds(r0, C), :])
        outs = []
        for g in range(2):
            yg = y[:, g * half:(g + 1) * half]
            ms = jnp.mean(yg * yg, axis=-1, keepdims=True)
            outs.append(yg * lax.rsqrt(ms + EPS))
        y_ref[pl.ds(r0, C), :] = jnp.concatenate(outs, axis=1) * nw_ref[...]
        return carry

    lax.fori_loop(0, Lb // C, chunk, 0, unroll=2)

    @pl.when(j == nblk - 1)
    def _():
        ssmn_ref[...] = hT_ref[...].T


def _ssd_seq(xbc, sm, z, P, i, B, L):
    Lb = SEQ_BLOCK
    nblk = L // Lb
    T = B * L
    tok = lambda n: pl.BlockSpec((Lb, n), lambda b, j: (b * nblk + j, 0))
    par = lambda r, n: pl.BlockSpec((None, r, n), lambda b, j: (i, 0, 0))
    y, convn, ssmn = pl.pallas_call(
        functools.partial(_ssd_seq_body, nblk=nblk),
        grid=(B, nblk),
        in_specs=[tok(SSD_CONV_DIM), tok(SMALL), tok(SSD_INNER),
                  par(SSD_CONV, SSD_CONV_DIM), par(1, SSD_CONV_DIM), par(1, SMALL), par(1, SMALL),
                  par(1, SSD_INNER), par(1, SSD_INNER)],
        out_specs=[tok(SSD_INNER),
                   pl.BlockSpec((None, SSD_CONV - 1, SSD_CONV_DIM), lambda b, j: (b, 0, 0)),
                   pl.BlockSpec((None, SSD_INNER, SSD_D_STATE), lambda b, j: (b, 0, 0))],
        out_shape=[jax.ShapeDtypeStruct((T, SSD_INNER), F32),
                   jax.ShapeDtypeStruct((B, SSD_CONV - 1, SSD_CONV_DIM), F32),
                   jax.ShapeDtypeStruct((B, SSD_INNER, SSD_D_STATE), F32)],
        scratch_shapes=[pltpu.VMEM((16, SSD_CONV_DIM), F32),
                        pltpu.VMEM((Lb, SSD_CONV_DIM), F32),
                        pltpu.VMEM((Lb, SSD_INNER), F32),
                        pltpu.VMEM((Lb, SSD_INNER), F32),
                        pltpu.VMEM((SSD_D_STATE, SSD_INNER), F32)],
        compiler_params=_cp(("parallel", "arbitrary")),
        name="ssd_seq",
    )(xbc, sm, z, P["conv_w"], P["conv_b"], P["dt_bias"], P["a_log"], P["d_skip_x"], P["ssd_norm"])
    return y, convn, ssmn.reshape(B, SSD_HEADS, SSD_HEAD_DIM, SSD_D_STATE)


def _gla_seq_body(q_ref, k_ref, v_ref, og_ref, sm_ref, w2_ref, gb_ref, nw_ref,
                  o_ref, glan_ref,
                  st_ref, dec_ref, qg_ref, kd_ref, oi_ref, *, nblk):
    Lb = SEQ_BLOCK
    j = pl.program_id(1)

    @pl.when(j == 0)
    def _():
        st_ref[...] = jnp.zeros_like(st_ref)

    gate = _dot(sm_ref[...].astype(BF16), w2_ref[...]) + gb_ref[...]
    lg = -_softplus_abs(-gate) / GLA_GATE_NORM
    ii = _iota((Lb, Lb), 0)
    jj = _iota((Lb, Lb), 1)
    same = (ii >> 4) == (jj >> 4)
    causal = same & (ii >= jj)
    b = _dot01_l(_onehot(causal), lg)
    blast = _dot01_l(_onehot(same), lg)
    qg = q_ref[...] * (GLA_DK ** -0.5) * jnp.exp(b)
    kk = k_ref[...]
    kg = kk * jnp.exp(-b)
    kd = kk * jnp.exp(blast - b)
    dec_ref[...] = jnp.exp(blast)
    qg_ref[...] = qg.astype(BF16)
    kd_ref[...] = kd.astype(BF16)

    def chunk(c, carry):
        r0 = pl.multiple_of(c * GLA_CHUNK, GLA_CHUNK)
        for h in range(GLA_HEADS):
            ks = slice(h * GLA_DK, (h + 1) * GLA_DK)
            vs = slice(h * GLA_DV, (h + 1) * GLA_DV)
            st = st_ref[h]
            oi_ref[pl.ds(r0, GLA_CHUNK), vs] = lax.dot_general(
                qg_ref[pl.ds(r0, GLA_CHUNK), ks], st.astype(BF16),
                (((1,), (1,)), ((), ())), preferred_element_type=F32)
            upd = lax.dot_general(v_ref[pl.ds(r0, GLA_CHUNK), vs].astype(BF16),
                                  kd_ref[pl.ds(r0, GLA_CHUNK), ks],
                                  (((0,), (0,)), ((), ())), preferred_element_type=F32)
            st_ref[h] = st * dec_ref[pl.ds(r0, 8), ks][0:1, :] + upd
        return carry

    lax.fori_loop(0, Lb // GLA_CHUNK, chunk, 0, unroll=8)

    for h in range(GLA_HEADS):
        ks = slice(h * GLA_DK, (h + 1) * GLA_DK)
        vs = slice(h * GLA_DV, (h + 1) * GLA_DV)
        att = lax.dot_general(qg[:, ks].astype(BF16), kg[:, ks].astype(BF16),
                              (((1,), (1,)), ((), ())), preferred_element_type=F32)
        att = jnp.where(causal, att, 0.0)
        o_h = _dot(att.astype(BF16), v_ref[:, vs].astype(BF16)) + oi_ref[:, vs]
        ms = jnp.mean(o_h * o_h, axis=-1, keepdims=True)
        o_h = o_h * lax.rsqrt(ms + EPS) * nw_ref[...]
        o_ref[:, vs] = o_h * _silu(og_ref[:, vs])

    @pl.when(j == nblk - 1)
    def _():
        for h in range(GLA_HEADS):
            glan_ref[h] = st_ref[h].T


def _gla_seq(q, k, v, og, sm, P, i, B, L):
    Lb = SEQ_BLOCK
    nblk = L // Lb
    T = B * L
    tok = lambda n: pl.BlockSpec((Lb, n), lambda b, j: (b * nblk + j, 0))
    par = lambda r, n: pl.BlockSpec((None, r, n), lambda b, j: (i, 0, 0))
    return pl.pallas_call(
        functools.partial(_gla_seq_body, nblk=nblk),
        grid=(B, nblk),
        in_specs=[tok(GLA_KEY), tok(GLA_KEY), tok(GLA_VAL), tok(GLA_VAL), tok(SMALL),
                  par(SMALL, GLA_KEY), par(1, GLA_KEY), par(1, GLA_DV)],
        out_specs=[tok(GLA_VAL),
                   pl.BlockSpec((None, GLA_HEADS, GLA_DK, GLA_DV), lambda b, j: (b, 0, 0, 0))],
        out_shape=[jax.ShapeDtypeStruct((T, GLA_VAL), F32),
                   jax.ShapeDtypeStruct((B, GLA_HEADS, GLA_DK, GLA_DV), F32)],
        scratch_shapes=[pltpu.VMEM((GLA_HEADS, GLA_DV, GLA_DK), F32),
                        pltpu.VMEM((Lb, GLA_KEY), F32),
                        pltpu.VMEM((Lb, GLA_KEY), BF16),
                        pltpu.VMEM((Lb, GLA_KEY), BF16),
                        pltpu.VMEM((Lb, GLA_VAL), F32)],
        compiler_params=_cp(("parallel", "arbitrary")),
        name="gla_seq",
    )(q, k, v, og, sm, P["gate_w2"], P["gate_b"], P["gla_norm"])


def _even_step_pre_body(xbc_ref, c0_ref, sm_ref, cw_ref, cb_ref, dtb_ref, alog_ref, dsk_ref,
                        w2_ref, gb_ref,
                        xact_ref, convn_ref, dax_ref, c2_ref, ybase_ref, eg_ref):
    u = xbc_ref[...]
    n = SSD_CONV_DIM
    acc = cb_ref[...] + c0_ref[:, 0:n] * cw_ref[0:1, :]
    acc = acc + c0_ref[:, n:2 * n] * cw_ref[1:2, :]
    acc = acc + c0_ref[:, 2 * n:3 * n] * cw_ref[2:3, :]
    acc = acc + u * cw_ref[3:4, :]
    xa = _silu(acc)
    xact_ref[...] = xa
    convn_ref[:, 0:n] = c0_ref[:, n:2 * n]
    convn_ref[:, n:2 * n] = c0_ref[:, 2 * n:3 * n]
    convn_ref[:, 2 * n:3 * n] = u
    dt = _softplus(sm_ref[...] + dtb_ref[...])
    dta = dt * (-jnp.exp(alog_ref[...]))
    expand = _head_expand()
    dtx = _dot01(dt, expand)
    dax_ref[...] = jnp.exp(_dot01(dta, expand))
    xs = xa[:, :SSD_INNER]
    c2_ref[...] = dtx * xs
    half = SSD_INNER // SSD_GROUPS
    cbs = []
    for g in range(2):
        bm = xa[:, SSD_INNER + g * SSD_D_STATE:SSD_INNER + (g + 1) * SSD_D_STATE]
        cm = xa[:, SSD_INNER + (2 + g) * SSD_D_STATE:SSD_INNER + (3 + g) * SSD_D_STATE]
        cb = jnp.sum(cm * bm, axis=-1, keepdims=True)
        cbs.append(jnp.broadcast_to(cb, (cb.shape[0], half)))
    ybase_ref[...] = jnp.concatenate(cbs, axis=1) * dtx * xs + dsk_ref[...] * xs
    gate = _dot(sm_ref[...].astype(BF16), w2_ref[...]) + gb_ref[...]
    eg_ref[...] = jnp.exp(-_softplus(-gate) / GLA_GATE_NORM)


def _even_step_pre(xbc, conv0, sm, P, i):
    Bn = xbc.shape[0]
    full = lambda n: pl.BlockSpec((Bn, n), lambda t: (0, 0))
    par = lambda r, n: pl.BlockSpec((None, r, n), lambda t: (i, 0, 0))
    outs = (SSD_CONV_DIM, 3 * SSD_CONV_DIM, SSD_INNER, SSD_INNER, SSD_INNER, GLA_KEY)
    return pl.pallas_call(
        _even_step_pre_body,
        grid=(1,),
        in_specs=[full(SSD_CONV_DIM), full(3 * SSD_CONV_DIM), full(SMALL),
                  par(SSD_CONV, SSD_CONV_DIM), par(1, SSD_CONV_DIM), par(1, SMALL), par(1, SMALL),
                  par(1, SSD_INNER), par(SMALL, GLA_KEY), par(1, GLA_KEY)],
        out_specs=[full(n) for n in outs],
        out_shape=[jax.ShapeDtypeStruct((Bn, n), F32) for n in outs],
        compiler_params=_cp(("arbitrary",)),
        name="even_step_pre",
    )(xbc, conv0, sm, P["conv_w"], P["conv_b"], P["dt_bias"], P["a_log"], P["d_skip_x"],
      P["gate_w2"], P["gate_b"])


def _to_cols(x):
    Bn, N = x.shape
    return x.reshape(Bn // STEP_BATCH, STEP_BATCH, N).transpose(0, 2, 1)


def _from_cols(x):
    nb, N, bb = x.shape
    return x.transpose(0, 2, 1).reshape(nb * bb, N)


def _layer_state_call(body, name, state_all, acc, i, rows, width, args, specs, out_spec, out_shape):
    nl, Bn = state_all.shape[:2]
    st = pl.BlockSpec((None, STEP_BATCH, rows, width), lambda t: (i, t, 0, 0))
    args = [state_all.reshape(nl, Bn, rows, width)] + list(args)
    specs = [st] + list(specs)
    aliases = {}
    if acc is not None:
        aliases = {len(args): 0}
        args.append(acc)
        specs.append(pl.BlockSpec(memory_space=pl.ANY))
    return pl.pallas_call(
        body,
        grid=(Bn // STEP_BATCH,),
        in_specs=specs,
        out_specs=[st, out_spec],
        out_shape=[jax.ShapeDtypeStruct((nl, Bn, rows, width), F32), out_shape],
        input_output_aliases=aliases,
        compiler_params=_cp(("parallel",)),
        name=name,
    )(*args)


def _ssd_step_body(h_ref, c1_ref, c2_ref, bm_ref, cm_ref, *rest):
    hn_ref, y_ref = rest[-2:]
    half = SSD_INNER // SSD_GROUPS
    lane = _iota((SSD_INNER, STEP_BATCH), 1)
    ycols = jnp.zeros((SSD_INNER, STEP_BATCH), F32)
    c1 = c1_ref[...]
    c2 = c2_ref[...]
    for b in range(STEP_BATCH):
        h0 = h_ref[b]
        bmat = jnp.concatenate(
            [jnp.broadcast_to(bm_ref[b:b + 1, g * SSD_D_STATE:(g + 1) * SSD_D_STATE],
                              (half, SSD_D_STATE)) for g in range(2)], axis=0)
        cmat = jnp.concatenate(
            [jnp.broadcast_to(cm_ref[b:b + 1, g * SSD_D_STATE:(g + 1) * SSD_D_STATE],
                              (half, SSD_D_STATE)) for g in range(2)], axis=0)
        ycol = jnp.sum(h0 * cmat, axis=1, keepdims=True)
        ycols = jnp.where(lane == b, ycol, ycols)
        hn_ref[b] = h0 * c1[:, b:b + 1] + c2[:, b:b + 1] * bmat
    y_ref[...] = ycols


def _ssd_step(state_all, acc, i, dax, c2, xact):
    Bn = state_all.shape[1]
    bb = STEP_BATCH
    bm = xact[:, SSD_INNER:SSD_INNER + 2 * SSD_D_STATE]
    cm = xact[:, SSD_INNER + 2 * SSD_D_STATE:]
    col = pl.BlockSpec((None, SSD_INNER, bb), lambda t: (t, 0, 0))
    row = pl.BlockSpec((bb, 2 * SSD_D_STATE), lambda t: (t, 0))
    acc, ycols = _layer_state_call(
        _ssd_step_body, "ssd_step", state_all, acc, i, SSD_INNER, SSD_D_STATE,
        [_to_cols(dax), _to_cols(c2), bm, cm], [col, col, row, row],
        col, jax.ShapeDtypeStruct((Bn // bb, SSD_INNER, bb), F32))
    return acc, _from_cols(ycols)


def _gla_step_body(s_ref, eg_ref, kc_ref, qc_ref, v_ref, *rest):
    sn_ref, o_ref = rest[-2:]
    eg = eg_ref[...]
    kc = kc_ref[...]
    qc = qc_ref[...] * (GLA_DK ** -0.5)
    for b in range(STEP_BATCH):
        s0 = s_ref[b]
        vmat = jnp.concatenate(
            [jnp.broadcast_to(v_ref[b:b + 1, h * GLA_DV:(h + 1) * GLA_DV], (GLA_DK, GLA_DV))
             for h in range(GLA_HEADS)], axis=0)
        sn = s0 * eg[:, b:b + 1] + kc[:, b:b + 1] * vmat
        sn_ref[b] = sn
        t = sn * qc[:, b:b + 1]
        o_ref[b:b + 1, :] = jnp.concatenate(
            [jnp.sum(t[h * GLA_DK:(h + 1) * GLA_DK], axis=0, keepdims=True)
             for h in range(GLA_HEADS)], axis=1)


def _gla_step(state_all, acc, i, eg, k, q, v):
    Bn = state_all.shape[1]
    bb = STEP_BATCH
    col = pl.BlockSpec((None, GLA_KEY, bb), lambda t: (t, 0, 0))
    row = pl.BlockSpec((bb, GLA_VAL), lambda t: (t, 0))
    return _layer_state_call(
        _gla_step_body, "gla_step", state_all, acc, i, GLA_KEY, GLA_DV,
        [_to_cols(eg), _to_cols(k), _to_cols(q), v], [col, col, col, row],
        row, jax.ShapeDtypeStruct((Bn, GLA_VAL), F32))


def _even_step_post_body(x_ref, ybase_ref, yi_ref, dax_ref, z_ref, o_ref, og_ref,
                         snw_ref, gnw_ref, w_ref, out_ref):
    y = (ybase_ref[...] + yi_ref[...] * dax_ref[...]) * _silu(z_ref[...])
    half = SSD_INNER // SSD_GROUPS
    acc = None
    for g in range(2):
        yg = y[:, g * half:(g + 1) * half]
        ms = jnp.mean(yg * yg, axis=-1, keepdims=True)
        yn = yg * lax.rsqrt(ms + EPS) * snw_ref[:, g * half:(g + 1) * half]
        part = _dot(yn.astype(BF16), w_ref[g * half:(g + 1) * half, :])
        acc = part if acc is None else acc + part
    for h in range(GLA_HEADS):
        vs = slice(h * GLA_DV, (h + 1) * GLA_DV)
        oh = o_ref[:, vs]
        ms = jnp.mean(oh * oh, axis=-1, keepdims=True)
        on = oh * lax.rsqrt(ms + EPS) * gnw_ref[...] * _silu(og_ref[:, vs])
        acc = acc + _dot(on.astype(BF16), w_ref[SSD_INNER + h * GLA_DV:SSD_INNER + (h + 1) * GLA_DV, :])
    out_ref[...] = x_ref[...] + acc


def _even_step_post(x, ybase, yi, dax, z, o, og, P, w_out, i):
    Bn = x.shape[0]
    full = pl.BlockSpec((Bn, D_MODEL), lambda t: (0, 0))
    return pl.pallas_call(
        _even_step_post_body,
        grid=(1,),
        in_specs=[full] * 7 + [pl.BlockSpec((None, 1, SSD_INNER), lambda t: (i, 0, 0)),
                               pl.BlockSpec((None, 1, GLA_DV), lambda t: (i, 0, 0)),
                               pl.BlockSpec((None, SSD_INNER + GLA_VAL, D_MODEL), lambda t: (i, 0, 0))],
        out_specs=full,
        out_shape=jax.ShapeDtypeStruct((Bn, D_MODEL), F32),
        compiler_params=_cp(("arbitrary",)),
        name="even_step_post",
    )(x, ybase, yi, dax, z, o, og, P["ssd_norm"], P["gla_norm"], w_out)


_ODD_OUTS = 8


def _odd_pre_body(*refs, step, has_vfirst, nblk_seq, tm):
    it = iter(refs)
    x_ref = next(it)
    if step:
        prev_ref = next(it)
    else:
        xp_ref = next(it)
        sh0_ref = next(it)
    g_ref, mix_ref = next(it), next(it)
    wr_ref, wk_ref, wv_ref = next(it), next(it), next(it)
    w0_ref, w1_ref, w2_ref = next(it), next(it), next(it)
    a0_ref, a1_ref, a2_ref = next(it), next(it), next(it)
    if has_vfirst:
        vf_ref, v0_ref, v1_ref, v2_ref = next(it), next(it), next(it), next(it)
    g1_ref, g2_ref = next(it), next(it)
    kkw_ref, kaw_ref = next(it), next(it)
    h_ref, r_ref, dec_ref, k_ref, v_ref, an_ref, b_ref, gg_ref = (next(it) for _ in range(_ODD_OUTS))

    h = _rms(x_ref[...], g_ref[...])
    if step:
        prev = prev_ref[...]
    else:
        i = pl.program_id(0)
        hp = _rms(xp_ref[...], g_ref[...])[7:8, :]
        prow = jnp.where(i % nblk_seq == 0, sh0_ref[...], hp)
        prev = jnp.where(_iota((tm, D_MODEL), 0) == 0, prow, pltpu.roll(h, 1, axis=0))
    h_ref[...] = h
    xx = prev - h
    mixed = lambda r: (h + xx * mix_ref[r:r + 1, :]).astype(BF16)
    xr, xw, xk, xv, xa, xg = (mixed(r) for r in range(6))
    r_ref[...] = _dot(xr, wr_ref[...])
    wl = w0_ref[...] + _dot(jnp.tanh(_dot(xw, w1_ref[...])).astype(BF16), w2_ref[...])
    w = -_softplus_abs(-wl) - 0.5
    dec_ref[...] = jnp.exp(-jnp.exp(w))
    k = _dot(xk, wk_ref[...])
    v = _dot(xv, wv_ref[...])
    if has_vfirst:
        gate = jax.nn.sigmoid(v0_ref[...] + _dot(_dot(xv, v1_ref[...]).astype(BF16), v2_ref[...]))
        v = v + (vf_ref[...] - v) * gate
    v_ref[...] = v
    a = jax.nn.sigmoid(a0_ref[...] + _dot(_dot(xa, a1_ref[...]).astype(BF16), a2_ref[...]))
    gg_ref[...] = _dot(jax.nn.sigmoid(_dot(xg, g1_ref[...])).astype(BF16), g2_ref[...])
    kkf = k * kkw_ref[...]
    red = _head_reduce()
    expand = _head_expand()
    ss = _dot01x2(kkf * kkf, jnp.concatenate([red, red], axis=0))
    kk = kkf * _dot01x2(lax.rsqrt(jnp.maximum(ss, 1e-24)), jnp.concatenate([expand, expand], axis=0))
    k_ref[...] = k * (1.0 + (a - 1.0) * kaw_ref[...])
    an_ref[...] = -kk
    b_ref[...] = kk * a


def _odd_pre(x, prev_or_shift0, vfirst, P, i, tm, L):
    T = x.shape[0]
    step = L == 1
    has_vfirst = vfirst is not None
    nblk_seq = max(L // tm, 1)
    tok = pl.BlockSpec((tm, D_MODEL), lambda t: (t, 0))
    par = lambda r, n, idx=i: pl.BlockSpec((None, r, n), lambda t: (idx, 0, 0))
    args = [x]
    specs = [tok]
    if step:
        args.append(prev_or_shift0)
        specs.append(tok)
    else:
        args += [x, prev_or_shift0]
        specs += [pl.BlockSpec((8, D_MODEL), lambda t: (jnp.maximum(t * (tm // 8) - 1, 0), 0)),
                  pl.BlockSpec((None, 1, D_MODEL), lambda t: (t // nblk_seq, 0, 0))]
    args += [P["mix_norm"], P["mix"], P["w_r"], P["w_k"], P["w_v"], P["w0"], P["w1"], P["w2"],
             P["a0"], P["a1"], P["a2"]]
    specs += [pl.BlockSpec((None, 1, D_MODEL), lambda t: (2 * i + 1, 0, 0)),
              par(8, D_MODEL), par(D_MODEL, D_MODEL), par(D_MODEL, D_MODEL), par(D_MODEL, D_MODEL),
              par(1, D_MODEL), par(D_MODEL, LANES), par(LANES, D_MODEL),
              par(1, D_MODEL), par(D_MODEL, LANES), par(LANES, D_MODEL)]
    if has_vfirst:
        args += [vfirst, P["v0"], P["v1"], P["v2"]]
        specs += [tok, par(1, D_MODEL, i - 1), par(D_MODEL, LANES, i - 1), par(LANES, D_MODEL, i - 1)]
    args += [P["g1"], P["g2"], P["k_k"], P["k_a"]]
    specs += [par(D_MODEL, 2 * LANES), par(2 * LANES, D_MODEL), par(1, D_MODEL), par(1, D_MODEL)]
    return pl.pallas_call(
        functools.partial(_odd_pre_body, step=step, has_vfirst=has_vfirst, nblk_seq=nblk_seq, tm=tm),
        grid=(T // tm,),
        in_specs=specs,
        out_specs=[tok] * _ODD_OUTS,
        out_shape=[jax.ShapeDtypeStruct((T, D_MODEL), F32)] * _ODD_OUTS,
        compiler_params=_cp(("parallel",)),
        name="odd_pre",
    )(*args)


def _dot01x2(x, e2):
    hi = x.astype(BF16)
    mid = (x - hi.astype(F32)).astype(BF16)
    return _dot(jnp.concatenate([hi, mid], axis=1), e2)


def _odd_post_body(x_ref, o_ref, r_ref, k_ref, v_ref, g_ref, gnw_ref, gnb_ref, rk_ref, wo_ref, out_ref):
    red = _head_reduce()
    red = jnp.concatenate([red, red], axis=0)
    expand = _head_expand()
    expand = jnp.concatenate([expand, expand], axis=0)
    o = o_ref[...]
    mu = _dot01x2(_dot01x2(o, red) * (1.0 / RWKV_HEAD), expand)
    d = o - mu
    var = _dot01x2(d * d, red) * (1.0 / RWKV_HEAD)
    on = d * _dot01x2(lax.rsqrt(var + RWKV_GN_EPS), expand) * gnw_ref[...] + gnb_ref[...]
    v = v_ref[...]
    bonus = _dot01x2(_dot01x2(r_ref[...] * k_ref[...] * rk_ref[...], red), expand) * v
    out_ref[...] = x_ref[...] + _dot(((on + bonus) * g_ref[...]).astype(BF16), wo_ref[...])


def _odd_post(x, o, r, k, v, g, P, i, tm):
    T = x.shape[0]
    tok = pl.BlockSpec((tm, D_MODEL), lambda t: (t, 0))
    par = lambda r_, n: pl.BlockSpec((None, r_, n), lambda t: (i, 0, 0))
    return pl.pallas_call(
        _odd_post_body,
        grid=(T // tm,),
        in_specs=[tok] * 6 + [par(1, D_MODEL), par(1, D_MODEL), par(1, D_MODEL), par(D_MODEL, D_MODEL)],
        out_specs=tok,
        out_shape=jax.ShapeDtypeStruct((T, D_MODEL), F32),
        compiler_params=_cp(("parallel",)),
        name="odd_post",
    )(x, o, r, k, v, g, P["gn_w"], P["gn_b"], P["r_k"], P["w_o"])


RWKV_TB = 64
RWKV_NPAIR = RWKV_HEADS // 2


RWKV_SEQS = 2
RWKV_UNROLL = 8


def _rwkv_seq_body(w_ref, k_ref, a_ref, b_ref, r_ref, v_ref, o_ref, sn_ref, s_ref, vt_ref, oacc_ref, *, ntb):
    N = RWKV_HEAD
    R = RWKV_NPAIR * N
    tbi = pl.program_id(1)

    @pl.when(tbi == 0)
    def _():
        s_ref[...] = jnp.zeros_like(s_ref)

    same_head = _onehot((_iota((LANES, LANES), 0) >> 6) == (_iota((LANES, LANES), 1) >> 6))
    seg = jnp.concatenate([same_head, same_head], axis=0)
    for c in range(RWKV_SEQS):
        for p in range(RWKV_NPAIR):
            bt = v_ref[c, :, p * LANES:(p + 1) * LANES].T
            vt_ref[c, p * N:(p + 1) * N, :] = jnp.concatenate([bt[0:N], bt[N:2 * N]], axis=1)
    oacc_ref[...] = jnp.zeros_like(oacc_ref)
    lane_step = _iota((R, LANES), 1) & (N - 1)
    lane_base = (_iota((R, LANES), 1) >> 6) * N

    def steps(tu, carry):
        row0 = pl.multiple_of(tu * RWKV_UNROLL, RWKV_UNROLL)
        for j in range(RWKV_UNROLL):
            t = tu * RWKV_UNROLL + j

            def rows(ref, c, t):
                return jnp.concatenate(
                    [jnp.broadcast_to(ref[c, pl.ds(row0, RWKV_UNROLL), p * LANES:(p + 1) * LANES][j:j + 1, :],
                                      (N, LANES)) for p in range(RWKV_NPAIR)], axis=0)

            for c in range(RWKV_SEQS):
                s = s_ref[c]
                x = s * rows(a_ref, c, t)
                hi = x.astype(BF16)
                mid = (x - hi.astype(F32)).astype(BF16)
                sa = _dot(jnp.concatenate([hi, mid], axis=1), seg)
                vcol = jnp.take_along_axis(vt_ref[c], lane_base + t, axis=1)
                s = s * rows(w_ref, c, t) + sa * rows(b_ref, c, t) + vcol * rows(k_ref, c, t)
                s_ref[c] = s
                y = s * rows(r_ref, c, t)
                yh = y.astype(BF16)
                ym = (y - yh.astype(F32)).astype(BF16)
                ob = _dot(jnp.concatenate([yh, ym], axis=1), seg)
                oacc_ref[c] = jnp.where(lane_step == t, ob, oacc_ref[c])
        return carry

    lax.fori_loop(0, RWKV_TB // RWKV_UNROLL, steps, 0)

    for c in range(RWKV_SEQS):
        for p in range(RWKV_NPAIR):
            ot = oacc_ref[c, p * N:(p + 1) * N, :].T
            o_ref[c, :, p * LANES:(p + 1) * LANES] = jnp.concatenate([ot[0:N], ot[N:2 * N]], axis=1)

    @pl.when(tbi == ntb - 1)
    def _():
        for c in range(RWKV_SEQS):
            for p in range(RWKV_NPAIR):
                for hh in range(2):
                    sn_ref[c, (2 * p + hh) * N:(2 * p + hh + 1) * N, :] = (
                        s_ref[c, p * N:(p + 1) * N, hh * N:(hh + 1) * N])


def _rwkv_seq(w, k, a, b, r, v, B, L):
    ntb = L // RWKV_TB
    nc = RWKV_SEQS
    as_seqs = lambda x: x.reshape(B, L, D_MODEL)
    tok = pl.BlockSpec((nc, RWKV_TB, D_MODEL), lambda bi, t: (bi, t, 0))
    state_rows = RWKV_NPAIR * RWKV_HEAD
    o, sn = pl.pallas_call(
        functools.partial(_rwkv_seq_body, ntb=ntb),
        grid=(B // nc, ntb),
        in_specs=[tok] * 6,
        out_specs=[tok, pl.BlockSpec((nc, D_MODEL, RWKV_HEAD), lambda bi, t: (bi, 0, 0))],
        out_shape=[jax.ShapeDtypeStruct((B, L, D_MODEL), F32),
                   jax.ShapeDtypeStruct((B, D_MODEL, RWKV_HEAD), F32)],
        scratch_shapes=[pltpu.VMEM((nc, state_rows, LANES), F32)] * 3,
        compiler_params=_cp(("parallel", "arbitrary")),
        name="rwkv_seq",
    )(as_seqs(w), as_seqs(k), as_seqs(a), as_seqs(b), as_seqs(r), as_seqs(v))
    return o.reshape(B * L, D_MODEL), sn.reshape(B, RWKV_HEADS, RWKV_HEAD, RWKV_HEAD)


def _rwkv_step_body(s_ref, w_ref, k_ref, a_ref, b_ref, r_ref, vc_ref, *rest):
    sn_ref, o_ref = rest[-2:]
    N = RWKV_HEAD
    lane = _iota((D_MODEL, STEP_BATCH), 1)
    ocols = jnp.zeros((D_MODEL, STEP_BATCH), F32)
    vc = vc_ref[...]

    def rows(ref, b):
        return jnp.concatenate(
            [jnp.broadcast_to(ref[b:b + 1, h * N:(h + 1) * N], (N, N)) for h in range(RWKV_HEADS)], axis=0)

    for b in range(STEP_BATCH):
        s = s_ref[b]
        sa = jnp.sum(s * rows(a_ref, b), axis=1, keepdims=True)
        s = s * rows(w_ref, b) + sa * rows(b_ref, b) + vc[:, b:b + 1] * rows(k_ref, b)
        sn_ref[b] = s
        ocol = jnp.sum(s * rows(r_ref, b), axis=1, keepdims=True)
        ocols = jnp.where(lane == b, ocol, ocols)
    o_ref[...] = ocols


def _rwkv_step(state_all, acc, i, w, k, a, b, r, v):
    Bn = state_all.shape[1]
    bb = STEP_BATCH
    col = pl.BlockSpec((None, D_MODEL, bb), lambda t: (t, 0, 0))
    row = pl.BlockSpec((bb, D_MODEL), lambda t: (t, 0))
    acc, ocols = _layer_state_call(
        _rwkv_step_body, "rwkv_step", state_all, acc, i, D_MODEL, RWKV_HEAD,
        [w, k, a, b, r, _to_cols(v)], [row] * 5 + [col],
        col, jax.ShapeDtypeStruct((Bn // bb, D_MODEL, bb), F32))
    return _from_cols(ocols), acc


def _pad_to(w, axis, n):
    pad = [(0, 0)] * w.ndim
    pad[axis] = (0, n - w.shape[axis])
    return jnp.pad(w, pad)


def _prepare(W):
    bf = lambda w: w.astype(BF16)
    row = lambda w: w[:, None, :]
    sizes = [SSD_INNER, SSD_CONV_DIM, SSD_HEADS, GLA_KEY, GLA_KEY, GLA_VAL, GLA_VAL, GLA_GATE_RANK]
    offs = [0]
    for s in sizes:
        offs.append(offs[-1] + s)
    piece = lambda n: W["ev_w_in"][:, :, offs[n]:offs[n + 1]]
    w_in = jnp.concatenate([piece(0), piece(1), piece(3), piece(4), piece(5), piece(6),
                            _pad_to(jnp.concatenate([piece(2), piece(7)], axis=-1), 2, SMALL)], axis=-1)
    gate_w2 = jnp.pad(W["ev_gla_gate_w2"], ((0, 0), (SSD_HEADS, SMALL - SSD_HEADS - GLA_GATE_RANK), (0, 0)))
    even = dict(
        w_in=bf(w_in), w_out=bf(W["ev_w_out"]),
        conv_w=W["ev_conv_w"], conv_b=row(W["ev_conv_b"]),
        dt_bias=row(_pad_to(W["ev_dt_bias"], 1, SMALL)), a_log=row(_pad_to(W["ev_a_log"], 1, SMALL)),
        d_skip_x=row(jnp.repeat(W["ev_d_skip"], SSD_HEAD_DIM, axis=1)),
        ssd_norm=row(W["ev_ssd_norm"]), gate_w2=bf(gate_w2), gate_b=row(W["ev_gla_gate_b"]),
        gla_norm=row(W["ev_gla_norm"]),
    )
    odd = dict(
        mix_norm=row(W["mix_norm"]), mix=_pad_to(W["od_mix"], 1, 8),
        w_r=bf(W["od_w_r"]), w_k=bf(W["od_w_k"]), w_v=bf(W["od_w_v"]), w_o=bf(W["od_w_o"]),
        w0=row(W["od_w0"]), w1=bf(_pad_to(W["od_w1"], 2, LANES)), w2=bf(_pad_to(W["od_w2"], 1, LANES)),
        a0=row(W["od_a0"]), a1=bf(_pad_to(W["od_a1"], 2, LANES)), a2=bf(_pad_to(W["od_a2"], 1, LANES)),
        v0=row(W["od_v0"]), v1=bf(_pad_to(W["od_v1"], 2, LANES)), v2=bf(_pad_to(W["od_v2"], 1, LANES)),
        g1=bf(_pad_to(W["od_g1"], 2, 2 * LANES)), g2=bf(_pad_to(W["od_g2"], 1, 2 * LANES)),
        k_k=row(W["od_k_k"]), k_a=row(W["od_k_a"]), r_k=row(W["od_r_k"]),
        gn_w=row(W["od_gn_w"]), gn_b=row(W["od_gn_b"]),
    )
    ffn = dict(
        norm=W["ffn_norm"].reshape(DEPTH * 2, 1, D_MODEL),
        w_gu=bf(W["ffn_w_gu"]).reshape(DEPTH * 2, D_MODEL, 2 * D_FF),
        w_down=bf(W["ffn_w_down"]).reshape(DEPTH * 2, D_FF, D_MODEL),
    )
    return dict(even=even, odd=odd, ffn=ffn, mix_norm=row(W["mix_norm"]),
                final_norm=W["final_norm"][None, :])


def _run_group(x3, states, Wp):
    B, L, _ = x3.shape
    T = B * L
    step = L == 1
    x = x3.reshape(T, D_MODEL)
    tm_ffn = min(512, T)
    tm_proj = min(256, T)
    ffn, even, odd = Wp["ffn"], Wp["even"], Wp["odd"]
    convs, ssms, glas, shifts, wkvs = [], [], [], [], []
    ssm_acc = gla_acc = wkv_acc = None
    v_first = None
    for layer in range(DEPTH):
        i = layer // 2
        x = _ffn(x, ffn["norm"], ffn["w_gu"], ffn["w_down"], 2 * layer, tm_ffn)
        if layer % 2 == 0:
            z, xbc, q, k, v, og, sm = _even_in(x, Wp["mix_norm"], even["w_in"], i, tm_proj)
            if step:
                xact, convn, dax, c2, ybase, eg = _even_step_pre(
                    xbc, states[0][i].reshape(B, 3 * SSD_CONV_DIM), sm, even, i)
                ssm_acc, yi = _ssd_step(states[1], ssm_acc, i, dax, c2, xact)
                gla_acc, o = _gla_step(states[2], gla_acc, i, eg, k, q, v)
                x = _even_step_post(x, ybase, yi, dax, z, o, og, even, even["w_out"], i)
                convn = convn.reshape(B, SSD_CONV - 1, SSD_CONV_DIM)
            else:
                y, convn, ssmn = _ssd_seq(xbc, sm, z, even, i, B, L)
                o, glan = _gla_seq(q, k, v, og, sm, even, i, B, L)
                x = _out_proj(x, y, o, even["w_out"], i, tm_ffn)
                ssms.append(ssmn)
                glas.append(glan)
            convs.append(convn)
        else:
            prev = states[3][i] if step else jnp.zeros((B, 1, D_MODEL), F32)
            h, r, dec, k, v, an, bb, gg = _odd_pre(x, prev, v_first, odd, i, tm_proj, L)
            if v_first is None:
                v_first = v
            if step:
                o, wkv_acc = _rwkv_step(states[4], wkv_acc, i, dec, k, an, bb, r, v)
            else:
                o, wkvn = _rwkv_seq(dec, k, an, bb, r, v, B, L)
                wkvs.append(wkvn)
            x = _odd_post(x, o, r, k, v, gg, odd, i, tm_ffn)
            shifts.append(h.reshape(B, L, D_MODEL)[:, -1])
        x = _ffn(x, ffn["norm"], ffn["w_gu"], ffn["w_down"], 2 * layer + 1, tm_ffn,
                 final_g=Wp["final_norm"] if layer == DEPTH - 1 else None)
    y = x.reshape(B, L, D_MODEL)
    if step:
        ssm_out = ssm_acc.reshape(N_EVEN, B, SSD_HEADS, SSD_HEAD_DIM, SSD_D_STATE)
        gla_out = gla_acc.reshape(N_EVEN, B, GLA_HEADS, GLA_DK, GLA_DV)
        wkv_out = wkv_acc.reshape(N_ODD, B, RWKV_HEADS, RWKV_HEAD, RWKV_HEAD)
    else:
        ssm_out, gla_out, wkv_out = jnp.stack(ssms), jnp.stack(glas), jnp.stack(wkvs)
    return y, jnp.stack(convs), ssm_out, gla_out, jnp.stack(shifts), wkv_out


def kernel(x_prompt, x_sample, state_conv, state_ssm, state_gla, state_shift, state_wkv, ffn_norm, ffn_w_gu, ffn_w_down, mix_norm, final_norm, ev_w_in, ev_conv_w, ev_conv_b, ev_dt_bias, ev_a_log, ev_d_skip, ev_ssd_norm, ev_gla_gate_w2, ev_gla_gate_b, ev_gla_norm, ev_w_out, od_mix, od_w0, od_w1, od_w2, od_a0, od_a1, od_a2, od_v0, od_v1, od_v2, od_g1, od_g2, od_k_k, od_k_a, od_r_k, od_w_r, od_w_k, od_w_v, od_w_o, od_gn_w, od_gn_b):
    W = dict(ffn_norm=ffn_norm, ffn_w_gu=ffn_w_gu, ffn_w_down=ffn_w_down, mix_norm=mix_norm,
             final_norm=final_norm, ev_w_in=ev_w_in, ev_conv_w=ev_conv_w, ev_conv_b=ev_conv_b,
             ev_dt_bias=ev_dt_bias, ev_a_log=ev_a_log, ev_d_skip=ev_d_skip, ev_ssd_norm=ev_ssd_norm,
             ev_gla_gate_w2=ev_gla_gate_w2, ev_gla_gate_b=ev_gla_gate_b, ev_gla_norm=ev_gla_norm,
             ev_w_out=ev_w_out, od_mix=od_mix, od_w0=od_w0, od_w1=od_w1, od_w2=od_w2,
             od_a0=od_a0, od_a1=od_a1, od_a2=od_a2, od_v0=od_v0, od_v1=od_v1, od_v2=od_v2,
             od_g1=od_g1, od_g2=od_g2, od_k_k=od_k_k, od_k_a=od_k_a, od_r_k=od_r_k,
             od_w_r=od_w_r, od_w_k=od_w_k, od_w_v=od_w_v, od_w_o=od_w_o,
             od_gn_w=od_gn_w, od_gn_b=od_gn_b)
    Wp = _prepare(W)
    prompt = _run_group(x_prompt, None, Wp)
    sample = _run_group(x_sample, (state_conv, state_ssm, state_gla, state_shift, state_wkv), Wp)
    return (prompt[0], sample[0]) + prompt[1:] + sample[1:]
```

```python
import functools

import jax
import jax.numpy as jnp
from jax import lax
from jax.experimental import pallas as pl
from jax.experimental.pallas import tpu as pltpu

F32 = jnp.float32
BF16 = jnp.bfloat16

D_MODEL = 1024
DEPTH = 4
N_EVEN = 2
N_ODD = 2
EPS = 1e-5
D_FF = 2816

SSD_HEADS = 16
SSD_HEAD_DIM = 64
SSD_INNER = 1024
SSD_GROUPS = 2
SSD_D_STATE = 128
SSD_CONV = 4
SSD_CONV_DIM = 1536
SSD_CHUNK = 64

GLA_HEADS = 4
GLA_DK = 128
GLA_DV = 256
GLA_KEY = 512
GLA_VAL = 1024
GLA_GATE_RANK = 16
GLA_GATE_NORM = 16.0
GLA_CHUNK = 16

RWKV_HEAD = 64
RWKV_HEADS = 16
RWKV_GN_EPS = 64e-5

LANES = 128
SMALL = LANES
IN_PERM = SSD_INNER + SSD_CONV_DIM + 2 * GLA_KEY + 2 * GLA_VAL + SMALL
VMEM_LIMIT = 56 * 1024 * 1024
FF_TILE = 1408
SEQ_BLOCK = 256
STEP_BATCH = 8


def _cp(sem):
    return pltpu.CompilerParams(dimension_semantics=sem, vmem_limit_bytes=VMEM_LIMIT)


def _iota(shape, axis):
    return lax.broadcasted_iota(jnp.int32, shape, axis)


def _onehot(mask):
    return jnp.where(mask, 1.0, 0.0).astype(BF16)


def _dot(a, b):
    return jnp.dot(a, b, preferred_element_type=F32)


def _split3(x):
    hi = x.astype(BF16)
    r = x - hi.astype(F32)
    mid = r.astype(BF16)
    lo = (r - mid.astype(F32)).astype(BF16)
    return hi, mid, lo


def _dot01(x, e):
    hi, mid, lo = _split3(x)
    return _dot(hi, e) + _dot(mid, e) + _dot(lo, e)


def _dot01_l(e, x):
    hi, mid, lo = _split3(x)
    return _dot(e, hi) + _dot(e, mid) + _dot(e, lo)


def _rms(x, g):
    ms = jnp.mean(x * x, axis=-1, keepdims=True)
    return x * lax.rsqrt(ms + EPS) * g


def _silu(x):
    return x * jax.nn.sigmoid(x)


def _softplus(x):
    return jnp.maximum(x, 0.0) + jnp.log1p(jnp.exp(-jnp.abs(x)))


def _softplus_abs(x):
    return jnp.maximum(x, 0.0) + jnp.log(1.0 + jnp.exp(-jnp.abs(x)))


def _head_expand():
    return _onehot((_iota((LANES, D_MODEL), 1) >> 6) == _iota((LANES, D_MODEL), 0))


def _head_reduce():
    return _onehot((_iota((D_MODEL, LANES), 0) >> 6) == _iota((D_MODEL, LANES), 1))


def _ffn_body(x_ref, g_ref, wgu_ref, wd_ref, *rest):
    o_ref = rest[-1]
    x = x_ref[...]
    xn = _rms(x, g_ref[...]).astype(BF16)
    acc = None
    for j in range(D_FF // FF_TILE):
        lo = j * FF_TILE
        gate = _dot(xn, wgu_ref[:, lo:lo + FF_TILE])
        up = _dot(xn, wgu_ref[:, D_FF + lo:D_FF + lo + FF_TILE])
        part = _dot((_silu(gate) * up).astype(BF16), wd_ref[lo:lo + FF_TILE, :])
        acc = part if acc is None else acc + part
    y = x + 0.5 * acc
    o_ref[...] = _rms(y, rest[0][...]) if len(rest) == 2 else y


def _ffn(x, nrm, wgu, wd, ls, tm, final_g=None):
    T = x.shape[0]
    resident = pl.Buffered(1)
    args = [x, nrm, wgu, wd]
    specs = [
        pl.BlockSpec((tm, D_MODEL), lambda i: (i, 0)),
        pl.BlockSpec((None, 1, D_MODEL), lambda i: (ls, 0, 0)),
        pl.BlockSpec((None, D_MODEL, 2 * D_FF), lambda i: (ls, 0, 0), pipeline_mode=resident),
        pl.BlockSpec((None, D_FF, D_MODEL), lambda i: (ls, 0, 0), pipeline_mode=resident),
    ]
    if final_g is not None:
        args.append(final_g)
        specs.append(pl.BlockSpec((1, D_MODEL), lambda i: (0, 0)))
    return pl.pallas_call(
        _ffn_body,
        grid=(T // tm,),
        in_specs=specs,
        out_specs=pl.BlockSpec((tm, D_MODEL), lambda i: (i, 0)),
        out_shape=jax.ShapeDtypeStruct((T, D_MODEL), F32),
        compiler_params=_cp(("parallel",)),
        name="ffn",
    )(*args)


_EVEN_PIECES = (SSD_INNER, SSD_CONV_DIM, GLA_KEY, GLA_KEY, GLA_VAL, GLA_VAL, SMALL)


def _even_in_body(x_ref, g_ref, w_ref, *out_refs):
    h = _rms(x_ref[...], g_ref[...]).astype(BF16)
    off = 0
    for ref, n in zip(out_refs, _EVEN_PIECES):
        ref[...] = _dot(h, w_ref[:, off:off + n])
        off += n


def _even_in(x, nrm, w, i, tm):
    T = x.shape[0]
    return pl.pallas_call(
        _even_in_body,
        grid=(T // tm,),
        in_specs=[
            pl.BlockSpec((tm, D_MODEL), lambda t: (t, 0)),
            pl.BlockSpec((None, 1, D_MODEL), lambda t: (2 * i, 0, 0)),
            pl.BlockSpec((None, D_MODEL, IN_PERM), lambda t: (i, 0, 0)),
        ],
        out_specs=[pl.BlockSpec((tm, n), lambda t: (t, 0)) for n in _EVEN_PIECES],
        out_shape=[jax.ShapeDtypeStruct((T, n), F32) for n in _EVEN_PIECES],
        compiler_params=_cp(("parallel",)),
        name="even_in",
    )(x, nrm, w)


def _out_proj_body(x_ref, y_ref, o_ref, w_ref, out_ref):
    acc = _dot(y_ref[...].astype(BF16), w_ref[0:SSD_INNER, :])
    acc = acc + _dot(o_ref[...].astype(BF16), w_ref[SSD_INNER:, :])
    out_ref[...] = x_ref[...] + acc


def _out_proj(x, y, o, w, i, tm):
    T = x.shape[0]
    tok = pl.BlockSpec((tm, D_MODEL), lambda t: (t, 0))
    return pl.pallas_call(
        _out_proj_body,
        grid=(T // tm,),
        in_specs=[tok, tok, tok,
                  pl.BlockSpec((None, SSD_INNER + GLA_VAL, D_MODEL), lambda t: (i, 0, 0))],
        out_specs=tok,
        out_shape=jax.ShapeDtypeStruct((T, D_MODEL), F32),
        compiler_params=_cp(("parallel",)),
        name="even_out",
    )(x, y, o, w)


def _ssd_seq_body(xbc_ref, sm_ref, z_ref, cw_ref, cb_ref, dtb_ref, alog_ref, dsk_ref, nw_ref,
                  y_ref, convn_ref, ssmn_ref,
                  cbuf_ref, xact_ref, acx_ref, dtx_ref, hT_ref, *, nblk):
    Lb = SEQ_BLOCK
    C = SSD_CHUNK
    j = pl.program_id(1)

    @pl.when(j == 0)
    def _():
        cbuf_ref[0:8, :] = jnp.zeros((8, SSD_CONV_DIM), F32)
        hT_ref[...] = jnp.zeros_like(hT_ref)

    x = xbc_ref[...]
    acc = cb_ref[...] + pltpu.roll(x, 3, axis=0) * cw_ref[0:1, :]
    acc = acc + pltpu.roll(x, 2, axis=0) * cw_ref[1:2, :]
    acc = acc + pltpu.roll(x, 1, axis=0) * cw_ref[2:3, :]
    acc = acc + x * cw_ref[3:4, :]
    xact_ref[...] = _silu(acc)
    cbuf_ref[8:16, :] = x[0:8, :]
    head = cb_ref[...] + cbuf_ref[5:13, :] * cw_ref[0:1, :]
    head = head + cbuf_ref[6:14, :] * cw_ref[1:2, :]
    head = head + cbuf_ref[7:15, :] * cw_ref[2:3, :]
    head = head + cbuf_ref[8:16, :] * cw_ref[3:4, :]
    xact_ref[0:8, :] = _silu(head)
    tail = x[Lb - 8:Lb, :]
    cbuf_ref[0:8, :] = tail

    @pl.when(j == nblk - 1)
    def _():
        convn_ref[...] = tail[5:8, :]

    dt = _softplus(sm_ref[...] + dtb_ref[...])
    dta = dt * (-jnp.exp(alog_ref[...]))
    ii = _iota((Lb, Lb), 0)
    jj = _iota((Lb, Lb), 1)
    tril = _onehot(((ii >> 6) == (jj >> 6)) & (ii >= jj))
    acum = _dot01_l(tril, dta)
    expand = _head_expand()
    acx_ref[...] = _dot01(acum, expand)
    dtx_ref[...] = _dot01(dt, expand)

    li = _iota((C, D_MODEL), 0)
    lj = _iota((C, D_MODEL), 1) & (C - 1)
    diag = li == lj
    causal = li >= lj
    ones_c = jnp.ones((C, C), BF16)
    tile8 = _onehot((_iota((C, 8 * C), 1) & (C - 1)) == _iota((C, 8 * C), 0))
    pair_mask = (_iota((LANES, LANES), 0) >> 6) == (_iota((LANES, LANES), 1) >> 6)
    half = SSD_INNER // SSD_GROUPS

    def chunk(c, carry):
        r0 = pl.multiple_of(c * C, C)
        xa = xact_ref[pl.ds(r0, C), :]
        xs = xa[:, :SSD_INNER]
        bms = [xa[:, SSD_INNER + g * SSD_D_STATE:SSD_INNER + (g + 1) * SSD_D_STATE] for g in range(2)]
        cms = [xa[:, SSD_INNER + (2 + g) * SSD_D_STATE:SSD_INNER + (3 + g) * SSD_D_STATE]
               for g in range(2)]
        acx = acx_ref[pl.ds(r0, C), :]
        dtx = dtx_ref[pl.ds(r0, C), :]
        alast = acx[C - 1:C, :]
        arow = _dot01_l(ones_c, jnp.where(diag, acx, 0.0))
        dtrow = _dot01_l(ones_c, jnp.where(diag, dtx, 0.0))
        decay = jnp.exp(jnp.where(causal, acx - arow, -jnp.inf))
        cbt = []
        for g in range(2):
            cb = lax.dot_general(cms[g].astype(BF16), bms[g].astype(BF16),
                                 (((1,), (1,)), ((), ())), preferred_element_type=F32)
            cbt.append(_dot01(cb, tile8))
        wts = (jnp.concatenate(cbt, axis=1) * decay * dtrow).astype(BF16)
        ys = []
        for p in range(SSD_HEADS // 2):
            xp = xs[:, p * LANES:(p + 1) * LANES]
            xbd = jnp.where(pair_mask, jnp.concatenate([xp, xp], axis=0), 0.0).astype(BF16)
            ys.append(_dot(wts[:, p * LANES:(p + 1) * LANES], xbd))
        y = jnp.concatenate(ys, axis=1)
        hT = hT_ref[...]
        hTb = hT.astype(BF16)
        yi = [_dot(cms[g].astype(BF16), hTb[:, g * half:(g + 1) * half]) for g in range(2)]
        y = y + jnp.concatenate(yi, axis=1) * jnp.exp(acx)
        xsc = (jnp.exp(alast - acx) * dtx * xs).astype(BF16)
        st = [lax.dot_general(bms[g].astype(BF16), xsc[:, g * half:(g + 1) * half],
                              (((0,), (0,)), ((), ())), preferred_element_type=F32) for g in range(2)]
        hT_ref[...] = hT * jnp.exp(alast) + jnp.concatenate(st, axis=1)
        y = y + dsk_ref[...] * xs
        y = y * _silu(z_ref[pl.ds(r0, C), :])
        outs = []
        for g in range(2):
            yg = y[:, g * half:(g + 1) * half]
            ms = jnp.mean(yg * yg, axis=-1, keepdims=True)
            outs.append(yg * lax.rsqrt(ms + EPS))
        y_ref[pl.ds(r0, C), :] = jnp.concatenate(outs, axis=1) * nw_ref[...]
        return carry

    lax.fori_loop(0, Lb // C, chunk, 0, unroll=2)

    @pl.when(j == nblk - 1)
    def _():
        ssmn_ref[...] = hT_ref[...].T


def _ssd_seq(xbc, sm, z, P, i, B, L):
    Lb = SEQ_BLOCK
    nblk = L // Lb
    T = B * L
    tok = lambda n: pl.BlockSpec((Lb, n), lambda b, j: (b * nblk + j, 0))
    par = lambda r, n: pl.BlockSpec((None, r, n), lambda b, j: (i, 0, 0))
    y, convn, ssmn = pl.pallas_call(
        functools.partial(_ssd_seq_body, nblk=nblk),
        grid=(B, nblk),
        in_specs=[tok(SSD_CONV_DIM), tok(SMALL), tok(SSD_INNER),
                  par(SSD_CONV, SSD_CONV_DIM), par(1, SSD_CONV_DIM), par(1, SMALL), par(1, SMALL),
                  par(1, SSD_INNER), par(1, SSD_INNER)],
        out_specs=[tok(SSD_INNER),
                   pl.BlockSpec((None, SSD_CONV - 1, SSD_CONV_DIM), lambda b, j: (b, 0, 0)),
                   pl.BlockSpec((None, SSD_INNER, SSD_D_STATE), lambda b, j: (b, 0, 0))],
        out_shape=[jax.ShapeDtypeStruct((T, SSD_INNER), F32),
                   jax.ShapeDtypeStruct((B, SSD_CONV - 1, SSD_CONV_DIM), F32),
                   jax.ShapeDtypeStruct((B, SSD_INNER, SSD_D_STATE), F32)],
        scratch_shapes=[pltpu.VMEM((16, SSD_CONV_DIM), F32),
                        pltpu.VMEM((Lb, SSD_CONV_DIM), F32),
                        pltpu.VMEM((Lb, SSD_INNER), F32),
                        pltpu.VMEM((Lb, SSD_INNER), F32),
                        pltpu.VMEM((SSD_D_STATE, SSD_INNER), F32)],
        compiler_params=_cp(("parallel", "arbitrary")),
        name="ssd_seq",
    )(xbc, sm, z, P["conv_w"], P["conv_b"], P["dt_bias"], P["a_log"], P["d_skip_x"], P["ssd_norm"])
    return y, convn, ssmn.reshape(B, SSD_HEADS, SSD_HEAD_DIM, SSD_D_STATE)


def _gla_seq_body(q_ref, k_ref, v_ref, og_ref, sm_ref, w2_ref, gb_ref, nw_ref,
                  o_ref, glan_ref,
                  st_ref, dec_ref, qg_ref, kd_ref, oi_ref, *, nblk):
    Lb = SEQ_BLOCK
    j = pl.program_id(1)

    @pl.when(j == 0)
    def _():
        st_ref[...] = jnp.zeros_like(st_ref)

    gate = _dot(sm_ref[...].astype(BF16), w2_ref[...]) + gb_ref[...]
    lg = -_softplus_abs(-gate) / GLA_GATE_NORM
    ii = _iota((Lb, Lb), 0)
    jj = _iota((Lb, Lb), 1)
    same = (ii >> 4) == (jj >> 4)
    causal = same & (ii >= jj)
    b = _dot01_l(_onehot(causal), lg)
    blast = _dot01_l(_onehot(same), lg)
    qg = q_ref[...] * (GLA_DK ** -0.5) * jnp.exp(b)
    kk = k_ref[...]
    kg = kk * jnp.exp(-b)
    kd = kk * jnp.exp(blast - b)
    dec_ref[...] = jnp.exp(blast)
    qg_ref[...] = qg.astype(BF16)
    kd_ref[...] = kd.astype(BF16)

    def chunk(c, carry):
        r0 = pl.multiple_of(c * GLA_CHUNK, GLA_CHUNK)
        for h in range(GLA_HEADS):
            ks = slice(h * GLA_DK, (h + 1) * GLA_DK)
            vs = slice(h * GLA_DV, (h + 1) * GLA_DV)
            st = st_ref[h]
            oi_ref[pl.ds(r0, GLA_CHUNK), vs] = lax.dot_general(
                qg_ref[pl.ds(r0, GLA_CHUNK), ks], st.astype(BF16),
                (((1,), (1,)), ((), ())), preferred_element_type=F32)
            upd = lax.dot_general(v_ref[pl.ds(r0, GLA_CHUNK), vs].astype(BF16),
                                  kd_ref[pl.ds(r0, GLA_CHUNK), ks],
                                  (((0,), (0,)), ((), ())), preferred_element_type=F32)
            st_ref[h] = st * dec_ref[pl.ds(r0, 8), ks][0:1, :] + upd
        return carry

    lax.fori_loop(0, Lb // GLA_CHUNK, chunk, 0, unroll=8)

    for h in range(GLA_HEADS):
        ks = slice(h * GLA_DK, (h + 1) * GLA_DK)
        vs = slice(h * GLA_DV, (h + 1) * GLA_DV)
        att = lax.dot_general(qg[:, ks].astype(BF16), kg[:, ks].astype(BF16),
                              (((1,), (1,)), ((), ())), preferred_element_type=F32)
        att = jnp.where(causal, att, 0.0)
        o_h = _dot(att.astype(BF16), v_ref[:, vs].astype(BF16)) + oi_ref[:, vs]
        ms = jnp.mean(o_h * o_h, axis=-1, keepdims=True)
        o_h = o_h * lax.rsqrt(ms + EPS) * nw_ref[...]
        o_ref[:, vs] = o_h * _silu(og_ref[:, vs])

    @pl.when(j == nblk - 1)
    def _():
        for h in range(GLA_HEADS):
            glan_ref[h] = st_ref[h].T


def _gla_seq(q, k, v, og, sm, P, i, B, L):
    Lb = SEQ_BLOCK
    nblk = L // Lb
    T = B * L
    tok = lambda n: pl.BlockSpec((Lb, n), lambda b, j: (b * nblk + j, 0))
    par = lambda r, n: pl.BlockSpec((None, r, n), lambda b, j: (i, 0, 0))
    return pl.pallas_call(
        functools.partial(_gla_seq_body, nblk=nblk),
        grid=(B, nblk),
        in_specs=[tok(GLA_KEY), tok(GLA_KEY), tok(GLA_VAL), tok(GLA_VAL), tok(SMALL),
                  par(SMALL, GLA_KEY), par(1, GLA_KEY), par(1, GLA_DV)],
        out_specs=[tok(GLA_VAL),
                   pl.BlockSpec((None, GLA_HEADS, GLA_DK, GLA_DV), lambda b, j: (b, 0, 0, 0))],
        out_shape=[jax.ShapeDtypeStruct((T, GLA_VAL), F32),
                   jax.ShapeDtypeStruct((B, GLA_HEADS, GLA_DK, GLA_DV), F32)],
        scratch_shapes=[pltpu.VMEM((GLA_HEADS, GLA_DV, GLA_DK), F32),
                        pltpu.VMEM((Lb, GLA_KEY), F32),
                        pltpu.VMEM((Lb, GLA_KEY), BF16),
                        pltpu.VMEM((Lb, GLA_KEY), BF16),
                        pltpu.VMEM((Lb, GLA_VAL), F32)],
        compiler_params=_cp(("parallel", "arbitrary")),
        name="gla_seq",
    )(q, k, v, og, sm, P["gate_w2"], P["gate_b"], P["gla_norm"])


def _even_step_pre_body(xbc_ref, c0_ref, sm_ref, cw_ref, cb_ref, dtb_ref, alog_ref, dsk_ref,
                        w2_ref, gb_ref,
                        xact_ref, convn_ref, dax_ref, c2_ref, ybase_ref, eg_ref):
    u = xbc_ref[...]
    n = SSD_CONV_DIM
    acc = cb_ref[...] + c0_ref[:, 0:n] * cw_ref[0:1, :]
    acc = acc + c0_ref[:, n:2 * n] * cw_ref[1:2, :]
    acc = acc + c0_ref[:, 2 * n:3 * n] * cw_ref[2:3, :]
    acc = acc + u * cw_ref[3:4, :]
    xa = _silu(acc)
    xact_ref[...] = xa
    convn_ref[:, 0:n] = c0_ref[:, n:2 * n]
    convn_ref[:, n:2 * n] = c0_ref[:, 2 * n:3 * n]
    convn_ref[:, 2 * n:3 * n] = u
    dt = _softplus(sm_ref[...] + dtb_ref[...])
    dta = dt * (-jnp.exp(alog_ref[...]))
    expand = _head_expand()
    dtx = _dot01(dt, expand)
    dax_ref[...] = jnp.exp(_dot01(dta, expand))
    xs = xa[:, :SSD_INNER]
    c2_ref[...] = dtx * xs
    half = SSD_INNER // SSD_GROUPS
    cbs = []
    for g in range(2):
        bm = xa[:, SSD_INNER + g * SSD_D_STATE:SSD_INNER + (g + 1) * SSD_D_STATE]
        cm = xa[:, SSD_INNER + (2 + g) * SSD_D_STATE:SSD_INNER + (3 + g) * SSD_D_STATE]
        cb = jnp.sum(cm * bm, axis=-1, keepdims=True)
        cbs.append(jnp.broadcast_to(cb, (cb.shape[0], half)))
    ybase_ref[...] = jnp.concatenate(cbs, axis=1) * dtx * xs + dsk_ref[...] * xs
    gate = _dot(sm_ref[...].astype(BF16), w2_ref[...]) + gb_ref[...]
    eg_ref[...] = jnp.exp(-_softplus(-gate) / GLA_GATE_NORM)


def _even_step_pre(xbc, conv0, sm, P, i):
    Bn = xbc.shape[0]
    full = lambda n: pl.BlockSpec((Bn, n), lambda t: (0, 0))
    par = lambda r, n: pl.BlockSpec((None, r, n), lambda t: (i, 0, 0))
    outs = (SSD_CONV_DIM, 3 * SSD_CONV_DIM, SSD_INNER, SSD_INNER, SSD_INNER, GLA_KEY)
    return pl.pallas_call(
        _even_step_pre_body,
        grid=(1,),
        in_specs=[full(SSD_CONV_DIM), full(3 * SSD_CONV_DIM), full(SMALL),
                  par(SSD_CONV, SSD_CONV_DIM), par(1, SSD_CONV_DIM), par(1, SMALL), par(1, SMALL),
                  par(1, SSD_INNER), par(SMALL, GLA_KEY), par(1, GLA_KEY)],
        out_specs=[full(n) for n in outs],
        out_shape=[jax.ShapeDtypeStruct((Bn, n), F32) for n in outs],
        compiler_params=_cp(("arbitrary",)),
        name="even_step_pre",
    )(xbc, conv0, sm, P["conv_w"], P["conv_b"], P["dt_bias"], P["a_log"], P["d_skip_x"],
      P["gate_w2"], P["gate_b"])


def _to_cols(x):
    Bn, N = x.shape
    return x.reshape(Bn // STEP_BATCH, STEP_BATCH, N).transpose(0, 2, 1)


def _from_cols(x):
    nb, N, bb = x.shape
    return x.transpose(0, 2, 1).reshape(nb * bb, N)


def _layer_state_call(body, name, state_all, acc, i, rows, width, args, specs, out_spec, out_shape):
    nl, Bn = state_all.shape[:2]
    st = pl.BlockSpec((None, STEP_BATCH, rows, width), lambda t: (i, t, 0, 0))
    args = [state_all.reshape(nl, Bn, rows, width)] + list(args)
    specs = [st] + list(specs)
    aliases = {}
    if acc is not None:
        aliases = {len(args): 0}
        args.append(acc)
        specs.append(pl.BlockSpec(memory_space=pl.ANY))
    return pl.pallas_call(
        body,
        grid=(Bn // STEP_BATCH,),
        in_specs=specs,
        out_specs=[st, out_spec],
        out_shape=[jax.ShapeDtypeStruct((nl, Bn, rows, width), F32), out_shape],
        input_output_aliases=aliases,
        compiler_params=_cp(("parallel",)),
        name=name,
    )(*args)


def _ssd_step_body(h_ref, c1_ref, c2_ref, bm_ref, cm_ref, *rest):
    hn_ref, y_ref = rest[-2:]
    half = SSD_INNER // SSD_GROUPS
    lane = _iota((SSD_INNER, STEP_BATCH), 1)
    ycols = jnp.zeros((SSD_INNER, STEP_BATCH), F32)
    c1 = c1_ref[...]
    c2 = c2_ref[...]
    for b in range(STEP_BATCH):
        h0 = h_ref[b]
        bmat = jnp.concatenate(
            [jnp.broadcast_to(bm_ref[b:b + 1, g * SSD_D_STATE:(g + 1) * SSD_D_STATE],
                              (half, SSD_D_STATE)) for g in range(2)], axis=0)
        cmat = jnp.concatenate(
            [jnp.broadcast_to(cm_ref[b:b + 1, g * SSD_D_STATE:(g + 1) * SSD_D_STATE],
                              (half, SSD_D_STATE)) for g in range(2)], axis=0)
        ycol = jnp.sum(h0 * cmat, axis=1, keepdims=True)
        ycols = jnp.where(lane == b, ycol, ycols)
        hn_ref[b] = h0 * c1[:, b:b + 1] + c2[:, b:b + 1] * bmat
    y_ref[...] = ycols


def _ssd_step(state_all, acc, i, dax, c2, xact):
    Bn = state_all.shape[1]
    bb = STEP_BATCH
    bm = xact[:, SSD_INNER:SSD_INNER + 2 * SSD_D_STATE]
    cm = xact[:, SSD_INNER + 2 * SSD_D_STATE:]
    col = pl.BlockSpec((None, SSD_INNER, bb), lambda t: (t, 0, 0))
    row = pl.BlockSpec((bb, 2 * SSD_D_STATE), lambda t: (t, 0))
    acc, ycols = _layer_state_call(
        _ssd_step_body, "ssd_step", state_all, acc, i, SSD_INNER, SSD_D_STATE,
        [_to_cols(dax), _to_cols(c2), bm, cm], [col, col, row, row],
        col, jax.ShapeDtypeStruct((Bn // bb, SSD_INNER, bb), F32))
    return acc, _from_cols(ycols)


def _gla_step_body(s_ref, eg_ref, kc_ref, qc_ref, v_ref, *rest):
    sn_ref, o_ref = rest[-2:]
    eg = eg_ref[...]
    kc = kc_ref[...]
    qc = qc_ref[...] * (GLA_DK ** -0.5)
    for b in range(STEP_BATCH):
        s0 = s_ref[b]
        vmat = jnp.concatenate(
            [jnp.broadcast_to(v_ref[b:b + 1, h * GLA_DV:(h + 1) * GLA_DV], (GLA_DK, GLA_DV))
             for h in range(GLA_HEADS)], axis=0)
        sn = s0 * eg[:, b:b + 1] + kc[:, b:b + 1] * vmat
        sn_ref[b] = sn
        t = sn * qc[:, b:b + 1]
        o_ref[b:b + 1, :] = jnp.concatenate(
            [jnp.sum(t[h * GLA_DK:(h + 1) * GLA_DK], axis=0, keepdims=True)
             for h in range(GLA_HEADS)], axis=1)


def _gla_step(state_all, acc, i, eg, k, q, v):
    Bn = state_all.shape[1]
    bb = STEP_BATCH
    col = pl.BlockSpec((None, GLA_KEY, bb), lambda t: (t, 0, 0))
    row = pl.BlockSpec((bb, GLA_VAL), lambda t: (t, 0))
    return _layer_state_call(
        _gla_step_body, "gla_step", state_all, acc, i, GLA_KEY, GLA_DV,
        [_to_cols(eg), _to_cols(k), _to_cols(q), v], [col, col, col, row],
        row, jax.ShapeDtypeStruct((Bn, GLA_VAL), F32))


def _even_step_post_body(x_ref, ybase_ref, yi_ref, dax_ref, z_ref, o_ref, og_ref,
                         snw_ref, gnw_ref, w_ref, out_ref):
    y = (ybase_ref[...] + yi_ref[...] * dax_ref[...]) * _silu(z_ref[...])
    half = SSD_INNER // SSD_GROUPS
    acc = None
    for g in range(2):
        yg = y[:, g * half:(g + 1) * half]
        ms = jnp.mean(yg * yg, axis=-1, keepdims=True)
        yn = yg * lax.rsqrt(ms + EPS) * snw_ref[:, g * half:(g + 1) * half]
        part = _dot(yn.astype(BF16), w_ref[g * half:(g + 1) * half, :])
        acc = part if acc is None else acc + part
    for h in range(GLA_HEADS):
        vs = slice(h * GLA_DV, (h + 1) * GLA_DV)
        oh = o_ref[:, vs]
        ms = jnp.mean(oh * oh, axis=-1, keepdims=True)
        on = oh * lax.rsqrt(ms + EPS) * gnw_ref[...] * _silu(og_ref[:, vs])
        acc = acc + _dot(on.astype(BF16), w_ref[SSD_INNER + h * GLA_DV:SSD_INNER + (h + 1) * GLA_DV, :])
    out_ref[...] = x_ref[...] + acc


def _even_step_post(x, ybase, yi, dax, z, o, og, P, w_out, i):
    Bn = x.shape[0]
    full = pl.BlockSpec((Bn, D_MODEL), lambda t: (0, 0))
    return pl.pallas_call(
        _even_step_post_body,
        grid=(1,),
        in_specs=[full] * 7 + [pl.BlockSpec((None, 1, SSD_INNER), lambda t: (i, 0, 0)),
                               pl.BlockSpec((None, 1, GLA_DV), lambda t: (i, 0, 0)),
                               pl.BlockSpec((None, SSD_INNER + GLA_VAL, D_MODEL), lambda t: (i, 0, 0))],
        out_specs=full,
        out_shape=jax.ShapeDtypeStruct((Bn, D_MODEL), F32),
        compiler_params=_cp(("arbitrary",)),
        name="even_step_post",
    )(x, ybase, yi, dax, z, o, og, P["ssd_norm"], P["gla_norm"], w_out)


_ODD_OUTS = 8


def _odd_pre_body(*refs, step, has_vfirst, nblk_seq, tm):
    it = iter(refs)
    x_ref = next(it)
    if step:
        prev_ref = next(it)
    else:
        xp_ref = next(it)
        sh0_ref = next(it)
    g_ref, mix_ref = next(it), next(it)
    wr_ref, wk_ref, wv_ref = next(it), next(it), next(it)
    w0_ref, w1_ref, w2_ref = next(it), next(it), next(it)
    a0_ref, a1_ref, a2_ref = next(it), next(it), next(it)
    if has_vfirst:
        vf_ref, v0_ref, v1_ref, v2_ref = next(it), next(it), next(it), next(it)
    g1_ref, g2_ref = next(it), next(it)
    kkw_ref, kaw_ref = next(it), next(it)
    h_ref, r_ref, dec_ref, k_ref, v_ref, an_ref, b_ref, gg_ref = (next(it) for _ in range(_ODD_OUTS))

    h = _rms(x_ref[...], g_ref[...])
    if step:
        prev = prev_ref[...]
    else:
        i = pl.program_id(0)
        hp = _rms(xp_ref[...], g_ref[...])[7:8, :]
        prow = jnp.where(i % nblk_seq == 0, sh0_ref[...], hp)
        prev = jnp.where(_iota((tm, D_MODEL), 0) == 0, prow, pltpu.roll(h, 1, axis=0))
    h_ref[...] = h
    xx = prev - h
    mixed = lambda r: (h + xx * mix_ref[r:r + 1, :]).astype(BF16)
    xr, xw, xk, xv, xa, xg = (mixed(r) for r in range(6))
    r_ref[...] = _dot(xr, wr_ref[...])
    wl = w0_ref[...] + _dot(jnp.tanh(_dot(xw, w1_ref[...])).astype(BF16), w2_ref[...])
    w = -_softplus_abs(-wl) - 0.5
    dec_ref[...] = jnp.exp(-jnp.exp(w))
    k = _dot(xk, wk_ref[...])
    v = _dot(xv, wv_ref[...])
    if has_vfirst:
        gate = jax.nn.sigmoid(v0_ref[...] + _dot(_dot(xv, v1_ref[...]).astype(BF16), v2_ref[...]))
        v = v + (vf_ref[...] - v) * gate
    v_ref[...] = v
    a = jax.nn.sigmoid(a0_ref[...] + _dot(_dot(xa, a1_ref[...]).astype(BF16), a2_ref[...]))
    gg_ref[...] = _dot(jax.nn.sigmoid(_dot(xg, g1_ref[...])).astype(BF16), g2_ref[...])
    kkf = k * kkw_ref[...]
    red = _head_reduce()
    expand = _head_expand()
    ss = _dot01x2(kkf * kkf, jnp.concatenate([red, red], axis=0))
    kk = kkf * _dot01x2(lax.rsqrt(jnp.maximum(ss, 1e-24)), jnp.concatenate([expand, expand], axis=0))
    k_ref[...] = k * (1.0 + (a - 1.0) * kaw_ref[...])
    an_ref[...] = -kk
    b_ref[...] = kk * a


def _odd_pre(x, prev_or_shift0, vfirst, P, i, tm, L):
    T = x.shape[0]
    step = L == 1
    has_vfirst = vfirst is not None
    nblk_seq = max(L // tm, 1)
    tok = pl.BlockSpec((tm, D_MODEL), lambda t: (t, 0))
    par = lambda r, n, idx=i: pl.BlockSpec((None, r, n), lambda t: (idx, 0, 0))
    args = [x]
    specs = [tok]
    if step:
        args.append(prev_or_shift0)
        specs.append(tok)
    else:
        args += [x, prev_or_shift0]
        specs += [pl.BlockSpec((8, D_MODEL), lambda t: (jnp.maximum(t * (tm // 8) - 1, 0), 0)),
                  pl.BlockSpec((None, 1, D_MODEL), lambda t: (t // nblk_seq, 0, 0))]
    args += [P["mix_norm"], P["mix"], P["w_r"], P["w_k"], P["w_v"], P["w0"], P["w1"], P["w2"],
             P["a0"], P["a1"], P["a2"]]
    specs += [pl.BlockSpec((None, 1, D_MODEL), lambda t: (2 * i + 1, 0, 0)),
              par(8, D_MODEL), par(D_MODEL, D_MODEL), par(D_MODEL, D_MODEL), par(D_MODEL, D_MODEL),
              par(1, D_MODEL), par(D_MODEL, LANES), par(LANES, D_MODEL),
              par(1, D_MODEL), par(D_MODEL, LANES), par(LANES, D_MODEL)]
    if has_vfirst:
        args += [vfirst, P["v0"], P["v1"], P["v2"]]
        specs += [tok, par(1, D_MODEL, i - 1), par(D_MODEL, LANES, i - 1), par(LANES, D_MODEL, i - 1)]
    args += [P["g1"], P["g2"], P["k_k"], P["k_a"]]
    specs += [par(D_MODEL, 2 * LANES), par(2 * LANES, D_MODEL), par(1, D_MODEL), par(1, D_MODEL)]
    return pl.pallas_call(
        functools.partial(_odd_pre_body, step=step, has_vfirst=has_vfirst, nblk_seq=nblk_seq, tm=tm),
        grid=(T // tm,),
        in_specs=specs,
        out_specs=[tok] * _ODD_OUTS,
        out_shape=[jax.ShapeDtypeStruct((T, D_MODEL), F32)] * _ODD_OUTS,
        compiler_params=_cp(("parallel",)),
        name="odd_pre",
    )(*args)


def _dot01x2(x, e2):
    hi = x.astype(BF16)
    mid = (x - hi.astype(F32)).astype(BF16)
    return _dot(jnp.concatenate([hi, mid], axis=1), e2)


def _odd_post_body(x_ref, o_ref, r_ref, k_ref, v_ref, g_ref, gnw_ref, gnb_ref, rk_ref, wo_ref, out_ref):
    red = _head_reduce()
    red = jnp.concatenate([red, red], axis=0)
    expand = _head_expand()
    expand = jnp.concatenate([expand, expand], axis=0)
    o = o_ref[...]
    mu = _dot01x2(_dot01x2(o, red) * (1.0 / RWKV_HEAD), expand)
    d = o - mu
    var = _dot01x2(d * d, red) * (1.0 / RWKV_HEAD)
    on = d * _dot01x2(lax.rsqrt(var + RWKV_GN_EPS), expand) * gnw_ref[...] + gnb_ref[...]
    v = v_ref[...]
    bonus = _dot01x2(_dot01x2(r_ref[...] * k_ref[...] * rk_ref[...], red), expand) * v
    out_ref[...] = x_ref[...] + _dot(((on + bonus) * g_ref[...]).astype(BF16), wo_ref[...])


def _odd_post(x, o, r, k, v, g, P, i, tm):
    T = x.shape[0]
    tok = pl.BlockSpec((tm, D_MODEL), lambda t: (t, 0))
    par = lambda r_, n: pl.BlockSpec((None, r_, n), lambda t: (i, 0, 0))
    return pl.pallas_call(
        _odd_post_body,
        grid=(T // tm,),
        in_specs=[tok] * 6 + [par(1, D_MODEL), par(1, D_MODEL), par(1, D_MODEL), par(D_MODEL, D_MODEL)],
        out_specs=tok,
        out_shape=jax.ShapeDtypeStruct((T, D_MODEL), F32),
        compiler_params=_cp(("parallel",)),
        name="odd_post",
    )(x, o, r, k, v, g, P["gn_w"], P["gn_b"], P["r_k"], P["w_o"])


RWKV_TB = 64
RWKV_NPAIR = RWKV_HEADS // 2


RWKV_SEQS = 2
RWKV_UNROLL = 8


def _rwkv_seq_body(w_ref, k_ref, a_ref, b_ref, r_ref, v_ref, o_ref, sn_ref, s_ref, vt_ref, oacc_ref, *, ntb):
    N = RWKV_HEAD
    R = RWKV_NPAIR * N
    tbi = pl.program_id(1)

    @pl.when(tbi == 0)
    def _():
        s_ref[...] = jnp.zeros_like(s_ref)

    seg = _onehot((_iota((2 * LANES, 2 * LANES), 0) >> 6) == (_iota((2 * LANES, 2 * LANES), 1) >> 6))

    def seg_sum(xb):
        both = _dot(jnp.concatenate([xb[:R // 2], xb[R // 2:]], axis=1), seg)
        return jnp.concatenate([both[:, :LANES], both[:, LANES:]], axis=0)
    for c in range(RWKV_SEQS):
        for p in range(RWKV_NPAIR):
            bt = v_ref[c, :, p * LANES:(p + 1) * LANES].T
            vt_ref[c, p * N:(p + 1) * N, :] = jnp.concatenate([bt[0:N], bt[N:2 * N]], axis=1)
    oacc_ref[...] = jnp.zeros_like(oacc_ref)
    lane_step = _iota((R, LANES), 1) & (N - 1)
    lane_base = (_iota((R, LANES), 1) >> 6) * N

    def steps(tu, carry):
        row0 = pl.multiple_of(tu * RWKV_UNROLL, RWKV_UNROLL)
        for j in range(RWKV_UNROLL):
            t = tu * RWKV_UNROLL + j

            def rows(ref, c, t):
                return jnp.concatenate(
                    [jnp.broadcast_to(ref[c, pl.ds(row0, RWKV_UNROLL), p * LANES:(p + 1) * LANES][j:j + 1, :],
                                      (N, LANES)) for p in range(RWKV_NPAIR)], axis=0)

            for c in range(RWKV_SEQS):
                s = s_ref[c]
                sa = seg_sum((s * rows(a_ref, c, t)).astype(BF16))
                vcol = jnp.take_along_axis(vt_ref[c], lane_base + t, axis=1)
                s = s * rows(w_ref, c, t) + sa * rows(b_ref, c, t) + vcol * rows(k_ref, c, t)
                s_ref[c] = s
                ob = seg_sum((s * rows(r_ref, c, t)).astype(BF16))
                oacc_ref[c] = jnp.where(lane_step == t, ob, oacc_ref[c])
        return carry

    lax.fori_loop(0, RWKV_TB // RWKV_UNROLL, steps, 0)

    for c in range(RWKV_SEQS):
        for p in range(RWKV_NPAIR):
            ot = oacc_ref[c, p * N:(p + 1) * N, :].T
            o_ref[c, :, p * LANES:(p + 1) * LANES] = jnp.concatenate([ot[0:N], ot[N:2 * N]], axis=1)

    @pl.when(tbi == ntb - 1)
    def _():
        for c in range(RWKV_SEQS):
            for p in range(RWKV_NPAIR):
                for hh in range(2):
                    sn_ref[c, (2 * p + hh) * N:(2 * p + hh + 1) * N, :] = (
                        s_ref[c, p * N:(p + 1) * N, hh * N:(hh + 1) * N])


def _rwkv_seq(w, k, a, b, r, v, B, L):
    ntb = L // RWKV_TB
    nc = RWKV_SEQS
    as_seqs = lambda x: x.reshape(B, L, D_MODEL)
    tok = pl.BlockSpec((nc, RWKV_TB, D_MODEL), lambda bi, t: (bi, t, 0))
    state_rows = RWKV_NPAIR * RWKV_HEAD
    o, sn = pl.pallas_call(
        functools.partial(_rwkv_seq_body, ntb=ntb),
        grid=(B // nc, ntb),
        in_specs=[tok] * 6,
        out_specs=[tok, pl.BlockSpec((nc, D_MODEL, RWKV_HEAD), lambda bi, t: (bi, 0, 0))],
        out_shape=[jax.ShapeDtypeStruct((B, L, D_MODEL), F32),
                   jax.ShapeDtypeStruct((B, D_MODEL, RWKV_HEAD), F32)],
        scratch_shapes=[pltpu.VMEM((nc, state_rows, LANES), F32)] * 3,
        compiler_params=_cp(("parallel", "arbitrary")),
        name="rwkv_seq",
    )(as_seqs(w), as_seqs(k), as_seqs(a), as_seqs(b), as_seqs(r), as_seqs(v))
    return o.reshape(B * L, D_MODEL), sn.reshape(B, RWKV_HEADS, RWKV_HEAD, RWKV_HEAD)


def _rwkv_step_body(s_ref, w_ref, k_ref, a_ref, b_ref, r_ref, vc_ref, *rest):
    sn_ref, o_ref = rest[-2:]
    N = RWKV_HEAD
    lane = _iota((D_MODEL, STEP_BATCH), 1)
    ocols = jnp.zeros((D_MODEL, STEP_BATCH), F32)
    vc = vc_ref[...]

    def rows(ref, b):
        return jnp.concatenate(
            [jnp.broadcast_to(ref[b:b + 1, h * N:(h + 1) * N], (N, N)) for h in range(RWKV_HEADS)], axis=0)

    for b in range(STEP_BATCH):
        s = s_ref[b]
        sa = jnp.sum(s * rows(a_ref, b), axis=1, keepdims=True)
        s = s * rows(w_ref, b) + sa * rows(b_ref, b) + vc[:, b:b + 1] * rows(k_ref, b)
        sn_ref[b] = s
        ocol = jnp.sum(s * rows(r_ref, b), axis=1, keepdims=True)
        ocols = jnp.where(lane == b, ocol, ocols)
    o_ref[...] = ocols


def _rwkv_step(state_all, acc, i, w, k, a, b, r, v):
    Bn = state_all.shape[1]
    bb = STEP_BATCH
    col = pl.BlockSpec((None, D_MODEL, bb), lambda t: (t, 0, 0))
    row = pl.BlockSpec((bb, D_MODEL), lambda t: (t, 0))
    acc, ocols = _layer_state_call(
        _rwkv_step_body, "rwkv_step", state_all, acc, i, D_MODEL, RWKV_HEAD,
        [w, k, a, b, r, _to_cols(v)], [row] * 5 + [col],
        col, jax.ShapeDtypeStruct((Bn // bb, D_MODEL, bb), F32))
    return _from_cols(ocols), acc


def _pad_to(w, axis, n):
    pad = [(0, 0)] * w.ndim
    pad[axis] = (0, n - w.shape[axis])
    return jnp.pad(w, pad)


def _prepare(W):
    bf = lambda w: w.astype(BF16)
    row = lambda w: w[:, None, :]
    sizes = [SSD_INNER, SSD_CONV_DIM, SSD_HEADS, GLA_KEY, GLA_KEY, GLA_VAL, GLA_VAL, GLA_GATE_RANK]
    offs = [0]
    for s in sizes:
        offs.append(offs[-1] + s)
    piece = lambda n: W["ev_w_in"][:, :, offs[n]:offs[n + 1]]
    w_in = jnp.concatenate([piece(0), piece(1), piece(3), piece(4), piece(5), piece(6),
                            _pad_to(jnp.concatenate([piece(2), piece(7)], axis=-1), 2, SMALL)], axis=-1)
    gate_w2 = jnp.pad(W["ev_gla_gate_w2"], ((0, 0), (SSD_HEADS, SMALL - SSD_HEADS - GLA_GATE_RANK), (0, 0)))
    even = dict(
        w_in=bf(w_in), w_out=bf(W["ev_w_out"]),
        conv_w=W["ev_conv_w"], conv_b=row(W["ev_conv_b"]),
        dt_bias=row(_pad_to(W["ev_dt_bias"], 1, SMALL)), a_log=row(_pad_to(W["ev_a_log"], 1, SMALL)),
        d_skip_x=row(jnp.repeat(W["ev_d_skip"], SSD_HEAD_DIM, axis=1)),
        ssd_norm=row(W["ev_ssd_norm"]), gate_w2=bf(gate_w2), gate_b=row(W["ev_gla_gate_b"]),
        gla_norm=row(W["ev_gla_norm"]),
    )
    odd = dict(
        mix_norm=row(W["mix_norm"]), mix=_pad_to(W["od_mix"], 1, 8),
        w_r=bf(W["od_w_r"]), w_k=bf(W["od_w_k"]), w_v=bf(W["od_w_v"]), w_o=bf(W["od_w_o"]),
        w0=row(W["od_w0"]), w1=bf(_pad_to(W["od_w1"], 2, LANES)), w2=bf(_pad_to(W["od_w2"], 1, LANES)),
        a0=row(W["od_a0"]), a1=bf(_pad_to(W["od_a1"], 2, LANES)), a2=bf(_pad_to(W["od_a2"], 1, LANES)),
        v0=row(W["od_v0"]), v1=bf(_pad_to(W["od_v1"], 2, LANES)), v2=bf(_pad_to(W["od_v2"], 1, LANES)),
        g1=bf(_pad_to(W["od_g1"], 2, 2 * LANES)), g2=bf(_pad_to(W["od_g2"], 1, 2 * LANES)),
        k_k=row(W["od_k_k"]), k_a=row(W["od_k_a"]), r_k=row(W["od_r_k"]),
        gn_w=row(W["od_gn_w"]), gn_b=row(W["od_gn_b"]),
    )
    ffn = dict(
        norm=W["ffn_norm"].reshape(DEPTH * 2, 1, D_MODEL),
        w_gu=bf(W["ffn_w_gu"]).reshape(DEPTH * 2, D_MODEL, 2 * D_FF),
        w_down=bf(W["ffn_w_down"]).reshape(DEPTH * 2, D_FF, D_MODEL),
    )
    return dict(even=even, odd=odd, ffn=ffn, mix_norm=row(W["mix_norm"]),
                final_norm=W["final_norm"][None, :])


def _run_group(x3, states, Wp):
    B, L, _ = x3.shape
    T = B * L
    step = L == 1
    x = x3.reshape(T, D_MODEL)
    tm_ffn = min(512, T)
    tm_proj = min(256, T)
    ffn, even, odd = Wp["ffn"], Wp["even"], Wp["odd"]
    convs, ssms, glas, shifts, wkvs = [], [], [], [], []
    ssm_acc = gla_acc = wkv_acc = None
    v_first = None
    for layer in range(DEPTH):
        i = layer // 2
        x = _ffn(x, ffn["norm"], ffn["w_gu"], ffn["w_down"], 2 * layer, tm_ffn)
        if layer % 2 == 0:
            z, xbc, q, k, v, og, sm = _even_in(x, Wp["mix_norm"], even["w_in"], i, tm_proj)
            if step:
                xact, convn, dax, c2, ybase, eg = _even_step_pre(
                    xbc, states[0][i].reshape(B, 3 * SSD_CONV_DIM), sm, even, i)
                ssm_acc, yi = _ssd_step(states[1], ssm_acc, i, dax, c2, xact)
                gla_acc, o = _gla_step(states[2], gla_acc, i, eg, k, q, v)
                x = _even_step_post(x, ybase, yi, dax, z, o, og, even, even["w_out"], i)
                convn = convn.reshape(B, SSD_CONV - 1, SSD_CONV_DIM)
            else:
                y, convn, ssmn = _ssd_seq(xbc, sm, z, even, i, B, L)
                o, glan = _gla_seq(q, k, v, og, sm, even, i, B, L)
                x = _out_proj(x, y, o, even["w_out"], i, tm_ffn)
                ssms.append(ssmn)
                glas.append(glan)
            convs.append(convn)
        else:
            prev = states[3][i] if step else jnp.zeros((B, 1, D_MODEL), F32)
            h, r, dec, k, v, an, bb, gg = _odd_pre(x, prev, v_first, odd, i, tm_proj, L)
            if v_first is None:
                v_first = v
            if step:
                o, wkv_acc = _rwkv_step(states[4], wkv_acc, i, dec, k, an, bb, r, v)
            else:
                o, wkvn = _rwkv_seq(dec, k, an, bb, r, v, B, L)
                wkvs.append(wkvn)
            x = _odd_post(x, o, r, k, v, gg, odd, i, tm_ffn)
            shifts.append(h.reshape(B, L, D_MODEL)[:, -1])
        x = _ffn(x, ffn["norm"], ffn["w_gu"], ffn["w_down"], 2 * layer + 1, tm_ffn,
                 final_g=Wp["final_norm"] if layer == DEPTH - 1 else None)
    y = x.reshape(B, L, D_MODEL)
    if step:
        ssm_out = ssm_acc.reshape(N_EVEN, B, SSD_HEADS, SSD_HEAD_DIM, SSD_D_STATE)
        gla_out = gla_acc.reshape(N_EVEN, B, GLA_HEADS, GLA_DK, GLA_DV)
        wkv_out = wkv_acc.reshape(N_ODD, B, RWKV_HEADS, RWKV_HEAD, RWKV_HEAD)
    else:
        ssm_out, gla_out, wkv_out = jnp.stack(ssms), jnp.stack(glas), jnp.stack(wkvs)
    return y, jnp.stack(convs), ssm_out, gla_out, jnp.stack(shifts), wkv_out


def kernel(x_prompt, x_sample, state_conv, state_ssm, state_gla, state_shift, state_wkv, ffn_norm, ffn_w_gu, ffn_w_down, mix_norm, final_norm, ev_w_in, ev_conv_w, ev_conv_b, ev_dt_bias, ev_a_log, ev_d_skip, ev_ssd_norm, ev_gla_gate_w2, ev_gla_gate_b, ev_gla_norm, ev_w_out, od_mix, od_w0, od_w1, od_w2, od_a0, od_a1, od_a2, od_v0, od_v1, od_v2, od_g1, od_g2, od_k_k, od_k_a, od_r_k, od_w_r, od_w_k, od_w_v, od_w_o, od_gn_w, od_gn_b):
    W = dict(ffn_norm=ffn_norm, ffn_w_gu=ffn_w_gu, ffn_w_down=ffn_w_down, mix_norm=mix_norm,
             final_norm=final_norm, ev_w_in=ev_w_in, ev_conv_w=ev_conv_w, ev_conv_b=ev_conv_b,
             ev_dt_bias=ev_dt_bias, ev_a_log=ev_a_log, ev_d_skip=ev_d_skip, ev_ssd_norm=ev_ssd_norm,
             ev_gla_gate_w2=ev_gla_gate_w2, ev_gla_gate_b=ev_gla_gate_b, ev_gla_norm=ev_gla_norm,
             ev_w_out=ev_w_out, od_mix=od_mix, od_w0=od_w0, od_w1=od_w1, od_w2=od_w2,
             od_a0=od_a0, od_a1=od_a1, od_a2=od_a2, od_v0=od_v0, od_v1=od_v1, od_v2=od_v2,
             od_g1=od_g1, od_g2=od_g2, od_k_k=od_k_k, od_k_a=od_k_a, od_r_k=od_r_k,
             od_w_r=od_w_r, od_w_k=od_w_k, od_w_v=od_w_v, od_w_o=od_w_o,
             od_gn_w=od_gn_w, od_gn_b=od_gn_b)
    Wp = _prepare(W)
    prompt = _run_group(x_prompt, None, Wp)
    sample = _run_group(x_sample, (state_conv, state_ssm, state_gla, state_shift, state_wkv), Wp)
    return (prompt[0], sample[0]) + prompt[1:] + sample[1:]
```

```python
import functools

import jax
import jax.numpy as jnp
from jax import lax
from jax.experimental import pallas as pl
from jax.experimental.pallas import tpu as pltpu

F32 = jnp.float32
BF16 = jnp.bfloat16

D_MODEL = 1024
DEPTH = 4
N_EVEN = 2
N_ODD = 2
EPS = 1e-5
D_FF = 2816

SSD_HEADS = 16
SSD_HEAD_DIM = 64
SSD_INNER = 1024
SSD_GROUPS = 2
SSD_D_STATE = 128
SSD_CONV = 4
SSD_CONV_DIM = 1536
SSD_CHUNK = 64

GLA_HEADS = 4
GLA_DK = 128
GLA_DV = 256
GLA_KEY = 512
GLA_VAL = 1024
GLA_GATE_RANK = 16
GLA_GATE_NORM = 16.0
GLA_CHUNK = 16

RWKV_HEAD = 64
RWKV_HEADS = 16
RWKV_GN_EPS = 64e-5

HEAD_SHIFT = 6
GLA_CHUNK_SHIFT = 4
assert 1 << HEAD_SHIFT == SSD_HEAD_DIM == RWKV_HEAD == SSD_CHUNK and 1 << GLA_CHUNK_SHIFT == GLA_CHUNK

LANES = 128
SMALL = LANES
IN_PERM = SSD_INNER + SSD_CONV_DIM + 2 * GLA_KEY + 2 * GLA_VAL + SMALL
VMEM_LIMIT = 56 * 1024 * 1024
FF_TILE = 1408
SEQ_BLOCK = 256
STEP_BATCH = 8


def _cp(sem):
    return pltpu.CompilerParams(dimension_semantics=sem, vmem_limit_bytes=VMEM_LIMIT)


def _iota(shape, axis):
    return lax.broadcasted_iota(jnp.int32, shape, axis)


def _onehot(mask):
    return jnp.where(mask, 1.0, 0.0).astype(BF16)


def _dot(a, b):
    return jnp.dot(a, b, preferred_element_type=F32)


def _split3(x):
    hi = x.astype(BF16)
    r = x - hi.astype(F32)
    mid = r.astype(BF16)
    lo = (r - mid.astype(F32)).astype(BF16)
    return hi, mid, lo


def _dot01(x, e):
    hi, mid, lo = _split3(x)
    return _dot(hi, e) + _dot(mid, e) + _dot(lo, e)


def _dot01_l(e, x):
    hi, mid, lo = _split3(x)
    return _dot(e, hi) + _dot(e, mid) + _dot(e, lo)


def _rms(x, g):
    ms = jnp.mean(x * x, axis=-1, keepdims=True)
    return x * lax.rsqrt(ms + EPS) * g


def _silu(x):
    return x * jax.nn.sigmoid(x)


def _softplus(x):
    return jnp.maximum(x, 0.0) + jnp.log1p(jnp.exp(-jnp.abs(x)))


def _softplus_abs(x):
    return jnp.maximum(x, 0.0) + jnp.log(1.0 + jnp.exp(-jnp.abs(x)))


def _head_expand():
    return _onehot((_iota((LANES, D_MODEL), 1) >> HEAD_SHIFT) == _iota((LANES, D_MODEL), 0))


def _head_reduce():
    return _onehot((_iota((D_MODEL, LANES), 0) >> HEAD_SHIFT) == _iota((D_MODEL, LANES), 1))


def _ffn_body(x_ref, g_ref, wgu_ref, wd_ref, *rest, mix_out, final_norm):
    o_ref = rest[-1]
    x = x_ref[...]
    if mix_out:
        y_ref, mo_ref, wo_ref = rest[:3]
        x = x + _dot(y_ref[...].astype(BF16), wo_ref[0:SSD_INNER, :])
        x = x + _dot(mo_ref[...].astype(BF16), wo_ref[SSD_INNER:, :])
    xn = _rms(x, g_ref[...]).astype(BF16)
    acc = None
    for j in range(D_FF // FF_TILE):
        lo = j * FF_TILE
        gate = _dot(xn, wgu_ref[:, lo:lo + FF_TILE])
        up = _dot(xn, wgu_ref[:, D_FF + lo:D_FF + lo + FF_TILE])
        part = _dot((_silu(gate) * up).astype(BF16), wd_ref[lo:lo + FF_TILE, :])
        acc = part if acc is None else acc + part
    y = x + 0.5 * acc
    o_ref[...] = _rms(y, rest[-2][...]) if final_norm else y


def _ffn(x, nrm, wgu, wd, ls, tm, final_g=None, mix=None):
    T = x.shape[0]
    resident = pl.Buffered(1)
    tok = pl.BlockSpec((tm, D_MODEL), lambda i: (i, 0))
    args = [x, nrm, wgu, wd]
    specs = [
        tok,
        pl.BlockSpec((None, 1, D_MODEL), lambda i: (ls, 0, 0)),
        pl.BlockSpec((None, D_MODEL, 2 * D_FF), lambda i: (ls, 0, 0), pipeline_mode=resident),
        pl.BlockSpec((None, D_FF, D_MODEL), lambda i: (ls, 0, 0), pipeline_mode=resident),
    ]
    if mix is not None:
        y, o, w_out, li = mix
        args += [y, o, w_out]
        specs += [tok, tok, pl.BlockSpec((None, SSD_INNER + GLA_VAL, D_MODEL), lambda i: (li, 0, 0),
                                         pipeline_mode=resident)]
    if final_g is not None:
        args.append(final_g)
        specs.append(pl.BlockSpec((1, D_MODEL), lambda i: (0, 0)))
    return pl.pallas_call(
        functools.partial(_ffn_body, mix_out=mix is not None, final_norm=final_g is not None),
        grid=(T // tm,),
        in_specs=specs,
        out_specs=pl.BlockSpec((tm, D_MODEL), lambda i: (i, 0)),
        out_shape=jax.ShapeDtypeStruct((T, D_MODEL), F32),
        compiler_params=_cp(("parallel",)),
        name="ffn",
    )(*args)


_EVEN_PIECES = (SSD_INNER, SSD_CONV_DIM, GLA_KEY, GLA_KEY, GLA_VAL, GLA_VAL, SMALL)


def _even_in_body(x_ref, g_ref, w_ref, *out_refs):
    h = _rms(x_ref[...], g_ref[...]).astype(BF16)
    off = 0
    for ref, n in zip(out_refs, _EVEN_PIECES):
        ref[...] = _dot(h, w_ref[:, off:off + n])
        off += n


def _even_in(x, nrm, w, i, tm):
    T = x.shape[0]
    return pl.pallas_call(
        _even_in_body,
        grid=(T // tm,),
        in_specs=[
            pl.BlockSpec((tm, D_MODEL), lambda t: (t, 0)),
            pl.BlockSpec((None, 1, D_MODEL), lambda t: (2 * i, 0, 0)),
            pl.BlockSpec((None, D_MODEL, IN_PERM), lambda t: (i, 0, 0)),
        ],
        out_specs=[pl.BlockSpec((tm, n), lambda t: (t, 0)) for n in _EVEN_PIECES],
        out_shape=[jax.ShapeDtypeStruct((T, n), F32) for n in _EVEN_PIECES],
        compiler_params=_cp(("parallel",)),
        name="even_in",
    )(x, nrm, w)


def _ssd_seq_body(xbc_ref, sm_ref, z_ref, cw_ref, cb_ref, dtb_ref, alog_ref, dsk_ref, nw_ref,
                  y_ref, convn_ref, ssmn_ref,
                  cbuf_ref, xact_ref, acx_ref, dtx_ref, hT_ref, *, nblk):
    Lb = SEQ_BLOCK
    C = SSD_CHUNK
    j = pl.program_id(1)

    @pl.when(j == 0)
    def _():
        cbuf_ref[0:8, :] = jnp.zeros((8, SSD_CONV_DIM), F32)
        hT_ref[...] = jnp.zeros_like(hT_ref)

    x = xbc_ref[...]
    acc = cb_ref[...] + pltpu.roll(x, 3, axis=0) * cw_ref[0:1, :]
    acc = acc + pltpu.roll(x, 2, axis=0) * cw_ref[1:2, :]
    acc = acc + pltpu.roll(x, 1, axis=0) * cw_ref[2:3, :]
    acc = acc + x * cw_ref[3:4, :]
    xact_ref[...] = _silu(acc)
    cbuf_ref[8:16, :] = x[0:8, :]
    head = cb_ref[...] + cbuf_ref[5:13, :] * cw_ref[0:1, :]
    head = head + cbuf_ref[6:14, :] * cw_ref[1:2, :]
    head = head + cbuf_ref[7:15, :] * cw_ref[2:3, :]
    head = head + cbuf_ref[8:16, :] * cw_ref[3:4, :]
    xact_ref[0:8, :] = _silu(head)
    tail = x[Lb - 8:Lb, :]
    cbuf_ref[0:8, :] = tail

    @pl.when(j == nblk - 1)
    def _():
        convn_ref[...] = tail[5:8, :]

    dt = _softplus(sm_ref[...] + dtb_ref[...])
    dta = dt * (-jnp.exp(alog_ref[...]))
    ii = _iota((Lb, Lb), 0)
    jj = _iota((Lb, Lb), 1)
    tril = _onehot(((ii >> HEAD_SHIFT) == (jj >> HEAD_SHIFT)) & (ii >= jj))
    acum = _dot01_l(tril, dta)
    expand = _head_expand()
    acx_ref[...] = _dot01(acum, expand)
    dtx_ref[...] = _dot01(dt, expand)

    li = _iota((C, D_MODEL), 0)
    lj = _iota((C, D_MODEL), 1) & (C - 1)
    diag = li == lj
    causal = li >= lj
    ones_c = jnp.ones((C, C), BF16)
    ones_c2 = jnp.ones((C, 2 * C), BF16)
    tile8 = _onehot((_iota((2 * C, 8 * C), 1) & (C - 1)) == (_iota((2 * C, 8 * C), 0) & (C - 1)))
    pair_mask = (_iota((LANES, LANES), 0) >> HEAD_SHIFT) == (_iota((LANES, LANES), 1) >> HEAD_SHIFT)
    half = SSD_INNER // SSD_GROUPS

    def chunk(c, carry):
        r0 = pl.multiple_of(c * C, C)
        xa = xact_ref[pl.ds(r0, C), :]
        xs = xa[:, :SSD_INNER]
        bms = [xa[:, SSD_INNER + g * SSD_D_STATE:SSD_INNER + (g + 1) * SSD_D_STATE] for g in range(2)]
        cms = [xa[:, SSD_INNER + (2 + g) * SSD_D_STATE:SSD_INNER + (3 + g) * SSD_D_STATE]
               for g in range(2)]
        acx = acx_ref[pl.ds(r0, C), :]
        dtx = dtx_ref[pl.ds(r0, C), :]
        alast = acx[C - 1:C, :]
        arow = _dot01_l(ones_c, jnp.where(diag, acx, 0.0))
        dsel = jnp.where(diag, dtx, 0.0)
        dhi = dsel.astype(BF16)
        dmid = (dsel - dhi.astype(F32)).astype(BF16)
        dtrow = _dot(ones_c2, jnp.concatenate([dhi, dmid], axis=0))
        decay = jnp.exp(jnp.where(causal, acx - arow, -jnp.inf))
        cbt = []
        for g in range(2):
            cb = lax.dot_general(cms[g].astype(BF16), bms[g].astype(BF16),
                                 (((1,), (1,)), ((), ())), preferred_element_type=F32)
            cbt.append(_dot01x2(cb, tile8))
        wts = (jnp.concatenate(cbt, axis=1) * decay * dtrow).astype(BF16)
        ys = []
        for p in range(SSD_HEADS // 2):
            xp = xs[:, p * LANES:(p + 1) * LANES]
            xbd = jnp.where(pair_mask, jnp.concatenate([xp, xp], axis=0), 0.0).astype(BF16)
            ys.append(_dot(wts[:, p * LANES:(p + 1) * LANES], xbd))
        y = jnp.concatenate(ys, axis=1)
        hT = hT_ref[...]
        hTb = hT.astype(BF16)
        yi = [_dot(cms[g].astype(BF16), hTb[:, g * half:(g + 1) * half]) for g in range(2)]
        y = y + jnp.concatenate(yi, axis=1) * jnp.exp(acx)
        xsc = (jnp.exp(alast - acx) * dtx * xs).astype(BF16)
        st = [lax.dot_general(bms[g].astype(BF16), xsc[:, g * half:(g + 1) * half],
                              (((0,), (0,)), ((), ())), preferred_element_type=F32) for g in range(2)]
        hT_ref[...] = hT * jnp.exp(alast) + jnp.concatenate(st, axis=1)
        y = y + dsk_ref[...] * xs
        y = y * _silu(z_ref[pl.ds(r0, C), :])
        outs = []
        for g in range(2):
            yg = y[:, g * half:(g + 1) * half]
            ms = jnp.mean(yg * yg, axis=-1, keepdims=True)
            outs.append(yg * lax.rsqrt(ms + EPS))
        y_ref[pl.ds(r0, C), :] = jnp.concatenate(outs, axis=1) * nw_ref[...]
        return carry

    lax.fori_loop(0, Lb // C, chunk, 0, unroll=2)

    @pl.when(j == nblk - 1)
    def _():
        ssmn_ref[...] = hT_ref[...].T


def _ssd_seq(xbc, sm, z, P, i, B, L):
    Lb = SEQ_BLOCK
    nblk = L // Lb
    T = B * L
    tok = lambda n: pl.BlockSpec((Lb, n), lambda b, j: (b * nblk + j, 0))
    par = lambda r, n: pl.BlockSpec((None, r, n), lambda b, j: (i, 0, 0))
    y, convn, ssmn = pl.pallas_call(
        functools.partial(_ssd_seq_body, nblk=nblk),
        grid=(B, nblk),
        in_specs=[tok(SSD_CONV_DIM), tok(SMALL), tok(SSD_INNER),
                  par(SSD_CONV, SSD_CONV_DIM), par(1, SSD_CONV_DIM), par(1, SMALL), par(1, SMALL),
                  par(1, SSD_INNER), par(1, SSD_INNER)],
        out_specs=[tok(SSD_INNER),
                   pl.BlockSpec((None, SSD_CONV - 1, SSD_CONV_DIM), lambda b, j: (b, 0, 0)),
                   pl.BlockSpec((None, SSD_INNER, SSD_D_STATE), lambda b, j: (b, 0, 0))],
        out_shape=[jax.ShapeDtypeStruct((T, SSD_INNER), F32),
                   jax.ShapeDtypeStruct((B, SSD_CONV - 1, SSD_CONV_DIM), F32),
                   jax.ShapeDtypeStruct((B, SSD_INNER, SSD_D_STATE), F32)],
        scratch_shapes=[pltpu.VMEM((16, SSD_CONV_DIM), F32),
                        pltpu.VMEM((Lb, SSD_CONV_DIM), F32),
                        pltpu.VMEM((Lb, SSD_INNER), F32),
                        pltpu.VMEM((Lb, SSD_INNER), F32),
                        pltpu.VMEM((SSD_D_STATE, SSD_INNER), F32)],
        compiler_params=_cp(("parallel", "arbitrary")),
        name="ssd_seq",
    )(xbc, sm, z, P["conv_w"], P["conv_b"], P["dt_bias"], P["a_log"], P["d_skip_x"], P["ssd_norm"])
    return y, convn, ssmn.reshape(B, SSD_HEADS, SSD_HEAD_DIM, SSD_D_STATE)


def _gla_seq_body(q_ref, k_ref, v_ref, og_ref, sm_ref, w2_ref, gb_ref, nw_ref,
                  o_ref, glan_ref,
                  st_ref, dec_ref, qg_ref, kd_ref, oi_ref, *, nblk):
    Lb = SEQ_BLOCK
    j = pl.program_id(1)

    @pl.when(j == 0)
    def _():
        st_ref[...] = jnp.zeros_like(st_ref)

    gate = _dot(sm_ref[...].astype(BF16), w2_ref[...]) + gb_ref[...]
    lg = -_softplus_abs(-gate) / GLA_GATE_NORM
    ii = _iota((Lb, Lb), 0)
    jj = _iota((Lb, Lb), 1)
    same = (ii >> GLA_CHUNK_SHIFT) == (jj >> GLA_CHUNK_SHIFT)
    causal = same & (ii >= jj)
    b = _dot01_l(_onehot(causal), lg)
    blast = _dot01_l(_onehot(same), lg)
    qg = q_ref[...] * (GLA_DK ** -0.5) * jnp.exp(b)
    kk = k_ref[...]
    kg = kk * jnp.exp(-b)
    kd = kk * jnp.exp(blast - b)
    dec_ref[...] = jnp.exp(blast)
    qg_ref[...] = qg.astype(BF16)
    kd_ref[...] = kd.astype(BF16)

    def chunk(c, carry):
        r0 = pl.multiple_of(c * GLA_CHUNK, GLA_CHUNK)
        for h in range(GLA_HEADS):
            ks = slice(h * GLA_DK, (h + 1) * GLA_DK)
            vs = slice(h * GLA_DV, (h + 1) * GLA_DV)
            st = st_ref[h]
            oi_ref[pl.ds(r0, GLA_CHUNK), vs] = lax.dot_general(
                qg_ref[pl.ds(r0, GLA_CHUNK), ks], st.astype(BF16),
                (((1,), (1,)), ((), ())), preferred_element_type=F32)
            upd = lax.dot_general(v_ref[pl.ds(r0, GLA_CHUNK), vs].astype(BF16),
                                  kd_ref[pl.ds(r0, GLA_CHUNK), ks],
                                  (((0,), (0,)), ((), ())), preferred_element_type=F32)
            st_ref[h] = st * dec_ref[pl.ds(r0, 8), ks][0:1, :] + upd
        return carry

    lax.fori_loop(0, Lb // GLA_CHUNK, chunk, 0, unroll=8)

    for h in range(GLA_HEADS):
        ks = slice(h * GLA_DK, (h + 1) * GLA_DK)
        vs = slice(h * GLA_DV, (h + 1) * GLA_DV)
        att = lax.dot_general(qg[:, ks].astype(BF16), kg[:, ks].astype(BF16),
                              (((1,), (1,)), ((), ())), preferred_element_type=F32)
        att = jnp.where(causal, att, 0.0)
        o_h = _dot(att.astype(BF16), v_ref[:, vs].astype(BF16)) + oi_ref[:, vs]
        ms = jnp.mean(o_h * o_h, axis=-1, keepdims=True)
        o_h = o_h * lax.rsqrt(ms + EPS) * nw_ref[...]
        o_ref[:, vs] = o_h * _silu(og_ref[:, vs])

    @pl.when(j == nblk - 1)
    def _():
        for h in range(GLA_HEADS):
            glan_ref[h] = st_ref[h].T


def _gla_seq(q, k, v, og, sm, P, i, B, L):
    Lb = SEQ_BLOCK
    nblk = L // Lb
    T = B * L
    tok = lambda n: pl.BlockSpec((Lb, n), lambda b, j: (b * nblk + j, 0))
    par = lambda r, n: pl.BlockSpec((None, r, n), lambda b, j: (i, 0, 0))
    return pl.pallas_call(
        functools.partial(_gla_seq_body, nblk=nblk),
        grid=(B, nblk),
        in_specs=[tok(GLA_KEY), tok(GLA_KEY), tok(GLA_VAL), tok(GLA_VAL), tok(SMALL),
                  par(SMALL, GLA_KEY), par(1, GLA_KEY), par(1, GLA_DV)],
        out_specs=[tok(GLA_VAL),
                   pl.BlockSpec((None, GLA_HEADS, GLA_DK, GLA_DV), lambda b, j: (b, 0, 0, 0))],
        out_shape=[jax.ShapeDtypeStruct((T, GLA_VAL), F32),
                   jax.ShapeDtypeStruct((B, GLA_HEADS, GLA_DK, GLA_DV), F32)],
        scratch_shapes=[pltpu.VMEM((GLA_HEADS, GLA_DV, GLA_DK), F32),
                        pltpu.VMEM((Lb, GLA_KEY), F32),
                        pltpu.VMEM((Lb, GLA_KEY), BF16),
                        pltpu.VMEM((Lb, GLA_KEY), BF16),
                        pltpu.VMEM((Lb, GLA_VAL), F32)],
        compiler_params=_cp(("parallel", "arbitrary")),
        name="gla_seq",
    )(q, k, v, og, sm, P["gate_w2"], P["gate_b"], P["gla_norm"])


def _even_step_pre_body(xbc_ref, c0_ref, sm_ref, cw_ref, cb_ref, dtb_ref, alog_ref, dsk_ref,
                        w2_ref, gb_ref,
                        xact_ref, convn_ref, dax_ref, c2_ref, ybase_ref, eg_ref):
    u = xbc_ref[...]
    n = SSD_CONV_DIM
    acc = cb_ref[...] + c0_ref[:, 0:n] * cw_ref[0:1, :]
    acc = acc + c0_ref[:, n:2 * n] * cw_ref[1:2, :]
    acc = acc + c0_ref[:, 2 * n:3 * n] * cw_ref[2:3, :]
    acc = acc + u * cw_ref[3:4, :]
    xa = _silu(acc)
    xact_ref[...] = xa
    convn_ref[:, 0:n] = c0_ref[:, n:2 * n]
    convn_ref[:, n:2 * n] = c0_ref[:, 2 * n:3 * n]
    convn_ref[:, 2 * n:3 * n] = u
    dt = _softplus(sm_ref[...] + dtb_ref[...])
    dta = dt * (-jnp.exp(alog_ref[...]))
    expand = _head_expand()
    dtx = _dot01(dt, expand)
    dax_ref[...] = jnp.exp(_dot01(dta, expand))
    xs = xa[:, :SSD_INNER]
    c2_ref[...] = dtx * xs
    half = SSD_INNER // SSD_GROUPS
    cbs = []
    for g in range(2):
        bm = xa[:, SSD_INNER + g * SSD_D_STATE:SSD_INNER + (g + 1) * SSD_D_STATE]
        cm = xa[:, SSD_INNER + (2 + g) * SSD_D_STATE:SSD_INNER + (3 + g) * SSD_D_STATE]
        cb = jnp.sum(cm * bm, axis=-1, keepdims=True)
        cbs.append(jnp.broadcast_to(cb, (cb.shape[0], half)))
    ybase_ref[...] = jnp.concatenate(cbs, axis=1) * dtx * xs + dsk_ref[...] * xs
    gate = _dot(sm_ref[...].astype(BF16), w2_ref[...]) + gb_ref[...]
    eg_ref[...] = jnp.exp(-_softplus(-gate) / GLA_GATE_NORM)


def _even_step_pre(xbc, conv0, sm, P, i):
    Bn = xbc.shape[0]
    full = lambda n: pl.BlockSpec((Bn, n), lambda t: (0, 0))
    par = lambda r, n: pl.BlockSpec((None, r, n), lambda t: (i, 0, 0))
    outs = (SSD_CONV_DIM, 3 * SSD_CONV_DIM, SSD_INNER, SSD_INNER, SSD_INNER, GLA_KEY)
    return pl.pallas_call(
        _even_step_pre_body,
        grid=(1,),
        in_specs=[full(SSD_CONV_DIM), full(3 * SSD_CONV_DIM), full(SMALL),
                  par(SSD_CONV, SSD_CONV_DIM), par(1, SSD_CONV_DIM), par(1, SMALL), par(1, SMALL),
                  par(1, SSD_INNER), par(SMALL, GLA_KEY), par(1, GLA_KEY)],
        out_specs=[full(n) for n in outs],
        out_shape=[jax.ShapeDtypeStruct((Bn, n), F32) for n in outs],
        compiler_params=_cp(("arbitrary",)),
        name="even_step_pre",
    )(xbc, conv0, sm, P["conv_w"], P["conv_b"], P["dt_bias"], P["a_log"], P["d_skip_x"],
      P["gate_w2"], P["gate_b"])


def _to_cols(x):
    Bn, N = x.shape
    return x.reshape(Bn // STEP_BATCH, STEP_BATCH, N).transpose(0, 2, 1)


def _from_cols(x):
    nb, N, bb = x.shape
    return x.transpose(0, 2, 1).reshape(nb * bb, N)


def _layer_state_call(body, name, state_all, acc, i, rows, width, args, specs, out_spec, out_shape):
    nl, Bn = state_all.shape[:2]
    st = pl.BlockSpec((None, STEP_BATCH, rows, width), lambda t: (i, t, 0, 0))
    args = [state_all.reshape(nl, Bn, rows, width)] + list(args)
    specs = [st] + list(specs)
    aliases = {}
    if acc is not None:
        aliases = {len(args): 0}
        args.append(acc)
        specs.append(pl.BlockSpec(memory_space=pl.ANY))
    return pl.pallas_call(
        body,
        grid=(Bn // STEP_BATCH,),
        in_specs=specs,
        out_specs=[st, out_spec],
        out_shape=[jax.ShapeDtypeStruct((nl, Bn, rows, width), F32), out_shape],
        input_output_aliases=aliases,
        compiler_params=_cp(("parallel",)),
        name=name,
    )(*args)


def _ssd_step_body(h_ref, c1_ref, c2_ref, bm_ref, cm_ref, *rest):
    hn_ref, y_ref = rest[-2:]
    half = SSD_INNER // SSD_GROUPS
    lane = _iota((SSD_INNER, STEP_BATCH), 1)
    ycols = jnp.zeros((SSD_INNER, STEP_BATCH), F32)
    c1 = c1_ref[...]
    c2 = c2_ref[...]
    for b in range(STEP_BATCH):
        h0 = h_ref[b]
        bmat = jnp.concatenate(
            [jnp.broadcast_to(bm_ref[b:b + 1, g * SSD_D_STATE:(g + 1) * SSD_D_STATE],
                              (half, SSD_D_STATE)) for g in range(2)], axis=0)
        cmat = jnp.concatenate(
            [jnp.broadcast_to(cm_ref[b:b + 1, g * SSD_D_STATE:(g + 1) * SSD_D_STATE],
                              (half, SSD_D_STATE)) for g in range(2)], axis=0)
        ycol = jnp.sum(h0 * cmat, axis=1, keepdims=True)
        ycols = jnp.where(lane == b, ycol, ycols)
        hn_ref[b] = h0 * c1[:, b:b + 1] + c2[:, b:b + 1] * bmat
    y_ref[...] = ycols


def _ssd_step(state_all, acc, i, dax, c2, xact):
    Bn = state_all.shape[1]
    bb = STEP_BATCH
    bm = xact[:, SSD_INNER:SSD_INNER + 2 * SSD_D_STATE]
    cm = xact[:, SSD_INNER + 2 * SSD_D_STATE:]
    col = pl.BlockSpec((None, SSD_INNER, bb), lambda t: (t, 0, 0))
    row = pl.BlockSpec((bb, 2 * SSD_D_STATE), lambda t: (t, 0))
    acc, ycols = _layer_state_call(
        _ssd_step_body, "ssd_step", state_all, acc, i, SSD_INNER, SSD_D_STATE,
        [_to_cols(dax), _to_cols(c2), bm, cm], [col, col, row, row],
        col, jax.ShapeDtypeStruct((Bn // bb, SSD_INNER, bb), F32))
    return acc, _from_cols(ycols)


def _gla_step_body(s_ref, eg_ref, kc_ref, qc_ref, v_ref, *rest):
    sn_ref, o_ref = rest[-2:]
    eg = eg_ref[...]
    kc = kc_ref[...]
    qc = qc_ref[...] * (GLA_DK ** -0.5)
    for b in range(STEP_BATCH):
        s0 = s_ref[b]
        vmat = jnp.concatenate(
            [jnp.broadcast_to(v_ref[b:b + 1, h * GLA_DV:(h + 1) * GLA_DV], (GLA_DK, GLA_DV))
             for h in range(GLA_HEADS)], axis=0)
        sn = s0 * eg[:, b:b + 1] + kc[:, b:b + 1] * vmat
        sn_ref[b] = sn
        t = sn * qc[:, b:b + 1]
        o_ref[b:b + 1, :] = jnp.concatenate(
            [jnp.sum(t[h * GLA_DK:(h + 1) * GLA_DK], axis=0, keepdims=True)
             for h in range(GLA_HEADS)], axis=1)


def _gla_step(state_all, acc, i, eg, k, q, v):
    Bn = state_all.shape[1]
    bb = STEP_BATCH
    col = pl.BlockSpec((None, GLA_KEY, bb), lambda t: (t, 0, 0))
    row = pl.BlockSpec((bb, GLA_VAL), lambda t: (t, 0))
    return _layer_state_call(
        _gla_step_body, "gla_step", state_all, acc, i, GLA_KEY, GLA_DV,
        [_to_cols(eg), _to_cols(k), _to_cols(q), v], [col, col, col, row],
        row, jax.ShapeDtypeStruct((Bn, GLA_VAL), F32))


def _even_step_post_body(x_ref, ybase_ref, yi_ref, dax_ref, z_ref, o_ref, og_ref,
                         snw_ref, gnw_ref, w_ref, out_ref):
    y = (ybase_ref[...] + yi_ref[...] * dax_ref[...]) * _silu(z_ref[...])
    half = SSD_INNER // SSD_GROUPS
    acc = None
    for g in range(2):
        yg = y[:, g * half:(g + 1) * half]
        ms = jnp.mean(yg * yg, axis=-1, keepdims=True)
        yn = yg * lax.rsqrt(ms + EPS) * snw_ref[:, g * half:(g + 1) * half]
        part = _dot(yn.astype(BF16), w_ref[g * half:(g + 1) * half, :])
        acc = part if acc is None else acc + part
    for h in range(GLA_HEADS):
        vs = slice(h * GLA_DV, (h + 1) * GLA_DV)
        oh = o_ref[:, vs]
        ms = jnp.mean(oh * oh, axis=-1, keepdims=True)
        on = oh * lax.rsqrt(ms + EPS) * gnw_ref[...] * _silu(og_ref[:, vs])
        acc = acc + _dot(on.astype(BF16), w_ref[SSD_INNER + h * GLA_DV:SSD_INNER + (h + 1) * GLA_DV, :])
    out_ref[...] = x_ref[...] + acc


def _even_step_post(x, ybase, yi, dax, z, o, og, P, w_out, i):
    Bn = x.shape[0]
    full = pl.BlockSpec((Bn, D_MODEL), lambda t: (0, 0))
    return pl.pallas_call(
        _even_step_post_body,
        grid=(1,),
        in_specs=[full] * 7 + [pl.BlockSpec((None, 1, SSD_INNER), lambda t: (i, 0, 0)),
                               pl.BlockSpec((None, 1, GLA_DV), lambda t: (i, 0, 0)),
                               pl.BlockSpec((None, SSD_INNER + GLA_VAL, D_MODEL), lambda t: (i, 0, 0))],
        out_specs=full,
        out_shape=jax.ShapeDtypeStruct((Bn, D_MODEL), F32),
        compiler_params=_cp(("arbitrary",)),
        name="even_step_post",
    )(x, ybase, yi, dax, z, o, og, P["ssd_norm"], P["gla_norm"], w_out)


_ODD_OUTS = 8


def _odd_pre_body(*refs, step, has_vfirst, nblk_seq, tm):
    it = iter(refs)
    x_ref = next(it)
    if step:
        prev_ref = next(it)
    else:
        xp_ref = next(it)
        sh0_ref = next(it)
    g_ref, mix_ref = next(it), next(it)
    wr_ref, wk_ref, wv_ref = next(it), next(it), next(it)
    w0_ref, w1_ref, w2_ref = next(it), next(it), next(it)
    a0_ref, a1_ref, a2_ref = next(it), next(it), next(it)
    if has_vfirst:
        vf_ref, v0_ref, v1_ref, v2_ref = next(it), next(it), next(it), next(it)
    g1_ref, g2_ref = next(it), next(it)
    kkw_ref, kaw_ref = next(it), next(it)
    h_ref, r_ref, dec_ref, k_ref, v_ref, an_ref, b_ref, gg_ref = (next(it) for _ in range(_ODD_OUTS))

    h = _rms(x_ref[...], g_ref[...])
    if step:
        prev = prev_ref[...]
    else:
        i = pl.program_id(0)
        hp = _rms(xp_ref[...], g_ref[...])[7:8, :]
        prow = jnp.where(i % nblk_seq == 0, sh0_ref[...], hp)
        prev = jnp.where(_iota((tm, D_MODEL), 0) == 0, prow, pltpu.roll(h, 1, axis=0))
    h_ref[...] = h
    xx = prev - h
    mixed = lambda r: (h + xx * mix_ref[r:r + 1, :]).astype(BF16)
    xr, xw, xk, xv, xa, xg = (mixed(r) for r in range(6))
    r_ref[...] = _dot(xr, wr_ref[...])
    wl = w0_ref[...] + _dot(jnp.tanh(_dot(xw, w1_ref[...])).astype(BF16), w2_ref[...])
    w = -_softplus_abs(-wl) - 0.5
    dec_ref[...] = jnp.exp(-jnp.exp(w))
    k = _dot(xk, wk_ref[...])
    v = _dot(xv, wv_ref[...])
    if has_vfirst:
        gate = jax.nn.sigmoid(v0_ref[...] + _dot(_dot(xv, v1_ref[...]).astype(BF16), v2_ref[...]))
        v = v + (vf_ref[...] - v) * gate
    v_ref[...] = v
    a = jax.nn.sigmoid(a0_ref[...] + _dot(_dot(xa, a1_ref[...]).astype(BF16), a2_ref[...]))
    gg_ref[...] = _dot(jax.nn.sigmoid(_dot(xg, g1_ref[...])).astype(BF16), g2_ref[...])
    kkf = k * kkw_ref[...]
    red = _head_reduce()
    expand = _head_expand()
    ss = _dot01x2(kkf * kkf, jnp.concatenate([red, red], axis=0))
    kk = kkf * _dot01x2(lax.rsqrt(jnp.maximum(ss, 1e-24)), jnp.concatenate([expand, expand], axis=0))
    k_ref[...] = k * (1.0 + (a - 1.0) * kaw_ref[...])
    an_ref[...] = -kk
    b_ref[...] = kk * a


def _odd_pre(x, prev_or_shift0, vfirst, P, i, tm, L):
    T = x.shape[0]
    step = L == 1
    has_vfirst = vfirst is not None
    nblk_seq = max(L // tm, 1)
    tok = pl.BlockSpec((tm, D_MODEL), lambda t: (t, 0))
    par = lambda r, n, idx=i: pl.BlockSpec((None, r, n), lambda t: (idx, 0, 0), pipeline_mode=pl.Buffered(1))
    args = [x]
    specs = [tok]
    if step:
        args.append(prev_or_shift0)
        specs.append(tok)
    else:
        args += [x, prev_or_shift0]
        specs += [pl.BlockSpec((8, D_MODEL), lambda t: (jnp.maximum(t * (tm // 8) - 1, 0), 0)),
                  pl.BlockSpec((None, 1, D_MODEL), lambda t: (t // nblk_seq, 0, 0))]
    args += [P["mix_norm"], P["mix"], P["w_r"], P["w_k"], P["w_v"], P["w0"], P["w1"], P["w2"],
             P["a0"], P["a1"], P["a2"]]
    specs += [pl.BlockSpec((None, 1, D_MODEL), lambda t: (2 * i + 1, 0, 0)),
              par(8, D_MODEL), par(D_MODEL, D_MODEL), par(D_MODEL, D_MODEL), par(D_MODEL, D_MODEL),
              par(1, D_MODEL), par(D_MODEL, LANES), par(LANES, D_MODEL),
              par(1, D_MODEL), par(D_MODEL, LANES), par(LANES, D_MODEL)]
    if has_vfirst:
        args += [vfirst, P["v0"], P["v1"], P["v2"]]
        specs += [tok, par(1, D_MODEL, i - 1), par(D_MODEL, LANES, i - 1), par(LANES, D_MODEL, i - 1)]
    args += [P["g1"], P["g2"], P["k_k"], P["k_a"]]
    specs += [par(D_MODEL, 2 * LANES), par(2 * LANES, D_MODEL), par(1, D_MODEL), par(1, D_MODEL)]
    return pl.pallas_call(
        functools.partial(_odd_pre_body, step=step, has_vfirst=has_vfirst, nblk_seq=nblk_seq, tm=tm),
        grid=(T // tm,),
        in_specs=specs,
        out_specs=[tok] * _ODD_OUTS,
        out_shape=[jax.ShapeDtypeStruct((T, D_MODEL), F32)] * _ODD_OUTS,
        compiler_params=_cp(("parallel",)),
        name="odd_pre",
    )(*args)


def _dot01x2(x, e2):
    hi = x.astype(BF16)
    mid = (x - hi.astype(F32)).astype(BF16)
    return _dot(jnp.concatenate([hi, mid], axis=1), e2)


def _odd_post_body(x_ref, o_ref, r_ref, k_ref, v_ref, g_ref, gnw_ref, gnb_ref, rk_ref, wo_ref, out_ref):
    red = _head_reduce()
    red = jnp.concatenate([red, red], axis=0)
    expand = _head_expand()
    expand = jnp.concatenate([expand, expand], axis=0)
    o = o_ref[...]
    mu = _dot01x2(_dot01x2(o, red) * (1.0 / RWKV_HEAD), expand)
    d = o - mu
    var = _dot01x2(d * d, red) * (1.0 / RWKV_HEAD)
    on = d * _dot01x2(lax.rsqrt(var + RWKV_GN_EPS), expand) * gnw_ref[...] + gnb_ref[...]
    v = v_ref[...]
    bonus = _dot01x2(_dot01x2(r_ref[...] * k_ref[...] * rk_ref[...], red), expand) * v
    out_ref[...] = x_ref[...] + _dot(((on + bonus) * g_ref[...]).astype(BF16), wo_ref[...])


def _odd_post(x, o, r, k, v, g, P, i, tm):
    T = x.shape[0]
    tok = pl.BlockSpec((tm, D_MODEL), lambda t: (t, 0))
    par = lambda r_, n: pl.BlockSpec((None, r_, n), lambda t: (i, 0, 0))
    return pl.pallas_call(
        _odd_post_body,
        grid=(T // tm,),
        in_specs=[tok] * 6 + [par(1, D_MODEL), par(1, D_MODEL), par(1, D_MODEL), par(D_MODEL, D_MODEL)],
        out_specs=tok,
        out_shape=jax.ShapeDtypeStruct((T, D_MODEL), F32),
        compiler_params=_cp(("parallel",)),
        name="odd_post",
    )(x, o, r, k, v, g, P["gn_w"], P["gn_b"], P["r_k"], P["w_o"])


RWKV_TB = 64
RWKV_NPAIR = RWKV_HEADS // 2


RWKV_SEQS = 2
RWKV_SPLIT = 1
RWKV_UNROLL = 8


def _rwkv_seq_body(w_ref, k_ref, a_ref, b_ref, r_ref, v_ref, o_ref, sn_ref, s_ref, vt_ref, oacc_ref, *, ntb):
    N = RWKV_HEAD
    R = RWKV_NPAIR * N
    tbi = pl.program_id(1)

    @pl.when(tbi == 0)
    def _():
        s_ref[...] = jnp.zeros_like(s_ref)

    seg = _onehot((_iota((2 * LANES, 2 * LANES), 0) >> HEAD_SHIFT) == (_iota((2 * LANES, 2 * LANES), 1) >> HEAD_SHIFT))

    def seg_sum(xb):
        h = xb.shape[0] // 2
        both = _dot(jnp.concatenate([xb[:h], xb[h:]], axis=1), seg)
        return jnp.concatenate([both[:, :LANES], both[:, LANES:]], axis=0)
    for c in range(RWKV_SEQS):
        for p in range(RWKV_NPAIR):
            bt = v_ref[c, :, p * LANES:(p + 1) * LANES].T
            vt_ref[c, p * N:(p + 1) * N, :] = jnp.concatenate([bt[0:N], bt[N:2 * N]], axis=1)
    oacc_ref[...] = jnp.zeros_like(oacc_ref)
    G = R // RWKV_SPLIT
    lane_step = _iota((G, LANES), 1) & (N - 1)
    lane_base = (_iota((G, LANES), 1) >> HEAD_SHIFT) * N

    def steps(tu, carry):
        row0 = pl.multiple_of(tu * RWKV_UNROLL, RWKV_UNROLL)
        for j in range(RWKV_UNROLL):
            t = tu * RWKV_UNROLL + j
            for c in range(RWKV_SEQS):
                for g in range(RWKV_SPLIT):
                    rs = slice(g * G, (g + 1) * G)

                    def rows(ref):
                        return jnp.concatenate(
                            [jnp.broadcast_to(
                                ref[c, pl.ds(row0, RWKV_UNROLL), p * LANES:(p + 1) * LANES][j:j + 1, :], (N, LANES))
                             for p in range(g * G // N, (g + 1) * G // N)], axis=0)

                    s = s_ref[c, rs, :]
                    sa = seg_sum((s * rows(a_ref)).astype(BF16))
                    vcol = jnp.take_along_axis(vt_ref[c, rs, :], lane_base + t, axis=1)
                    s = (s * rows(w_ref) + vcol * rows(k_ref)) + sa * rows(b_ref)
                    s_ref[c, rs, :] = s
                    ob = seg_sum((s * rows(r_ref)).astype(BF16))
                    oacc_ref[c, rs, :] = jnp.where(lane_step == t, ob, oacc_ref[c, rs, :])
        return carry

    lax.fori_loop(0, RWKV_TB // RWKV_UNROLL, steps, 0)

    for c in range(RWKV_SEQS):
        for p in range(RWKV_NPAIR):
            ot = oacc_ref[c, p * N:(p + 1) * N, :].T
            o_ref[c, :, p * LANES:(p + 1) * LANES] = jnp.concatenate([ot[0:N], ot[N:2 * N]], axis=1)

    @pl.when(tbi == ntb - 1)
    def _():
        for c in range(RWKV_SEQS):
            for p in range(RWKV_NPAIR):
                for hh in range(2):
                    sn_ref[c, (2 * p + hh) * N:(2 * p + hh + 1) * N, :] = (
                        s_ref[c, p * N:(p + 1) * N, hh * N:(hh + 1) * N])


def _rwkv_seq(w, k, a, b, r, v, B, L):
    ntb = L // RWKV_TB
    nc = RWKV_SEQS
    as_seqs = lambda x: x.reshape(B, L, D_MODEL)
    tok = pl.BlockSpec((nc, RWKV_TB, D_MODEL), lambda bi, t: (bi, t, 0))
    state_rows = RWKV_NPAIR * RWKV_HEAD
    o, sn = pl.pallas_call(
        functools.partial(_rwkv_seq_body, ntb=ntb),
        grid=(B // nc, ntb),
        in_specs=[tok] * 6,
        out_specs=[tok, pl.BlockSpec((nc, D_MODEL, RWKV_HEAD), lambda bi, t: (bi, 0, 0))],
        out_shape=[jax.ShapeDtypeStruct((B, L, D_MODEL), F32),
                   jax.ShapeDtypeStruct((B, D_MODEL, RWKV_HEAD), F32)],
        scratch_shapes=[pltpu.VMEM((nc, state_rows, LANES), F32)] * 3,
        compiler_params=_cp(("parallel", "arbitrary")),
        name="rwkv_seq",
    )(as_seqs(w), as_seqs(k), as_seqs(a), as_seqs(b), as_seqs(r), as_seqs(v))
    return o.reshape(B * L, D_MODEL), sn.reshape(B, RWKV_HEADS, RWKV_HEAD, RWKV_HEAD)


def _rwkv_step_body(s_ref, w_ref, k_ref, a_ref, b_ref, r_ref, vc_ref, *rest):
    sn_ref, o_ref = rest[-2:]
    N = RWKV_HEAD
    lane = _iota((D_MODEL, STEP_BATCH), 1)
    ocols = jnp.zeros((D_MODEL, STEP_BATCH), F32)
    vc = vc_ref[...]

    def rows(ref, b):
        return jnp.concatenate(
            [jnp.broadcast_to(ref[b:b + 1, h * N:(h + 1) * N], (N, N)) for h in range(RWKV_HEADS)], axis=0)

    for b in range(STEP_BATCH):
        s = s_ref[b]
        sa = jnp.sum(s * rows(a_ref, b), axis=1, keepdims=True)
        s = s * rows(w_ref, b) + sa * rows(b_ref, b) + vc[:, b:b + 1] * rows(k_ref, b)
        sn_ref[b] = s
        ocol = jnp.sum(s * rows(r_ref, b), axis=1, keepdims=True)
        ocols = jnp.where(lane == b, ocol, ocols)
    o_ref[...] = ocols


def _rwkv_step(state_all, acc, i, w, k, a, b, r, v):
    Bn = state_all.shape[1]
    bb = STEP_BATCH
    col = pl.BlockSpec((None, D_MODEL, bb), lambda t: (t, 0, 0))
    row = pl.BlockSpec((bb, D_MODEL), lambda t: (t, 0))
    acc, ocols = _layer_state_call(
        _rwkv_step_body, "rwkv_step", state_all, acc, i, D_MODEL, RWKV_HEAD,
        [w, k, a, b, r, _to_cols(v)], [row] * 5 + [col],
        col, jax.ShapeDtypeStruct((Bn // bb, D_MODEL, bb), F32))
    return _from_cols(ocols), acc


def _pad_to(w, axis, n):
    pad = [(0, 0)] * w.ndim
    pad[axis] = (0, n - w.shape[axis])
    return jnp.pad(w, pad)


def _prepare(W):
    bf = lambda w: w.astype(BF16)
    row = lambda w: w[:, None, :]
    sizes = [SSD_INNER, SSD_CONV_DIM, SSD_HEADS, GLA_KEY, GLA_KEY, GLA_VAL, GLA_VAL, GLA_GATE_RANK]
    offs = [0]
    for s in sizes:
        offs.append(offs[-1] + s)
    piece = lambda n: W["ev_w_in"][:, :, offs[n]:offs[n + 1]]
    w_in = jnp.concatenate([piece(0), piece(1), piece(3), piece(4), piece(5), piece(6),
                            _pad_to(jnp.concatenate([piece(2), piece(7)], axis=-1), 2, SMALL)], axis=-1)
    gate_w2 = jnp.pad(W["ev_gla_gate_w2"], ((0, 0), (SSD_HEADS, SMALL - SSD_HEADS - GLA_GATE_RANK), (0, 0)))
    even = dict(
        w_in=bf(w_in), w_out=bf(W["ev_w_out"]),
        conv_w=W["ev_conv_w"], conv_b=row(W["ev_conv_b"]),
        dt_bias=row(_pad_to(W["ev_dt_bias"], 1, SMALL)), a_log=row(_pad_to(W["ev_a_log"], 1, SMALL)),
        d_skip_x=row(jnp.repeat(W["ev_d_skip"], SSD_HEAD_DIM, axis=1)),
        ssd_norm=row(W["ev_ssd_norm"]), gate_w2=bf(gate_w2), gate_b=row(W["ev_gla_gate_b"]),
        gla_norm=row(W["ev_gla_norm"]),
    )
    odd = dict(
        mix_norm=row(W["mix_norm"]), mix=_pad_to(W["od_mix"], 1, 8),
        w_r=bf(W["od_w_r"]), w_k=bf(W["od_w_k"]), w_v=bf(W["od_w_v"]), w_o=bf(W["od_w_o"]),
        w0=row(W["od_w0"]), w1=bf(_pad_to(W["od_w1"], 2, LANES)), w2=bf(_pad_to(W["od_w2"], 1, LANES)),
        a0=row(W["od_a0"]), a1=bf(_pad_to(W["od_a1"], 2, LANES)), a2=bf(_pad_to(W["od_a2"], 1, LANES)),
        v0=row(W["od_v0"]), v1=bf(_pad_to(W["od_v1"], 2, LANES)), v2=bf(_pad_to(W["od_v2"], 1, LANES)),
        g1=bf(_pad_to(W["od_g1"], 2, 2 * LANES)), g2=bf(_pad_to(W["od_g2"], 1, 2 * LANES)),
        k_k=row(W["od_k_k"]), k_a=row(W["od_k_a"]), r_k=row(W["od_r_k"]),
        gn_w=row(W["od_gn_w"]), gn_b=row(W["od_gn_b"]),
    )
    ffn = dict(
        norm=W["ffn_norm"].reshape(DEPTH * 2, 1, D_MODEL),
        w_gu=bf(W["ffn_w_gu"]).reshape(DEPTH * 2, D_MODEL, 2 * D_FF),
        w_down=bf(W["ffn_w_down"]).reshape(DEPTH * 2, D_FF, D_MODEL),
    )
    return dict(even=even, odd=odd, ffn=ffn, mix_norm=row(W["mix_norm"]),
                final_norm=W["final_norm"][None, :])


def _run_group(x3, states, Wp):
    B, L, _ = x3.shape
    T = B * L
    step = L == 1
    x = x3.reshape(T, D_MODEL)
    tm_ffn = min(512, T)
    tm_proj = min(256, T)
    ffn, even, odd = Wp["ffn"], Wp["even"], Wp["odd"]
    convs, ssms, glas, shifts, wkvs = [], [], [], [], []
    ssm_acc = gla_acc = wkv_acc = None
    v_first = None
    for layer in range(DEPTH):
        i = layer // 2
        mix = None
        x = _ffn(x, ffn["norm"], ffn["w_gu"], ffn["w_down"], 2 * layer, tm_ffn)
        if layer % 2 == 0:
            z, xbc, q, k, v, og, sm = _even_in(x, Wp["mix_norm"], even["w_in"], i, tm_proj)
            if step:
                xact, convn, dax, c2, ybase, eg = _even_step_pre(
                    xbc, states[0][i].reshape(B, 3 * SSD_CONV_DIM), sm, even, i)
                ssm_acc, yi = _ssd_step(states[1], ssm_acc, i, dax, c2, xact)
                gla_acc, o = _gla_step(states[2], gla_acc, i, eg, k, q, v)
                x = _even_step_post(x, ybase, yi, dax, z, o, og, even, even["w_out"], i)
                convn = convn.reshape(B, SSD_CONV - 1, SSD_CONV_DIM)
            else:
                y, convn, ssmn = _ssd_seq(xbc, sm, z, even, i, B, L)
                o, glan = _gla_seq(q, k, v, og, sm, even, i, B, L)
                mix = (y, o, even["w_out"], i)
                ssms.append(ssmn)
                glas.append(glan)
            convs.append(convn)
        else:
            prev = states[3][i] if step else jnp.zeros((B, 1, D_MODEL), F32)
            h, r, dec, k, v, an, bb, gg = _odd_pre(x, prev, v_first, odd, i, tm_proj, L)
            if v_first is None:
                v_first = v
            if step:
                o, wkv_acc = _rwkv_step(states[4], wkv_acc, i, dec, k, an, bb, r, v)
            else:
                o, wkvn = _rwkv_seq(dec, k, an, bb, r, v, B, L)
                wkvs.append(wkvn)
            x = _odd_post(x, o, r, k, v, gg, odd, i, tm_ffn)
            shifts.append(h.reshape(B, L, D_MODEL)[:, -1])
        x = _ffn(x, ffn["norm"], ffn["w_gu"], ffn["w_down"], 2 * layer + 1, tm_ffn,
                 final_g=Wp["final_norm"] if layer == DEPTH - 1 else None, mix=mix)
    y = x.reshape(B, L, D_MODEL)
    if step:
        ssm_out = ssm_acc.reshape(N_EVEN, B, SSD_HEADS, SSD_HEAD_DIM, SSD_D_STATE)
        gla_out = gla_acc.reshape(N_EVEN, B, GLA_HEADS, GLA_DK, GLA_DV)
        wkv_out = wkv_acc.reshape(N_ODD, B, RWKV_HEADS, RWKV_HEAD, RWKV_HEAD)
    else:
        ssm_out, gla_out, wkv_out = jnp.stack(ssms), jnp.stack(glas), jnp.stack(wkvs)
    return y, jnp.stack(convs), ssm_out, gla_out, jnp.stack(shifts), wkv_out


def kernel(x_prompt, x_sample, state_conv, state_ssm, state_gla, state_shift, state_wkv, ffn_norm, ffn_w_gu, ffn_w_down, mix_norm, final_norm, ev_w_in, ev_conv_w, ev_conv_b, ev_dt_bias, ev_a_log, ev_d_skip, ev_ssd_norm, ev_gla_gate_w2, ev_gla_gate_b, ev_gla_norm, ev_w_out, od_mix, od_w0, od_w1, od_w2, od_a0, od_a1, od_a2, od_v0, od_v1, od_v2, od_g1, od_g2, od_k_k, od_k_a, od_r_k, od_w_r, od_w_k, od_w_v, od_w_o, od_gn_w, od_gn_b):
    W = dict(ffn_norm=ffn_norm, ffn_w_gu=ffn_w_gu, ffn_w_down=ffn_w_down, mix_norm=mix_norm,
             final_norm=final_norm, ev_w_in=ev_w_in, ev_conv_w=ev_conv_w, ev_conv_b=ev_conv_b,
             ev_dt_bias=ev_dt_bias, ev_a_log=ev_a_log, ev_d_skip=ev_d_skip, ev_ssd_norm=ev_ssd_norm,
             ev_gla_gate_w2=ev_gla_gate_w2, ev_gla_gate_b=ev_gla_gate_b, ev_gla_norm=ev_gla_norm,
             ev_w_out=ev_w_out, od_mix=od_mix, od_w0=od_w0, od_w1=od_w1, od_w2=od_w2,
             od_a0=od_a0, od_a1=od_a1, od_a2=od_a2, od_v0=od_v0, od_v1=od_v1, od_v2=od_v2,
             od_g1=od_g1, od_g2=od_g2, od_k_k=od_k_k, od_k_a=od_k_a, od_r_k=od_r_k,
             od_w_r=od_w_r, od_w_k=od_w_k, od_w_v=od_w_v, od_w_o=od_w_o,
             od_gn_w=od_gn_w, od_gn_b=od_gn_b)
    Wp = _prepare(W)
    prompt = _run_group(x_prompt, None, Wp)
    sample = _run_group(x_sample, (state_conv, state_ssm, state_gla, state_shift, state_wkv), Wp)
    return (prompt[0], sample[0]) + prompt[1:] + sample[1:]
```

```python
import functools

import jax
import jax.numpy as jnp
from jax import lax
from jax.experimental import pallas as pl
from jax.experimental.pallas import tpu as pltpu

F32 = jnp.float32
BF16 = jnp.bfloat16

D_MODEL = 1024
DEPTH = 4
N_EVEN = 2
N_ODD = 2
EPS = 1e-5
D_FF = 2816

SSD_HEADS = 16
SSD_HEAD_DIM = 64
SSD_INNER = 1024
SSD_GROUPS = 2
SSD_D_STATE = 128
SSD_CONV = 4
SSD_CONV_DIM = 1536
SSD_CHUNK = 64

GLA_HEADS = 4
GLA_DK = 128
GLA_DV = 256
GLA_KEY = 512
GLA_VAL = 1024
GLA_GATE_RANK = 16
GLA_GATE_NORM = 16.0
GLA_CHUNK = 16

RWKV_HEAD = 64
RWKV_HEADS = 16
RWKV_GN_EPS = 64e-5

HEAD_SHIFT = 6
GLA_CHUNK_SHIFT = 4
assert 1 << HEAD_SHIFT == SSD_HEAD_DIM == RWKV_HEAD == SSD_CHUNK and 1 << GLA_CHUNK_SHIFT == GLA_CHUNK

LANES = 128
SMALL = LANES
IN_PERM = SSD_INNER + SSD_CONV_DIM + 2 * GLA_KEY + 2 * GLA_VAL + SMALL
VMEM_LIMIT = 56 * 1024 * 1024
FF_TILE = 1408
SEQ_BLOCK = 256
STEP_BATCH = 8


def _cp(sem):
    return pltpu.CompilerParams(dimension_semantics=sem, vmem_limit_bytes=VMEM_LIMIT)


def _iota(shape, axis):
    return lax.broadcasted_iota(jnp.int32, shape, axis)


def _onehot(mask):
    return jnp.where(mask, 1.0, 0.0).astype(BF16)


def _dot(a, b):
    return jnp.dot(a, b, preferred_element_type=F32)


def _split3(x):
    hi = x.astype(BF16)
    r = x - hi.astype(F32)
    mid = r.astype(BF16)
    lo = (r - mid.astype(F32)).astype(BF16)
    return hi, mid, lo


def _dot01(x, e):
    hi, mid, lo = _split3(x)
    return _dot(hi, e) + _dot(mid, e) + _dot(lo, e)


def _dot01_l(e, x):
    hi, mid, lo = _split3(x)
    return _dot(e, hi) + _dot(e, mid) + _dot(e, lo)


def _rms(x, g):
    ms = jnp.mean(x * x, axis=-1, keepdims=True)
    return x * lax.rsqrt(ms + EPS) * g


def _silu(x):
    return x * jax.nn.sigmoid(x)


def _softplus(x):
    return jnp.maximum(x, 0.0) + jnp.log1p(jnp.exp(-jnp.abs(x)))


def _softplus_abs(x):
    return jnp.maximum(x, 0.0) + jnp.log(1.0 + jnp.exp(-jnp.abs(x)))


def _head_expand():
    return _onehot((_iota((LANES, D_MODEL), 1) >> HEAD_SHIFT) == _iota((LANES, D_MODEL), 0))


def _head_reduce():
    return _onehot((_iota((D_MODEL, LANES), 0) >> HEAD_SHIFT) == _iota((D_MODEL, LANES), 1))


def _ffn_body(x_ref, g_ref, wgu_ref, wd_ref, *rest, mix_out, final_norm):
    o_ref = rest[-1]
    x = x_ref[...]
    if mix_out:
        y_ref, mo_ref, wo_ref = rest[:3]
        x = x + _dot(y_ref[...].astype(BF16), wo_ref[0:SSD_INNER, :])
        x = x + _dot(mo_ref[...].astype(BF16), wo_ref[SSD_INNER:, :])
    xn = _rms(x, g_ref[...]).astype(BF16)
    acc = None
    for j in range(D_FF // FF_TILE):
        lo = j * FF_TILE
        gate = _dot(xn, wgu_ref[:, lo:lo + FF_TILE])
        up = _dot(xn, wgu_ref[:, D_FF + lo:D_FF + lo + FF_TILE])
        part = _dot((_silu(gate) * up).astype(BF16), wd_ref[lo:lo + FF_TILE, :])
        acc = part if acc is None else acc + part
    y = x + 0.5 * acc
    o_ref[...] = _rms(y, rest[-2][...]) if final_norm else y


def _ffn(x, nrm, wgu, wd, ls, tm, final_g=None, mix=None):
    T = x.shape[0]
    resident = pl.Buffered(1)
    tok = pl.BlockSpec((tm, D_MODEL), lambda i: (i, 0))
    args = [x, nrm, wgu, wd]
    specs = [
        tok,
        pl.BlockSpec((None, 1, D_MODEL), lambda i: (ls, 0, 0)),
        pl.BlockSpec((None, D_MODEL, 2 * D_FF), lambda i: (ls, 0, 0), pipeline_mode=resident),
        pl.BlockSpec((None, D_FF, D_MODEL), lambda i: (ls, 0, 0), pipeline_mode=resident),
    ]
    if mix is not None:
        y, o, w_out, li = mix
        args += [y, o, w_out]
        specs += [tok, tok, pl.BlockSpec((None, SSD_INNER + GLA_VAL, D_MODEL), lambda i: (li, 0, 0),
                                         pipeline_mode=resident)]
    if final_g is not None:
        args.append(final_g)
        specs.append(pl.BlockSpec((1, D_MODEL), lambda i: (0, 0)))
    return pl.pallas_call(
        functools.partial(_ffn_body, mix_out=mix is not None, final_norm=final_g is not None),
        grid=(T // tm,),
        in_specs=specs,
        out_specs=pl.BlockSpec((tm, D_MODEL), lambda i: (i, 0)),
        out_shape=jax.ShapeDtypeStruct((T, D_MODEL), F32),
        compiler_params=_cp(("parallel",)),
        name="ffn",
    )(*args)


_EVEN_PIECES = (SSD_INNER, SSD_CONV_DIM, GLA_KEY, GLA_KEY, GLA_VAL, GLA_VAL, SMALL)


def _even_in_body(x_ref, g_ref, w_ref, *out_refs):
    h = _rms(x_ref[...], g_ref[...]).astype(BF16)
    off = 0
    for ref, n in zip(out_refs, _EVEN_PIECES):
        ref[...] = _dot(h, w_ref[:, off:off + n])
        off += n


def _even_in(x, nrm, w, i, tm):
    T = x.shape[0]
    return pl.pallas_call(
        _even_in_body,
        grid=(T // tm,),
        in_specs=[
            pl.BlockSpec((tm, D_MODEL), lambda t: (t, 0)),
            pl.BlockSpec((None, 1, D_MODEL), lambda t: (2 * i, 0, 0)),
            pl.BlockSpec((None, D_MODEL, IN_PERM), lambda t: (i, 0, 0)),
        ],
        out_specs=[pl.BlockSpec((tm, n), lambda t: (t, 0)) for n in _EVEN_PIECES],
        out_shape=[jax.ShapeDtypeStruct((T, n), F32) for n in _EVEN_PIECES],
        compiler_params=_cp(("parallel",)),
        name="even_in",
    )(x, nrm, w)


def _ssd_seq_body(xbc_ref, sm_ref, z_ref, cw_ref, cb_ref, dtb_ref, alog_ref, dsk_ref, nw_ref,
                  y_ref, convn_ref, ssmn_ref,
                  cbuf_ref, xact_ref, acx_ref, dtx_ref, hT_ref, *, nblk):
    Lb = SEQ_BLOCK
    C = SSD_CHUNK
    j = pl.program_id(1)

    @pl.when(j == 0)
    def _():
        cbuf_ref[0:8, :] = jnp.zeros((8, SSD_CONV_DIM), F32)
        hT_ref[...] = jnp.zeros_like(hT_ref)

    x = xbc_ref[...]
    acc = cb_ref[...] + pltpu.roll(x, 3, axis=0) * cw_ref[0:1, :]
    acc = acc + pltpu.roll(x, 2, axis=0) * cw_ref[1:2, :]
    acc = acc + pltpu.roll(x, 1, axis=0) * cw_ref[2:3, :]
    acc = acc + x * cw_ref[3:4, :]
    xact_ref[...] = _silu(acc)
    cbuf_ref[8:16, :] = x[0:8, :]
    head = cb_ref[...] + cbuf_ref[5:13, :] * cw_ref[0:1, :]
    head = head + cbuf_ref[6:14, :] * cw_ref[1:2, :]
    head = head + cbuf_ref[7:15, :] * cw_ref[2:3, :]
    head = head + cbuf_ref[8:16, :] * cw_ref[3:4, :]
    xact_ref[0:8, :] = _silu(head)
    tail = x[Lb - 8:Lb, :]
    cbuf_ref[0:8, :] = tail

    @pl.when(j == nblk - 1)
    def _():
        convn_ref[...] = tail[5:8, :]

    dt = _softplus(sm_ref[...] + dtb_ref[...])
    dta = dt * (-jnp.exp(alog_ref[...]))
    ii = _iota((Lb, Lb), 0)
    jj = _iota((Lb, Lb), 1)
    tril = _onehot(((ii >> HEAD_SHIFT) == (jj >> HEAD_SHIFT)) & (ii >= jj))
    acum = _dot01_l(tril, dta)
    expand = _head_expand()
    acx_ref[...] = _dot01(acum, expand)
    dtx_ref[...] = _dot01(dt, expand)

    li = _iota((C, D_MODEL), 0)
    lj = _iota((C, D_MODEL), 1) & (C - 1)
    diag = li == lj
    causal = li >= lj
    ones_c = jnp.ones((C, C), BF16)
    ones_c2 = jnp.ones((C, 2 * C), BF16)
    tile8 = _onehot((_iota((2 * C, 8 * C), 1) & (C - 1)) == (_iota((2 * C, 8 * C), 0) & (C - 1)))
    pair_mask = (_iota((LANES, LANES), 0) >> HEAD_SHIFT) == (_iota((LANES, LANES), 1) >> HEAD_SHIFT)
    half = SSD_INNER // SSD_GROUPS

    def chunk(c, carry):
        r0 = pl.multiple_of(c * C, C)
        xa = xact_ref[pl.ds(r0, C), :]
        xs = xa[:, :SSD_INNER]
        bms = [xa[:, SSD_INNER + g * SSD_D_STATE:SSD_INNER + (g + 1) * SSD_D_STATE] for g in range(2)]
        cms = [xa[:, SSD_INNER + (2 + g) * SSD_D_STATE:SSD_INNER + (3 + g) * SSD_D_STATE]
               for g in range(2)]
        acx = acx_ref[pl.ds(r0, C), :]
        dtx = dtx_ref[pl.ds(r0, C), :]
        alast = acx[C - 1:C, :]
        arow = _dot01_l(ones_c, jnp.where(diag, acx, 0.0))
        dsel = jnp.where(diag, dtx, 0.0)
        dhi = dsel.astype(BF16)
        dmid = (dsel - dhi.astype(F32)).astype(BF16)
        dtrow = _dot(ones_c2, jnp.concatenate([dhi, dmid], axis=0))
        decay = jnp.exp(jnp.where(causal, acx - arow, -jnp.inf))
        cbt = []
        for g in range(2):
            cb = lax.dot_general(cms[g].astype(BF16), bms[g].astype(BF16),
                                 (((1,), (1,)), ((), ())), preferred_element_type=F32)
            cbt.append(_dot01x2(cb, tile8))
        wts = (jnp.concatenate(cbt, axis=1) * decay * dtrow).astype(BF16)
        ys = []
        for p in range(SSD_HEADS // 2):
            xp = xs[:, p * LANES:(p + 1) * LANES]
            xbd = jnp.where(pair_mask, jnp.concatenate([xp, xp], axis=0), 0.0).astype(BF16)
            ys.append(_dot(wts[:, p * LANES:(p + 1) * LANES], xbd))
        y = jnp.concatenate(ys, axis=1)
        hT = hT_ref[...]
        hTb = hT.astype(BF16)
        yi = [_dot(cms[g].astype(BF16), hTb[:, g * half:(g + 1) * half]) for g in range(2)]
        y = y + jnp.concatenate(yi, axis=1) * jnp.exp(acx)
        xsc = (jnp.exp(alast - acx) * dtx * xs).astype(BF16)
        st = [lax.dot_general(bms[g].astype(BF16), xsc[:, g * half:(g + 1) * half],
                              (((0,), (0,)), ((), ())), preferred_element_type=F32) for g in range(2)]
        hT_ref[...] = hT * jnp.exp(alast) + jnp.concatenate(st, axis=1)
        y = y + dsk_ref[...] * xs
        y = y * _silu(z_ref[pl.ds(r0, C), :])
        outs = []
        for g in range(2):
            yg = y[:, g * half:(g + 1) * half]
            ms = jnp.mean(yg * yg, axis=-1, keepdims=True)
            outs.append(yg * lax.rsqrt(ms + EPS))
        y_ref[pl.ds(r0, C), :] = jnp.concatenate(outs, axis=1) * nw_ref[...]
        return carry

    lax.fori_loop(0, Lb // C, chunk, 0, unroll=4)

    @pl.when(j == nblk - 1)
    def _():
        ssmn_ref[...] = hT_ref[...].T


def _ssd_seq(xbc, sm, z, P, i, B, L):
    Lb = SEQ_BLOCK
    nblk = L // Lb
    T = B * L
    tok = lambda n: pl.BlockSpec((Lb, n), lambda b, j: (b * nblk + j, 0))
    par = lambda r, n: pl.BlockSpec((None, r, n), lambda b, j: (i, 0, 0))
    y, convn, ssmn = pl.pallas_call(
        functools.partial(_ssd_seq_body, nblk=nblk),
        grid=(B, nblk),
        in_specs=[tok(SSD_CONV_DIM), tok(SMALL), tok(SSD_INNER),
                  par(SSD_CONV, SSD_CONV_DIM), par(1, SSD_CONV_DIM), par(1, SMALL), par(1, SMALL),
                  par(1, SSD_INNER), par(1, SSD_INNER)],
        out_specs=[tok(SSD_INNER),
                   pl.BlockSpec((None, SSD_CONV - 1, SSD_CONV_DIM), lambda b, j: (b, 0, 0)),
                   pl.BlockSpec((None, SSD_INNER, SSD_D_STATE), lambda b, j: (b, 0, 0))],
        out_shape=[jax.ShapeDtypeStruct((T, SSD_INNER), F32),
                   jax.ShapeDtypeStruct((B, SSD_CONV - 1, SSD_CONV_DIM), F32),
                   jax.ShapeDtypeStruct((B, SSD_INNER, SSD_D_STATE), F32)],
        scratch_shapes=[pltpu.VMEM((16, SSD_CONV_DIM), F32),
                        pltpu.VMEM((Lb, SSD_CONV_DIM), F32),
                        pltpu.VMEM((Lb, SSD_INNER), F32),
                        pltpu.VMEM((Lb, SSD_INNER), F32),
                        pltpu.VMEM((SSD_D_STATE, SSD_INNER), F32)],
        compiler_params=_cp(("parallel", "arbitrary")),
        name="ssd_seq",
    )(xbc, sm, z, P["conv_w"], P["conv_b"], P["dt_bias"], P["a_log"], P["d_skip_x"], P["ssd_norm"])
    return y, convn, ssmn.reshape(B, SSD_HEADS, SSD_HEAD_DIM, SSD_D_STATE)


def _gla_seq_body(q_ref, k_ref, v_ref, og_ref, sm_ref, w2_ref, gb_ref, nw_ref,
                  o_ref, glan_ref,
                  st_ref, dec_ref, qg_ref, kd_ref, oi_ref, *, nblk):
    Lb = SEQ_BLOCK
    j = pl.program_id(1)

    @pl.when(j == 0)
    def _():
        st_ref[...] = jnp.zeros_like(st_ref)

    gate = _dot(sm_ref[...].astype(BF16), w2_ref[...]) + gb_ref[...]
    lg = -_softplus_abs(-gate) / GLA_GATE_NORM
    ii = _iota((Lb, Lb), 0)
    jj = _iota((Lb, Lb), 1)
    same = (ii >> GLA_CHUNK_SHIFT) == (jj >> GLA_CHUNK_SHIFT)
    causal = same & (ii >= jj)
    lg_hi = lg.astype(BF16)
    lg2 = jnp.concatenate([lg_hi, (lg - lg_hi.astype(F32)).astype(BF16)], axis=0)
    sel_c = _onehot(causal)
    sel_s = _onehot(same)
    b = _dot(jnp.concatenate([sel_c, sel_c], axis=1), lg2)
    blast = _dot(jnp.concatenate([sel_s, sel_s], axis=1), lg2)
    qg = q_ref[...] * (GLA_DK ** -0.5) * jnp.exp(b)
    kk = k_ref[...]
    kg = kk * jnp.exp(-b)
    kd = kk * jnp.exp(blast - b)
    dec_ref[...] = jnp.exp(blast)
    qg_ref[...] = qg.astype(BF16)
    kd_ref[...] = kd.astype(BF16)

    def chunk(c, carry):
        r0 = pl.multiple_of(c * GLA_CHUNK, GLA_CHUNK)
        for h in range(GLA_HEADS):
            ks = slice(h * GLA_DK, (h + 1) * GLA_DK)
            vs = slice(h * GLA_DV, (h + 1) * GLA_DV)
            st = st_ref[h]
            oi_ref[pl.ds(r0, GLA_CHUNK), vs] = lax.dot_general(
                qg_ref[pl.ds(r0, GLA_CHUNK), ks], st.astype(BF16),
                (((1,), (1,)), ((), ())), preferred_element_type=F32)
            upd = lax.dot_general(v_ref[pl.ds(r0, GLA_CHUNK), vs].astype(BF16),
                                  kd_ref[pl.ds(r0, GLA_CHUNK), ks],
                                  (((0,), (0,)), ((), ())), preferred_element_type=F32)
            st_ref[h] = st * dec_ref[pl.ds(r0, 8), ks][0:1, :] + upd
        return carry

    lax.fori_loop(0, Lb // GLA_CHUNK, chunk, 0, unroll=16)

    for h in range(GLA_HEADS):
        ks = slice(h * GLA_DK, (h + 1) * GLA_DK)
        vs = slice(h * GLA_DV, (h + 1) * GLA_DV)
        att = lax.dot_general(qg[:, ks].astype(BF16), kg[:, ks].astype(BF16),
                              (((1,), (1,)), ((), ())), preferred_element_type=F32)
        att = jnp.where(causal, att, 0.0)
        o_h = _dot(att.astype(BF16), v_ref[:, vs].astype(BF16)) + oi_ref[:, vs]
        ms = jnp.mean(o_h * o_h, axis=-1, keepdims=True)
        o_h = o_h * lax.rsqrt(ms + EPS) * nw_ref[...]
        o_ref[:, vs] = o_h * _silu(og_ref[:, vs])

    @pl.when(j == nblk - 1)
    def _():
        for h in range(GLA_HEADS):
            glan_ref[h] = st_ref[h].T


def _gla_seq(q, k, v, og, sm, P, i, B, L):
    Lb = SEQ_BLOCK
    nblk = L // Lb
    T = B * L
    tok = lambda n: pl.BlockSpec((Lb, n), lambda b, j: (b * nblk + j, 0))
    par = lambda r, n: pl.BlockSpec((None, r, n), lambda b, j: (i, 0, 0))
    return pl.pallas_call(
        functools.partial(_gla_seq_body, nblk=nblk),
        grid=(B, nblk),
        in_specs=[tok(GLA_KEY), tok(GLA_KEY), tok(GLA_VAL), tok(GLA_VAL), tok(SMALL),
                  par(SMALL, GLA_KEY), par(1, GLA_KEY), par(1, GLA_DV)],
        out_specs=[tok(GLA_VAL),
                   pl.BlockSpec((None, GLA_HEADS, GLA_DK, GLA_DV), lambda b, j: (b, 0, 0, 0))],
        out_shape=[jax.ShapeDtypeStruct((T, GLA_VAL), F32),
                   jax.ShapeDtypeStruct((B, GLA_HEADS, GLA_DK, GLA_DV), F32)],
        scratch_shapes=[pltpu.VMEM((GLA_HEADS, GLA_DV, GLA_DK), F32),
                        pltpu.VMEM((Lb, GLA_KEY), F32),
                        pltpu.VMEM((Lb, GLA_KEY), BF16),
                        pltpu.VMEM((Lb, GLA_KEY), BF16),
                        pltpu.VMEM((Lb, GLA_VAL), F32)],
        compiler_params=_cp(("parallel", "arbitrary")),
        name="gla_seq",
    )(q, k, v, og, sm, P["gate_w2"], P["gate_b"], P["gla_norm"])


def _even_step_pre_body(xbc_ref, c0_ref, sm_ref, cw_ref, cb_ref, dtb_ref, alog_ref, dsk_ref,
                        w2_ref, gb_ref,
                        xact_ref, convn_ref, dax_ref, c2_ref, ybase_ref, eg_ref):
    u = xbc_ref[...]
    n = SSD_CONV_DIM
    acc = cb_ref[...] + c0_ref[:, 0:n] * cw_ref[0:1, :]
    acc = acc + c0_ref[:, n:2 * n] * cw_ref[1:2, :]
    acc = acc + c0_ref[:, 2 * n:3 * n] * cw_ref[2:3, :]
    acc = acc + u * cw_ref[3:4, :]
    xa = _silu(acc)
    xact_ref[...] = xa
    convn_ref[:, 0:n] = c0_ref[:, n:2 * n]
    convn_ref[:, n:2 * n] = c0_ref[:, 2 * n:3 * n]
    convn_ref[:, 2 * n:3 * n] = u
    dt = _softplus(sm_ref[...] + dtb_ref[...])
    dta = dt * (-jnp.exp(alog_ref[...]))
    expand = _head_expand()
    dtx = _dot01(dt, expand)
    dax_ref[...] = jnp.exp(_dot01(dta, expand))
    xs = xa[:, :SSD_INNER]
    c2_ref[...] = dtx * xs
    half = SSD_INNER // SSD_GROUPS
    cbs = []
    for g in range(2):
        bm = xa[:, SSD_INNER + g * SSD_D_STATE:SSD_INNER + (g + 1) * SSD_D_STATE]
        cm = xa[:, SSD_INNER + (2 + g) * SSD_D_STATE:SSD_INNER + (3 + g) * SSD_D_STATE]
        cb = jnp.sum(cm * bm, axis=-1, keepdims=True)
        cbs.append(jnp.broadcast_to(cb, (cb.shape[0], half)))
    ybase_ref[...] = jnp.concatenate(cbs, axis=1) * dtx * xs + dsk_ref[...] * xs
    gate = _dot(sm_ref[...].astype(BF16), w2_ref[...]) + gb_ref[...]
    eg_ref[...] = jnp.exp(-_softplus(-gate) / GLA_GATE_NORM)


def _even_step_pre(xbc, conv0, sm, P, i):
    Bn = xbc.shape[0]
    full = lambda n: pl.BlockSpec((Bn, n), lambda t: (0, 0))
    par = lambda r, n: pl.BlockSpec((None, r, n), lambda t: (i, 0, 0))
    outs = (SSD_CONV_DIM, 3 * SSD_CONV_DIM, SSD_INNER, SSD_INNER, SSD_INNER, GLA_KEY)
    return pl.pallas_call(
        _even_step_pre_body,
        grid=(1,),
        in_specs=[full(SSD_CONV_DIM), full(3 * SSD_CONV_DIM), full(SMALL),
                  par(SSD_CONV, SSD_CONV_DIM), par(1, SSD_CONV_DIM), par(1, SMALL), par(1, SMALL),
                  par(1, SSD_INNER), par(SMALL, GLA_KEY), par(1, GLA_KEY)],
        out_specs=[full(n) for n in outs],
        out_shape=[jax.ShapeDtypeStruct((Bn, n), F32) for n in outs],
        compiler_params=_cp(("arbitrary",)),
        name="even_step_pre",
    )(xbc, conv0, sm, P["conv_w"], P["conv_b"], P["dt_bias"], P["a_log"], P["d_skip_x"],
      P["gate_w2"], P["gate_b"])


def _to_cols(x):
    Bn, N = x.shape
    return x.reshape(Bn // STEP_BATCH, STEP_BATCH, N).transpose(0, 2, 1)


def _from_cols(x):
    nb, N, bb = x.shape
    return x.transpose(0, 2, 1).reshape(nb * bb, N)


def _layer_state_call(body, name, state_all, acc, i, rows, width, args, specs, out_spec, out_shape):
    nl, Bn = state_all.shape[:2]
    st = pl.BlockSpec((None, STEP_BATCH, rows, width), lambda t: (i, t, 0, 0))
    args = [state_all.reshape(nl, Bn, rows, width)] + list(args)
    specs = [st] + list(specs)
    aliases = {}
    if acc is not None:
        aliases = {len(args): 0}
        args.append(acc)
        specs.append(pl.BlockSpec(memory_space=pl.ANY))
    return pl.pallas_call(
        body,
        grid=(Bn // STEP_BATCH,),
        in_specs=specs,
        out_specs=[st, out_spec],
        out_shape=[jax.ShapeDtypeStruct((nl, Bn, rows, width), F32), out_shape],
        input_output_aliases=aliases,
        compiler_params=_cp(("parallel",)),
        name=name,
    )(*args)


def _ssd_step_body(h_ref, c1_ref, c2_ref, bm_ref, cm_ref, *rest):
    hn_ref, y_ref = rest[-2:]
    half = SSD_INNER // SSD_GROUPS
    lane = _iota((SSD_INNER, STEP_BATCH), 1)
    ycols = jnp.zeros((SSD_INNER, STEP_BATCH), F32)
    c1 = c1_ref[...]
    c2 = c2_ref[...]
    for b in range(STEP_BATCH):
        h0 = h_ref[b]
        bmat = jnp.concatenate(
            [jnp.broadcast_to(bm_ref[b:b + 1, g * SSD_D_STATE:(g + 1) * SSD_D_STATE],
                              (half, SSD_D_STATE)) for g in range(2)], axis=0)
        cmat = jnp.concatenate(
            [jnp.broadcast_to(cm_ref[b:b + 1, g * SSD_D_STATE:(g + 1) * SSD_D_STATE],
                              (half, SSD_D_STATE)) for g in range(2)], axis=0)
        ycol = jnp.sum(h0 * cmat, axis=1, keepdims=True)
        ycols = jnp.where(lane == b, ycol, ycols)
        hn_ref[b] = h0 * c1[:, b:b + 1] + c2[:, b:b + 1] * bmat
    y_ref[...] = ycols


def _ssd_step(state_all, acc, i, dax, c2, xact):
    Bn = state_all.shape[1]
    bb = STEP_BATCH
    bm = xact[:, SSD_INNER:SSD_INNER + 2 * SSD_D_STATE]
    cm = xact[:, SSD_INNER + 2 * SSD_D_STATE:]
    col = pl.BlockSpec((None, SSD_INNER, bb), lambda t: (t, 0, 0))
    row = pl.BlockSpec((bb, 2 * SSD_D_STATE), lambda t: (t, 0))
    acc, ycols = _layer_state_call(
        _ssd_step_body, "ssd_step", state_all, acc, i, SSD_INNER, SSD_D_STATE,
        [_to_cols(dax), _to_cols(c2), bm, cm], [col, col, row, row],
        col, jax.ShapeDtypeStruct((Bn // bb, SSD_INNER, bb), F32))
    return acc, _from_cols(ycols)


def _gla_step_body(s_ref, eg_ref, kc_ref, qc_ref, v_ref, *rest):
    sn_ref, o_ref = rest[-2:]
    eg = eg_ref[...]
    kc = kc_ref[...]
    qc = qc_ref[...] * (GLA_DK ** -0.5)
    for b in range(STEP_BATCH):
        s0 = s_ref[b]
        vmat = jnp.concatenate(
            [jnp.broadcast_to(v_ref[b:b + 1, h * GLA_DV:(h + 1) * GLA_DV], (GLA_DK, GLA_DV))
             for h in range(GLA_HEADS)], axis=0)
        sn = s0 * eg[:, b:b + 1] + kc[:, b:b + 1] * vmat
        sn_ref[b] = sn
        t = sn * qc[:, b:b + 1]
        o_ref[b:b + 1, :] = jnp.concatenate(
            [jnp.sum(t[h * GLA_DK:(h + 1) * GLA_DK], axis=0, keepdims=True)
             for h in range(GLA_HEADS)], axis=1)


def _gla_step(state_all, acc, i, eg, k, q, v):
    Bn = state_all.shape[1]
    bb = STEP_BATCH
    col = pl.BlockSpec((None, GLA_KEY, bb), lambda t: (t, 0, 0))
    row = pl.BlockSpec((bb, GLA_VAL), lambda t: (t, 0))
    return _layer_state_call(
        _gla_step_body, "gla_step", state_all, acc, i, GLA_KEY, GLA_DV,
        [_to_cols(eg), _to_cols(k), _to_cols(q), v], [col, col, col, row],
        row, jax.ShapeDtypeStruct((Bn, GLA_VAL), F32))


def _even_step_post_body(x_ref, ybase_ref, yi_ref, dax_ref, z_ref, o_ref, og_ref,
                         snw_ref, gnw_ref, w_ref, out_ref):
    y = (ybase_ref[...] + yi_ref[...] * dax_ref[...]) * _silu(z_ref[...])
    half = SSD_INNER // SSD_GROUPS
    acc = None
    for g in range(2):
        yg = y[:, g * half:(g + 1) * half]
        ms = jnp.mean(yg * yg, axis=-1, keepdims=True)
        yn = yg * lax.rsqrt(ms + EPS) * snw_ref[:, g * half:(g + 1) * half]
        part = _dot(yn.astype(BF16), w_ref[g * half:(g + 1) * half, :])
        acc = part if acc is None else acc + part
    for h in range(GLA_HEADS):
        vs = slice(h * GLA_DV, (h + 1) * GLA_DV)
        oh = o_ref[:, vs]
        ms = jnp.mean(oh * oh, axis=-1, keepdims=True)
        on = oh * lax.rsqrt(ms + EPS) * gnw_ref[...] * _silu(og_ref[:, vs])
        acc = acc + _dot(on.astype(BF16), w_ref[SSD_INNER + h * GLA_DV:SSD_INNER + (h + 1) * GLA_DV, :])
    out_ref[...] = x_ref[...] + acc


def _even_step_post(x, ybase, yi, dax, z, o, og, P, w_out, i):
    Bn = x.shape[0]
    full = pl.BlockSpec((Bn, D_MODEL), lambda t: (0, 0))
    return pl.pallas_call(
        _even_step_post_body,
        grid=(1,),
        in_specs=[full] * 7 + [pl.BlockSpec((None, 1, SSD_INNER), lambda t: (i, 0, 0)),
                               pl.BlockSpec((None, 1, GLA_DV), lambda t: (i, 0, 0)),
                               pl.BlockSpec((None, SSD_INNER + GLA_VAL, D_MODEL), lambda t: (i, 0, 0))],
        out_specs=full,
        out_shape=jax.ShapeDtypeStruct((Bn, D_MODEL), F32),
        compiler_params=_cp(("arbitrary",)),
        name="even_step_post",
    )(x, ybase, yi, dax, z, o, og, P["ssd_norm"], P["gla_norm"], w_out)


_ODD_OUTS = 8


def _odd_pre_body(*refs, step, has_vfirst, nblk_seq, tm):
    it = iter(refs)
    x_ref = next(it)
    if step:
        prev_ref = next(it)
    else:
        xp_ref = next(it)
        sh0_ref = next(it)
    g_ref, mix_ref = next(it), next(it)
    wr_ref, wk_ref, wv_ref = next(it), next(it), next(it)
    w0_ref, w1_ref, w2_ref = next(it), next(it), next(it)
    a0_ref, a1_ref, a2_ref = next(it), next(it), next(it)
    if has_vfirst:
        vf_ref, v0_ref, v1_ref, v2_ref = next(it), next(it), next(it), next(it)
    g1_ref, g2_ref = next(it), next(it)
    kkw_ref, kaw_ref = next(it), next(it)
    h_ref, r_ref, dec_ref, k_ref, v_ref, an_ref, b_ref, gg_ref = (next(it) for _ in range(_ODD_OUTS))

    h = _rms(x_ref[...], g_ref[...])
    if step:
        prev = prev_ref[...]
    else:
        i = pl.program_id(0)
        hp = _rms(xp_ref[...], g_ref[...])[7:8, :]
        prow = jnp.where(i % nblk_seq == 0, sh0_ref[...], hp)
        prev = jnp.where(_iota((tm, D_MODEL), 0) == 0, prow, pltpu.roll(h, 1, axis=0))
    h_ref[...] = h
    xx = prev - h
    mixed = lambda r: (h + xx * mix_ref[r:r + 1, :]).astype(BF16)
    xr, xw, xk, xv, xa, xg = (mixed(r) for r in range(6))
    r_ref[...] = _dot(xr, wr_ref[...])
    wl = w0_ref[...] + _dot(jnp.tanh(_dot(xw, w1_ref[...])).astype(BF16), w2_ref[...])
    w = -_softplus_abs(-wl) - 0.5
    dec_ref[...] = jnp.exp(-jnp.exp(w))
    k = _dot(xk, wk_ref[...])
    v = _dot(xv, wv_ref[...])
    if has_vfirst:
        gate = jax.nn.sigmoid(v0_ref[...] + _dot(_dot(xv, v1_ref[...]).astype(BF16), v2_ref[...]))
        v = v + (vf_ref[...] - v) * gate
    v_ref[...] = v
    a = jax.nn.sigmoid(a0_ref[...] + _dot(_dot(xa, a1_ref[...]).astype(BF16), a2_ref[...]))
    gg_ref[...] = _dot(jax.nn.sigmoid(_dot(xg, g1_ref[...])).astype(BF16), g2_ref[...])
    kkf = k * kkw_ref[...]
    red = _head_reduce()
    expand = _head_expand()
    ss = _dot01x2(kkf * kkf, jnp.concatenate([red, red], axis=0))
    kk = kkf * _dot01x2(lax.rsqrt(jnp.maximum(ss, 1e-24)), jnp.concatenate([expand, expand], axis=0))
    k_ref[...] = k * (1.0 + (a - 1.0) * kaw_ref[...])
    an_ref[...] = -kk
    b_ref[...] = kk * a


def _odd_pre(x, prev_or_shift0, vfirst, P, i, tm, L):
    T = x.shape[0]
    step = L == 1
    has_vfirst = vfirst is not None
    nblk_seq = max(L // tm, 1)
    tok = pl.BlockSpec((tm, D_MODEL), lambda t: (t, 0))
    par = lambda r, n, idx=i: pl.BlockSpec((None, r, n), lambda t: (idx, 0, 0), pipeline_mode=pl.Buffered(1))
    args = [x]
    specs = [tok]
    if step:
        args.append(prev_or_shift0)
        specs.append(tok)
    else:
        args += [x, prev_or_shift0]
        specs += [pl.BlockSpec((8, D_MODEL), lambda t: (jnp.maximum(t * (tm // 8) - 1, 0), 0)),
                  pl.BlockSpec((None, 1, D_MODEL), lambda t: (t // nblk_seq, 0, 0))]
    args += [P["mix_norm"], P["mix"], P["w_r"], P["w_k"], P["w_v"], P["w0"], P["w1"], P["w2"],
             P["a0"], P["a1"], P["a2"]]
    specs += [pl.BlockSpec((None, 1, D_MODEL), lambda t: (2 * i + 1, 0, 0)),
              par(8, D_MODEL), par(D_MODEL, D_MODEL), par(D_MODEL, D_MODEL), par(D_MODEL, D_MODEL),
              par(1, D_MODEL), par(D_MODEL, LANES), par(LANES, D_MODEL),
              par(1, D_MODEL), par(D_MODEL, LANES), par(LANES, D_MODEL)]
    if has_vfirst:
        args += [vfirst, P["v0"], P["v1"], P["v2"]]
        specs += [tok, par(1, D_MODEL, i - 1), par(D_MODEL, LANES, i - 1), par(LANES, D_MODEL, i - 1)]
    args += [P["g1"], P["g2"], P["k_k"], P["k_a"]]
    specs += [par(D_MODEL, 2 * LANES), par(2 * LANES, D_MODEL), par(1, D_MODEL), par(1, D_MODEL)]
    return pl.pallas_call(
        functools.partial(_odd_pre_body, step=step, has_vfirst=has_vfirst, nblk_seq=nblk_seq, tm=tm),
        grid=(T // tm,),
        in_specs=specs,
        out_specs=[tok] * _ODD_OUTS,
        out_shape=[jax.ShapeDtypeStruct((T, D_MODEL), F32)] * _ODD_OUTS,
        compiler_params=_cp(("parallel",)),
        name="odd_pre",
    )(*args)


def _dot01x2(x, e2):
    hi = x.astype(BF16)
    mid = (x - hi.astype(F32)).astype(BF16)
    return _dot(jnp.concatenate([hi, mid], axis=1), e2)


def _odd_post_body(x_ref, o_ref, r_ref, k_ref, v_ref, g_ref, gnw_ref, gnb_ref, rk_ref, wo_ref, out_ref):
    red = _head_reduce()
    red = jnp.concatenate([red, red], axis=0)
    expand = _head_expand()
    expand = jnp.concatenate([expand, expand], axis=0)
    o = o_ref[...]
    mu = _dot01x2(_dot01x2(o, red) * (1.0 / RWKV_HEAD), expand)
    d = o - mu
    var = _dot01x2(d * d, red) * (1.0 / RWKV_HEAD)
    on = d * _dot01x2(lax.rsqrt(var + RWKV_GN_EPS), expand) * gnw_ref[...] + gnb_ref[...]
    v = v_ref[...]
    bonus = _dot01x2(_dot01x2(r_ref[...] * k_ref[...] * rk_ref[...], red), expand) * v
    out_ref[...] = x_ref[...] + _dot(((on + bonus) * g_ref[...]).astype(BF16), wo_ref[...])


def _odd_post(x, o, r, k, v, g, P, i, tm):
    T = x.shape[0]
    tok = pl.BlockSpec((tm, D_MODEL), lambda t: (t, 0))
    par = lambda r_, n: pl.BlockSpec((None, r_, n), lambda t: (i, 0, 0))
    return pl.pallas_call(
        _odd_post_body,
        grid=(T // tm,),
        in_specs=[tok] * 6 + [par(1, D_MODEL), par(1, D_MODEL), par(1, D_MODEL), par(D_MODEL, D_MODEL)],
        out_specs=tok,
        out_shape=jax.ShapeDtypeStruct((T, D_MODEL), F32),
        compiler_params=_cp(("parallel",)),
        name="odd_post",
    )(x, o, r, k, v, g, P["gn_w"], P["gn_b"], P["r_k"], P["w_o"])


RWKV_TB = 64
RWKV_NPAIR = RWKV_HEADS // 2


RWKV_SEQS = 2
RWKV_SPLIT = 1
RWKV_UNROLL = 8


def _rwkv_seq_body(w_ref, k_ref, a_ref, b_ref, r_ref, v_ref, o_ref, sn_ref, s_ref, vt_ref, oacc_ref, *, ntb):
    N = RWKV_HEAD
    R = RWKV_NPAIR * N
    tbi = pl.program_id(1)

    @pl.when(tbi == 0)
    def _():
        s_ref[...] = jnp.zeros_like(s_ref)

    seg = _onehot((_iota((2 * LANES, 2 * LANES), 0) >> HEAD_SHIFT) == (_iota((2 * LANES, 2 * LANES), 1) >> HEAD_SHIFT))

    def seg_sum(xb):
        h = xb.shape[0] // 2
        both = _dot(jnp.concatenate([xb[:h], xb[h:]], axis=1), seg)
        return jnp.concatenate([both[:, :LANES], both[:, LANES:]], axis=0)
    for c in range(RWKV_SEQS):
        for p in range(RWKV_NPAIR):
            bt = v_ref[c, :, p * LANES:(p + 1) * LANES].T
            vt_ref[c, p * N:(p + 1) * N, :] = jnp.concatenate([bt[0:N], bt[N:2 * N]], axis=1)
    oacc_ref[...] = jnp.zeros_like(oacc_ref)
    G = R // RWKV_SPLIT
    OSTEPS = LANES // 4
    seg_out = _onehot((_iota((2 * LANES, LANES), 0) >> HEAD_SHIFT) == (_iota((2 * LANES, LANES), 1) >> (OSTEPS.bit_length() - 1)))
    lane_copy = _iota((R // 2, LANES), 1) & (OSTEPS - 1)
    lane_base = (_iota((G, LANES), 1) >> HEAD_SHIFT) * N

    def steps(tu, carry):
        row0 = pl.multiple_of(tu * RWKV_UNROLL, RWKV_UNROLL)
        for j in range(RWKV_UNROLL):
            t = tu * RWKV_UNROLL + j
            for c in range(RWKV_SEQS):
                for g in range(RWKV_SPLIT):
                    rs = slice(g * G, (g + 1) * G)

                    def rows(ref):
                        return jnp.concatenate(
                            [jnp.broadcast_to(
                                ref[c, pl.ds(row0, RWKV_UNROLL), p * LANES:(p + 1) * LANES][j:j + 1, :], (N, LANES))
                             for p in range(g * G // N, (g + 1) * G // N)], axis=0)

                    s = s_ref[c, rs, :]
                    sa = seg_sum((s * rows(a_ref)).astype(BF16))
                    vcol = jnp.take_along_axis(vt_ref[c, rs, :], lane_base + t, axis=1)
                    s = (s * rows(w_ref) + vcol * rows(k_ref)) + sa * rows(b_ref)
                    s_ref[c, rs, :] = s
                    xr = (s * rows(r_ref)).astype(BF16)
                    ob = _dot(jnp.concatenate([xr[:R // 2], xr[R // 2:]], axis=1), seg_out)
                    hb = tu // (OSTEPS // RWKV_UNROLL)
                    oacc_ref[c, hb] = jnp.where(lane_copy == (t & (OSTEPS - 1)), ob, oacc_ref[c, hb])
        return carry

    lax.fori_loop(0, RWKV_TB // RWKV_UNROLL, steps, 0)

    for c in range(RWKV_SEQS):
        for hb in range(RWKV_TB // OSTEPS):
            ot = oacc_ref[c, hb].T
            for gh in range(4):
                g, hh = divmod(gh, 2)
                for q in range(RWKV_NPAIR // 2):
                    lane0 = (g * (RWKV_NPAIR // 2) + q) * LANES + hh * N
                    o_ref[c, hb * OSTEPS:(hb + 1) * OSTEPS, lane0:lane0 + N] = (
                        ot[gh * OSTEPS:(gh + 1) * OSTEPS, q * N:(q + 1) * N])

    @pl.when(tbi == ntb - 1)
    def _():
        for c in range(RWKV_SEQS):
            for p in range(RWKV_NPAIR):
                for hh in range(2):
                    sn_ref[c, (2 * p + hh) * N:(2 * p + hh + 1) * N, :] = (
                        s_ref[c, p * N:(p + 1) * N, hh * N:(hh + 1) * N])


def _rwkv_seq(w, k, a, b, r, v, B, L):
    ntb = L // RWKV_TB
    nc = RWKV_SEQS
    as_seqs = lambda x: x.reshape(B, L, D_MODEL)
    tok = pl.BlockSpec((nc, RWKV_TB, D_MODEL), lambda bi, t: (bi, t, 0))
    state_rows = RWKV_NPAIR * RWKV_HEAD
    o, sn = pl.pallas_call(
        functools.partial(_rwkv_seq_body, ntb=ntb),
        grid=(B // nc, ntb),
        in_specs=[tok] * 6,
        out_specs=[tok, pl.BlockSpec((nc, D_MODEL, RWKV_HEAD), lambda bi, t: (bi, 0, 0))],
        out_shape=[jax.ShapeDtypeStruct((B, L, D_MODEL), F32),
                   jax.ShapeDtypeStruct((B, D_MODEL, RWKV_HEAD), F32)],
        scratch_shapes=[pltpu.VMEM((nc, state_rows, LANES), F32)] * 2
        + [pltpu.VMEM((nc, 2, state_rows // 2, LANES), F32)],
        compiler_params=_cp(("parallel", "arbitrary")),
        name="rwkv_seq",
    )(as_seqs(w), as_seqs(k), as_seqs(a), as_seqs(b), as_seqs(r), as_seqs(v))
    return o.reshape(B * L, D_MODEL), sn.reshape(B, RWKV_HEADS, RWKV_HEAD, RWKV_HEAD)


def _rwkv_step_body(s_ref, w_ref, k_ref, a_ref, b_ref, r_ref, vc_ref, *rest):
    sn_ref, o_ref = rest[-2:]
    N = RWKV_HEAD
    lane = _iota((D_MODEL, STEP_BATCH), 1)
    ocols = jnp.zeros((D_MODEL, STEP_BATCH), F32)
    vc = vc_ref[...]

    def rows(ref, b):
        return jnp.concatenate(
            [jnp.broadcast_to(ref[b:b + 1, h * N:(h + 1) * N], (N, N)) for h in range(RWKV_HEADS)], axis=0)

    for b in range(STEP_BATCH):
        s = s_ref[b]
        sa = jnp.sum(s * rows(a_ref, b), axis=1, keepdims=True)
        s = s * rows(w_ref, b) + sa * rows(b_ref, b) + vc[:, b:b + 1] * rows(k_ref, b)
        sn_ref[b] = s
        ocol = jnp.sum(s * rows(r_ref, b), axis=1, keepdims=True)
        ocols = jnp.where(lane == b, ocol, ocols)
    o_ref[...] = ocols


def _rwkv_step(state_all, acc, i, w, k, a, b, r, v):
    Bn = state_all.shape[1]
    bb = STEP_BATCH
    col = pl.BlockSpec((None, D_MODEL, bb), lambda t: (t, 0, 0))
    row = pl.BlockSpec((bb, D_MODEL), lambda t: (t, 0))
    acc, ocols = _layer_state_call(
        _rwkv_step_body, "rwkv_step", state_all, acc, i, D_MODEL, RWKV_HEAD,
        [w, k, a, b, r, _to_cols(v)], [row] * 5 + [col],
        col, jax.ShapeDtypeStruct((Bn // bb, D_MODEL, bb), F32))
    return _from_cols(ocols), acc


def _pad_to(w, axis, n):
    pad = [(0, 0)] * w.ndim
    pad[axis] = (0, n - w.shape[axis])
    return jnp.pad(w, pad)


def _prepare(W):
    bf = lambda w: w.astype(BF16)
    row = lambda w: w[:, None, :]
    sizes = [SSD_INNER, SSD_CONV_DIM, SSD_HEADS, GLA_KEY, GLA_KEY, GLA_VAL, GLA_VAL, GLA_GATE_RANK]
    offs = [0]
    for s in sizes:
        offs.append(offs[-1] + s)
    piece = lambda n: W["ev_w_in"][:, :, offs[n]:offs[n + 1]]
    w_in = jnp.concatenate([piece(0), piece(1), piece(3), piece(4), piece(5), piece(6),
                            _pad_to(jnp.concatenate([piece(2), piece(7)], axis=-1), 2, SMALL)], axis=-1)
    gate_w2 = jnp.pad(W["ev_gla_gate_w2"], ((0, 0), (SSD_HEADS, SMALL - SSD_HEADS - GLA_GATE_RANK), (0, 0)))
    even = dict(
        w_in=bf(w_in), w_out=bf(W["ev_w_out"]),
        conv_w=W["ev_conv_w"], conv_b=row(W["ev_conv_b"]),
        dt_bias=row(_pad_to(W["ev_dt_bias"], 1, SMALL)), a_log=row(_pad_to(W["ev_a_log"], 1, SMALL)),
        d_skip_x=row(jnp.repeat(W["ev_d_skip"], SSD_HEAD_DIM, axis=1)),
        ssd_norm=row(W["ev_ssd_norm"]), gate_w2=bf(gate_w2), gate_b=row(W["ev_gla_gate_b"]),
        gla_norm=row(W["ev_gla_norm"]),
    )
    odd = dict(
        mix_norm=row(W["mix_norm"]), mix=_pad_to(W["od_mix"], 1, 8),
        w_r=bf(W["od_w_r"]), w_k=bf(W["od_w_k"]), w_v=bf(W["od_w_v"]), w_o=bf(W["od_w_o"]),
        w0=row(W["od_w0"]), w1=bf(_pad_to(W["od_w1"], 2, LANES)), w2=bf(_pad_to(W["od_w2"], 1, LANES)),
        a0=row(W["od_a0"]), a1=bf(_pad_to(W["od_a1"], 2, LANES)), a2=bf(_pad_to(W["od_a2"], 1, LANES)),
        v0=row(W["od_v0"]), v1=bf(_pad_to(W["od_v1"], 2, LANES)), v2=bf(_pad_to(W["od_v2"], 1, LANES)),
        g1=bf(_pad_to(W["od_g1"], 2, 2 * LANES)), g2=bf(_pad_to(W["od_g2"], 1, 2 * LANES)),
        k_k=row(W["od_k_k"]), k_a=row(W["od_k_a"]), r_k=row(W["od_r_k"]),
        gn_w=row(W["od_gn_w"]), gn_b=row(W["od_gn_b"]),
    )
    ffn = dict(
        norm=W["ffn_norm"].reshape(DEPTH * 2, 1, D_MODEL),
        w_gu=bf(W["ffn_w_gu"]).reshape(DEPTH * 2, D_MODEL, 2 * D_FF),
        w_down=bf(W["ffn_w_down"]).reshape(DEPTH * 2, D_FF, D_MODEL),
    )
    return dict(even=even, odd=odd, ffn=ffn, mix_norm=row(W["mix_norm"]),
                final_norm=W["final_norm"][None, :])


def _run_group(x3, states, Wp):
    B, L, _ = x3.shape
    T = B * L
    step = L == 1
    x = x3.reshape(T, D_MODEL)
    tm_ffn = min(512, T)
    tm_proj = min(256, T)
    ffn, even, odd = Wp["ffn"], Wp["even"], Wp["odd"]
    convs, ssms, glas, shifts, wkvs = [], [], [], [], []
    ssm_acc = gla_acc = wkv_acc = None
    v_first = None
    for layer in range(DEPTH):
        i = layer // 2
        mix = None
        x = _ffn(x, ffn["norm"], ffn["w_gu"], ffn["w_down"], 2 * layer, tm_ffn)
        if layer % 2 == 0:
            z, xbc, q, k, v, og, sm = _even_in(x, Wp["mix_norm"], even["w_in"], i, tm_proj)
            if step:
                xact, convn, dax, c2, ybase, eg = _even_step_pre(
                    xbc, states[0][i].reshape(B, 3 * SSD_CONV_DIM), sm, even, i)
                ssm_acc, yi = _ssd_step(states[1], ssm_acc, i, dax, c2, xact)
                gla_acc, o = _gla_step(states[2], gla_acc, i, eg, k, q, v)
                x = _even_step_post(x, ybase, yi, dax, z, o, og, even, even["w_out"], i)
                convn = convn.reshape(B, SSD_CONV - 1, SSD_CONV_DIM)
            else:
                y, convn, ssmn = _ssd_seq(xbc, sm, z, even, i, B, L)
                o, glan = _gla_seq(q, k, v, og, sm, even, i, B, L)
                mix = (y, o, even["w_out"], i)
                ssms.append(ssmn)
                glas.append(glan)
            convs.append(convn)
        else:
            prev = states[3][i] if step else jnp.zeros((B, 1, D_MODEL), F32)
            h, r, dec, k, v, an, bb, gg = _odd_pre(x, prev, v_first, odd, i, tm_proj, L)
            if v_first is None:
                v_first = v
            if step:
                o, wkv_acc = _rwkv_step(states[4], wkv_acc, i, dec, k, an, bb, r, v)
            else:
                o, wkvn = _rwkv_seq(dec, k, an, bb, r, v, B, L)
                wkvs.append(wkvn)
            x = _odd_post(x, o, r, k, v, gg, odd, i, tm_ffn)
            shifts.append(h.reshape(B, L, D_MODEL)[:, -1])
        x = _ffn(x, ffn["norm"], ffn["w_gu"], ffn["w_down"], 2 * layer + 1, tm_ffn,
                 final_g=Wp["final_norm"] if layer == DEPTH - 1 else None, mix=mix)
    y = x.reshape(B, L, D_MODEL)
    if step:
        ssm_out = ssm_acc.reshape(N_EVEN, B, SSD_HEADS, SSD_HEAD_DIM, SSD_D_STATE)
        gla_out = gla_acc.reshape(N_EVEN, B, GLA_HEADS, GLA_DK, GLA_DV)
        wkv_out = wkv_acc.reshape(N_ODD, B, RWKV_HEADS, RWKV_HEAD, RWKV_HEAD)
    else:
        ssm_out, gla_out, wkv_out = jnp.stack(ssms), jnp.stack(glas), jnp.stack(wkvs)
    return y, jnp.stack(convs), ssm_out, gla_out, jnp.stack(shifts), wkv_out


def kernel(x_prompt, x_sample, state_conv, state_ssm, state_gla, state_shift, state_wkv, ffn_norm, ffn_w_gu, ffn_w_down, mix_norm, final_norm, ev_w_in, ev_conv_w, ev_conv_b, ev_dt_bias, ev_a_log, ev_d_skip, ev_ssd_norm, ev_gla_gate_w2, ev_gla_gate_b, ev_gla_norm, ev_w_out, od_mix, od_w0, od_w1, od_w2, od_a0, od_a1, od_a2, od_v0, od_v1, od_v2, od_g1, od_g2, od_k_k, od_k_a, od_r_k, od_w_r, od_w_k, od_w_v, od_w_o, od_gn_w, od_gn_b):
    W = dict(ffn_norm=ffn_norm, ffn_w_gu=ffn_w_gu, ffn_w_down=ffn_w_down, mix_norm=mix_norm,
             final_norm=final_norm, ev_w_in=ev_w_in, ev_conv_w=ev_conv_w, ev_conv_b=ev_conv_b,
             ev_dt_bias=ev_dt_bias, ev_a_log=ev_a_log, ev_d_skip=ev_d_skip, ev_ssd_norm=ev_ssd_norm,
             ev_gla_gate_w2=ev_gla_gate_w2, ev_gla_gate_b=ev_gla_gate_b, ev_gla_norm=ev_gla_norm,
             ev_w_out=ev_w_out, od_mix=od_mix, od_w0=od_w0, od_w1=od_w1, od_w2=od_w2,
             od_a0=od_a0, od_a1=od_a1, od_a2=od_a2, od_v0=od_v0, od_v1=od_v1, od_v2=od_v2,
             od_g1=od_g1, od_g2=od_g2, od_k_k=od_k_k, od_k_a=od_k_a, od_r_k=od_r_k,
             od_w_r=od_w_r, od_w_k=od_w_k, od_w_v=od_w_v, od_w_o=od_w_o,
             od_gn_w=od_gn_w, od_gn_b=od_gn_b)
    Wp = _prepare(W)
    prompt = _run_group(x_prompt, None, Wp)
    sample = _run_group(x_sample, (state_conv, state_ssm, state_gla, state_shift, state_wkv), Wp)
    return (prompt[0], sample[0]) + prompt[1:] + sample[1:]
```

```python
import functools

import jax
import jax.numpy as jnp
from jax import lax
from jax.experimental import pallas as pl
from jax.experimental.pallas import tpu as pltpu

F32 = jnp.float32
BF16 = jnp.bfloat16

D_MODEL = 1024
DEPTH = 4
N_EVEN = 2
N_ODD = 2
EPS = 1e-5
D_FF = 2816

SSD_HEADS = 16
SSD_HEAD_DIM = 64
SSD_INNER = 1024
SSD_GROUPS = 2
SSD_D_STATE = 128
SSD_CONV = 4
SSD_CONV_DIM = 1536
SSD_CHUNK = 64

GLA_HEADS = 4
GLA_DK = 128
GLA_DV = 256
GLA_KEY = 512
GLA_VAL = 1024
GLA_GATE_RANK = 16
GLA_GATE_NORM = 16.0
GLA_CHUNK = 16

RWKV_HEAD = 64
RWKV_HEADS = 16
RWKV_GN_EPS = 64e-5

HEAD_SHIFT = 6
GLA_CHUNK_SHIFT = 4
assert 1 << HEAD_SHIFT == SSD_HEAD_DIM == RWKV_HEAD == SSD_CHUNK and 1 << GLA_CHUNK_SHIFT == GLA_CHUNK

LANES = 128
SMALL = LANES
IN_PERM = SSD_INNER + SSD_CONV_DIM + 2 * GLA_KEY + 2 * GLA_VAL + SMALL
VMEM_LIMIT = 56 * 1024 * 1024
FF_TILE = 1408
SEQ_BLOCK = 256
STEP_BATCH = 8


def _cp(sem):
    return pltpu.CompilerParams(dimension_semantics=sem, vmem_limit_bytes=VMEM_LIMIT)


def _iota(shape, axis):
    return lax.broadcasted_iota(jnp.int32, shape, axis)


def _onehot(mask):
    return jnp.where(mask, 1.0, 0.0).astype(BF16)


def _dot(a, b):
    return jnp.dot(a, b, preferred_element_type=F32)


def _split3(x):
    hi = x.astype(BF16)
    r = x - hi.astype(F32)
    mid = r.astype(BF16)
    lo = (r - mid.astype(F32)).astype(BF16)
    return hi, mid, lo


def _dot01(x, e):
    hi, mid, lo = _split3(x)
    return _dot(hi, e) + _dot(mid, e) + _dot(lo, e)


def _dot01_l(e, x):
    hi, mid, lo = _split3(x)
    return _dot(e, hi) + _dot(e, mid) + _dot(e, lo)


def _rms(x, g):
    ms = jnp.mean(x * x, axis=-1, keepdims=True)
    return x * lax.rsqrt(ms + EPS) * g


def _silu(x):
    return x * jax.nn.sigmoid(x)


def _softplus(x):
    return jnp.maximum(x, 0.0) + jnp.log1p(jnp.exp(-jnp.abs(x)))


def _softplus_abs(x):
    return jnp.maximum(x, 0.0) + jnp.log(1.0 + jnp.exp(-jnp.abs(x)))


def _head_expand():
    return _onehot((_iota((LANES, D_MODEL), 1) >> HEAD_SHIFT) == _iota((LANES, D_MODEL), 0))


def _head_reduce():
    return _onehot((_iota((D_MODEL, LANES), 0) >> HEAD_SHIFT) == _iota((D_MODEL, LANES), 1))


def _ffn_body(x_ref, g_ref, wgu_ref, wd_ref, *rest, mix_out, final_norm):
    o_ref = rest[-1]
    x = x_ref[...]
    if mix_out:
        y_ref, mo_ref, wo_ref = rest[:3]
        x = x + _dot(y_ref[...].astype(BF16), wo_ref[0:SSD_INNER, :])
        x = x + _dot(mo_ref[...].astype(BF16), wo_ref[SSD_INNER:, :])
    xn = _rms(x, g_ref[...]).astype(BF16)
    acc = None
    for j in range(D_FF // FF_TILE):
        lo = j * FF_TILE
        gate = _dot(xn, wgu_ref[:, lo:lo + FF_TILE])
        up = _dot(xn, wgu_ref[:, D_FF + lo:D_FF + lo + FF_TILE])
        part = _dot((_silu(gate) * up).astype(BF16), wd_ref[lo:lo + FF_TILE, :])
        acc = part if acc is None else acc + part
    y = x + 0.5 * acc
    o_ref[...] = _rms(y, rest[-2][...]) if final_norm else y


def _ffn(x, nrm, wgu, wd, ls, tm, final_g=None, mix=None):
    T = x.shape[0]
    resident = pl.Buffered(1)
    tok = pl.BlockSpec((tm, D_MODEL), lambda i: (i, 0))
    args = [x, nrm, wgu, wd]
    specs = [
        tok,
        pl.BlockSpec((None, 1, D_MODEL), lambda i: (ls, 0, 0)),
        pl.BlockSpec((None, D_MODEL, 2 * D_FF), lambda i: (ls, 0, 0), pipeline_mode=resident),
        pl.BlockSpec((None, D_FF, D_MODEL), lambda i: (ls, 0, 0), pipeline_mode=resident),
    ]
    if mix is not None:
        y, o, w_out, li = mix
        args += [y, o, w_out]
        specs += [tok, tok, pl.BlockSpec((None, SSD_INNER + GLA_VAL, D_MODEL), lambda i: (li, 0, 0),
                                         pipeline_mode=resident)]
    if final_g is not None:
        args.append(final_g)
        specs.append(pl.BlockSpec((1, D_MODEL), lambda i: (0, 0)))
    return pl.pallas_call(
        functools.partial(_ffn_body, mix_out=mix is not None, final_norm=final_g is not None),
        grid=(T // tm,),
        in_specs=specs,
        out_specs=pl.BlockSpec((tm, D_MODEL), lambda i: (i, 0)),
        out_shape=jax.ShapeDtypeStruct((T, D_MODEL), F32),
        compiler_params=_cp(("parallel",)),
        name="ffn",
    )(*args)


_EVEN_PIECES = (SSD_INNER, SSD_CONV_DIM, GLA_KEY, GLA_KEY, GLA_VAL, GLA_VAL, SMALL)


def _even_in_body(x_ref, g_ref, w_ref, *out_refs):
    h = _rms(x_ref[...], g_ref[...]).astype(BF16)
    off = 0
    for ref, n in zip(out_refs, _EVEN_PIECES):
        ref[...] = _dot(h, w_ref[:, off:off + n])
        off += n


def _even_in(x, nrm, w, i, tm):
    T = x.shape[0]
    return pl.pallas_call(
        _even_in_body,
        grid=(T // tm,),
        in_specs=[
            pl.BlockSpec((tm, D_MODEL), lambda t: (t, 0)),
            pl.BlockSpec((None, 1, D_MODEL), lambda t: (2 * i, 0, 0)),
            pl.BlockSpec((None, D_MODEL, IN_PERM), lambda t: (i, 0, 0)),
        ],
        out_specs=[pl.BlockSpec((tm, n), lambda t: (t, 0)) for n in _EVEN_PIECES],
        out_shape=[jax.ShapeDtypeStruct((T, n), F32) for n in _EVEN_PIECES],
        compiler_params=_cp(("parallel",)),
        name="even_in",
    )(x, nrm, w)


def _ssd_seq_body(xbc_ref, sm_ref, z_ref, cw_ref, cb_ref, dtb_ref, alog_ref, dsk_ref, nw_ref,
                  y_ref, convn_ref, ssmn_ref,
                  cbuf_ref, xact_ref, acx_ref, dtx_ref, hT_ref, *, nblk):
    Lb = SEQ_BLOCK
    C = SSD_CHUNK
    j = pl.program_id(1)

    @pl.when(j == 0)
    def _():
        cbuf_ref[0:8, :] = jnp.zeros((8, SSD_CONV_DIM), F32)
        hT_ref[...] = jnp.zeros_like(hT_ref)

    x = xbc_ref[...]
    acc = cb_ref[...] + pltpu.roll(x, 3, axis=0) * cw_ref[0:1, :]
    acc = acc + pltpu.roll(x, 2, axis=0) * cw_ref[1:2, :]
    acc = acc + pltpu.roll(x, 1, axis=0) * cw_ref[2:3, :]
    acc = acc + x * cw_ref[3:4, :]
    xact_ref[...] = _silu(acc)
    cbuf_ref[8:16, :] = x[0:8, :]
    head = cb_ref[...] + cbuf_ref[5:13, :] * cw_ref[0:1, :]
    head = head + cbuf_ref[6:14, :] * cw_ref[1:2, :]
    head = head + cbuf_ref[7:15, :] * cw_ref[2:3, :]
    head = head + cbuf_ref[8:16, :] * cw_ref[3:4, :]
    xact_ref[0:8, :] = _silu(head)
    tail = x[Lb - 8:Lb, :]
    cbuf_ref[0:8, :] = tail

    @pl.when(j == nblk - 1)
    def _():
        convn_ref[...] = tail[5:8, :]

    dt = _softplus(sm_ref[...] + dtb_ref[...])
    dta = dt * (-jnp.exp(alog_ref[...]))
    ii = _iota((Lb, Lb), 0)
    jj = _iota((Lb, Lb), 1)
    tril = _onehot(((ii >> HEAD_SHIFT) == (jj >> HEAD_SHIFT)) & (ii >= jj))
    acum = _dot01_l(tril, dta)
    expand = _head_expand()
    acx_ref[...] = _dot01(acum, expand)
    dtx_ref[...] = _dot01(dt, expand)

    li = _iota((C, D_MODEL), 0)
    lj = _iota((C, D_MODEL), 1) & (C - 1)
    diag = li == lj
    causal = li >= lj
    ones_c = jnp.ones((C, C), BF16)
    ones_c2 = jnp.ones((C, 2 * C), BF16)
    tile8 = _onehot((_iota((2 * C, 8 * C), 1) & (C - 1)) == (_iota((2 * C, 8 * C), 0) & (C - 1)))
    pair_mask = (_iota((LANES, LANES), 0) >> HEAD_SHIFT) == (_iota((LANES, LANES), 1) >> HEAD_SHIFT)
    half = SSD_INNER // SSD_GROUPS

    def chunk(c, carry):
        r0 = pl.multiple_of(c * C, C)
        xa = xact_ref[pl.ds(r0, C), :]
        xs = xa[:, :SSD_INNER]
        bms = [xa[:, SSD_INNER + g * SSD_D_STATE:SSD_INNER + (g + 1) * SSD_D_STATE] for g in range(2)]
        cms = [xa[:, SSD_INNER + (2 + g) * SSD_D_STATE:SSD_INNER + (3 + g) * SSD_D_STATE]
               for g in range(2)]
        acx = acx_ref[pl.ds(r0, C), :]
        dtx = dtx_ref[pl.ds(r0, C), :]
        alast = acx[C - 1:C, :]
        arow = _dot01_l(ones_c, jnp.where(diag, acx, 0.0))
        dsel = jnp.where(diag, dtx, 0.0)
        dhi = dsel.astype(BF16)
        dmid = (dsel - dhi.astype(F32)).astype(BF16)
        dtrow = _dot(ones_c2, jnp.concatenate([dhi, dmid], axis=0))
        decay = jnp.exp(jnp.where(causal, acx - arow, -jnp.inf))
        cbt = []
        for g in range(2):
            cb = lax.dot_general(cms[g].astype(BF16), bms[g].astype(BF16),
                                 (((1,), (1,)), ((), ())), preferred_element_type=F32)
            cbt.append(_dot01x2(cb, tile8))
        wts = (jnp.concatenate(cbt, axis=1) * decay * dtrow).astype(BF16)
        ys = []
        for p in range(SSD_HEADS // 2):
            xp = xs[:, p * LANES:(p + 1) * LANES]
            xbd = jnp.where(pair_mask, jnp.concatenate([xp, xp], axis=0), 0.0).astype(BF16)
            ys.append(_dot(wts[:, p * LANES:(p + 1) * LANES], xbd))
        y = jnp.concatenate(ys, axis=1)
        hT = hT_ref[...]
        hTb = hT.astype(BF16)
        yi = [_dot(cms[g].astype(BF16), hTb[:, g * half:(g + 1) * half]) for g in range(2)]
        y = y + jnp.concatenate(yi, axis=1) * jnp.exp(acx)
        xsc = (jnp.exp(alast - acx) * dtx * xs).astype(BF16)
        st = [lax.dot_general(bms[g].astype(BF16), xsc[:, g * half:(g + 1) * half],
                              (((0,), (0,)), ((), ())), preferred_element_type=F32) for g in range(2)]
        hT_ref[...] = hT * jnp.exp(alast) + jnp.concatenate(st, axis=1)
        y = y + dsk_ref[...] * xs
        y = y * _silu(z_ref[pl.ds(r0, C), :])
        outs = []
        for g in range(2):
            yg = y[:, g * half:(g + 1) * half]
            ms = jnp.mean(yg * yg, axis=-1, keepdims=True)
            outs.append(yg * lax.rsqrt(ms + EPS))
        y_ref[pl.ds(r0, C), :] = jnp.concatenate(outs, axis=1) * nw_ref[...]
        return carry

    lax.fori_loop(0, Lb // C, chunk, 0, unroll=4)

    @pl.when(j == nblk - 1)
    def _():
        ssmn_ref[...] = hT_ref[...].T


def _ssd_seq(xbc, sm, z, P, i, B, L):
    Lb = SEQ_BLOCK
    nblk = L // Lb
    T = B * L
    tok = lambda n: pl.BlockSpec((Lb, n), lambda b, j: (b * nblk + j, 0))
    par = lambda r, n: pl.BlockSpec((None, r, n), lambda b, j: (i, 0, 0))
    y, convn, ssmn = pl.pallas_call(
        functools.partial(_ssd_seq_body, nblk=nblk),
        grid=(B, nblk),
        in_specs=[tok(SSD_CONV_DIM), tok(SMALL), tok(SSD_INNER),
                  par(SSD_CONV, SSD_CONV_DIM), par(1, SSD_CONV_DIM), par(1, SMALL), par(1, SMALL),
                  par(1, SSD_INNER), par(1, SSD_INNER)],
        out_specs=[tok(SSD_INNER),
                   pl.BlockSpec((None, SSD_CONV - 1, SSD_CONV_DIM), lambda b, j: (b, 0, 0)),
                   pl.BlockSpec((None, SSD_INNER, SSD_D_STATE), lambda b, j: (b, 0, 0))],
        out_shape=[jax.ShapeDtypeStruct((T, SSD_INNER), F32),
                   jax.ShapeDtypeStruct((B, SSD_CONV - 1, SSD_CONV_DIM), F32),
                   jax.ShapeDtypeStruct((B, SSD_INNER, SSD_D_STATE), F32)],
        scratch_shapes=[pltpu.VMEM((16, SSD_CONV_DIM), F32),
                        pltpu.VMEM((Lb, SSD_CONV_DIM), F32),
                        pltpu.VMEM((Lb, SSD_INNER), F32),
                        pltpu.VMEM((Lb, SSD_INNER), F32),
                        pltpu.VMEM((SSD_D_STATE, SSD_INNER), F32)],
        compiler_params=_cp(("parallel", "arbitrary")),
        name="ssd_seq",
    )(xbc, sm, z, P["conv_w"], P["conv_b"], P["dt_bias"], P["a_log"], P["d_skip_x"], P["ssd_norm"])
    return y, convn, ssmn.reshape(B, SSD_HEADS, SSD_HEAD_DIM, SSD_D_STATE)


def _gla_seq_body(q_ref, k_ref, v_ref, og_ref, sm_ref, w2_ref, gb_ref, nw_ref,
                  o_ref, glan_ref,
                  st_ref, dec_ref, qg_ref, kd_ref, oi_ref, *, nblk):
    Lb = SEQ_BLOCK
    j = pl.program_id(1)

    @pl.when(j == 0)
    def _():
        st_ref[...] = jnp.zeros_like(st_ref)

    gate = _dot(sm_ref[...].astype(BF16), w2_ref[...]) + gb_ref[...]
    lg = -_softplus_abs(-gate) / GLA_GATE_NORM
    ii = _iota((Lb, Lb), 0)
    jj = _iota((Lb, Lb), 1)
    same = (ii >> GLA_CHUNK_SHIFT) == (jj >> GLA_CHUNK_SHIFT)
    causal = same & (ii >= jj)
    lg_hi = lg.astype(BF16)
    lg2 = jnp.concatenate([lg_hi, (lg - lg_hi.astype(F32)).astype(BF16)], axis=0)
    sel_c = _onehot(causal)
    sel_s = _onehot(same)
    b = _dot(jnp.concatenate([sel_c, sel_c], axis=1), lg2)
    blast = _dot(jnp.concatenate([sel_s, sel_s], axis=1), lg2)
    qg = q_ref[...] * (GLA_DK ** -0.5) * jnp.exp(b)
    kk = k_ref[...]
    kg = kk * jnp.exp(-b)
    kd = kk * jnp.exp(blast - b)
    dec_ref[...] = jnp.exp(blast)
    qg_ref[...] = qg.astype(BF16)
    kd_ref[...] = kd.astype(BF16)

    def chunk(c, carry):
        r0 = pl.multiple_of(c * GLA_CHUNK, GLA_CHUNK)
        for h in range(GLA_HEADS):
            ks = slice(h * GLA_DK, (h + 1) * GLA_DK)
            vs = slice(h * GLA_DV, (h + 1) * GLA_DV)
            st = st_ref[h]
            oi_ref[pl.ds(r0, GLA_CHUNK), vs] = lax.dot_general(
                qg_ref[pl.ds(r0, GLA_CHUNK), ks], st.astype(BF16),
                (((1,), (1,)), ((), ())), preferred_element_type=F32)
            upd = lax.dot_general(v_ref[pl.ds(r0, GLA_CHUNK), vs].astype(BF16),
                                  kd_ref[pl.ds(r0, GLA_CHUNK), ks],
                                  (((0,), (0,)), ((), ())), preferred_element_type=F32)
            st_ref[h] = st * dec_ref[pl.ds(r0, 8), ks][0:1, :] + upd
        return carry

    lax.fori_loop(0, Lb // GLA_CHUNK, chunk, 0, unroll=16)

    for h in range(GLA_HEADS):
        ks = slice(h * GLA_DK, (h + 1) * GLA_DK)
        vs = slice(h * GLA_DV, (h + 1) * GLA_DV)
        att = lax.dot_general(qg[:, ks].astype(BF16), kg[:, ks].astype(BF16),
                              (((1,), (1,)), ((), ())), preferred_element_type=F32)
        att = jnp.where(causal, att, 0.0)
        o_h = _dot(att.astype(BF16), v_ref[:, vs].astype(BF16)) + oi_ref[:, vs]
        ms = jnp.mean(o_h * o_h, axis=-1, keepdims=True)
        o_h = o_h * lax.rsqrt(ms + EPS) * nw_ref[...]
        o_ref[:, vs] = o_h * _silu(og_ref[:, vs])

    @pl.when(j == nblk - 1)
    def _():
        for h in range(GLA_HEADS):
            glan_ref[h] = st_ref[h].T


def _gla_seq(q, k, v, og, sm, P, i, B, L):
    Lb = SEQ_BLOCK
    nblk = L // Lb
    T = B * L
    tok = lambda n: pl.BlockSpec((Lb, n), lambda b, j: (b * nblk + j, 0))
    par = lambda r, n: pl.BlockSpec((None, r, n), lambda b, j: (i, 0, 0))
    return pl.pallas_call(
        functools.partial(_gla_seq_body, nblk=nblk),
        grid=(B, nblk),
        in_specs=[tok(GLA_KEY), tok(GLA_KEY), tok(GLA_VAL), tok(GLA_VAL), tok(SMALL),
                  par(SMALL, GLA_KEY), par(1, GLA_KEY), par(1, GLA_DV)],
        out_specs=[tok(GLA_VAL),
                   pl.BlockSpec((None, GLA_HEADS, GLA_DK, GLA_DV), lambda b, j: (b, 0, 0, 0))],
        out_shape=[jax.ShapeDtypeStruct((T, GLA_VAL), F32),
                   jax.ShapeDtypeStruct((B, GLA_HEADS, GLA_DK, GLA_DV), F32)],
        scratch_shapes=[pltpu.VMEM((GLA_HEADS, GLA_DV, GLA_DK), F32),
                        pltpu.VMEM((Lb, GLA_KEY), F32),
                        pltpu.VMEM((Lb, GLA_KEY), BF16),
                        pltpu.VMEM((Lb, GLA_KEY), BF16),
                        pltpu.VMEM((Lb, GLA_VAL), F32)],
        compiler_params=_cp(("parallel", "arbitrary")),
        name="gla_seq",
    )(q, k, v, og, sm, P["gate_w2"], P["gate_b"], P["gla_norm"])


def _even_step_pre_body(xbc_ref, c0_ref, sm_ref, cw_ref, cb_ref, dtb_ref, alog_ref, dsk_ref,
                        w2_ref, gb_ref,
                        xact_ref, convn_ref, dax_ref, c2_ref, ybase_ref, eg_ref):
    u = xbc_ref[...]
    n = SSD_CONV_DIM
    acc = cb_ref[...] + c0_ref[:, 0:n] * cw_ref[0:1, :]
    acc = acc + c0_ref[:, n:2 * n] * cw_ref[1:2, :]
    acc = acc + c0_ref[:, 2 * n:3 * n] * cw_ref[2:3, :]
    acc = acc + u * cw_ref[3:4, :]
    xa = _silu(acc)
    xact_ref[...] = xa
    convn_ref[:, 0:n] = c0_ref[:, n:2 * n]
    convn_ref[:, n:2 * n] = c0_ref[:, 2 * n:3 * n]
    convn_ref[:, 2 * n:3 * n] = u
    dt = _softplus(sm_ref[...] + dtb_ref[...])
    dta = dt * (-jnp.exp(alog_ref[...]))
    expand = _head_expand()
    dtx = _dot01(dt, expand)
    dax_ref[...] = jnp.exp(_dot01(dta, expand))
    xs = xa[:, :SSD_INNER]
    c2_ref[...] = dtx * xs
    half = SSD_INNER // SSD_GROUPS
    cbs = []
    for g in range(2):
        bm = xa[:, SSD_INNER + g * SSD_D_STATE:SSD_INNER + (g + 1) * SSD_D_STATE]
        cm = xa[:, SSD_INNER + (2 + g) * SSD_D_STATE:SSD_INNER + (3 + g) * SSD_D_STATE]
        cb = jnp.sum(cm * bm, axis=-1, keepdims=True)
        cbs.append(jnp.broadcast_to(cb, (cb.shape[0], half)))
    ybase_ref[...] = jnp.concatenate(cbs, axis=1) * dtx * xs + dsk_ref[...] * xs
    gate = _dot(sm_ref[...].astype(BF16), w2_ref[...]) + gb_ref[...]
    eg_ref[...] = jnp.exp(-_softplus(-gate) / GLA_GATE_NORM)


def _even_step_pre(xbc, conv0, sm, P, i):
    Bn = xbc.shape[0]
    full = lambda n: pl.BlockSpec((Bn, n), lambda t: (0, 0))
    par = lambda r, n: pl.BlockSpec((None, r, n), lambda t: (i, 0, 0))
    outs = (SSD_CONV_DIM, 3 * SSD_CONV_DIM, SSD_INNER, SSD_INNER, SSD_INNER, GLA_KEY)
    return pl.pallas_call(
        _even_step_pre_body,
        grid=(1,),
        in_specs=[full(SSD_CONV_DIM), full(3 * SSD_CONV_DIM), full(SMALL),
                  par(SSD_CONV, SSD_CONV_DIM), par(1, SSD_CONV_DIM), par(1, SMALL), par(1, SMALL),
                  par(1, SSD_INNER), par(SMALL, GLA_KEY), par(1, GLA_KEY)],
        out_specs=[full(n) for n in outs],
        out_shape=[jax.ShapeDtypeStruct((Bn, n), F32) for n in outs],
        compiler_params=_cp(("arbitrary",)),
        name="even_step_pre",
    )(xbc, conv0, sm, P["conv_w"], P["conv_b"], P["dt_bias"], P["a_log"], P["d_skip_x"],
      P["gate_w2"], P["gate_b"])


def _to_cols(x):
    Bn, N = x.shape
    return x.reshape(Bn // STEP_BATCH, STEP_BATCH, N).transpose(0, 2, 1)


def _from_cols(x):
    nb, N, bb = x.shape
    return x.transpose(0, 2, 1).reshape(nb * bb, N)


def _layer_state_call(body, name, state_all, acc, i, rows, width, args, specs, out_spec, out_shape):
    nl, Bn = state_all.shape[:2]
    st = pl.BlockSpec((None, STEP_BATCH, rows, width), lambda t: (i, t, 0, 0))
    args = [state_all.reshape(nl, Bn, rows, width)] + list(args)
    specs = [st] + list(specs)
    aliases = {}
    if acc is not None:
        aliases = {len(args): 0}
        args.append(acc)
        specs.append(pl.BlockSpec(memory_space=pl.ANY))
    return pl.pallas_call(
        body,
        grid=(Bn // STEP_BATCH,),
        in_specs=specs,
        out_specs=[st, out_spec],
        out_shape=[jax.ShapeDtypeStruct((nl, Bn, rows, width), F32), out_shape],
        input_output_aliases=aliases,
        compiler_params=_cp(("parallel",)),
        name=name,
    )(*args)


def _ssd_step_body(h_ref, c1_ref, c2_ref, bm_ref, cm_ref, *rest):
    hn_ref, y_ref = rest[-2:]
    half = SSD_INNER // SSD_GROUPS
    lane = _iota((SSD_INNER, STEP_BATCH), 1)
    ycols = jnp.zeros((SSD_INNER, STEP_BATCH), F32)
    c1 = c1_ref[...]
    c2 = c2_ref[...]
    for b in range(STEP_BATCH):
        h0 = h_ref[b]
        bmat = jnp.concatenate(
            [jnp.broadcast_to(bm_ref[b:b + 1, g * SSD_D_STATE:(g + 1) * SSD_D_STATE],
                              (half, SSD_D_STATE)) for g in range(2)], axis=0)
        cmat = jnp.concatenate(
            [jnp.broadcast_to(cm_ref[b:b + 1, g * SSD_D_STATE:(g + 1) * SSD_D_STATE],
                              (half, SSD_D_STATE)) for g in range(2)], axis=0)
        ycol = jnp.sum(h0 * cmat, axis=1, keepdims=True)
        ycols = jnp.where(lane == b, ycol, ycols)
        hn_ref[b] = h0 * c1[:, b:b + 1] + c2[:, b:b + 1] * bmat
    y_ref[...] = ycols


def _ssd_step(state_all, acc, i, dax, c2, xact):
    Bn = state_all.shape[1]
    bb = STEP_BATCH
    bm = xact[:, SSD_INNER:SSD_INNER + 2 * SSD_D_STATE]
    cm = xact[:, SSD_INNER + 2 * SSD_D_STATE:]
    col = pl.BlockSpec((None, SSD_INNER, bb), lambda t: (t, 0, 0))
    row = pl.BlockSpec((bb, 2 * SSD_D_STATE), lambda t: (t, 0))
    acc, ycols = _layer_state_call(
        _ssd_step_body, "ssd_step", state_all, acc, i, SSD_INNER, SSD_D_STATE,
        [_to_cols(dax), _to_cols(c2), bm, cm], [col, col, row, row],
        col, jax.ShapeDtypeStruct((Bn // bb, SSD_INNER, bb), F32))
    return acc, _from_cols(ycols)


def _gla_step_body(s_ref, eg_ref, kc_ref, qc_ref, v_ref, *rest):
    sn_ref, o_ref = rest[-2:]
    eg = eg_ref[...]
    kc = kc_ref[...]
    qc = qc_ref[...] * (GLA_DK ** -0.5)
    for b in range(STEP_BATCH):
        s0 = s_ref[b]
        vmat = jnp.concatenate(
            [jnp.broadcast_to(v_ref[b:b + 1, h * GLA_DV:(h + 1) * GLA_DV], (GLA_DK, GLA_DV))
             for h in range(GLA_HEADS)], axis=0)
        sn = s0 * eg[:, b:b + 1] + kc[:, b:b + 1] * vmat
        sn_ref[b] = sn
        t = sn * qc[:, b:b + 1]
        o_ref[b:b + 1, :] = jnp.concatenate(
            [jnp.sum(t[h * GLA_DK:(h + 1) * GLA_DK], axis=0, keepdims=True)
             for h in range(GLA_HEADS)], axis=1)


def _gla_step(state_all, acc, i, eg, k, q, v):
    Bn = state_all.shape[1]
    bb = STEP_BATCH
    col = pl.BlockSpec((None, GLA_KEY, bb), lambda t: (t, 0, 0))
    row = pl.BlockSpec((bb, GLA_VAL), lambda t: (t, 0))
    return _layer_state_call(
        _gla_step_body, "gla_step", state_all, acc, i, GLA_KEY, GLA_DV,
        [_to_cols(eg), _to_cols(k), _to_cols(q), v], [col, col, col, row],
        row, jax.ShapeDtypeStruct((Bn, GLA_VAL), F32))


def _even_step_post_body(x_ref, ybase_ref, yi_ref, dax_ref, z_ref, o_ref, og_ref,
                         snw_ref, gnw_ref, w_ref, out_ref):
    y = (ybase_ref[...] + yi_ref[...] * dax_ref[...]) * _silu(z_ref[...])
    half = SSD_INNER // SSD_GROUPS
    acc = None
    for g in range(2):
        yg = y[:, g * half:(g + 1) * half]
        ms = jnp.mean(yg * yg, axis=-1, keepdims=True)
        yn = yg * lax.rsqrt(ms + EPS) * snw_ref[:, g * half:(g + 1) * half]
        part = _dot(yn.astype(BF16), w_ref[g * half:(g + 1) * half, :])
        acc = part if acc is None else acc + part
    for h in range(GLA_HEADS):
        vs = slice(h * GLA_DV, (h + 1) * GLA_DV)
        oh = o_ref[:, vs]
        ms = jnp.mean(oh * oh, axis=-1, keepdims=True)
        on = oh * lax.rsqrt(ms + EPS) * gnw_ref[...] * _silu(og_ref[:, vs])
        acc = acc + _dot(on.astype(BF16), w_ref[SSD_INNER + h * GLA_DV:SSD_INNER + (h + 1) * GLA_DV, :])
    out_ref[...] = x_ref[...] + acc


def _even_step_post(x, ybase, yi, dax, z, o, og, P, w_out, i):
    Bn = x.shape[0]
    full = pl.BlockSpec((Bn, D_MODEL), lambda t: (0, 0))
    return pl.pallas_call(
        _even_step_post_body,
        grid=(1,),
        in_specs=[full] * 7 + [pl.BlockSpec((None, 1, SSD_INNER), lambda t: (i, 0, 0)),
                               pl.BlockSpec((None, 1, GLA_DV), lambda t: (i, 0, 0)),
                               pl.BlockSpec((None, SSD_INNER + GLA_VAL, D_MODEL), lambda t: (i, 0, 0))],
        out_specs=full,
        out_shape=jax.ShapeDtypeStruct((Bn, D_MODEL), F32),
        compiler_params=_cp(("arbitrary",)),
        name="even_step_post",
    )(x, ybase, yi, dax, z, o, og, P["ssd_norm"], P["gla_norm"], w_out)


_ODD_OUTS = 8


def _odd_pre_body(*refs, step, has_vfirst, nblk_seq, tm):
    it = iter(refs)
    x_ref = next(it)
    if step:
        prev_ref = next(it)
    else:
        xp_ref = next(it)
        sh0_ref = next(it)
    g_ref, mix_ref = next(it), next(it)
    wr_ref, wk_ref, wv_ref = next(it), next(it), next(it)
    w0_ref, w1_ref, w2_ref = next(it), next(it), next(it)
    a0_ref, a1_ref, a2_ref = next(it), next(it), next(it)
    if has_vfirst:
        vf_ref, v0_ref, v1_ref, v2_ref = next(it), next(it), next(it), next(it)
    g1_ref, g2_ref = next(it), next(it)
    kkw_ref, kaw_ref = next(it), next(it)
    h_ref, r_ref, dec_ref, k_ref, v_ref, an_ref, b_ref, gg_ref = (next(it) for _ in range(_ODD_OUTS))

    h = _rms(x_ref[...], g_ref[...])
    if step:
        prev = prev_ref[...]
    else:
        i = pl.program_id(0)
        hp = _rms(xp_ref[...], g_ref[...])[7:8, :]
        prow = jnp.where(i % nblk_seq == 0, sh0_ref[...], hp)
        prev = jnp.where(_iota((tm, D_MODEL), 0) == 0, prow, pltpu.roll(h, 1, axis=0))
    h_ref[...] = h
    xx = prev - h
    mixed = lambda r: (h + xx * mix_ref[r:r + 1, :]).astype(BF16)
    xr, xw, xk, xv, xa, xg = (mixed(r) for r in range(6))
    r_ref[...] = _dot(xr, wr_ref[...])
    wl = w0_ref[...] + _dot(jnp.tanh(_dot(xw, w1_ref[...])).astype(BF16), w2_ref[...])
    w = -_softplus_abs(-wl) - 0.5
    dec_ref[...] = jnp.exp(-jnp.exp(w))
    k = _dot(xk, wk_ref[...])
    v = _dot(xv, wv_ref[...])
    if has_vfirst:
        gate = jax.nn.sigmoid(v0_ref[...] + _dot(_dot(xv, v1_ref[...]).astype(BF16), v2_ref[...]))
        v = v + (vf_ref[...] - v) * gate
    v_ref[...] = v
    a = jax.nn.sigmoid(a0_ref[...] + _dot(_dot(xa, a1_ref[...]).astype(BF16), a2_ref[...]))
    gg_ref[...] = _dot(jax.nn.sigmoid(_dot(xg, g1_ref[...])).astype(BF16), g2_ref[...])
    kkf = k * kkw_ref[...]
    red = _head_reduce()
    expand = _head_expand()
    ss = _dot01x2(kkf * kkf, jnp.concatenate([red, red], axis=0))
    kk = kkf * _dot01x2(lax.rsqrt(jnp.maximum(ss, 1e-24)), jnp.concatenate([expand, expand], axis=0))
    k_ref[...] = k * (1.0 + (a - 1.0) * kaw_ref[...])
    an_ref[...] = -kk
    b_ref[...] = kk * a


def _odd_pre(x, prev_or_shift0, vfirst, P, i, tm, L):
    T = x.shape[0]
    step = L == 1
    has_vfirst = vfirst is not None
    nblk_seq = max(L // tm, 1)
    tok = pl.BlockSpec((tm, D_MODEL), lambda t: (t, 0))
    par = lambda r, n, idx=i: pl.BlockSpec((None, r, n), lambda t: (idx, 0, 0), pipeline_mode=pl.Buffered(1))
    args = [x]
    specs = [tok]
    if step:
        args.append(prev_or_shift0)
        specs.append(tok)
    else:
        args += [x, prev_or_shift0]
        specs += [pl.BlockSpec((8, D_MODEL), lambda t: (jnp.maximum(t * (tm // 8) - 1, 0), 0)),
                  pl.BlockSpec((None, 1, D_MODEL), lambda t: (t // nblk_seq, 0, 0))]
    args += [P["mix_norm"], P["mix"], P["w_r"], P["w_k"], P["w_v"], P["w0"], P["w1"], P["w2"],
             P["a0"], P["a1"], P["a2"]]
    specs += [pl.BlockSpec((None, 1, D_MODEL), lambda t: (2 * i + 1, 0, 0)),
              par(8, D_MODEL), par(D_MODEL, D_MODEL), par(D_MODEL, D_MODEL), par(D_MODEL, D_MODEL),
              par(1, D_MODEL), par(D_MODEL, LANES), par(LANES, D_MODEL),
              par(1, D_MODEL), par(D_MODEL, LANES), par(LANES, D_MODEL)]
    if has_vfirst:
        args += [vfirst, P["v0"], P["v1"], P["v2"]]
        specs += [tok, par(1, D_MODEL, i - 1), par(D_MODEL, LANES, i - 1), par(LANES, D_MODEL, i - 1)]
    args += [P["g1"], P["g2"], P["k_k"], P["k_a"]]
    specs += [par(D_MODEL, 2 * LANES), par(2 * LANES, D_MODEL), par(1, D_MODEL), par(1, D_MODEL)]
    return pl.pallas_call(
        functools.partial(_odd_pre_body, step=step, has_vfirst=has_vfirst, nblk_seq=nblk_seq, tm=tm),
        grid=(T // tm,),
        in_specs=specs,
        out_specs=[tok] * _ODD_OUTS,
        out_shape=[jax.ShapeDtypeStruct((T, D_MODEL), F32)] * _ODD_OUTS,
        compiler_params=_cp(("parallel",)),
        name="odd_pre",
    )(*args)


def _dot01x2(x, e2):
    hi = x.astype(BF16)
    mid = (x - hi.astype(F32)).astype(BF16)
    return _dot(jnp.concatenate([hi, mid], axis=1), e2)


def _odd_post_body(x_ref, o_ref, r_ref, k_ref, v_ref, g_ref, gnw_ref, gnb_ref, rk_ref, wo_ref, out_ref):
    red = _head_reduce()
    red = jnp.concatenate([red, red], axis=0)
    expand = _head_expand()
    expand = jnp.concatenate([expand, expand], axis=0)
    o = o_ref[...]
    mu = _dot01x2(_dot01x2(o, red) * (1.0 / RWKV_HEAD), expand)
    d = o - mu
    var = _dot01x2(d * d, red) * (1.0 / RWKV_HEAD)
    on = d * _dot01x2(lax.rsqrt(var + RWKV_GN_EPS), expand) * gnw_ref[...] + gnb_ref[...]
    v = v_ref[...]
    bonus = _dot01x2(_dot01x2(r_ref[...] * k_ref[...] * rk_ref[...], red), expand) * v
    out_ref[...] = x_ref[...] + _dot(((on + bonus) * g_ref[...]).astype(BF16), wo_ref[...])


def _odd_post(x, o, r, k, v, g, P, i, tm):
    T = x.shape[0]
    tok = pl.BlockSpec((tm, D_MODEL), lambda t: (t, 0))
    par = lambda r_, n: pl.BlockSpec((None, r_, n), lambda t: (i, 0, 0))
    return pl.pallas_call(
        _odd_post_body,
        grid=(T // tm,),
        in_specs=[tok] * 6 + [par(1, D_MODEL), par(1, D_MODEL), par(1, D_MODEL), par(D_MODEL, D_MODEL)],
        out_specs=tok,
        out_shape=jax.ShapeDtypeStruct((T, D_MODEL), F32),
        compiler_params=_cp(("parallel",)),
        name="odd_post",
    )(x, o, r, k, v, g, P["gn_w"], P["gn_b"], P["r_k"], P["w_o"])


RWKV_TB = 64
RWKV_NPAIR = RWKV_HEADS // 2


RWKV_SEQS = 2
RWKV_SPLIT = 1
RWKV_UNROLL = 8


def _rwkv_seq_body(w_ref, k_ref, a_ref, b_ref, r_ref, v_ref, o_ref, sn_ref,
                   s_ref, vt_ref, oacc_ref, ap_ref, bp_ref, kp_ref, rp_ref, gl_ref, *, ntb):
    N = RWKV_HEAD
    R = RWKV_NPAIR * N
    tbi = pl.program_id(1)

    @pl.when(tbi == 0)
    def _():
        s_ref[...] = jnp.zeros_like(s_ref)

    seg = _onehot((_iota((2 * LANES, 2 * LANES), 0) >> HEAD_SHIFT) == (_iota((2 * LANES, 2 * LANES), 1) >> HEAD_SHIFT))

    def seg_sum(xb):
        h = xb.shape[0] // 2
        both = _dot(jnp.concatenate([xb[:h], xb[h:]], axis=1), seg)
        return jnp.concatenate([both[:, :LANES], both[:, LANES:]], axis=0)
    incl = _onehot(_iota((RWKV_TB, RWKV_TB), 0) >= _iota((RWKV_TB, RWKV_TB), 1))
    for c in range(RWKV_SEQS):
        logw = jnp.log(w_ref[c])
        cum = _dot01_l(incl, logw)
        g_inc = jnp.exp(cum)
        g_inv = jnp.exp(-cum)
        ap_ref[c] = a_ref[c] * jnp.exp(cum - logw)
        bp_ref[c] = b_ref[c] * g_inv
        kp_ref[c] = k_ref[c] * g_inv
        rp_ref[c] = r_ref[c] * g_inc
        gl_ref[c] = g_inc[RWKV_TB - 8:RWKV_TB, :]
        for p in range(RWKV_NPAIR):
            bt = v_ref[c, :, p * LANES:(p + 1) * LANES].T
            vt_ref[c, p * N:(p + 1) * N, :] = jnp.concatenate([bt[0:N], bt[N:2 * N]], axis=1)
    oacc_ref[...] = jnp.zeros_like(oacc_ref)
    G = R // RWKV_SPLIT
    OSTEPS = LANES // 4
    seg_out = _onehot((_iota((2 * LANES, LANES), 0) >> HEAD_SHIFT) == (_iota((2 * LANES, LANES), 1) >> (OSTEPS.bit_length() - 1)))
    lane_copy = _iota((R // 2, LANES), 1) & (OSTEPS - 1)
    lane_base = (_iota((G, LANES), 1) >> HEAD_SHIFT) * N

    def steps(tu, carry):
        row0 = pl.multiple_of(tu * RWKV_UNROLL, RWKV_UNROLL)
        for j in range(RWKV_UNROLL):
            t = tu * RWKV_UNROLL + j
            for c in range(RWKV_SEQS):
                for g in range(RWKV_SPLIT):
                    rs = slice(g * G, (g + 1) * G)

                    def rows(ref):
                        return jnp.concatenate(
                            [jnp.broadcast_to(
                                ref[c, pl.ds(row0, RWKV_UNROLL), p * LANES:(p + 1) * LANES][j:j + 1, :], (N, LANES))
                             for p in range(g * G // N, (g + 1) * G // N)], axis=0)

                    s = s_ref[c, rs, :]
                    sa = seg_sum((s * rows(ap_ref)).astype(BF16))
                    vcol = jnp.take_along_axis(vt_ref[c, rs, :], lane_base + t, axis=1)
                    s = (s + vcol * rows(kp_ref)) + sa * rows(bp_ref)
                    s_ref[c, rs, :] = s
                    xr = (s * rows(rp_ref)).astype(BF16)
                    ob = _dot(jnp.concatenate([xr[:R // 2], xr[R // 2:]], axis=1), seg_out)
                    hb = tu // (OSTEPS // RWKV_UNROLL)
                    oacc_ref[c, hb] = jnp.where(lane_copy == (t & (OSTEPS - 1)), ob, oacc_ref[c, hb])
        return carry

    lax.fori_loop(0, RWKV_TB // RWKV_UNROLL, steps, 0)

    for c in range(RWKV_SEQS):
        s_ref[c] = s_ref[c] * jnp.concatenate(
            [jnp.broadcast_to(gl_ref[c, 7:8, p * LANES:(p + 1) * LANES], (N, LANES)) for p in range(RWKV_NPAIR)],
            axis=0)
        for hb in range(RWKV_TB // OSTEPS):
            ot = oacc_ref[c, hb].T
            for gh in range(4):
                g, hh = divmod(gh, 2)
                for q in range(RWKV_NPAIR // 2):
                    lane0 = (g * (RWKV_NPAIR // 2) + q) * LANES + hh * N
                    o_ref[c, hb * OSTEPS:(hb + 1) * OSTEPS, lane0:lane0 + N] = (
                        ot[gh * OSTEPS:(gh + 1) * OSTEPS, q * N:(q + 1) * N])

    @pl.when(tbi == ntb - 1)
    def _():
        for c in range(RWKV_SEQS):
            for p in range(RWKV_NPAIR):
                for hh in range(2):
                    sn_ref[c, (2 * p + hh) * N:(2 * p + hh + 1) * N, :] = (
                        s_ref[c, p * N:(p + 1) * N, hh * N:(hh + 1) * N])


def _rwkv_seq(w, k, a, b, r, v, B, L):
    ntb = L // RWKV_TB
    nc = RWKV_SEQS
    as_seqs = lambda x: x.reshape(B, L, D_MODEL)
    tok = pl.BlockSpec((nc, RWKV_TB, D_MODEL), lambda bi, t: (bi, t, 0))
    state_rows = RWKV_NPAIR * RWKV_HEAD
    o, sn = pl.pallas_call(
        functools.partial(_rwkv_seq_body, ntb=ntb),
        grid=(B // nc, ntb),
        in_specs=[tok] * 6,
        out_specs=[tok, pl.BlockSpec((nc, D_MODEL, RWKV_HEAD), lambda bi, t: (bi, 0, 0))],
        out_shape=[jax.ShapeDtypeStruct((B, L, D_MODEL), F32),
                   jax.ShapeDtypeStruct((B, D_MODEL, RWKV_HEAD), F32)],
        scratch_shapes=[pltpu.VMEM((nc, state_rows, LANES), F32)] * 2
        + [pltpu.VMEM((nc, 2, state_rows // 2, LANES), F32)]
        + [pltpu.VMEM((nc, RWKV_TB, D_MODEL), F32)] * 4
        + [pltpu.VMEM((nc, 8, D_MODEL), F32)],
        compiler_params=_cp(("parallel", "arbitrary")),
        name="rwkv_seq",
    )(as_seqs(w), as_seqs(k), as_seqs(a), as_seqs(b), as_seqs(r), as_seqs(v))
    return o.reshape(B * L, D_MODEL), sn.reshape(B, RWKV_HEADS, RWKV_HEAD, RWKV_HEAD)


def _rwkv_step_body(s_ref, w_ref, k_ref, a_ref, b_ref, r_ref, vc_ref, *rest):
    sn_ref, o_ref = rest[-2:]
    N = RWKV_HEAD
    lane = _iota((D_MODEL, STEP_BATCH), 1)
    ocols = jnp.zeros((D_MODEL, STEP_BATCH), F32)
    vc = vc_ref[...]

    def rows(ref, b):
        return jnp.concatenate(
            [jnp.broadcast_to(ref[b:b + 1, h * N:(h + 1) * N], (N, N)) for h in range(RWKV_HEADS)], axis=0)

    for b in range(STEP_BATCH):
        s = s_ref[b]
        sa = jnp.sum(s * rows(a_ref, b), axis=1, keepdims=True)
        s = s * rows(w_ref, b) + sa * rows(b_ref, b) + vc[:, b:b + 1] * rows(k_ref, b)
        sn_ref[b] = s
        ocol = jnp.sum(s * rows(r_ref, b), axis=1, keepdims=True)
        ocols = jnp.where(lane == b, ocol, ocols)
    o_ref[...] = ocols


def _rwkv_step(state_all, acc, i, w, k, a, b, r, v):
    Bn = state_all.shape[1]
    bb = STEP_BATCH
    col = pl.BlockSpec((None, D_MODEL, bb), lambda t: (t, 0, 0))
    row = pl.BlockSpec((bb, D_MODEL), lambda t: (t, 0))
    acc, ocols = _layer_state_call(
        _rwkv_step_body, "rwkv_step", state_all, acc, i, D_MODEL, RWKV_HEAD,
        [w, k, a, b, r, _to_cols(v)], [row] * 5 + [col],
        col, jax.ShapeDtypeStruct((Bn // bb, D_MODEL, bb), F32))
    return _from_cols(ocols), acc


def _pad_to(w, axis, n):
    pad = [(0, 0)] * w.ndim
    pad[axis] = (0, n - w.shape[axis])
    return jnp.pad(w, pad)


def _prepare(W):
    bf = lambda w: w.astype(BF16)
    row = lambda w: w[:, None, :]
    sizes = [SSD_INNER, SSD_CONV_DIM, SSD_HEADS, GLA_KEY, GLA_KEY, GLA_VAL, GLA_VAL, GLA_GATE_RANK]
    offs = [0]
    for s in sizes:
        offs.append(offs[-1] + s)
    piece = lambda n: W["ev_w_in"][:, :, offs[n]:offs[n + 1]]
    w_in = jnp.concatenate([piece(0), piece(1), piece(3), piece(4), piece(5), piece(6),
                            _pad_to(jnp.concatenate([piece(2), piece(7)], axis=-1), 2, SMALL)], axis=-1)
    gate_w2 = jnp.pad(W["ev_gla_gate_w2"], ((0, 0), (SSD_HEADS, SMALL - SSD_HEADS - GLA_GATE_RANK), (0, 0)))
    even = dict(
        w_in=bf(w_in), w_out=bf(W["ev_w_out"]),
        conv_w=W["ev_conv_w"], conv_b=row(W["ev_conv_b"]),
        dt_bias=row(_pad_to(W["ev_dt_bias"], 1, SMALL)), a_log=row(_pad_to(W["ev_a_log"], 1, SMALL)),
        d_skip_x=row(jnp.repeat(W["ev_d_skip"], SSD_HEAD_DIM, axis=1)),
        ssd_norm=row(W["ev_ssd_norm"]), gate_w2=bf(gate_w2), gate_b=row(W["ev_gla_gate_b"]),
        gla_norm=row(W["ev_gla_norm"]),
    )
    odd = dict(
        mix_norm=row(W["mix_norm"]), mix=_pad_to(W["od_mix"], 1, 8),
        w_r=bf(W["od_w_r"]), w_k=bf(W["od_w_k"]), w_v=bf(W["od_w_v"]), w_o=bf(W["od_w_o"]),
        w0=row(W["od_w0"]), w1=bf(_pad_to(W["od_w1"], 2, LANES)), w2=bf(_pad_to(W["od_w2"], 1, LANES)),
        a0=row(W["od_a0"]), a1=bf(_pad_to(W["od_a1"], 2, LANES)), a2=bf(_pad_to(W["od_a2"], 1, LANES)),
        v0=row(W["od_v0"]), v1=bf(_pad_to(W["od_v1"], 2, LANES)), v2=bf(_pad_to(W["od_v2"], 1, LANES)),
        g1=bf(_pad_to(W["od_g1"], 2, 2 * LANES)), g2=bf(_pad_to(W["od_g2"], 1, 2 * LANES)),
        k_k=row(W["od_k_k"]), k_a=row(W["od_k_a"]), r_k=row(W["od_r_k"]),
        gn_w=row(W["od_gn_w"]), gn_b=row(W["od_gn_b"]),
    )
    ffn = dict(
        norm=W["ffn_norm"].reshape(DEPTH * 2, 1, D_MODEL),
        w_gu=bf(W["ffn_w_gu"]).reshape(DEPTH * 2, D_MODEL, 2 * D_FF),
        w_down=bf(W["ffn_w_down"]).reshape(DEPTH * 2, D_FF, D_MODEL),
    )
    return dict(even=even, odd=odd, ffn=ffn, mix_norm=row(W["mix_norm"]),
                final_norm=W["final_norm"][None, :])


def _run_group(x3, states, Wp):
    B, L, _ = x3.shape
    T = B * L
    step = L == 1
    x = x3.reshape(T, D_MODEL)
    tm_ffn = min(512, T)
    tm_proj = min(256, T)
    ffn, even, odd = Wp["ffn"], Wp["even"], Wp["odd"]
    convs, ssms, glas, shifts, wkvs = [], [], [], [], []
    ssm_acc = gla_acc = wkv_acc = None
    v_first = None
    for layer in range(DEPTH):
        i = layer // 2
        mix = None
        x = _ffn(x, ffn["norm"], ffn["w_gu"], ffn["w_down"], 2 * layer, tm_ffn)
        if layer % 2 == 0:
            z, xbc, q, k, v, og, sm = _even_in(x, Wp["mix_norm"], even["w_in"], i, tm_proj)
            if step:
                xact, convn, dax, c2, ybase, eg = _even_step_pre(
                    xbc, states[0][i].reshape(B, 3 * SSD_CONV_DIM), sm, even, i)
                ssm_acc, yi = _ssd_step(states[1], ssm_acc, i, dax, c2, xact)
                gla_acc, o = _gla_step(states[2], gla_acc, i, eg, k, q, v)
                x = _even_step_post(x, ybase, yi, dax, z, o, og, even, even["w_out"], i)
                convn = convn.reshape(B, SSD_CONV - 1, SSD_CONV_DIM)
            else:
                y, convn, ssmn = _ssd_seq(xbc, sm, z, even, i, B, L)
                o, glan = _gla_seq(q, k, v, og, sm, even, i, B, L)
                mix = (y, o, even["w_out"], i)
                ssms.append(ssmn)
                glas.append(glan)
            convs.append(convn)
        else:
            prev = states[3][i] if step else jnp.zeros((B, 1, D_MODEL), F32)
            h, r, dec, k, v, an, bb, gg = _odd_pre(x, prev, v_first, odd, i, tm_proj, L)
            if v_first is None:
                v_first = v
            if step:
                o, wkv_acc = _rwkv_step(states[4], wkv_acc, i, dec, k, an, bb, r, v)
            else:
                o, wkvn = _rwkv_seq(dec, k, an, bb, r, v, B, L)
                wkvs.append(wkvn)
            x = _odd_post(x, o, r, k, v, gg, odd, i, tm_ffn)
            shifts.append(h.reshape(B, L, D_MODEL)[:, -1])
        x = _ffn(x, ffn["norm"], ffn["w_gu"], ffn["w_down"], 2 * layer + 1, tm_ffn,
                 final_g=Wp["final_norm"] if layer == DEPTH - 1 else None, mix=mix)
    y = x.reshape(B, L, D_MODEL)
    if step:
        ssm_out = ssm_acc.reshape(N_EVEN, B, SSD_HEADS, SSD_HEAD_DIM, SSD_D_STATE)
        gla_out = gla_acc.reshape(N_EVEN, B, GLA_HEADS, GLA_DK, GLA_DV)
        wkv_out = wkv_acc.reshape(N_ODD, B, RWKV_HEADS, RWKV_HEAD, RWKV_HEAD)
    else:
        ssm_out, gla_out, wkv_out = jnp.stack(ssms), jnp.stack(glas), jnp.stack(wkvs)
    return y, jnp.stack(convs), ssm_out, gla_out, jnp.stack(shifts), wkv_out


def kernel(x_prompt, x_sample, state_conv, state_ssm, state_gla, state_shift, state_wkv, ffn_norm, ffn_w_gu, ffn_w_down, mix_norm, final_norm, ev_w_in, ev_conv_w, ev_conv_b, ev_dt_bias, ev_a_log, ev_d_skip, ev_ssd_norm, ev_gla_gate_w2, ev_gla_gate_b, ev_gla_norm, ev_w_out, od_mix, od_w0, od_w1, od_w2, od_a0, od_a1, od_a2, od_v0, od_v1, od_v2, od_g1, od_g2, od_k_k, od_k_a, od_r_k, od_w_r, od_w_k, od_w_v, od_w_o, od_gn_w, od_gn_b):
    W = dict(ffn_norm=ffn_norm, ffn_w_gu=ffn_w_gu, ffn_w_down=ffn_w_down, mix_norm=mix_norm,
             final_norm=final_norm, ev_w_in=ev_w_in, ev_conv_w=ev_conv_w, ev_conv_b=ev_conv_b,
             ev_dt_bias=ev_dt_bias, ev_a_log=ev_a_log, ev_d_skip=ev_d_skip, ev_ssd_norm=ev_ssd_norm,
             ev_gla_gate_w2=ev_gla_gate_w2, ev_gla_gate_b=ev_gla_gate_b, ev_gla_norm=ev_gla_norm,
             ev_w_out=ev_w_out, od_mix=od_mix, od_w0=od_w0, od_w1=od_w1, od_w2=od_w2,
             od_a0=od_a0, od_a1=od_a1, od_a2=od_a2, od_v0=od_v0, od_v1=od_v1, od_v2=od_v2,
             od_g1=od_g1, od_g2=od_g2, od_k_k=od_k_k, od_k_a=od_k_a, od_r_k=od_r_k,
             od_w_r=od_w_r, od_w_k=od_w_k, od_w_v=od_w_v, od_w_o=od_w_o,
             od_gn_w=od_gn_w, od_gn_b=od_gn_b)
    Wp = _prepare(W)
    prompt = _run_group(x_prompt, None, Wp)
    sample = _run_group(x_sample, (state_conv, state_ssm, state_gla, state_shift, state_wkv), Wp)
    return (prompt[0], sample[0]) + prompt[1:] + sample[1:]
```

```python
import functools

import jax
import jax.numpy as jnp
from jax import lax
from jax.experimental import pallas as pl
from jax.experimental.pallas import tpu as pltpu

F32 = jnp.float32
BF16 = jnp.bfloat16

D_MODEL = 1024
DEPTH = 4
N_EVEN = 2
N_ODD = 2
EPS = 1e-5
D_FF = 2816

SSD_HEADS = 16
SSD_HEAD_DIM = 64
SSD_INNER = 1024
SSD_GROUPS = 2
SSD_D_STATE = 128
SSD_CONV = 4
SSD_CONV_DIM = 1536
SSD_CHUNK = 64

GLA_HEADS = 4
GLA_DK = 128
GLA_DV = 256
GLA_KEY = 512
GLA_VAL = 1024
GLA_GATE_RANK = 16
GLA_GATE_NORM = 16.0
GLA_CHUNK = 16

RWKV_HEAD = 64
RWKV_HEADS = 16
RWKV_GN_EPS = 64e-5

HEAD_SHIFT = 6
GLA_CHUNK_SHIFT = 4
assert 1 << HEAD_SHIFT == SSD_HEAD_DIM == RWKV_HEAD == SSD_CHUNK and 1 << GLA_CHUNK_SHIFT == GLA_CHUNK

LANES = 128
SMALL = LANES
IN_PERM = SSD_INNER + SSD_CONV_DIM + 2 * GLA_KEY + 2 * GLA_VAL + SMALL
VMEM_LIMIT = 56 * 1024 * 1024
FF_TILE = 1408
SEQ_BLOCK = 256
STEP_BATCH = 8


def _cp(sem):
    return pltpu.CompilerParams(dimension_semantics=sem, vmem_limit_bytes=VMEM_LIMIT)


def _iota(shape, axis):
    return lax.broadcasted_iota(jnp.int32, shape, axis)


def _onehot(mask):
    return jnp.where(mask, 1.0, 0.0).astype(BF16)


def _dot(a, b):
    return jnp.dot(a, b, preferred_element_type=F32)


def _split3(x):
    hi = x.astype(BF16)
    r = x - hi.astype(F32)
    mid = r.astype(BF16)
    lo = (r - mid.astype(F32)).astype(BF16)
    return hi, mid, lo


def _dot01(x, e):
    hi, mid, lo = _split3(x)
    return _dot(hi, e) + _dot(mid, e) + _dot(lo, e)


def _dot01_l(e, x):
    hi, mid, lo = _split3(x)
    return _dot(e, hi) + _dot(e, mid) + _dot(e, lo)


def _rms(x, g):
    ms = jnp.mean(x * x, axis=-1, keepdims=True)
    return x * lax.rsqrt(ms + EPS) * g


def _silu(x):
    return x * jax.nn.sigmoid(x)


def _softplus(x):
    return jnp.maximum(x, 0.0) + jnp.log1p(jnp.exp(-jnp.abs(x)))


def _softplus_abs(x):
    return jnp.maximum(x, 0.0) + jnp.log(1.0 + jnp.exp(-jnp.abs(x)))


def _head_expand():
    return _onehot((_iota((LANES, D_MODEL), 1) >> HEAD_SHIFT) == _iota((LANES, D_MODEL), 0))


def _head_reduce():
    return _onehot((_iota((D_MODEL, LANES), 0) >> HEAD_SHIFT) == _iota((D_MODEL, LANES), 1))


def _ffn_body(x_ref, g_ref, wgu_ref, wd_ref, *rest, mix_out, final_norm):
    o_ref = rest[-1]
    x = x_ref[...]
    if mix_out:
        y_ref, mo_ref, wo_ref = rest[:3]
        x = x + _dot(y_ref[...].astype(BF16), wo_ref[0:SSD_INNER, :])
        x = x + _dot(mo_ref[...].astype(BF16), wo_ref[SSD_INNER:, :])
    xn = _rms(x, g_ref[...]).astype(BF16)
    acc = None
    for j in range(D_FF // FF_TILE):
        lo = j * FF_TILE
        gate = _dot(xn, wgu_ref[:, lo:lo + FF_TILE])
        up = _dot(xn, wgu_ref[:, D_FF + lo:D_FF + lo + FF_TILE])
        part = _dot((_silu(gate) * up).astype(BF16), wd_ref[lo:lo + FF_TILE, :])
        acc = part if acc is None else acc + part
    y = x + 0.5 * acc
    o_ref[...] = _rms(y, rest[-2][...]) if final_norm else y


def _ffn(x, nrm, wgu, wd, ls, tm, final_g=None, mix=None):
    T = x.shape[0]
    resident = pl.Buffered(1)
    tok = pl.BlockSpec((tm, D_MODEL), lambda i: (i, 0))
    args = [x, nrm, wgu, wd]
    specs = [
        tok,
        pl.BlockSpec((None, 1, D_MODEL), lambda i: (ls, 0, 0)),
        pl.BlockSpec((None, D_MODEL, 2 * D_FF), lambda i: (ls, 0, 0), pipeline_mode=resident),
        pl.BlockSpec((None, D_FF, D_MODEL), lambda i: (ls, 0, 0), pipeline_mode=resident),
    ]
    if mix is not None:
        y, o, w_out, li = mix
        args += [y, o, w_out]
        specs += [tok, tok, pl.BlockSpec((None, SSD_INNER + GLA_VAL, D_MODEL), lambda i: (li, 0, 0),
                                         pipeline_mode=resident)]
    if final_g is not None:
        args.append(final_g)
        specs.append(pl.BlockSpec((1, D_MODEL), lambda i: (0, 0)))
    return pl.pallas_call(
        functools.partial(_ffn_body, mix_out=mix is not None, final_norm=final_g is not None),
        grid=(T // tm,),
        in_specs=specs,
        out_specs=pl.BlockSpec((tm, D_MODEL), lambda i: (i, 0)),
        out_shape=jax.ShapeDtypeStruct((T, D_MODEL), F32),
        compiler_params=_cp(("parallel",)),
        name="ffn",
    )(*args)


_EVEN_PIECES = (SSD_INNER, SSD_CONV_DIM, GLA_KEY, GLA_KEY, GLA_VAL, GLA_VAL, SMALL)


def _even_in_body(x_ref, g_ref, w_ref, *out_refs):
    h = _rms(x_ref[...], g_ref[...]).astype(BF16)
    off = 0
    for ref, n in zip(out_refs, _EVEN_PIECES):
        ref[...] = _dot(h, w_ref[:, off:off + n])
        off += n


def _even_in(x, nrm, w, i, tm):
    T = x.shape[0]
    return pl.pallas_call(
        _even_in_body,
        grid=(T // tm,),
        in_specs=[
            pl.BlockSpec((tm, D_MODEL), lambda t: (t, 0)),
            pl.BlockSpec((None, 1, D_MODEL), lambda t: (2 * i, 0, 0)),
            pl.BlockSpec((None, D_MODEL, IN_PERM), lambda t: (i, 0, 0), pipeline_mode=pl.Buffered(1)),
        ],
        out_specs=[pl.BlockSpec((tm, n), lambda t: (t, 0)) for n in _EVEN_PIECES],
        out_shape=[jax.ShapeDtypeStruct((T, n), F32) for n in _EVEN_PIECES],
        compiler_params=_cp(("parallel",)),
        name="even_in",
    )(x, nrm, w)


def _ssd_seq_body(xbc_ref, sm_ref, z_ref, cw_ref, cb_ref, dtb_ref, alog_ref, dsk_ref, nw_ref,
                  y_ref, convn_ref, ssmn_ref,
                  cbuf_ref, xact_ref, acx_ref, dtx_ref, hT_ref, *, nblk):
    Lb = SEQ_BLOCK
    C = SSD_CHUNK
    j = pl.program_id(1)

    @pl.when(j == 0)
    def _():
        cbuf_ref[0:8, :] = jnp.zeros((8, SSD_CONV_DIM), F32)
        hT_ref[...] = jnp.zeros_like(hT_ref)

    x = xbc_ref[...]
    acc = cb_ref[...] + pltpu.roll(x, 3, axis=0) * cw_ref[0:1, :]
    acc = acc + pltpu.roll(x, 2, axis=0) * cw_ref[1:2, :]
    acc = acc + pltpu.roll(x, 1, axis=0) * cw_ref[2:3, :]
    acc = acc + x * cw_ref[3:4, :]
    xact_ref[...] = _silu(acc)
    cbuf_ref[8:16, :] = x[0:8, :]
    head = cb_ref[...] + cbuf_ref[5:13, :] * cw_ref[0:1, :]
    head = head + cbuf_ref[6:14, :] * cw_ref[1:2, :]
    head = head + cbuf_ref[7:15, :] * cw_ref[2:3, :]
    head = head + cbuf_ref[8:16, :] * cw_ref[3:4, :]
    xact_ref[0:8, :] = _silu(head)
    tail = x[Lb - 8:Lb, :]
    cbuf_ref[0:8, :] = tail

    @pl.when(j == nblk - 1)
    def _():
        convn_ref[...] = tail[5:8, :]

    dt = _softplus(sm_ref[...] + dtb_ref[...])
    dta = dt * (-jnp.exp(alog_ref[...]))
    ii = _iota((Lb, Lb), 0)
    jj = _iota((Lb, Lb), 1)
    tril = _onehot(((ii >> HEAD_SHIFT) == (jj >> HEAD_SHIFT)) & (ii >= jj))
    acum = _dot01_l(tril, dta)
    expand = _head_expand()
    acx_ref[...] = _dot01(acum, expand)
    dtx_ref[...] = _dot01(dt, expand)

    li = _iota((C, D_MODEL), 0)
    lj = _iota((C, D_MODEL), 1) & (C - 1)
    diag = li == lj
    causal = li >= lj
    ones_c = jnp.ones((C, C), BF16)
    ones_c2 = jnp.ones((C, 2 * C), BF16)
    tile8 = _onehot((_iota((2 * C, 8 * C), 1) & (C - 1)) == (_iota((2 * C, 8 * C), 0) & (C - 1)))
    pair_mask = (_iota((LANES, LANES), 0) >> HEAD_SHIFT) == (_iota((LANES, LANES), 1) >> HEAD_SHIFT)
    half = SSD_INNER // SSD_GROUPS

    def chunk(c, carry):
        r0 = pl.multiple_of(c * C, C)
        xa = xact_ref[pl.ds(r0, C), :]
        xs = xa[:, :SSD_INNER]
        bms = [xa[:, SSD_INNER + g * SSD_D_STATE:SSD_INNER + (g + 1) * SSD_D_STATE] for g in range(2)]
        cms = [xa[:, SSD_INNER + (2 + g) * SSD_D_STATE:SSD_INNER + (3 + g) * SSD_D_STATE]
               for g in range(2)]
        acx = acx_ref[pl.ds(r0, C), :]
        dtx = dtx_ref[pl.ds(r0, C), :]
        alast = acx[C - 1:C, :]
        arow = _dot01_l(ones_c, jnp.where(diag, acx, 0.0))
        dsel = jnp.where(diag, dtx, 0.0)
        dhi = dsel.astype(BF16)
        dmid = (dsel - dhi.astype(F32)).astype(BF16)
        dtrow = _dot(ones_c2, jnp.concatenate([dhi, dmid], axis=0))
        decay = jnp.exp(jnp.where(causal, acx - arow, -jnp.inf))
        cbt = []
        for g in range(2):
            cb = lax.dot_general(cms[g].astype(BF16), bms[g].astype(BF16),
                                 (((1,), (1,)), ((), ())), preferred_element_type=F32)
            cbt.append(_dot01x2(cb, tile8))
        wts = (jnp.concatenate(cbt, axis=1) * decay * dtrow).astype(BF16)
        ys = []
        for p in range(SSD_HEADS // 2):
            xp = xs[:, p * LANES:(p + 1) * LANES]
            xbd = jnp.where(pair_mask, jnp.concatenate([xp, xp], axis=0), 0.0).astype(BF16)
            ys.append(_dot(wts[:, p * LANES:(p + 1) * LANES], xbd))
        y = jnp.concatenate(ys, axis=1)
        hT = hT_ref[...]
        hTb = hT.astype(BF16)
        yi = [_dot(cms[g].astype(BF16), hTb[:, g * half:(g + 1) * half]) for g in range(2)]
        y = y + jnp.concatenate(yi, axis=1) * jnp.exp(acx)
        xsc = (jnp.exp(alast - acx) * dtx * xs).astype(BF16)
        st = [lax.dot_general(bms[g].astype(BF16), xsc[:, g * half:(g + 1) * half],
                              (((0,), (0,)), ((), ())), preferred_element_type=F32) for g in range(2)]
        hT_ref[...] = hT * jnp.exp(alast) + jnp.concatenate(st, axis=1)
        y = y + dsk_ref[...] * xs
        y = y * _silu(z_ref[pl.ds(r0, C), :])
        outs = []
        for g in range(2):
            yg = y[:, g * half:(g + 1) * half]
            ms = jnp.mean(yg * yg, axis=-1, keepdims=True)
            outs.append(yg * lax.rsqrt(ms + EPS))
        y_ref[pl.ds(r0, C), :] = jnp.concatenate(outs, axis=1) * nw_ref[...]
        return carry

    lax.fori_loop(0, Lb // C, chunk, 0, unroll=4)

    @pl.when(j == nblk - 1)
    def _():
        ssmn_ref[...] = hT_ref[...].T


def _ssd_seq(xbc, sm, z, P, i, B, L):
    Lb = SEQ_BLOCK
    nblk = L // Lb
    T = B * L
    tok = lambda n: pl.BlockSpec((Lb, n), lambda b, j: (b * nblk + j, 0))
    par = lambda r, n: pl.BlockSpec((None, r, n), lambda b, j: (i, 0, 0))
    y, convn, ssmn = pl.pallas_call(
        functools.partial(_ssd_seq_body, nblk=nblk),
        grid=(B, nblk),
        in_specs=[tok(SSD_CONV_DIM), tok(SMALL), tok(SSD_INNER),
                  par(SSD_CONV, SSD_CONV_DIM), par(1, SSD_CONV_DIM), par(1, SMALL), par(1, SMALL),
                  par(1, SSD_INNER), par(1, SSD_INNER)],
        out_specs=[tok(SSD_INNER),
                   pl.BlockSpec((None, SSD_CONV - 1, SSD_CONV_DIM), lambda b, j: (b, 0, 0)),
                   pl.BlockSpec((None, SSD_INNER, SSD_D_STATE), lambda b, j: (b, 0, 0))],
        out_shape=[jax.ShapeDtypeStruct((T, SSD_INNER), F32),
                   jax.ShapeDtypeStruct((B, SSD_CONV - 1, SSD_CONV_DIM), F32),
                   jax.ShapeDtypeStruct((B, SSD_INNER, SSD_D_STATE), F32)],
        scratch_shapes=[pltpu.VMEM((16, SSD_CONV_DIM), F32),
                        pltpu.VMEM((Lb, SSD_CONV_DIM), F32),
                        pltpu.VMEM((Lb, SSD_INNER), F32),
                        pltpu.VMEM((Lb, SSD_INNER), F32),
                        pltpu.VMEM((SSD_D_STATE, SSD_INNER), F32)],
        compiler_params=_cp(("parallel", "arbitrary")),
        name="ssd_seq",
    )(xbc, sm, z, P["conv_w"], P["conv_b"], P["dt_bias"], P["a_log"], P["d_skip_x"], P["ssd_norm"])
    return y, convn, ssmn.reshape(B, SSD_HEADS, SSD_HEAD_DIM, SSD_D_STATE)


def _gla_seq_body(q_ref, k_ref, v_ref, og_ref, sm_ref, w2_ref, gb_ref, nw_ref,
                  o_ref, glan_ref,
                  st_ref, dec_ref, qg_ref, kd_ref, oi_ref, *, nblk):
    Lb = SEQ_BLOCK
    j = pl.program_id(1)

    @pl.when(j == 0)
    def _():
        st_ref[...] = jnp.zeros_like(st_ref)

    gate = _dot(sm_ref[...].astype(BF16), w2_ref[...]) + gb_ref[...]
    lg = -_softplus_abs(-gate) / GLA_GATE_NORM
    ii = _iota((Lb, Lb), 0)
    jj = _iota((Lb, Lb), 1)
    same = (ii >> GLA_CHUNK_SHIFT) == (jj >> GLA_CHUNK_SHIFT)
    causal = same & (ii >= jj)
    lg_hi = lg.astype(BF16)
    lg2 = jnp.concatenate([lg_hi, (lg - lg_hi.astype(F32)).astype(BF16)], axis=0)
    sel_c = _onehot(causal)
    sel_s = _onehot(same)
    b = _dot(jnp.concatenate([sel_c, sel_c], axis=1), lg2)
    blast = _dot(jnp.concatenate([sel_s, sel_s], axis=1), lg2)
    qg = q_ref[...] * (GLA_DK ** -0.5) * jnp.exp(b)
    kk = k_ref[...]
    kg = kk * jnp.exp(-b)
    kd = kk * jnp.exp(blast - b)
    dec_ref[...] = jnp.exp(blast)
    qg_ref[...] = qg.astype(BF16)
    kd_ref[...] = kd.astype(BF16)

    def chunk(c, carry):
        r0 = pl.multiple_of(c * GLA_CHUNK, GLA_CHUNK)
        for h in range(GLA_HEADS):
            ks = slice(h * GLA_DK, (h + 1) * GLA_DK)
            vs = slice(h * GLA_DV, (h + 1) * GLA_DV)
            st = st_ref[h]
            oi_ref[pl.ds(r0, GLA_CHUNK), vs] = lax.dot_general(
                qg_ref[pl.ds(r0, GLA_CHUNK), ks], st.astype(BF16),
                (((1,), (1,)), ((), ())), preferred_element_type=F32)
            upd = lax.dot_general(v_ref[pl.ds(r0, GLA_CHUNK), vs].astype(BF16),
                                  kd_ref[pl.ds(r0, GLA_CHUNK), ks],
                                  (((0,), (0,)), ((), ())), preferred_element_type=F32)
            st_ref[h] = st * dec_ref[pl.ds(r0, 8), ks][0:1, :] + upd
        return carry

    lax.fori_loop(0, Lb // GLA_CHUNK, chunk, 0, unroll=16)

    for h in range(GLA_HEADS):
        ks = slice(h * GLA_DK, (h + 1) * GLA_DK)
        vs = slice(h * GLA_DV, (h + 1) * GLA_DV)
        att = lax.dot_general(qg[:, ks].astype(BF16), kg[:, ks].astype(BF16),
                              (((1,), (1,)), ((), ())), preferred_element_type=F32)
        att = jnp.where(causal, att, 0.0)
        o_h = _dot(att.astype(BF16), v_ref[:, vs].astype(BF16)) + oi_ref[:, vs]
        ms = jnp.mean(o_h * o_h, axis=-1, keepdims=True)
        o_h = o_h * lax.rsqrt(ms + EPS) * nw_ref[...]
        o_ref[:, vs] = o_h * _silu(og_ref[:, vs])

    @pl.when(j == nblk - 1)
    def _():
        for h in range(GLA_HEADS):
            glan_ref[h] = st_ref[h].T


def _gla_seq(q, k, v, og, sm, P, i, B, L):
    Lb = SEQ_BLOCK
    nblk = L // Lb
    T = B * L
    tok = lambda n: pl.BlockSpec((Lb, n), lambda b, j: (b * nblk + j, 0))
    par = lambda r, n: pl.BlockSpec((None, r, n), lambda b, j: (i, 0, 0))
    return pl.pallas_call(
        functools.partial(_gla_seq_body, nblk=nblk),
        grid=(B, nblk),
        in_specs=[tok(GLA_KEY), tok(GLA_KEY), tok(GLA_VAL), tok(GLA_VAL), tok(SMALL),
                  par(SMALL, GLA_KEY), par(1, GLA_KEY), par(1, GLA_DV)],
        out_specs=[tok(GLA_VAL),
                   pl.BlockSpec((None, GLA_HEADS, GLA_DK, GLA_DV), lambda b, j: (b, 0, 0, 0))],
        out_shape=[jax.ShapeDtypeStruct((T, GLA_VAL), F32),
                   jax.ShapeDtypeStruct((B, GLA_HEADS, GLA_DK, GLA_DV), F32)],
        scratch_shapes=[pltpu.VMEM((GLA_HEADS, GLA_DV, GLA_DK), F32),
                        pltpu.VMEM((Lb, GLA_KEY), F32),
                        pltpu.VMEM((Lb, GLA_KEY), BF16),
                        pltpu.VMEM((Lb, GLA_KEY), BF16),
                        pltpu.VMEM((Lb, GLA_VAL), F32)],
        compiler_params=_cp(("parallel", "arbitrary")),
        name="gla_seq",
    )(q, k, v, og, sm, P["gate_w2"], P["gate_b"], P["gla_norm"])


def _even_step_pre_body(xbc_ref, c0_ref, sm_ref, cw_ref, cb_ref, dtb_ref, alog_ref, dsk_ref,
                        w2_ref, gb_ref,
                        xact_ref, convn_ref, dax_ref, c2_ref, ybase_ref, eg_ref):
    u = xbc_ref[...]
    n = SSD_CONV_DIM
    acc = cb_ref[...] + c0_ref[:, 0:n] * cw_ref[0:1, :]
    acc = acc + c0_ref[:, n:2 * n] * cw_ref[1:2, :]
    acc = acc + c0_ref[:, 2 * n:3 * n] * cw_ref[2:3, :]
    acc = acc + u * cw_ref[3:4, :]
    xa = _silu(acc)
    xact_ref[...] = xa
    convn_ref[:, 0:n] = c0_ref[:, n:2 * n]
    convn_ref[:, n:2 * n] = c0_ref[:, 2 * n:3 * n]
    convn_ref[:, 2 * n:3 * n] = u
    dt = _softplus(sm_ref[...] + dtb_ref[...])
    dta = dt * (-jnp.exp(alog_ref[...]))
    expand = _head_expand()
    dtx = _dot01(dt, expand)
    dax_ref[...] = jnp.exp(_dot01(dta, expand))
    xs = xa[:, :SSD_INNER]
    c2_ref[...] = dtx * xs
    half = SSD_INNER // SSD_GROUPS
    cbs = []
    for g in range(2):
        bm = xa[:, SSD_INNER + g * SSD_D_STATE:SSD_INNER + (g + 1) * SSD_D_STATE]
        cm = xa[:, SSD_INNER + (2 + g) * SSD_D_STATE:SSD_INNER + (3 + g) * SSD_D_STATE]
        cb = jnp.sum(cm * bm, axis=-1, keepdims=True)
        cbs.append(jnp.broadcast_to(cb, (cb.shape[0], half)))
    ybase_ref[...] = jnp.concatenate(cbs, axis=1) * dtx * xs + dsk_ref[...] * xs
    gate = _dot(sm_ref[...].astype(BF16), w2_ref[...]) + gb_ref[...]
    eg_ref[...] = jnp.exp(-_softplus(-gate) / GLA_GATE_NORM)


def _even_step_pre(xbc, conv0, sm, P, i):
    Bn = xbc.shape[0]
    full = lambda n: pl.BlockSpec((Bn, n), lambda t: (0, 0))
    par = lambda r, n: pl.BlockSpec((None, r, n), lambda t: (i, 0, 0))
    outs = (SSD_CONV_DIM, 3 * SSD_CONV_DIM, SSD_INNER, SSD_INNER, SSD_INNER, GLA_KEY)
    return pl.pallas_call(
        _even_step_pre_body,
        grid=(1,),
        in_specs=[full(SSD_CONV_DIM), full(3 * SSD_CONV_DIM), full(SMALL),
                  par(SSD_CONV, SSD_CONV_DIM), par(1, SSD_CONV_DIM), par(1, SMALL), par(1, SMALL),
                  par(1, SSD_INNER), par(SMALL, GLA_KEY), par(1, GLA_KEY)],
        out_specs=[full(n) for n in outs],
        out_shape=[jax.ShapeDtypeStruct((Bn, n), F32) for n in outs],
        compiler_params=_cp(("arbitrary",)),
        name="even_step_pre",
    )(xbc, conv0, sm, P["conv_w"], P["conv_b"], P["dt_bias"], P["a_log"], P["d_skip_x"],
      P["gate_w2"], P["gate_b"])


def _to_cols(x):
    Bn, N = x.shape
    return x.reshape(Bn // STEP_BATCH, STEP_BATCH, N).transpose(0, 2, 1)


def _from_cols(x):
    nb, N, bb = x.shape
    return x.transpose(0, 2, 1).reshape(nb * bb, N)


def _layer_state_call(body, name, state_all, acc, i, rows, width, args, specs, out_spec, out_shape):
    nl, Bn = state_all.shape[:2]
    st = pl.BlockSpec((None, STEP_BATCH, rows, width), lambda t: (i, t, 0, 0))
    args = [state_all.reshape(nl, Bn, rows, width)] + list(args)
    specs = [st] + list(specs)
    aliases = {}
    if acc is not None:
        aliases = {len(args): 0}
        args.append(acc)
        specs.append(pl.BlockSpec(memory_space=pl.ANY))
    return pl.pallas_call(
        body,
        grid=(Bn // STEP_BATCH,),
        in_specs=specs,
        out_specs=[st, out_spec],
        out_shape=[jax.ShapeDtypeStruct((nl, Bn, rows, width), F32), out_shape],
        input_output_aliases=aliases,
        compiler_params=_cp(("parallel",)),
        name=name,
    )(*args)


def _ssd_step_body(h_ref, c1_ref, c2_ref, bm_ref, cm_ref, *rest):
    hn_ref, y_ref = rest[-2:]
    half = SSD_INNER // SSD_GROUPS
    lane = _iota((SSD_INNER, STEP_BATCH), 1)
    ycols = jnp.zeros((SSD_INNER, STEP_BATCH), F32)
    c1 = c1_ref[...]
    c2 = c2_ref[...]
    for b in range(STEP_BATCH):
        h0 = h_ref[b]
        bmat = jnp.concatenate(
            [jnp.broadcast_to(bm_ref[b:b + 1, g * SSD_D_STATE:(g + 1) * SSD_D_STATE],
                              (half, SSD_D_STATE)) for g in range(2)], axis=0)
        cmat = jnp.concatenate(
            [jnp.broadcast_to(cm_ref[b:b + 1, g * SSD_D_STATE:(g + 1) * SSD_D_STATE],
                              (half, SSD_D_STATE)) for g in range(2)], axis=0)
        ycol = jnp.sum(h0 * cmat, axis=1, keepdims=True)
        ycols = jnp.where(lane == b, ycol, ycols)
        hn_ref[b] = h0 * c1[:, b:b + 1] + c2[:, b:b + 1] * bmat
    y_ref[...] = ycols


def _ssd_step(state_all, acc, i, dax, c2, xact):
    Bn = state_all.shape[1]
    bb = STEP_BATCH
    bm = xact[:, SSD_INNER:SSD_INNER + 2 * SSD_D_STATE]
    cm = xact[:, SSD_INNER + 2 * SSD_D_STATE:]
    col = pl.BlockSpec((None, SSD_INNER, bb), lambda t: (t, 0, 0))
    row = pl.BlockSpec((bb, 2 * SSD_D_STATE), lambda t: (t, 0))
    acc, ycols = _layer_state_call(
        _ssd_step_body, "ssd_step", state_all, acc, i, SSD_INNER, SSD_D_STATE,
        [_to_cols(dax), _to_cols(c2), bm, cm], [col, col, row, row],
        col, jax.ShapeDtypeStruct((Bn // bb, SSD_INNER, bb), F32))
    return acc, _from_cols(ycols)


def _gla_step_body(s_ref, eg_ref, kc_ref, qc_ref, v_ref, *rest):
    sn_ref, o_ref = rest[-2:]
    eg = eg_ref[...]
    kc = kc_ref[...]
    qc = qc_ref[...] * (GLA_DK ** -0.5)
    for b in range(STEP_BATCH):
        s0 = s_ref[b]
        vmat = jnp.concatenate(
            [jnp.broadcast_to(v_ref[b:b + 1, h * GLA_DV:(h + 1) * GLA_DV], (GLA_DK, GLA_DV))
             for h in range(GLA_HEADS)], axis=0)
        sn = s0 * eg[:, b:b + 1] + kc[:, b:b + 1] * vmat
        sn_ref[b] = sn
        t = sn * qc[:, b:b + 1]
        o_ref[b:b + 1, :] = jnp.concatenate(
            [jnp.sum(t[h * GLA_DK:(h + 1) * GLA_DK], axis=0, keepdims=True)
             for h in range(GLA_HEADS)], axis=1)


def _gla_step(state_all, acc, i, eg, k, q, v):
    Bn = state_all.shape[1]
    bb = STEP_BATCH
    col = pl.BlockSpec((None, GLA_KEY, bb), lambda t: (t, 0, 0))
    row = pl.BlockSpec((bb, GLA_VAL), lambda t: (t, 0))
    return _layer_state_call(
        _gla_step_body, "gla_step", state_all, acc, i, GLA_KEY, GLA_DV,
        [_to_cols(eg), _to_cols(k), _to_cols(q), v], [col, col, col, row],
        row, jax.ShapeDtypeStruct((Bn, GLA_VAL), F32))


def _even_step_post_body(x_ref, ybase_ref, yi_ref, dax_ref, z_ref, o_ref, og_ref,
                         snw_ref, gnw_ref, w_ref, out_ref):
    y = (ybase_ref[...] + yi_ref[...] * dax_ref[...]) * _silu(z_ref[...])
    half = SSD_INNER // SSD_GROUPS
    acc = None
    for g in range(2):
        yg = y[:, g * half:(g + 1) * half]
        ms = jnp.mean(yg * yg, axis=-1, keepdims=True)
        yn = yg * lax.rsqrt(ms + EPS) * snw_ref[:, g * half:(g + 1) * half]
        part = _dot(yn.astype(BF16), w_ref[g * half:(g + 1) * half, :])
        acc = part if acc is None else acc + part
    for h in range(GLA_HEADS):
        vs = slice(h * GLA_DV, (h + 1) * GLA_DV)
        oh = o_ref[:, vs]
        ms = jnp.mean(oh * oh, axis=-1, keepdims=True)
        on = oh * lax.rsqrt(ms + EPS) * gnw_ref[...] * _silu(og_ref[:, vs])
        acc = acc + _dot(on.astype(BF16), w_ref[SSD_INNER + h * GLA_DV:SSD_INNER + (h + 1) * GLA_DV, :])
    out_ref[...] = x_ref[...] + acc


def _even_step_post(x, ybase, yi, dax, z, o, og, P, w_out, i):
    Bn = x.shape[0]
    full = pl.BlockSpec((Bn, D_MODEL), lambda t: (0, 0))
    return pl.pallas_call(
        _even_step_post_body,
        grid=(1,),
        in_specs=[full] * 7 + [pl.BlockSpec((None, 1, SSD_INNER), lambda t: (i, 0, 0)),
                               pl.BlockSpec((None, 1, GLA_DV), lambda t: (i, 0, 0)),
                               pl.BlockSpec((None, SSD_INNER + GLA_VAL, D_MODEL), lambda t: (i, 0, 0))],
        out_specs=full,
        out_shape=jax.ShapeDtypeStruct((Bn, D_MODEL), F32),
        compiler_params=_cp(("arbitrary",)),
        name="even_step_post",
    )(x, ybase, yi, dax, z, o, og, P["ssd_norm"], P["gla_norm"], w_out)


_ODD_OUTS = 8


def _odd_pre_body(*refs, step, has_vfirst, nblk_seq, tm):
    it = iter(refs)
    x_ref = next(it)
    if step:
        prev_ref = next(it)
    else:
        xp_ref = next(it)
        sh0_ref = next(it)
    g_ref, mix_ref = next(it), next(it)
    wr_ref, wk_ref, wv_ref = next(it), next(it), next(it)
    w0_ref, w1_ref, w2_ref = next(it), next(it), next(it)
    a0_ref, a1_ref, a2_ref = next(it), next(it), next(it)
    if has_vfirst:
        vf_ref, v0_ref, v1_ref, v2_ref = next(it), next(it), next(it), next(it)
    g1_ref, g2_ref = next(it), next(it)
    kkw_ref, kaw_ref = next(it), next(it)
    h_ref, r_ref, dec_ref, k_ref, v_ref, an_ref, b_ref, gg_ref = (next(it) for _ in range(_ODD_OUTS))

    h = _rms(x_ref[...], g_ref[...])
    if step:
        prev = prev_ref[...]
    else:
        i = pl.program_id(0)
        hp = _rms(xp_ref[...], g_ref[...])[7:8, :]
        prow = jnp.where(i % nblk_seq == 0, sh0_ref[...], hp)
        prev = jnp.where(_iota((tm, D_MODEL), 0) == 0, prow, pltpu.roll(h, 1, axis=0))
    h_ref[...] = h
    xx = prev - h
    mixed = lambda r: (h + xx * mix_ref[r:r + 1, :]).astype(BF16)
    xr, xw, xk, xv, xa, xg = (mixed(r) for r in range(6))
    r_ref[...] = _dot(xr, wr_ref[...])
    wl = w0_ref[...] + _dot(jnp.tanh(_dot(xw, w1_ref[...])).astype(BF16), w2_ref[...])
    w = -_softplus_abs(-wl) - 0.5
    dec_ref[...] = jnp.exp(-jnp.exp(w))
    k = _dot(xk, wk_ref[...])
    v = _dot(xv, wv_ref[...])
    if has_vfirst:
        gate = jax.nn.sigmoid(v0_ref[...] + _dot(_dot(xv, v1_ref[...]).astype(BF16), v2_ref[...]))
        v = v + (vf_ref[...] - v) * gate
    v_ref[...] = v
    a = jax.nn.sigmoid(a0_ref[...] + _dot(_dot(xa, a1_ref[...]).astype(BF16), a2_ref[...]))
    gg_ref[...] = _dot(jax.nn.sigmoid(_dot(xg, g1_ref[...])).astype(BF16), g2_ref[...])
    kkf = k * kkw_ref[...]
    red = _head_reduce()
    expand = _head_expand()
    ss = _dot01x2(kkf * kkf, jnp.concatenate([red, red], axis=0))
    kk = kkf * _dot01x2(lax.rsqrt(jnp.maximum(ss, 1e-24)), jnp.concatenate([expand, expand], axis=0))
    k_ref[...] = k * (1.0 + (a - 1.0) * kaw_ref[...])
    an_ref[...] = -kk
    b_ref[...] = kk * a


def _odd_pre(x, prev_or_shift0, vfirst, P, i, tm, L):
    T = x.shape[0]
    step = L == 1
    has_vfirst = vfirst is not None
    nblk_seq = max(L // tm, 1)
    tok = pl.BlockSpec((tm, D_MODEL), lambda t: (t, 0))
    par = lambda r, n, idx=i: pl.BlockSpec((None, r, n), lambda t: (idx, 0, 0), pipeline_mode=pl.Buffered(1))
    args = [x]
    specs = [tok]
    if step:
        args.append(prev_or_shift0)
        specs.append(tok)
    else:
        args += [x, prev_or_shift0]
        specs += [pl.BlockSpec((8, D_MODEL), lambda t: (jnp.maximum(t * (tm // 8) - 1, 0), 0)),
                  pl.BlockSpec((None, 1, D_MODEL), lambda t: (t // nblk_seq, 0, 0))]
    args += [P["mix_norm"], P["mix"], P["w_r"], P["w_k"], P["w_v"], P["w0"], P["w1"], P["w2"],
             P["a0"], P["a1"], P["a2"]]
    specs += [pl.BlockSpec((None, 1, D_MODEL), lambda t: (2 * i + 1, 0, 0)),
              par(8, D_MODEL), par(D_MODEL, D_MODEL), par(D_MODEL, D_MODEL), par(D_MODEL, D_MODEL),
              par(1, D_MODEL), par(D_MODEL, LANES), par(LANES, D_MODEL),
              par(1, D_MODEL), par(D_MODEL, LANES), par(LANES, D_MODEL)]
    if has_vfirst:
        args += [vfirst, P["v0"], P["v1"], P["v2"]]
        specs += [tok, par(1, D_MODEL, i - 1), par(D_MODEL, LANES, i - 1), par(LANES, D_MODEL, i - 1)]
    args += [P["g1"], P["g2"], P["k_k"], P["k_a"]]
    specs += [par(D_MODEL, 2 * LANES), par(2 * LANES, D_MODEL), par(1, D_MODEL), par(1, D_MODEL)]
    return pl.pallas_call(
        functools.partial(_odd_pre_body, step=step, has_vfirst=has_vfirst, nblk_seq=nblk_seq, tm=tm),
        grid=(T // tm,),
        in_specs=specs,
        out_specs=[tok] * _ODD_OUTS,
        out_shape=[jax.ShapeDtypeStruct((T, D_MODEL), F32)] * _ODD_OUTS,
        compiler_params=_cp(("parallel",)),
        name="odd_pre",
    )(*args)


def _dot01x2(x, e2):
    hi = x.astype(BF16)
    mid = (x - hi.astype(F32)).astype(BF16)
    return _dot(jnp.concatenate([hi, mid], axis=1), e2)


def _odd_post_body(x_ref, o_ref, r_ref, k_ref, v_ref, g_ref, gnw_ref, gnb_ref, rk_ref, wo_ref, out_ref):
    red = _head_reduce()
    red = jnp.concatenate([red, red], axis=0)
    expand = _head_expand()
    expand = jnp.concatenate([expand, expand], axis=0)
    for c0 in range(0, x_ref.shape[0], LANES):
        rows = slice(c0, c0 + LANES)
        o = o_ref[rows, :]
        mu = _dot01x2(_dot01x2(o, red) * (1.0 / RWKV_HEAD), expand)
        d = o - mu
        var = _dot01x2(d * d, red) * (1.0 / RWKV_HEAD)
        on = d * _dot01x2(lax.rsqrt(var + RWKV_GN_EPS), expand) * gnw_ref[...] + gnb_ref[...]
        bonus = _dot01x2(_dot01x2(r_ref[rows, :] * k_ref[rows, :] * rk_ref[...], red), expand) * v_ref[rows, :]
        out_ref[rows, :] = x_ref[rows, :] + _dot(((on + bonus) * g_ref[rows, :]).astype(BF16), wo_ref[...])


def _odd_post(x, o, r, k, v, g, P, i, tm):
    T = x.shape[0]
    tok = pl.BlockSpec((tm, D_MODEL), lambda t: (t, 0))
    par = lambda r_, n: pl.BlockSpec((None, r_, n), lambda t: (i, 0, 0))
    return pl.pallas_call(
        _odd_post_body,
        grid=(T // tm,),
        in_specs=[tok] * 6 + [par(1, D_MODEL), par(1, D_MODEL), par(1, D_MODEL), par(D_MODEL, D_MODEL)],
        out_specs=tok,
        out_shape=jax.ShapeDtypeStruct((T, D_MODEL), F32),
        compiler_params=_cp(("parallel",)),
        name="odd_post",
    )(x, o, r, k, v, g, P["gn_w"], P["gn_b"], P["r_k"], P["w_o"])


RWKV_TB = 64
RWKV_NPAIR = RWKV_HEADS // 2


RWKV_SEQS = 2
RWKV_SPLIT = 1
RWKV_UNROLL = 8


def _rwkv_seq_body(w_ref, k_ref, a_ref, b_ref, r_ref, v_ref, o_ref, sn_ref,
                   s_ref, vt_ref, oacc_ref, ap_ref, bp_ref, kp_ref, rp_ref, gl_ref, *, ntb):
    N = RWKV_HEAD
    R = RWKV_NPAIR * N
    tbi = pl.program_id(1)

    @pl.when(tbi == 0)
    def _():
        s_ref[...] = jnp.zeros_like(s_ref)

    seg = _onehot((_iota((2 * LANES, 2 * LANES), 0) >> HEAD_SHIFT) == (_iota((2 * LANES, 2 * LANES), 1) >> HEAD_SHIFT))

    def seg_sum(xb):
        h = xb.shape[0] // 2
        both = _dot(jnp.concatenate([xb[:h], xb[h:]], axis=1), seg)
        return jnp.concatenate([both[:, :LANES], both[:, LANES:]], axis=0)
    incl = _onehot(_iota((RWKV_TB, RWKV_TB), 0) >= _iota((RWKV_TB, RWKV_TB), 1))
    for c in range(RWKV_SEQS):
        logw = jnp.log(w_ref[c])
        cum = _dot01_l(incl, logw)
        g_inc = jnp.exp(cum)
        g_inv = jnp.exp(-cum)
        ap_ref[c] = a_ref[c] * jnp.exp(cum - logw)
        bp_ref[c] = b_ref[c] * g_inv
        kp_ref[c] = k_ref[c] * g_inv
        rp_ref[c] = r_ref[c] * g_inc
        gl_ref[c] = g_inc[RWKV_TB - 8:RWKV_TB, :]
        for p in range(RWKV_NPAIR):
            bt = v_ref[c, :, p * LANES:(p + 1) * LANES].T
            vt_ref[c, p * N:(p + 1) * N, :] = jnp.concatenate([bt[0:N], bt[N:2 * N]], axis=1)
    oacc_ref[...] = jnp.zeros_like(oacc_ref)
    G = R // RWKV_SPLIT
    OSTEPS = LANES // 4
    seg_out = _onehot((_iota((2 * LANES, LANES), 0) >> HEAD_SHIFT) == (_iota((2 * LANES, LANES), 1) >> (OSTEPS.bit_length() - 1)))
    lane_copy = _iota((R // 2, LANES), 1) & (OSTEPS - 1)
    lane_base = (_iota((G, LANES), 1) >> HEAD_SHIFT) * N

    def steps(tu, carry):
        row0 = pl.multiple_of(tu * RWKV_UNROLL, RWKV_UNROLL)
        for j in range(RWKV_UNROLL):
            t = tu * RWKV_UNROLL + j
            for c in range(RWKV_SEQS):
                for g in range(RWKV_SPLIT):
                    rs = slice(g * G, (g + 1) * G)

                    def rows(ref):
                        return jnp.concatenate(
                            [jnp.broadcast_to(
                                ref[c, pl.ds(row0, RWKV_UNROLL), p * LANES:(p + 1) * LANES][j:j + 1, :], (N, LANES))
                             for p in range(g * G // N, (g + 1) * G // N)], axis=0)

                    s = s_ref[c, rs, :]
                    sa = seg_sum((s * rows(ap_ref)).astype(BF16))
                    vcol = jnp.take_along_axis(vt_ref[c, rs, :], lane_base + t, axis=1)
                    s = (s + vcol * rows(kp_ref)) + sa * rows(bp_ref)
                    s_ref[c, rs, :] = s
                    xr = (s * rows(rp_ref)).astype(BF16)
                    ob = _dot(jnp.concatenate([xr[:R // 2], xr[R // 2:]], axis=1), seg_out)
                    hb = tu // (OSTEPS // RWKV_UNROLL)
                    oacc_ref[c, hb] = jnp.where(lane_copy == (t & (OSTEPS - 1)), ob, oacc_ref[c, hb])
        return carry

    lax.fori_loop(0, RWKV_TB // RWKV_UNROLL, steps, 0)

    for c in range(RWKV_SEQS):
        s_ref[c] = s_ref[c] * jnp.concatenate(
            [jnp.broadcast_to(gl_ref[c, 7:8, p * LANES:(p + 1) * LANES], (N, LANES)) for p in range(RWKV_NPAIR)],
            axis=0)
        for hb in range(RWKV_TB // OSTEPS):
            ot = oacc_ref[c, hb].T
            for gh in range(4):
                g, hh = divmod(gh, 2)
                for q in range(RWKV_NPAIR // 2):
                    lane0 = (g * (RWKV_NPAIR // 2) + q) * LANES + hh * N
                    o_ref[c, hb * OSTEPS:(hb + 1) * OSTEPS, lane0:lane0 + N] = (
                        ot[gh * OSTEPS:(gh + 1) * OSTEPS, q * N:(q + 1) * N])

    @pl.when(tbi == ntb - 1)
    def _():
        for c in range(RWKV_SEQS):
            for p in range(RWKV_NPAIR):
                for hh in range(2):
                    sn_ref[c, (2 * p + hh) * N:(2 * p + hh + 1) * N, :] = (
                        s_ref[c, p * N:(p + 1) * N, hh * N:(hh + 1) * N])


def _rwkv_seq(w, k, a, b, r, v, B, L):
    ntb = L // RWKV_TB
    nc = RWKV_SEQS
    as_seqs = lambda x: x.reshape(B, L, D_MODEL)
    tok = pl.BlockSpec((nc, RWKV_TB, D_MODEL), lambda bi, t: (bi, t, 0))
    state_rows = RWKV_NPAIR * RWKV_HEAD
    o, sn = pl.pallas_call(
        functools.partial(_rwkv_seq_body, ntb=ntb),
        grid=(B // nc, ntb),
        in_specs=[tok] * 6,
        out_specs=[tok, pl.BlockSpec((nc, D_MODEL, RWKV_HEAD), lambda bi, t: (bi, 0, 0))],
        out_shape=[jax.ShapeDtypeStruct((B, L, D_MODEL), F32),
                   jax.ShapeDtypeStruct((B, D_MODEL, RWKV_HEAD), F32)],
        scratch_shapes=[pltpu.VMEM((nc, state_rows, LANES), F32)] * 2
        + [pltpu.VMEM((nc, 2, state_rows // 2, LANES), F32)]
        + [pltpu.VMEM((nc, RWKV_TB, D_MODEL), F32)] * 4
        + [pltpu.VMEM((nc, 8, D_MODEL), F32)],
        compiler_params=_cp(("parallel", "arbitrary")),
        name="rwkv_seq",
    )(as_seqs(w), as_seqs(k), as_seqs(a), as_seqs(b), as_seqs(r), as_seqs(v))
    return o.reshape(B * L, D_MODEL), sn.reshape(B, RWKV_HEADS, RWKV_HEAD, RWKV_HEAD)


def _rwkv_step_body(s_ref, w_ref, k_ref, a_ref, b_ref, r_ref, vc_ref, *rest):
    sn_ref, o_ref = rest[-2:]
    N = RWKV_HEAD
    lane = _iota((D_MODEL, STEP_BATCH), 1)
    ocols = jnp.zeros((D_MODEL, STEP_BATCH), F32)
    vc = vc_ref[...]

    def rows(ref, b):
        return jnp.concatenate(
            [jnp.broadcast_to(ref[b:b + 1, h * N:(h + 1) * N], (N, N)) for h in range(RWKV_HEADS)], axis=0)

    for b in range(STEP_BATCH):
        s = s_ref[b]
        sa = jnp.sum(s * rows(a_ref, b), axis=1, keepdims=True)
        s = s * rows(w_ref, b) + sa * rows(b_ref, b) + vc[:, b:b + 1] * rows(k_ref, b)
        sn_ref[b] = s
        ocol = jnp.sum(s * rows(r_ref, b), axis=1, keepdims=True)
        ocols = jnp.where(lane == b, ocol, ocols)
    o_ref[...] = ocols


def _rwkv_step(state_all, acc, i, w, k, a, b, r, v):
    Bn = state_all.shape[1]
    bb = STEP_BATCH
    col = pl.BlockSpec((None, D_MODEL, bb), lambda t: (t, 0, 0))
    row = pl.BlockSpec((bb, D_MODEL), lambda t: (t, 0))
    acc, ocols = _layer_state_call(
        _rwkv_step_body, "rwkv_step", state_all, acc, i, D_MODEL, RWKV_HEAD,
        [w, k, a, b, r, _to_cols(v)], [row] * 5 + [col],
        col, jax.ShapeDtypeStruct((Bn // bb, D_MODEL, bb), F32))
    return _from_cols(ocols), acc


def _pad_to(w, axis, n):
    pad = [(0, 0)] * w.ndim
    pad[axis] = (0, n - w.shape[axis])
    return jnp.pad(w, pad)


def _prepare(W):
    bf = lambda w: w.astype(BF16)
    row = lambda w: w[:, None, :]
    sizes = [SSD_INNER, SSD_CONV_DIM, SSD_HEADS, GLA_KEY, GLA_KEY, GLA_VAL, GLA_VAL, GLA_GATE_RANK]
    offs = [0]
    for s in sizes:
        offs.append(offs[-1] + s)
    piece = lambda n: W["ev_w_in"][:, :, offs[n]:offs[n + 1]]
    w_in = jnp.concatenate([piece(0), piece(1), piece(3), piece(4), piece(5), piece(6),
                            _pad_to(jnp.concatenate([piece(2), piece(7)], axis=-1), 2, SMALL)], axis=-1)
    gate_w2 = jnp.pad(W["ev_gla_gate_w2"], ((0, 0), (SSD_HEADS, SMALL - SSD_HEADS - GLA_GATE_RANK), (0, 0)))
    even = dict(
        w_in=bf(w_in), w_out=bf(W["ev_w_out"]),
        conv_w=W["ev_conv_w"], conv_b=row(W["ev_conv_b"]),
        dt_bias=row(_pad_to(W["ev_dt_bias"], 1, SMALL)), a_log=row(_pad_to(W["ev_a_log"], 1, SMALL)),
        d_skip_x=row(jnp.repeat(W["ev_d_skip"], SSD_HEAD_DIM, axis=1)),
        ssd_norm=row(W["ev_ssd_norm"]), gate_w2=bf(gate_w2), gate_b=row(W["ev_gla_gate_b"]),
        gla_norm=row(W["ev_gla_norm"]),
    )
    odd = dict(
        mix_norm=row(W["mix_norm"]), mix=_pad_to(W["od_mix"], 1, 8),
        w_r=bf(W["od_w_r"]), w_k=bf(W["od_w_k"]), w_v=bf(W["od_w_v"]), w_o=bf(W["od_w_o"]),
        w0=row(W["od_w0"]), w1=bf(_pad_to(W["od_w1"], 2, LANES)), w2=bf(_pad_to(W["od_w2"], 1, LANES)),
        a0=row(W["od_a0"]), a1=bf(_pad_to(W["od_a1"], 2, LANES)), a2=bf(_pad_to(W["od_a2"], 1, LANES)),
        v0=row(W["od_v0"]), v1=bf(_pad_to(W["od_v1"], 2, LANES)), v2=bf(_pad_to(W["od_v2"], 1, LANES)),
        g1=bf(_pad_to(W["od_g1"], 2, 2 * LANES)), g2=bf(_pad_to(W["od_g2"], 1, 2 * LANES)),
        k_k=row(W["od_k_k"]), k_a=row(W["od_k_a"]), r_k=row(W["od_r_k"]),
        gn_w=row(W["od_gn_w"]), gn_b=row(W["od_gn_b"]),
    )
    ffn = dict(
        norm=W["ffn_norm"].reshape(DEPTH * 2, 1, D_MODEL),
        w_gu=bf(W["ffn_w_gu"]).reshape(DEPTH * 2, D_MODEL, 2 * D_FF),
        w_down=bf(W["ffn_w_down"]).reshape(DEPTH * 2, D_FF, D_MODEL),
    )
    return dict(even=even, odd=odd, ffn=ffn, mix_norm=row(W["mix_norm"]),
                final_norm=W["final_norm"][None, :])


def _run_group(x3, states, Wp):
    B, L, _ = x3.shape
    T = B * L
    step = L == 1
    x = x3.reshape(T, D_MODEL)
    tm_ffn = min(512, T)
    tm_proj = min(256, T)
    ffn, even, odd = Wp["ffn"], Wp["even"], Wp["odd"]
    convs, ssms, glas, shifts, wkvs = [], [], [], [], []
    ssm_acc = gla_acc = wkv_acc = None
    v_first = None
    for layer in range(DEPTH):
        i = layer // 2
        mix = None
        x = _ffn(x, ffn["norm"], ffn["w_gu"], ffn["w_down"], 2 * layer, tm_ffn)
        if layer % 2 == 0:
            z, xbc, q, k, v, og, sm = _even_in(x, Wp["mix_norm"], even["w_in"], i, tm_ffn)
            if step:
                xact, convn, dax, c2, ybase, eg = _even_step_pre(
                    xbc, states[0][i].reshape(B, 3 * SSD_CONV_DIM), sm, even, i)
                ssm_acc, yi = _ssd_step(states[1], ssm_acc, i, dax, c2, xact)
                gla_acc, o = _gla_step(states[2], gla_acc, i, eg, k, q, v)
                x = _even_step_post(x, ybase, yi, dax, z, o, og, even, even["w_out"], i)
                convn = convn.reshape(B, SSD_CONV - 1, SSD_CONV_DIM)
            else:
                y, convn, ssmn = _ssd_seq(xbc, sm, z, even, i, B, L)
                o, glan = _gla_seq(q, k, v, og, sm, even, i, B, L)
                mix = (y, o, even["w_out"], i)
                ssms.append(ssmn)
                glas.append(glan)
            convs.append(convn)
        else:
            prev = states[3][i] if step else jnp.zeros((B, 1, D_MODEL), F32)
            h, r, dec, k, v, an, bb, gg = _odd_pre(x, prev, v_first, odd, i, tm_proj, L)
            if v_first is None:
                v_first = v
            if step:
                o, wkv_acc = _rwkv_step(states[4], wkv_acc, i, dec, k, an, bb, r, v)
            else:
                o, wkvn = _rwkv_seq(dec, k, an, bb, r, v, B, L)
                wkvs.append(wkvn)
            x = _odd_post(x, o, r, k, v, gg, odd, i, tm_ffn)
            shifts.append(h.reshape(B, L, D_MODEL)[:, -1])
        x = _ffn(x, ffn["norm"], ffn["w_gu"], ffn["w_down"], 2 * layer + 1, tm_ffn,
                 final_g=Wp["final_norm"] if layer == DEPTH - 1 else None, mix=mix)
    y = x.reshape(B, L, D_MODEL)
    if step:
        ssm_out = ssm_acc.reshape(N_EVEN, B, SSD_HEADS, SSD_HEAD_DIM, SSD_D_STATE)
        gla_out = gla_acc.reshape(N_EVEN, B, GLA_HEADS, GLA_DK, GLA_DV)
        wkv_out = wkv_acc.reshape(N_ODD, B, RWKV_HEADS, RWKV_HEAD, RWKV_HEAD)
    else:
        ssm_out, gla_out, wkv_out = jnp.stack(ssms), jnp.stack(glas), jnp.stack(wkvs)
    return y, jnp.stack(convs), ssm_out, gla_out, jnp.stack(shifts), wkv_out


def kernel(x_prompt, x_sample, state_conv, state_ssm, state_gla, state_shift, state_wkv, ffn_norm, ffn_w_gu, ffn_w_down, mix_norm, final_norm, ev_w_in, ev_conv_w, ev_conv_b, ev_dt_bias, ev_a_log, ev_d_skip, ev_ssd_norm, ev_gla_gate_w2, ev_gla_gate_b, ev_gla_norm, ev_w_out, od_mix, od_w0, od_w1, od_w2, od_a0, od_a1, od_a2, od_v0, od_v1, od_v2, od_g1, od_g2, od_k_k, od_k_a, od_r_k, od_w_r, od_w_k, od_w_v, od_w_o, od_gn_w, od_gn_b):
    W = dict(ffn_norm=ffn_norm, ffn_w_gu=ffn_w_gu, ffn_w_down=ffn_w_down, mix_norm=mix_norm,
             final_norm=final_norm, ev_w_in=ev_w_in, ev_conv_w=ev_conv_w, ev_conv_b=ev_conv_b,
             ev_dt_bias=ev_dt_bias, ev_a_log=ev_a_log, ev_d_skip=ev_d_skip, ev_ssd_norm=ev_ssd_norm,
             ev_gla_gate_w2=ev_gla_gate_w2, ev_gla_gate_b=ev_gla_gate_b, ev_gla_norm=ev_gla_norm,
             ev_w_out=ev_w_out, od_mix=od_mix, od_w0=od_w0, od_w1=od_w1, od_w2=od_w2,
             od_a0=od_a0, od_a1=od_a1, od_a2=od_a2, od_v0=od_v0, od_v1=od_v1, od_v2=od_v2,
             od_g1=od_g1, od_g2=od_g2, od_k_k=od_k_k, od_k_a=od_k_a, od_r_k=od_r_k,
             od_w_r=od_w_r, od_w_k=od_w_k, od_w_v=od_w_v, od_w_o=od_w_o,
             od_gn_w=od_gn_w, od_gn_b=od_gn_b)
    Wp = _prepare(W)
    prompt = _run_group(x_prompt, None, Wp)
    sample = _run_group(x_sample, (state_conv, state_ssm, state_gla, state_shift, state_wkv), Wp)
    return (prompt[0], sample[0]) + prompt[1:] + sample[1:]
```

```python
import functools

import jax
import jax.numpy as jnp
from jax import lax
from jax.experimental import pallas as pl
from jax.experimental.pallas import tpu as pltpu

F32 = jnp.float32
BF16 = jnp.bfloat16

D_MODEL = 1024
DEPTH = 4
N_EVEN = 2
N_ODD = 2
EPS = 1e-5
D_FF = 2816

SSD_HEADS = 16
SSD_HEAD_DIM = 64
SSD_INNER = 1024
SSD_GROUPS = 2
SSD_D_STATE = 128
SSD_CONV = 4
SSD_CONV_DIM = 1536
SSD_CHUNK = 64

GLA_HEADS = 4
GLA_DK = 128
GLA_DV = 256
GLA_KEY = 512
GLA_VAL = 1024
GLA_GATE_RANK = 16
GLA_GATE_NORM = 16.0
GLA_CHUNK = 16

RWKV_HEAD = 64
RWKV_HEADS = 16
RWKV_GN_EPS = 64e-5

HEAD_SHIFT = 6
GLA_CHUNK_SHIFT = 4
assert 1 << HEAD_SHIFT == SSD_HEAD_DIM == RWKV_HEAD == SSD_CHUNK and 1 << GLA_CHUNK_SHIFT == GLA_CHUNK

LANES = 128
SMALL = LANES
IN_PERM = SSD_INNER + SSD_CONV_DIM + 2 * GLA_KEY + 2 * GLA_VAL + SMALL
VMEM_LIMIT = 56 * 1024 * 1024
FF_TILE = 1408
SEQ_BLOCK = 256
STEP_BATCH = 8


def _cp(sem):
    return pltpu.CompilerParams(dimension_semantics=sem, vmem_limit_bytes=VMEM_LIMIT)


def _iota(shape, axis):
    return lax.broadcasted_iota(jnp.int32, shape, axis)


def _onehot(mask):
    return jnp.where(mask, 1.0, 0.0).astype(BF16)


def _dot(a, b):
    return jnp.dot(a, b, preferred_element_type=F32)


def _split3(x):
    hi = x.astype(BF16)
    r = x - hi.astype(F32)
    mid = r.astype(BF16)
    lo = (r - mid.astype(F32)).astype(BF16)
    return hi, mid, lo


def _dot01(x, e):
    hi, mid, lo = _split3(x)
    return _dot(hi, e) + _dot(mid, e) + _dot(lo, e)


def _dot01_l(e, x):
    hi, mid, lo = _split3(x)
    return _dot(e, hi) + _dot(e, mid) + _dot(e, lo)


def _rms(x, g):
    ms = jnp.mean(x * x, axis=-1, keepdims=True)
    return x * lax.rsqrt(ms + EPS) * g


def _silu(x):
    return x * jax.nn.sigmoid(x)


def _softplus(x):
    return jnp.maximum(x, 0.0) + jnp.log1p(jnp.exp(-jnp.abs(x)))


def _softplus_abs(x):
    return jnp.maximum(x, 0.0) + jnp.log(1.0 + jnp.exp(-jnp.abs(x)))


def _head_expand():
    return _onehot((_iota((LANES, D_MODEL), 1) >> HEAD_SHIFT) == _iota((LANES, D_MODEL), 0))


def _head_reduce():
    return _onehot((_iota((D_MODEL, LANES), 0) >> HEAD_SHIFT) == _iota((D_MODEL, LANES), 1))


def _ffn_body(x_ref, g_ref, wgu_ref, wd_ref, *rest, mix_out, final_norm):
    o_ref = rest[-1]
    x = x_ref[...]
    if mix_out:
        y_ref, mo_ref, wo_ref = rest[:3]
        x = x + _dot(y_ref[...].astype(BF16), wo_ref[0:SSD_INNER, :])
        x = x + _dot(mo_ref[...].astype(BF16), wo_ref[SSD_INNER:, :])
    xn = _rms(x, g_ref[...]).astype(BF16)
    acc = None
    for j in range(D_FF // FF_TILE):
        lo = j * FF_TILE
        gate = _dot(xn, wgu_ref[:, lo:lo + FF_TILE])
        up = _dot(xn, wgu_ref[:, D_FF + lo:D_FF + lo + FF_TILE])
        part = _dot((_silu(gate) * up).astype(BF16), wd_ref[lo:lo + FF_TILE, :])
        acc = part if acc is None else acc + part
    y = x + 0.5 * acc
    o_ref[...] = _rms(y, rest[-2][...]) if final_norm else y


def _ffn(x, nrm, wgu, wd, ls, tm, final_g=None, mix=None):
    T = x.shape[0]
    resident = pl.Buffered(1)
    tok = pl.BlockSpec((tm, D_MODEL), lambda i: (i, 0))
    args = [x, nrm, wgu, wd]
    specs = [
        tok,
        pl.BlockSpec((None, 1, D_MODEL), lambda i: (ls, 0, 0)),
        pl.BlockSpec((None, D_MODEL, 2 * D_FF), lambda i: (ls, 0, 0), pipeline_mode=resident),
        pl.BlockSpec((None, D_FF, D_MODEL), lambda i: (ls, 0, 0), pipeline_mode=resident),
    ]
    if mix is not None:
        y, o, w_out, li = mix
        args += [y, o, w_out]
        specs += [tok, tok, pl.BlockSpec((None, SSD_INNER + GLA_VAL, D_MODEL), lambda i: (li, 0, 0),
                                         pipeline_mode=resident)]
    if final_g is not None:
        args.append(final_g)
        specs.append(pl.BlockSpec((1, D_MODEL), lambda i: (0, 0)))
    return pl.pallas_call(
        functools.partial(_ffn_body, mix_out=mix is not None, final_norm=final_g is not None),
        grid=(T // tm,),
        in_specs=specs,
        out_specs=pl.BlockSpec((tm, D_MODEL), lambda i: (i, 0)),
        out_shape=jax.ShapeDtypeStruct((T, D_MODEL), F32),
        compiler_params=_cp(("parallel",)),
        name="ffn",
    )(*args)


_EVEN_PIECES = (SSD_INNER, SSD_CONV_DIM, GLA_KEY, GLA_KEY, GLA_VAL, GLA_VAL, SMALL)


def _even_in_body(x_ref, g_ref, w_ref, *out_refs):
    h = _rms(x_ref[...], g_ref[...]).astype(BF16)
    off = 0
    for ref, n in zip(out_refs, _EVEN_PIECES):
        ref[...] = _dot(h, w_ref[:, off:off + n])
        off += n


def _even_in(x, nrm, w, i, tm):
    T = x.shape[0]
    return pl.pallas_call(
        _even_in_body,
        grid=(T // tm,),
        in_specs=[
            pl.BlockSpec((tm, D_MODEL), lambda t: (t, 0)),
            pl.BlockSpec((None, 1, D_MODEL), lambda t: (2 * i, 0, 0)),
            pl.BlockSpec((None, D_MODEL, IN_PERM), lambda t: (i, 0, 0), pipeline_mode=pl.Buffered(1)),
        ],
        out_specs=[pl.BlockSpec((tm, n), lambda t: (t, 0)) for n in _EVEN_PIECES],
        out_shape=[jax.ShapeDtypeStruct((T, n), F32) for n in _EVEN_PIECES],
        compiler_params=_cp(("parallel",)),
        name="even_in",
    )(x, nrm, w)


def _ssd_seq_body(xbc_ref, sm_ref, z_ref, cw_ref, cb_ref, dtb_ref, alog_ref, dsk_ref, nw_ref,
                  y_ref, convn_ref, ssmn_ref,
                  cbuf_ref, xact_ref, acx_ref, dtx_ref, hT_ref, *, nblk):
    Lb = SEQ_BLOCK
    C = SSD_CHUNK
    j = pl.program_id(1)

    @pl.when(j == 0)
    def _():
        cbuf_ref[0:8, :] = jnp.zeros((8, SSD_CONV_DIM), F32)
        hT_ref[...] = jnp.zeros_like(hT_ref)

    x = xbc_ref[...]
    acc = cb_ref[...] + pltpu.roll(x, 3, axis=0) * cw_ref[0:1, :]
    acc = acc + pltpu.roll(x, 2, axis=0) * cw_ref[1:2, :]
    acc = acc + pltpu.roll(x, 1, axis=0) * cw_ref[2:3, :]
    acc = acc + x * cw_ref[3:4, :]
    xact_ref[...] = _silu(acc)
    cbuf_ref[8:16, :] = x[0:8, :]
    head = cb_ref[...] + cbuf_ref[5:13, :] * cw_ref[0:1, :]
    head = head + cbuf_ref[6:14, :] * cw_ref[1:2, :]
    head = head + cbuf_ref[7:15, :] * cw_ref[2:3, :]
    head = head + cbuf_ref[8:16, :] * cw_ref[3:4, :]
    xact_ref[0:8, :] = _silu(head)
    tail = x[Lb - 8:Lb, :]
    cbuf_ref[0:8, :] = tail

    @pl.when(j == nblk - 1)
    def _():
        convn_ref[...] = tail[5:8, :]

    dt = _softplus(sm_ref[...] + dtb_ref[...])
    dta = dt * (-jnp.exp(alog_ref[...]))
    ii = _iota((Lb, Lb), 0)
    jj = _iota((Lb, Lb), 1)
    tril = _onehot(((ii >> HEAD_SHIFT) == (jj >> HEAD_SHIFT)) & (ii >= jj))
    acum = _dot01_l(tril, dta)
    expand = _head_expand()
    acx_ref[...] = _dot01(acum, expand)
    dtx_ref[...] = _dot01(dt, expand)

    li = _iota((C, D_MODEL), 0)
    lj = _iota((C, D_MODEL), 1) & (C - 1)
    diag = li == lj
    causal = li >= lj
    ones_c = jnp.ones((C, C), BF16)
    ones_c2 = jnp.ones((C, 2 * C), BF16)
    tile8 = _onehot((_iota((2 * C, 8 * C), 1) & (C - 1)) == (_iota((2 * C, 8 * C), 0) & (C - 1)))
    pair_mask = (_iota((LANES, LANES), 0) >> HEAD_SHIFT) == (_iota((LANES, LANES), 1) >> HEAD_SHIFT)
    half = SSD_INNER // SSD_GROUPS

    def chunk(c, carry):
        r0 = pl.multiple_of(c * C, C)
        xa = xact_ref[pl.ds(r0, C), :]
        xs = xa[:, :SSD_INNER]
        bms = [xa[:, SSD_INNER + g * SSD_D_STATE:SSD_INNER + (g + 1) * SSD_D_STATE] for g in range(2)]
        cms = [xa[:, SSD_INNER + (2 + g) * SSD_D_STATE:SSD_INNER + (3 + g) * SSD_D_STATE]
               for g in range(2)]
        acx = acx_ref[pl.ds(r0, C), :]
        dtx = dtx_ref[pl.ds(r0, C), :]
        alast = acx[C - 1:C, :]
        arow = _dot01_l(ones_c, jnp.where(diag, acx, 0.0))
        dsel = jnp.where(diag, dtx, 0.0)
        dhi = dsel.astype(BF16)
        dmid = (dsel - dhi.astype(F32)).astype(BF16)
        dtrow = _dot(ones_c2, jnp.concatenate([dhi, dmid], axis=0))
        decay = jnp.exp(jnp.where(causal, acx - arow, -jnp.inf))
        cbt = []
        for g in range(2):
            cb = lax.dot_general(cms[g].astype(BF16), bms[g].astype(BF16),
                                 (((1,), (1,)), ((), ())), preferred_element_type=F32)
            cbt.append(_dot01x2(cb, tile8))
        wts = (jnp.concatenate(cbt, axis=1) * decay * dtrow).astype(BF16)
        ys = []
        for p in range(SSD_HEADS // 2):
            xp = xs[:, p * LANES:(p + 1) * LANES]
            xbd = jnp.where(pair_mask, jnp.concatenate([xp, xp], axis=0), 0.0).astype(BF16)
            ys.append(_dot(wts[:, p * LANES:(p + 1) * LANES], xbd))
        y = jnp.concatenate(ys, axis=1)
        hT = hT_ref[...]
        hTb = hT.astype(BF16)
        yi = [_dot(cms[g].astype(BF16), hTb[:, g * half:(g + 1) * half]) for g in range(2)]
        y = y + jnp.concatenate(yi, axis=1) * jnp.exp(acx)
        xsc = (jnp.exp(alast - acx) * dtx * xs).astype(BF16)
        st = [lax.dot_general(bms[g].astype(BF16), xsc[:, g * half:(g + 1) * half],
                              (((0,), (0,)), ((), ())), preferred_element_type=F32) for g in range(2)]
        hT_ref[...] = hT * jnp.exp(alast) + jnp.concatenate(st, axis=1)
        y = y + dsk_ref[...] * xs
        y = y * _silu(z_ref[pl.ds(r0, C), :])
        outs = []
        for g in range(2):
            yg = y[:, g * half:(g + 1) * half]
            ms = jnp.mean(yg * yg, axis=-1, keepdims=True)
            outs.append(yg * lax.rsqrt(ms + EPS))
        y_ref[pl.ds(r0, C), :] = jnp.concatenate(outs, axis=1) * nw_ref[...]
        return carry

    lax.fori_loop(0, Lb // C, chunk, 0, unroll=4)

    @pl.when(j == nblk - 1)
    def _():
        ssmn_ref[...] = hT_ref[...].T


def _ssd_seq(xbc, sm, z, P, i, B, L):
    Lb = SEQ_BLOCK
    nblk = L // Lb
    T = B * L
    tok = lambda n: pl.BlockSpec((Lb, n), lambda b, j: (b * nblk + j, 0))
    par = lambda r, n: pl.BlockSpec((None, r, n), lambda b, j: (i, 0, 0))
    y, convn, ssmn = pl.pallas_call(
        functools.partial(_ssd_seq_body, nblk=nblk),
        grid=(B, nblk),
        in_specs=[tok(SSD_CONV_DIM), tok(SMALL), tok(SSD_INNER),
                  par(SSD_CONV, SSD_CONV_DIM), par(1, SSD_CONV_DIM), par(1, SMALL), par(1, SMALL),
                  par(1, SSD_INNER), par(1, SSD_INNER)],
        out_specs=[tok(SSD_INNER),
                   pl.BlockSpec((None, SSD_CONV - 1, SSD_CONV_DIM), lambda b, j: (b, 0, 0)),
                   pl.BlockSpec((None, SSD_INNER, SSD_D_STATE), lambda b, j: (b, 0, 0))],
        out_shape=[jax.ShapeDtypeStruct((T, SSD_INNER), F32),
                   jax.ShapeDtypeStruct((B, SSD_CONV - 1, SSD_CONV_DIM), F32),
                   jax.ShapeDtypeStruct((B, SSD_INNER, SSD_D_STATE), F32)],
        scratch_shapes=[pltpu.VMEM((16, SSD_CONV_DIM), F32),
                        pltpu.VMEM((Lb, SSD_CONV_DIM), F32),
                        pltpu.VMEM((Lb, SSD_INNER), F32),
                        pltpu.VMEM((Lb, SSD_INNER), F32),
                        pltpu.VMEM((SSD_D_STATE, SSD_INNER), F32)],
        compiler_params=_cp(("parallel", "arbitrary")),
        name="ssd_seq",
    )(xbc, sm, z, P["conv_w"], P["conv_b"], P["dt_bias"], P["a_log"], P["d_skip_x"], P["ssd_norm"])
    return y, convn, ssmn.reshape(B, SSD_HEADS, SSD_HEAD_DIM, SSD_D_STATE)


def _gla_seq_body(q_ref, k_ref, v_ref, og_ref, sm_ref, w2_ref, gb_ref, nw_ref,
                  o_ref, glan_ref,
                  st_ref, dec_ref, qg_ref, kd_ref, oi_ref, *, nblk):
    Lb = SEQ_BLOCK
    j = pl.program_id(1)

    @pl.when(j == 0)
    def _():
        st_ref[...] = jnp.zeros_like(st_ref)

    gate = _dot(sm_ref[...].astype(BF16), w2_ref[...]) + gb_ref[...]
    lg = -_softplus_abs(-gate) / GLA_GATE_NORM
    ii = _iota((Lb, Lb), 0)
    jj = _iota((Lb, Lb), 1)
    same = (ii >> GLA_CHUNK_SHIFT) == (jj >> GLA_CHUNK_SHIFT)
    causal = same & (ii >= jj)
    lg_hi = lg.astype(BF16)
    lg2 = jnp.concatenate([lg_hi, (lg - lg_hi.astype(F32)).astype(BF16)], axis=0)
    sel_c = _onehot(causal)
    sel_s = _onehot(same)
    b = _dot(jnp.concatenate([sel_c, sel_c], axis=1), lg2)
    blast = _dot(jnp.concatenate([sel_s, sel_s], axis=1), lg2)
    qg = q_ref[...] * (GLA_DK ** -0.5) * jnp.exp(b)
    kk = k_ref[...]
    kg = kk * jnp.exp(-b)
    kd = kk * jnp.exp(blast - b)
    dec_ref[...] = jnp.exp(blast)
    qg_ref[...] = qg.astype(BF16)
    kd_ref[...] = kd.astype(BF16)

    def chunk(c, carry):
        r0 = pl.multiple_of(c * GLA_CHUNK, GLA_CHUNK)
        for h in range(GLA_HEADS):
            ks = slice(h * GLA_DK, (h + 1) * GLA_DK)
            vs = slice(h * GLA_DV, (h + 1) * GLA_DV)
            st = st_ref[h]
            oi_ref[pl.ds(r0, GLA_CHUNK), vs] = lax.dot_general(
                qg_ref[pl.ds(r0, GLA_CHUNK), ks], st.astype(BF16),
                (((1,), (1,)), ((), ())), preferred_element_type=F32)
            upd = lax.dot_general(v_ref[pl.ds(r0, GLA_CHUNK), vs].astype(BF16),
                                  kd_ref[pl.ds(r0, GLA_CHUNK), ks],
                                  (((0,), (0,)), ((), ())), preferred_element_type=F32)
            st_ref[h] = st * dec_ref[pl.ds(r0, 8), ks][0:1, :] + upd
        return carry

    lax.fori_loop(0, Lb // GLA_CHUNK, chunk, 0, unroll=16)

    for h in range(GLA_HEADS):
        ks = slice(h * GLA_DK, (h + 1) * GLA_DK)
        vs = slice(h * GLA_DV, (h + 1) * GLA_DV)
        att = lax.dot_general(qg[:, ks].astype(BF16), kg[:, ks].astype(BF16),
                              (((1,), (1,)), ((), ())), preferred_element_type=F32)
        att = jnp.where(causal, att, 0.0)
        o_h = _dot(att.astype(BF16), v_ref[:, vs].astype(BF16)) + oi_ref[:, vs]
        ms = jnp.mean(o_h * o_h, axis=-1, keepdims=True)
        o_h = o_h * lax.rsqrt(ms + EPS) * nw_ref[...]
        o_ref[:, vs] = o_h * _silu(og_ref[:, vs])

    @pl.when(j == nblk - 1)
    def _():
        for h in range(GLA_HEADS):
            glan_ref[h] = st_ref[h].T


def _gla_seq(q, k, v, og, sm, P, i, B, L):
    Lb = SEQ_BLOCK
    nblk = L // Lb
    T = B * L
    tok = lambda n: pl.BlockSpec((Lb, n), lambda b, j: (b * nblk + j, 0))
    par = lambda r, n: pl.BlockSpec((None, r, n), lambda b, j: (i, 0, 0))
    return pl.pallas_call(
        functools.partial(_gla_seq_body, nblk=nblk),
        grid=(B, nblk),
        in_specs=[tok(GLA_KEY), tok(GLA_KEY), tok(GLA_VAL), tok(GLA_VAL), tok(SMALL),
                  par(SMALL, GLA_KEY), par(1, GLA_KEY), par(1, GLA_DV)],
        out_specs=[tok(GLA_VAL),
                   pl.BlockSpec((None, GLA_HEADS, GLA_DK, GLA_DV), lambda b, j: (b, 0, 0, 0))],
        out_shape=[jax.ShapeDtypeStruct((T, GLA_VAL), F32),
                   jax.ShapeDtypeStruct((B, GLA_HEADS, GLA_DK, GLA_DV), F32)],
        scratch_shapes=[pltpu.VMEM((GLA_HEADS, GLA_DV, GLA_DK), F32),
                        pltpu.VMEM((Lb, GLA_KEY), F32),
                        pltpu.VMEM((Lb, GLA_KEY), BF16),
                        pltpu.VMEM((Lb, GLA_KEY), BF16),
                        pltpu.VMEM((Lb, GLA_VAL), F32)],
        compiler_params=_cp(("parallel", "arbitrary")),
        name="gla_seq",
    )(q, k, v, og, sm, P["gate_w2"], P["gate_b"], P["gla_norm"])


def _even_step_pre_body(xbc_ref, c0_ref, sm_ref, cw_ref, cb_ref, dtb_ref, alog_ref, dsk_ref,
                        w2_ref, gb_ref,
                        xact_ref, convn_ref, dax_ref, c2_ref, ybase_ref, eg_ref):
    u = xbc_ref[...]
    n = SSD_CONV_DIM
    acc = cb_ref[...] + c0_ref[:, 0:n] * cw_ref[0:1, :]
    acc = acc + c0_ref[:, n:2 * n] * cw_ref[1:2, :]
    acc = acc + c0_ref[:, 2 * n:3 * n] * cw_ref[2:3, :]
    acc = acc + u * cw_ref[3:4, :]
    xa = _silu(acc)
    xact_ref[...] = xa
    convn_ref[:, 0:n] = c0_ref[:, n:2 * n]
    convn_ref[:, n:2 * n] = c0_ref[:, 2 * n:3 * n]
    convn_ref[:, 2 * n:3 * n] = u
    dt = _softplus(sm_ref[...] + dtb_ref[...])
    dta = dt * (-jnp.exp(alog_ref[...]))
    expand = _head_expand()
    dtx = _dot01(dt, expand)
    dax_ref[...] = jnp.exp(_dot01(dta, expand))
    xs = xa[:, :SSD_INNER]
    c2_ref[...] = dtx * xs
    half = SSD_INNER // SSD_GROUPS
    cbs = []
    for g in range(2):
        bm = xa[:, SSD_INNER + g * SSD_D_STATE:SSD_INNER + (g + 1) * SSD_D_STATE]
        cm = xa[:, SSD_INNER + (2 + g) * SSD_D_STATE:SSD_INNER + (3 + g) * SSD_D_STATE]
        cb = jnp.sum(cm * bm, axis=-1, keepdims=True)
        cbs.append(jnp.broadcast_to(cb, (cb.shape[0], half)))
    ybase_ref[...] = jnp.concatenate(cbs, axis=1) * dtx * xs + dsk_ref[...] * xs
    gate = _dot(sm_ref[...].astype(BF16), w2_ref[...]) + gb_ref[...]
    eg_ref[...] = jnp.exp(-_softplus(-gate) / GLA_GATE_NORM)


def _even_step_pre(xbc, conv0, sm, P, i):
    Bn = xbc.shape[0]
    full = lambda n: pl.BlockSpec((Bn, n), lambda t: (0, 0))
    par = lambda r, n: pl.BlockSpec((None, r, n), lambda t: (i, 0, 0))
    outs = (SSD_CONV_DIM, 3 * SSD_CONV_DIM, SSD_INNER, SSD_INNER, SSD_INNER, GLA_KEY)
    return pl.pallas_call(
        _even_step_pre_body,
        grid=(1,),
        in_specs=[full(SSD_CONV_DIM), full(3 * SSD_CONV_DIM), full(SMALL),
                  par(SSD_CONV, SSD_CONV_DIM), par(1, SSD_CONV_DIM), par(1, SMALL), par(1, SMALL),
                  par(1, SSD_INNER), par(SMALL, GLA_KEY), par(1, GLA_KEY)],
        out_specs=[full(n) for n in outs],
        out_shape=[jax.ShapeDtypeStruct((Bn, n), F32) for n in outs],
        compiler_params=_cp(("arbitrary",)),
        name="even_step_pre",
    )(xbc, conv0, sm, P["conv_w"], P["conv_b"], P["dt_bias"], P["a_log"], P["d_skip_x"],
      P["gate_w2"], P["gate_b"])


def _to_cols(x):
    Bn, N = x.shape
    return x.reshape(Bn // STEP_BATCH, STEP_BATCH, N).transpose(0, 2, 1)


def _from_cols(x):
    nb, N, bb = x.shape
    return x.transpose(0, 2, 1).reshape(nb * bb, N)


def _layer_state_call(body, name, state_all, acc, i, rows, width, args, specs, out_spec, out_shape):
    nl, Bn = state_all.shape[:2]
    st = pl.BlockSpec((None, STEP_BATCH, rows, width), lambda t: (i, t, 0, 0))
    args = [state_all.reshape(nl, Bn, rows, width)] + list(args)
    specs = [st] + list(specs)
    aliases = {}
    if acc is not None:
        aliases = {len(args): 0}
        args.append(acc)
        specs.append(pl.BlockSpec(memory_space=pl.ANY))
    return pl.pallas_call(
        body,
        grid=(Bn // STEP_BATCH,),
        in_specs=specs,
        out_specs=[st, out_spec],
        out_shape=[jax.ShapeDtypeStruct((nl, Bn, rows, width), F32), out_shape],
        input_output_aliases=aliases,
        compiler_params=_cp(("parallel",)),
        name=name,
    )(*args)


def _ssd_step_body(h_ref, c1_ref, c2_ref, bm_ref, cm_ref, *rest):
    hn_ref, y_ref = rest[-2:]
    half = SSD_INNER // SSD_GROUPS
    lane = _iota((SSD_INNER, STEP_BATCH), 1)
    ycols = jnp.zeros((SSD_INNER, STEP_BATCH), F32)
    c1 = c1_ref[...]
    c2 = c2_ref[...]
    for b in range(STEP_BATCH):
        h0 = h_ref[b]
        bmat = jnp.concatenate(
            [jnp.broadcast_to(bm_ref[b:b + 1, g * SSD_D_STATE:(g + 1) * SSD_D_STATE],
                              (half, SSD_D_STATE)) for g in range(2)], axis=0)
        cmat = jnp.concatenate(
            [jnp.broadcast_to(cm_ref[b:b + 1, g * SSD_D_STATE:(g + 1) * SSD_D_STATE],
                              (half, SSD_D_STATE)) for g in range(2)], axis=0)
        ycol = jnp.sum(h0 * cmat, axis=1, keepdims=True)
        ycols = jnp.where(lane == b, ycol, ycols)
        hn_ref[b] = h0 * c1[:, b:b + 1] + c2[:, b:b + 1] * bmat
    y_ref[...] = ycols


def _ssd_step(state_all, acc, i, dax, c2, xact):
    Bn = state_all.shape[1]
    bb = STEP_BATCH
    bm = xact[:, SSD_INNER:SSD_INNER + 2 * SSD_D_STATE]
    cm = xact[:, SSD_INNER + 2 * SSD_D_STATE:]
    col = pl.BlockSpec((None, SSD_INNER, bb), lambda t: (t, 0, 0))
    row = pl.BlockSpec((bb, 2 * SSD_D_STATE), lambda t: (t, 0))
    acc, ycols = _layer_state_call(
        _ssd_step_body, "ssd_step", state_all, acc, i, SSD_INNER, SSD_D_STATE,
        [_to_cols(dax), _to_cols(c2), bm, cm], [col, col, row, row],
        col, jax.ShapeDtypeStruct((Bn // bb, SSD_INNER, bb), F32))
    return acc, _from_cols(ycols)


def _gla_step_body(s_ref, eg_ref, kc_ref, qc_ref, v_ref, *rest):
    sn_ref, o_ref = rest[-2:]
    eg = eg_ref[...]
    kc = kc_ref[...]
    qc = qc_ref[...] * (GLA_DK ** -0.5)
    for b in range(STEP_BATCH):
        s0 = s_ref[b]
        vmat = jnp.concatenate(
            [jnp.broadcast_to(v_ref[b:b + 1, h * GLA_DV:(h + 1) * GLA_DV], (GLA_DK, GLA_DV))
             for h in range(GLA_HEADS)], axis=0)
        sn = s0 * eg[:, b:b + 1] + kc[:, b:b + 1] * vmat
        sn_ref[b] = sn
        t = sn * qc[:, b:b + 1]
        o_ref[b:b + 1, :] = jnp.concatenate(
            [jnp.sum(t[h * GLA_DK:(h + 1) * GLA_DK], axis=0, keepdims=True)
             for h in range(GLA_HEADS)], axis=1)


def _gla_step(state_all, acc, i, eg, k, q, v):
    Bn = state_all.shape[1]
    bb = STEP_BATCH
    col = pl.BlockSpec((None, GLA_KEY, bb), lambda t: (t, 0, 0))
    row = pl.BlockSpec((bb, GLA_VAL), lambda t: (t, 0))
    return _layer_state_call(
        _gla_step_body, "gla_step", state_all, acc, i, GLA_KEY, GLA_DV,
        [_to_cols(eg), _to_cols(k), _to_cols(q), v], [col, col, col, row],
        row, jax.ShapeDtypeStruct((Bn, GLA_VAL), F32))


def _even_step_post_body(x_ref, ybase_ref, yi_ref, dax_ref, z_ref, o_ref, og_ref,
                         snw_ref, gnw_ref, w_ref, out_ref):
    y = (ybase_ref[...] + yi_ref[...] * dax_ref[...]) * _silu(z_ref[...])
    half = SSD_INNER // SSD_GROUPS
    acc = None
    for g in range(2):
        yg = y[:, g * half:(g + 1) * half]
        ms = jnp.mean(yg * yg, axis=-1, keepdims=True)
        yn = yg * lax.rsqrt(ms + EPS) * snw_ref[:, g * half:(g + 1) * half]
        part = _dot(yn.astype(BF16), w_ref[g * half:(g + 1) * half, :])
        acc = part if acc is None else acc + part
    for h in range(GLA_HEADS):
        vs = slice(h * GLA_DV, (h + 1) * GLA_DV)
        oh = o_ref[:, vs]
        ms = jnp.mean(oh * oh, axis=-1, keepdims=True)
        on = oh * lax.rsqrt(ms + EPS) * gnw_ref[...] * _silu(og_ref[:, vs])
        acc = acc + _dot(on.astype(BF16), w_ref[SSD_INNER + h * GLA_DV:SSD_INNER + (h + 1) * GLA_DV, :])
    out_ref[...] = x_ref[...] + acc


def _even_step_post(x, ybase, yi, dax, z, o, og, P, w_out, i):
    Bn = x.shape[0]
    full = pl.BlockSpec((Bn, D_MODEL), lambda t: (0, 0))
    return pl.pallas_call(
        _even_step_post_body,
        grid=(1,),
        in_specs=[full] * 7 + [pl.BlockSpec((None, 1, SSD_INNER), lambda t: (i, 0, 0)),
                               pl.BlockSpec((None, 1, GLA_DV), lambda t: (i, 0, 0)),
                               pl.BlockSpec((None, SSD_INNER + GLA_VAL, D_MODEL), lambda t: (i, 0, 0))],
        out_specs=full,
        out_shape=jax.ShapeDtypeStruct((Bn, D_MODEL), F32),
        compiler_params=_cp(("arbitrary",)),
        name="even_step_post",
    )(x, ybase, yi, dax, z, o, og, P["ssd_norm"], P["gla_norm"], w_out)


_ODD_OUTS = 8


def _odd_pre_body(*refs, step, has_vfirst, nblk_seq, tm):
    it = iter(refs)
    x_ref = next(it)
    if step:
        prev_ref = next(it)
    else:
        xp_ref = next(it)
        sh0_ref = next(it)
    g_ref, mix_ref = next(it), next(it)
    wr_ref, wk_ref, wv_ref = next(it), next(it), next(it)
    w0_ref, w1_ref, w2_ref = next(it), next(it), next(it)
    a0_ref, a1_ref, a2_ref = next(it), next(it), next(it)
    if has_vfirst:
        vf_ref, v0_ref, v1_ref, v2_ref = next(it), next(it), next(it), next(it)
    g1_ref, g2_ref = next(it), next(it)
    kkw_ref, kaw_ref = next(it), next(it)
    h_ref, r_ref, dec_ref, k_ref, v_ref, an_ref, b_ref, gg_ref = (next(it) for _ in range(_ODD_OUTS))

    h = _rms(x_ref[...], g_ref[...])
    if step:
        prev = prev_ref[...]
    else:
        i = pl.program_id(0)
        hp = _rms(xp_ref[...], g_ref[...])[7:8, :]
        prow = jnp.where(i % nblk_seq == 0, sh0_ref[...], hp)
        prev = jnp.where(_iota((tm, D_MODEL), 0) == 0, prow, pltpu.roll(h, 1, axis=0))
    h_ref[...] = h[tm - h_ref.shape[0]:, :]
    xx = prev - h
    mixed = lambda r: (h + xx * mix_ref[r:r + 1, :]).astype(BF16)
    xr, xw, xk, xv, xa, xg = (mixed(r) for r in range(6))
    r_ref[...] = _dot(xr, wr_ref[...])
    wl = w0_ref[...] + _dot(jnp.tanh(_dot(xw, w1_ref[...])).astype(BF16), w2_ref[...])
    w = -_softplus_abs(-wl) - 0.5
    dec_ref[...] = jnp.exp(-jnp.exp(w))
    k = _dot(xk, wk_ref[...])
    v = _dot(xv, wv_ref[...])
    if has_vfirst:
        gate = jax.nn.sigmoid(v0_ref[...] + _dot(_dot(xv, v1_ref[...]).astype(BF16), v2_ref[...]))
        v = v + (vf_ref[...] - v) * gate
    v_ref[...] = v
    a = jax.nn.sigmoid(a0_ref[...] + _dot(_dot(xa, a1_ref[...]).astype(BF16), a2_ref[...]))
    gg_ref[...] = _dot(jax.nn.sigmoid(_dot(xg, g1_ref[...])).astype(BF16), g2_ref[...])
    kkf = k * kkw_ref[...]
    red = _head_reduce()
    expand = _head_expand()
    ss = _dot01x2(kkf * kkf, jnp.concatenate([red, red], axis=0))
    kk = kkf * _dot01x2(lax.rsqrt(jnp.maximum(ss, 1e-24)), jnp.concatenate([expand, expand], axis=0))
    k_ref[...] = k * (1.0 + (a - 1.0) * kaw_ref[...])
    an_ref[...] = -kk
    b_ref[...] = kk * a


def _odd_pre(x, prev_or_shift0, vfirst, P, i, tm, L):
    T = x.shape[0]
    step = L == 1
    has_vfirst = vfirst is not None
    nblk_seq = max(L // tm, 1)
    hrows = tm if step else 8
    tok = pl.BlockSpec((tm, D_MODEL), lambda t: (t, 0))
    par = lambda r, n, idx=i: pl.BlockSpec((None, r, n), lambda t: (idx, 0, 0), pipeline_mode=pl.Buffered(1))
    args = [x]
    specs = [tok]
    if step:
        args.append(prev_or_shift0)
        specs.append(tok)
    else:
        args += [x, prev_or_shift0]
        specs += [pl.BlockSpec((8, D_MODEL), lambda t: (jnp.maximum(t * (tm // 8) - 1, 0), 0)),
                  pl.BlockSpec((None, 1, D_MODEL), lambda t: (t // nblk_seq, 0, 0))]
    args += [P["mix_norm"], P["mix"], P["w_r"], P["w_k"], P["w_v"], P["w0"], P["w1"], P["w2"],
             P["a0"], P["a1"], P["a2"]]
    specs += [pl.BlockSpec((None, 1, D_MODEL), lambda t: (2 * i + 1, 0, 0)),
              par(8, D_MODEL), par(D_MODEL, D_MODEL), par(D_MODEL, D_MODEL), par(D_MODEL, D_MODEL),
              par(1, D_MODEL), par(D_MODEL, LANES), par(LANES, D_MODEL),
              par(1, D_MODEL), par(D_MODEL, LANES), par(LANES, D_MODEL)]
    if has_vfirst:
        args += [vfirst, P["v0"], P["v1"], P["v2"]]
        specs += [tok, par(1, D_MODEL, i - 1), par(D_MODEL, LANES, i - 1), par(LANES, D_MODEL, i - 1)]
    args += [P["g1"], P["g2"], P["k_k"], P["k_a"]]
    specs += [par(D_MODEL, 2 * LANES), par(2 * LANES, D_MODEL), par(1, D_MODEL), par(1, D_MODEL)]
    return pl.pallas_call(
        functools.partial(_odd_pre_body, step=step, has_vfirst=has_vfirst, nblk_seq=nblk_seq, tm=tm),
        grid=(T // tm,),
        in_specs=specs,
        out_specs=[pl.BlockSpec((None, hrows, D_MODEL), lambda t: (t, 0, 0))] + [tok] * (_ODD_OUTS - 1),
        out_shape=[jax.ShapeDtypeStruct((T // tm, hrows, D_MODEL), F32)]
        + [jax.ShapeDtypeStruct((T, D_MODEL), F32)] * (_ODD_OUTS - 1),
        compiler_params=_cp(("parallel",)),
        name="odd_pre",
    )(*args)


def _dot01x2(x, e2):
    hi = x.astype(BF16)
    mid = (x - hi.astype(F32)).astype(BF16)
    return _dot(jnp.concatenate([hi, mid], axis=1), e2)


def _odd_post_body(x_ref, o_ref, r_ref, k_ref, v_ref, g_ref, gnw_ref, gnb_ref, rk_ref, wo_ref, out_ref):
    red = _head_reduce()
    red = jnp.concatenate([red, red], axis=0)
    expand = _head_expand()
    expand = jnp.concatenate([expand, expand], axis=0)
    o = o_ref[...]
    mu = _dot01x2(_dot01x2(o, red) * (1.0 / RWKV_HEAD), expand)
    d = o - mu
    var = _dot01x2(d * d, red) * (1.0 / RWKV_HEAD)
    on = d * _dot01x2(lax.rsqrt(var + RWKV_GN_EPS), expand) * gnw_ref[...] + gnb_ref[...]
    v = v_ref[...]
    bonus = _dot01x2(_dot01x2(r_ref[...] * k_ref[...] * rk_ref[...], red), expand) * v
    out_ref[...] = x_ref[...] + _dot(((on + bonus) * g_ref[...]).astype(BF16), wo_ref[...])


def _odd_post(x, o, r, k, v, g, P, i, tm):
    T = x.shape[0]
    tok = pl.BlockSpec((tm, D_MODEL), lambda t: (t, 0))
    par = lambda r_, n: pl.BlockSpec((None, r_, n), lambda t: (i, 0, 0))
    return pl.pallas_call(
        _odd_post_body,
        grid=(T // tm,),
        in_specs=[tok] * 6 + [par(1, D_MODEL), par(1, D_MODEL), par(1, D_MODEL), par(D_MODEL, D_MODEL)],
        out_specs=tok,
        out_shape=jax.ShapeDtypeStruct((T, D_MODEL), F32),
        compiler_params=_cp(("parallel",)),
        name="odd_post",
    )(x, o, r, k, v, g, P["gn_w"], P["gn_b"], P["r_k"], P["w_o"])


RWKV_TB = 64
RWKV_NPAIR = RWKV_HEADS // 2


RWKV_SEQS = 2
RWKV_SPLIT = 1
RWKV_UNROLL = 8


def _rwkv_seq_body(w_ref, k_ref, a_ref, b_ref, r_ref, v_ref, o_ref, sn_ref,
                   s_ref, vt_ref, oacc_ref, ap_ref, bp_ref, kp_ref, rp_ref, gl_ref, *, ntb):
    N = RWKV_HEAD
    R = RWKV_NPAIR * N
    tbi = pl.program_id(1)

    @pl.when(tbi == 0)
    def _():
        s_ref[...] = jnp.zeros_like(s_ref)

    seg = _onehot((_iota((2 * LANES, 2 * LANES), 0) >> HEAD_SHIFT) == (_iota((2 * LANES, 2 * LANES), 1) >> HEAD_SHIFT))

    def seg_sum(xb):
        h = xb.shape[0] // 2
        both = _dot(jnp.concatenate([xb[:h], xb[h:]], axis=1), seg)
        return jnp.concatenate([both[:, :LANES], both[:, LANES:]], axis=0)
    incl = _onehot(_iota((RWKV_TB, RWKV_TB), 0) >= _iota((RWKV_TB, RWKV_TB), 1))
    for c in range(RWKV_SEQS):
        logw = jnp.log(w_ref[c])
        cum = _dot01_l(incl, logw)
        g_inc = jnp.exp(cum)
        g_inv = jnp.exp(-cum)
        ap_ref[c] = a_ref[c] * jnp.exp(cum - logw)
        bp_ref[c] = b_ref[c] * g_inv
        kp_ref[c] = k_ref[c] * g_inv
        rp_ref[c] = r_ref[c] * g_inc
        gl_ref[c] = g_inc[RWKV_TB - 8:RWKV_TB, :]
        for p in range(RWKV_NPAIR):
            bt = v_ref[c, :, p * LANES:(p + 1) * LANES].T
            vt_ref[c, p * N:(p + 1) * N, :] = jnp.concatenate([bt[0:N], bt[N:2 * N]], axis=1)
    oacc_ref[...] = jnp.zeros_like(oacc_ref)
    G = R // RWKV_SPLIT
    OSTEPS = LANES // 4
    seg_out = _onehot((_iota((2 * LANES, LANES), 0) >> HEAD_SHIFT) == (_iota((2 * LANES, LANES), 1) >> (OSTEPS.bit_length() - 1)))
    lane_copy = _iota((R // 2, LANES), 1) & (OSTEPS - 1)
    lane_base = (_iota((G, LANES), 1) >> HEAD_SHIFT) * N

    def steps(tu, carry):
        row0 = pl.multiple_of(tu * RWKV_UNROLL, RWKV_UNROLL)
        for j in range(RWKV_UNROLL):
            t = tu * RWKV_UNROLL + j
            for c in range(RWKV_SEQS):
                for g in range(RWKV_SPLIT):
                    rs = slice(g * G, (g + 1) * G)

                    def rows(ref):
                        return jnp.concatenate(
                            [jnp.broadcast_to(
                                ref[c, pl.ds(row0, RWKV_UNROLL), p * LANES:(p + 1) * LANES][j:j + 1, :], (N, LANES))
                             for p in range(g * G // N, (g + 1) * G // N)], axis=0)

                    s = s_ref[c, rs, :]
                    sa = seg_sum((s * rows(ap_ref)).astype(BF16))
                    vcol = jnp.take_along_axis(vt_ref[c, rs, :], lane_base + t, axis=1)
                    s = (s + vcol * rows(kp_ref)) + sa * rows(bp_ref)
                    s_ref[c, rs, :] = s
                    xr = (s * rows(rp_ref)).astype(BF16)
                    ob = _dot(jnp.concatenate([xr[:R // 2], xr[R // 2:]], axis=1), seg_out)
                    hb = tu // (OSTEPS // RWKV_UNROLL)
                    oacc_ref[c, hb] = jnp.where(lane_copy == (t & (OSTEPS - 1)), ob, oacc_ref[c, hb])
        return carry

    lax.fori_loop(0, RWKV_TB // RWKV_UNROLL, steps, 0)

    for c in range(RWKV_SEQS):
        s_ref[c] = s_ref[c] * jnp.concatenate(
            [jnp.broadcast_to(gl_ref[c, 7:8, p * LANES:(p + 1) * LANES], (N, LANES)) for p in range(RWKV_NPAIR)],
            axis=0)
        for hb in range(RWKV_TB // OSTEPS):
            ot = oacc_ref[c, hb].T
            for gh in range(4):
                g, hh = divmod(gh, 2)
                for q in range(RWKV_NPAIR // 2):
                    lane0 = (g * (RWKV_NPAIR // 2) + q) * LANES + hh * N
                    o_ref[c, hb * OSTEPS:(hb + 1) * OSTEPS, lane0:lane0 + N] = (
                        ot[gh * OSTEPS:(gh + 1) * OSTEPS, q * N:(q + 1) * N])

    @pl.when(tbi == ntb - 1)
    def _():
        for c in range(RWKV_SEQS):
            for p in range(RWKV_NPAIR):
                for hh in range(2):
                    sn_ref[c, (2 * p + hh) * N:(2 * p + hh + 1) * N, :] = (
                        s_ref[c, p * N:(p + 1) * N, hh * N:(hh + 1) * N])


def _rwkv_seq(w, k, a, b, r, v, B, L):
    ntb = L // RWKV_TB
    nc = RWKV_SEQS
    as_seqs = lambda x: x.reshape(B, L, D_MODEL)
    tok = pl.BlockSpec((nc, RWKV_TB, D_MODEL), lambda bi, t: (bi, t, 0))
    state_rows = RWKV_NPAIR * RWKV_HEAD
    o, sn = pl.pallas_call(
        functools.partial(_rwkv_seq_body, ntb=ntb),
        grid=(B // nc, ntb),
        in_specs=[tok] * 6,
        out_specs=[tok, pl.BlockSpec((nc, D_MODEL, RWKV_HEAD), lambda bi, t: (bi, 0, 0))],
        out_shape=[jax.ShapeDtypeStruct((B, L, D_MODEL), F32),
                   jax.ShapeDtypeStruct((B, D_MODEL, RWKV_HEAD), F32)],
        scratch_shapes=[pltpu.VMEM((nc, state_rows, LANES), F32)] * 2
        + [pltpu.VMEM((nc, 2, state_rows // 2, LANES), F32)]
        + [pltpu.VMEM((nc, RWKV_TB, D_MODEL), F32)] * 4
        + [pltpu.VMEM((nc, 8, D_MODEL), F32)],
        compiler_params=_cp(("parallel", "arbitrary")),
        name="rwkv_seq",
    )(as_seqs(w), as_seqs(k), as_seqs(a), as_seqs(b), as_seqs(r), as_seqs(v))
    return o.reshape(B * L, D_MODEL), sn.reshape(B, RWKV_HEADS, RWKV_HEAD, RWKV_HEAD)


def _rwkv_step_body(s_ref, w_ref, k_ref, a_ref, b_ref, r_ref, vc_ref, *rest):
    sn_ref, o_ref = rest[-2:]
    N = RWKV_HEAD
    lane = _iota((D_MODEL, STEP_BATCH), 1)
    ocols = jnp.zeros((D_MODEL, STEP_BATCH), F32)
    vc = vc_ref[...]

    def rows(ref, b):
        return jnp.concatenate(
            [jnp.broadcast_to(ref[b:b + 1, h * N:(h + 1) * N], (N, N)) for h in range(RWKV_HEADS)], axis=0)

    for b in range(STEP_BATCH):
        s = s_ref[b]
        sa = jnp.sum(s * rows(a_ref, b), axis=1, keepdims=True)
        s = s * rows(w_ref, b) + sa * rows(b_ref, b) + vc[:, b:b + 1] * rows(k_ref, b)
        sn_ref[b] = s
        ocol = jnp.sum(s * rows(r_ref, b), axis=1, keepdims=True)
        ocols = jnp.where(lane == b, ocol, ocols)
    o_ref[...] = ocols


def _rwkv_step(state_all, acc, i, w, k, a, b, r, v):
    Bn = state_all.shape[1]
    bb = STEP_BATCH
    col = pl.BlockSpec((None, D_MODEL, bb), lambda t: (t, 0, 0))
    row = pl.BlockSpec((bb, D_MODEL), lambda t: (t, 0))
    acc, ocols = _layer_state_call(
        _rwkv_step_body, "rwkv_step", state_all, acc, i, D_MODEL, RWKV_HEAD,
        [w, k, a, b, r, _to_cols(v)], [row] * 5 + [col],
        col, jax.ShapeDtypeStruct((Bn // bb, D_MODEL, bb), F32))
    return _from_cols(ocols), acc


def _pad_to(w, axis, n):
    pad = [(0, 0)] * w.ndim
    pad[axis] = (0, n - w.shape[axis])
    return jnp.pad(w, pad)


def _prepare(W):
    bf = lambda w: w.astype(BF16)
    row = lambda w: w[:, None, :]
    sizes = [SSD_INNER, SSD_CONV_DIM, SSD_HEADS, GLA_KEY, GLA_KEY, GLA_VAL, GLA_VAL, GLA_GATE_RANK]
    offs = [0]
    for s in sizes:
        offs.append(offs[-1] + s)
    piece = lambda n: W["ev_w_in"][:, :, offs[n]:offs[n + 1]]
    w_in = jnp.concatenate([piece(0), piece(1), piece(3), piece(4), piece(5), piece(6),
                            _pad_to(jnp.concatenate([piece(2), piece(7)], axis=-1), 2, SMALL)], axis=-1)
    gate_w2 = jnp.pad(W["ev_gla_gate_w2"], ((0, 0), (SSD_HEADS, SMALL - SSD_HEADS - GLA_GATE_RANK), (0, 0)))
    even = dict(
        w_in=bf(w_in), w_out=bf(W["ev_w_out"]),
        conv_w=W["ev_conv_w"], conv_b=row(W["ev_conv_b"]),
        dt_bias=row(_pad_to(W["ev_dt_bias"], 1, SMALL)), a_log=row(_pad_to(W["ev_a_log"], 1, SMALL)),
        d_skip_x=row(jnp.repeat(W["ev_d_skip"], SSD_HEAD_DIM, axis=1)),
        ssd_norm=row(W["ev_ssd_norm"]), gate_w2=bf(gate_w2), gate_b=row(W["ev_gla_gate_b"]),
        gla_norm=row(W["ev_gla_norm"]),
    )
    odd = dict(
        mix_norm=row(W["mix_norm"]), mix=_pad_to(W["od_mix"], 1, 8),
        w_r=bf(W["od_w_r"]), w_k=bf(W["od_w_k"]), w_v=bf(W["od_w_v"]), w_o=bf(W["od_w_o"]),
        w0=row(W["od_w0"]), w1=bf(_pad_to(W["od_w1"], 2, LANES)), w2=bf(_pad_to(W["od_w2"], 1, LANES)),
        a0=row(W["od_a0"]), a1=bf(_pad_to(W["od_a1"], 2, LANES)), a2=bf(_pad_to(W["od_a2"], 1, LANES)),
        v0=row(W["od_v0"]), v1=bf(_pad_to(W["od_v1"], 2, LANES)), v2=bf(_pad_to(W["od_v2"], 1, LANES)),
        g1=bf(_pad_to(W["od_g1"], 2, 2 * LANES)), g2=bf(_pad_to(W["od_g2"], 1, 2 * LANES)),
        k_k=row(W["od_k_k"]), k_a=row(W["od_k_a"]), r_k=row(W["od_r_k"]),
        gn_w=row(W["od_gn_w"]), gn_b=row(W["od_gn_b"]),
    )
    ffn = dict(
        norm=W["ffn_norm"].reshape(DEPTH * 2, 1, D_MODEL),
        w_gu=bf(W["ffn_w_gu"]).reshape(DEPTH * 2, D_MODEL, 2 * D_FF),
        w_down=bf(W["ffn_w_down"]).reshape(DEPTH * 2, D_FF, D_MODEL),
    )
    return dict(even=even, odd=odd, ffn=ffn, mix_norm=row(W["mix_norm"]),
                final_norm=W["final_norm"][None, :])


def _run_group(x3, states, Wp):
    B, L, _ = x3.shape
    T = B * L
    step = L == 1
    x = x3.reshape(T, D_MODEL)
    tm_ffn = min(512, T)
    tm_proj = min(256, T)
    ffn, even, odd = Wp["ffn"], Wp["even"], Wp["odd"]
    convs, ssms, glas, shifts, wkvs = [], [], [], [], []
    ssm_acc = gla_acc = wkv_acc = None
    v_first = None
    for layer in range(DEPTH):
        i = layer // 2
        mix = None
        x = _ffn(x, ffn["norm"], ffn["w_gu"], ffn["w_down"], 2 * layer, tm_ffn)
        if layer % 2 == 0:
            z, xbc, q, k, v, og, sm = _even_in(x, Wp["mix_norm"], even["w_in"], i, tm_ffn)
            if step:
                xact, convn, dax, c2, ybase, eg = _even_step_pre(
                    xbc, states[0][i].reshape(B, 3 * SSD_CONV_DIM), sm, even, i)
                ssm_acc, yi = _ssd_step(states[1], ssm_acc, i, dax, c2, xact)
                gla_acc, o = _gla_step(states[2], gla_acc, i, eg, k, q, v)
                x = _even_step_post(x, ybase, yi, dax, z, o, og, even, even["w_out"], i)
                convn = convn.reshape(B, SSD_CONV - 1, SSD_CONV_DIM)
            else:
                y, convn, ssmn = _ssd_seq(xbc, sm, z, even, i, B, L)
                o, glan = _gla_seq(q, k, v, og, sm, even, i, B, L)
                mix = (y, o, even["w_out"], i)
                ssms.append(ssmn)
                glas.append(glan)
            convs.append(convn)
        else:
            prev = states[3][i] if step else jnp.zeros((B, 1, D_MODEL), F32)
            h, r, dec, k, v, an, bb, gg = _odd_pre(x, prev, v_first, odd, i, tm_proj, L)
            if v_first is None:
                v_first = v
            if step:
                o, wkv_acc = _rwkv_step(states[4], wkv_acc, i, dec, k, an, bb, r, v)
            else:
                o, wkvn = _rwkv_seq(dec, k, an, bb, r, v, B, L)
                wkvs.append(wkvn)
            x = _odd_post(x, o, r, k, v, gg, odd, i, tm_ffn)
            shifts.append(h.reshape(B, -1, D_MODEL)[:, -1])
        x = _ffn(x, ffn["norm"], ffn["w_gu"], ffn["w_down"], 2 * layer + 1, tm_ffn,
                 final_g=Wp["final_norm"] if layer == DEPTH - 1 else None, mix=mix)
    y = x.reshape(B, L, D_MODEL)
    if step:
        ssm_out = ssm_acc.reshape(N_EVEN, B, SSD_HEADS, SSD_HEAD_DIM, SSD_D_STATE)
        gla_out = gla_acc.reshape(N_EVEN, B, GLA_HEADS, GLA_DK, GLA_DV)
        wkv_out = wkv_acc.reshape(N_ODD, B, RWKV_HEADS, RWKV_HEAD, RWKV_HEAD)
    else:
        ssm_out, gla_out, wkv_out = jnp.stack(ssms), jnp.stack(glas), jnp.stack(wkvs)
    return y, jnp.stack(convs), ssm_out, gla_out, jnp.stack(shifts), wkv_out


def kernel(x_prompt, x_sample, state_conv, state_ssm, state_gla, state_shift, state_wkv, ffn_norm, ffn_w_gu, ffn_w_down, mix_norm, final_norm, ev_w_in, ev_conv_w, ev_conv_b, ev_dt_bias, ev_a_log, ev_d_skip, ev_ssd_norm, ev_gla_gate_w2, ev_gla_gate_b, ev_gla_norm, ev_w_out, od_mix, od_w0, od_w1, od_w2, od_a0, od_a1, od_a2, od_v0, od_v1, od_v2, od_g1, od_g2, od_k_k, od_k_a, od_r_k, od_w_r, od_w_k, od_w_v, od_w_o, od_gn_w, od_gn_b):
    W = dict(ffn_norm=ffn_norm, ffn_w_gu=ffn_w_gu, ffn_w_down=ffn_w_down, mix_norm=mix_norm,
             final_norm=final_norm, ev_w_in=ev_w_in, ev_conv_w=ev_conv_w, ev_conv_b=ev_conv_b,
             ev_dt_bias=ev_dt_bias, ev_a_log=ev_a_log, ev_d_skip=ev_d_skip, ev_ssd_norm=ev_ssd_norm,
             ev_gla_gate_w2=ev_gla_gate_w2, ev_gla_gate_b=ev_gla_gate_b, ev_gla_norm=ev_gla_norm,
             ev_w_out=ev_w_out, od_mix=od_mix, od_w0=od_w0, od_w1=od_w1, od_w2=od_w2,
             od_a0=od_a0, od_a1=od_a1, od_a2=od_a2, od_v0=od_v0, od_v1=od_v1, od_v2=od_v2,
             od_g1=od_g1, od_g2=od_g2, od_k_k=od_k_k, od_k_a=od_k_a, od_r_k=od_r_k,
             od_w_r=od_w_r, od_w_k=od_w_k, od_w_v=od_w_v, od_w_o=od_w_o,
             od_gn_w=od_gn_w, od_gn_b=od_gn_b)
    Wp = _prepare(W)
    prompt = _run_group(x_prompt, None, Wp)
    sample = _run_group(x_sample, (state_conv, state_ssm, state_gla, state_shift, state_wkv), Wp)
    return (prompt[0], sample[0]) + prompt[1:] + sample[1:]
```

```python
import functools

import jax
import jax.numpy as jnp
from jax import lax
from jax.experimental import pallas as pl
from jax.experimental.pallas import tpu as pltpu

F32 = jnp.float32
BF16 = jnp.bfloat16

D_MODEL = 1024
DEPTH = 4
N_EVEN = 2
N_ODD = 2
EPS = 1e-5
D_FF = 2816

SSD_HEADS = 16
SSD_HEAD_DIM = 64
SSD_INNER = 1024
SSD_GROUPS = 2
SSD_D_STATE = 128
SSD_CONV = 4
SSD_CONV_DIM = 1536
SSD_CHUNK = 64

GLA_HEADS = 4
GLA_DK = 128
GLA_DV = 256
GLA_KEY = 512
GLA_VAL = 1024
GLA_GATE_RANK = 16
GLA_GATE_NORM = 16.0
GLA_CHUNK = 16

RWKV_HEAD = 64
RWKV_HEADS = 16
RWKV_GN_EPS = 64e-5

HEAD_SHIFT = 6
GLA_CHUNK_SHIFT = 4
assert 1 << HEAD_SHIFT == SSD_HEAD_DIM == RWKV_HEAD == SSD_CHUNK and 1 << GLA_CHUNK_SHIFT == GLA_CHUNK

LANES = 128
SMALL = LANES
IN_PERM = SSD_INNER + SSD_CONV_DIM + 2 * GLA_KEY + 2 * GLA_VAL + SMALL
VMEM_LIMIT = 56 * 1024 * 1024
FF_TILE = 1408
SEQ_BLOCK = 256
STEP_BATCH = 8


def _cp(sem):
    return pltpu.CompilerParams(dimension_semantics=sem, vmem_limit_bytes=VMEM_LIMIT)


def _iota(shape, axis):
    return lax.broadcasted_iota(jnp.int32, shape, axis)


def _onehot(mask):
    return jnp.where(mask, 1.0, 0.0).astype(BF16)


def _dot(a, b):
    return jnp.dot(a, b, preferred_element_type=F32)


def _split3(x):
    hi = x.astype(BF16)
    r = x - hi.astype(F32)
    mid = r.astype(BF16)
    lo = (r - mid.astype(F32)).astype(BF16)
    return hi, mid, lo


def _dot01(x, e):
    hi, mid, lo = _split3(x)
    return _dot(hi, e) + _dot(mid, e) + _dot(lo, e)


def _dot01_l(e, x):
    hi, mid, lo = _split3(x)
    return _dot(e, hi) + _dot(e, mid) + _dot(e, lo)


def _rms(x, g):
    ms = jnp.mean(x * x, axis=-1, keepdims=True)
    return x * lax.rsqrt(ms + EPS) * g


def _silu(x):
    return x * jax.nn.sigmoid(x)


def _softplus(x):
    return jnp.maximum(x, 0.0) + jnp.log1p(jnp.exp(-jnp.abs(x)))


def _softplus_abs(x):
    return jnp.maximum(x, 0.0) + jnp.log(1.0 + jnp.exp(-jnp.abs(x)))


def _head_expand():
    return _onehot((_iota((LANES, D_MODEL), 1) >> HEAD_SHIFT) == _iota((LANES, D_MODEL), 0))


def _head_reduce():
    return _onehot((_iota((D_MODEL, LANES), 0) >> HEAD_SHIFT) == _iota((D_MODEL, LANES), 1))


def _ffn_body(x_ref, g_ref, wgu_ref, wd_ref, *rest, mix_out, final_norm):
    o_ref = rest[-1]
    x = x_ref[...]
    if mix_out:
        y_ref, mo_ref, wo_ref = rest[:3]
        x = x + _dot(y_ref[...].astype(BF16), wo_ref[0:SSD_INNER, :])
        x = x + _dot(mo_ref[...].astype(BF16), wo_ref[SSD_INNER:, :])
    xn = _rms(x, g_ref[...]).astype(BF16)
    acc = None
    for j in range(D_FF // FF_TILE):
        lo = j * FF_TILE
        gate = _dot(xn, wgu_ref[:, lo:lo + FF_TILE])
        up = _dot(xn, wgu_ref[:, D_FF + lo:D_FF + lo + FF_TILE])
        part = _dot((_silu(gate) * up).astype(BF16), wd_ref[lo:lo + FF_TILE, :])
        acc = part if acc is None else acc + part
    y = x + 0.5 * acc
    o_ref[...] = _rms(y, rest[-2][...]) if final_norm else y


def _ffn(x, nrm, wgu, wd, ls, tm, final_g=None, mix=None):
    T = x.shape[0]
    resident = pl.Buffered(1)
    tok = pl.BlockSpec((tm, D_MODEL), lambda i: (i, 0))
    args = [x, nrm, wgu, wd]
    specs = [
        tok,
        pl.BlockSpec((None, 1, D_MODEL), lambda i: (ls, 0, 0)),
        pl.BlockSpec((None, D_MODEL, 2 * D_FF), lambda i: (ls, 0, 0), pipeline_mode=resident),
        pl.BlockSpec((None, D_FF, D_MODEL), lambda i: (ls, 0, 0), pipeline_mode=resident),
    ]
    if mix is not None:
        y, o, w_out, li = mix
        args += [y, o, w_out]
        specs += [tok, tok, pl.BlockSpec((None, SSD_INNER + GLA_VAL, D_MODEL), lambda i: (li, 0, 0),
                                         pipeline_mode=resident)]
    if final_g is not None:
        args.append(final_g)
        specs.append(pl.BlockSpec((1, D_MODEL), lambda i: (0, 0)))
    return pl.pallas_call(
        functools.partial(_ffn_body, mix_out=mix is not None, final_norm=final_g is not None),
        grid=(T // tm,),
        in_specs=specs,
        out_specs=pl.BlockSpec((tm, D_MODEL), lambda i: (i, 0)),
        out_shape=jax.ShapeDtypeStruct((T, D_MODEL), F32),
        compiler_params=_cp(("parallel",)),
        name="ffn",
    )(*args)


_EVEN_PIECES = (SSD_INNER, SSD_CONV_DIM, GLA_KEY, GLA_KEY, GLA_VAL, GLA_VAL, SMALL)


def _even_in_body(x_ref, g_ref, w_ref, *out_refs):
    h = _rms(x_ref[...], g_ref[...]).astype(BF16)
    off = 0
    for ref, n in zip(out_refs, _EVEN_PIECES):
        ref[...] = _dot(h, w_ref[:, off:off + n])
        off += n


def _even_in(x, nrm, w, i, tm):
    T = x.shape[0]
    return pl.pallas_call(
        _even_in_body,
        grid=(T // tm,),
        in_specs=[
            pl.BlockSpec((tm, D_MODEL), lambda t: (t, 0)),
            pl.BlockSpec((None, 1, D_MODEL), lambda t: (2 * i, 0, 0)),
            pl.BlockSpec((None, D_MODEL, IN_PERM), lambda t: (i, 0, 0), pipeline_mode=pl.Buffered(1)),
        ],
        out_specs=[pl.BlockSpec((tm, n), lambda t: (t, 0)) for n in _EVEN_PIECES],
        out_shape=[jax.ShapeDtypeStruct((T, n), F32) for n in _EVEN_PIECES],
        compiler_params=_cp(("parallel",)),
        name="even_in",
    )(x, nrm, w)


def _ssd_seq_body(xbc_ref, sm_ref, z_ref, cw_ref, cb_ref, dtb_ref, alog_ref, dsk_ref, nw_ref,
                  y_ref, convn_ref, ssmn_ref,
                  cbuf_ref, xact_ref, acx_ref, dtx_ref, hT_ref, *, nblk):
    Lb = SEQ_BLOCK
    C = SSD_CHUNK
    j = pl.program_id(1)

    @pl.when(j == 0)
    def _():
        cbuf_ref[0:8, :] = jnp.zeros((8, SSD_CONV_DIM), F32)
        hT_ref[...] = jnp.zeros_like(hT_ref)

    x = xbc_ref[...]
    acc = cb_ref[...] + pltpu.roll(x, 3, axis=0) * cw_ref[0:1, :]
    acc = acc + pltpu.roll(x, 2, axis=0) * cw_ref[1:2, :]
    acc = acc + pltpu.roll(x, 1, axis=0) * cw_ref[2:3, :]
    acc = acc + x * cw_ref[3:4, :]
    xact_ref[...] = _silu(acc)
    cbuf_ref[8:16, :] = x[0:8, :]
    head = cb_ref[...] + cbuf_ref[5:13, :] * cw_ref[0:1, :]
    head = head + cbuf_ref[6:14, :] * cw_ref[1:2, :]
    head = head + cbuf_ref[7:15, :] * cw_ref[2:3, :]
    head = head + cbuf_ref[8:16, :] * cw_ref[3:4, :]
    xact_ref[0:8, :] = _silu(head)
    tail = x[Lb - 8:Lb, :]
    cbuf_ref[0:8, :] = tail

    @pl.when(j == nblk - 1)
    def _():
        convn_ref[...] = tail[5:8, :]

    dt = _softplus(sm_ref[...] + dtb_ref[...])
    dta = dt * (-jnp.exp(alog_ref[...]))
    ii = _iota((Lb, Lb), 0)
    jj = _iota((Lb, Lb), 1)
    tril = _onehot(((ii >> HEAD_SHIFT) == (jj >> HEAD_SHIFT)) & (ii >= jj))
    acum = _dot01_l(tril, dta)
    expand = _head_expand()
    acx_ref[...] = _dot01(acum, expand)
    dtx_ref[...] = _dot01(dt, expand)

    li = _iota((C, D_MODEL), 0)
    lj = _iota((C, D_MODEL), 1) & (C - 1)
    diag = li == lj
    causal = li >= lj
    ones_c = jnp.ones((C, C), BF16)
    ones_c2 = jnp.ones((C, 2 * C), BF16)
    tile8 = _onehot((_iota((2 * C, 8 * C), 1) & (C - 1)) == (_iota((2 * C, 8 * C), 0) & (C - 1)))
    pair_mask = (_iota((LANES, LANES), 0) >> HEAD_SHIFT) == (_iota((LANES, LANES), 1) >> HEAD_SHIFT)
    half = SSD_INNER // SSD_GROUPS

    def chunk(c, carry):
        r0 = pl.multiple_of(c * C, C)
        xa = xact_ref[pl.ds(r0, C), :]
        xs = xa[:, :SSD_INNER]
        bms = [xa[:, SSD_INNER + g * SSD_D_STATE:SSD_INNER + (g + 1) * SSD_D_STATE] for g in range(2)]
        cms = [xa[:, SSD_INNER + (2 + g) * SSD_D_STATE:SSD_INNER + (3 + g) * SSD_D_STATE]
               for g in range(2)]
        acx = acx_ref[pl.ds(r0, C), :]
        dtx = dtx_ref[pl.ds(r0, C), :]
        alast = acx[C - 1:C, :]
        arow = _dot01_l(ones_c, jnp.where(diag, acx, 0.0))
        dsel = jnp.where(diag, dtx, 0.0)
        dhi = dsel.astype(BF16)
        dmid = (dsel - dhi.astype(F32)).astype(BF16)
        dtrow = _dot(ones_c2, jnp.concatenate([dhi, dmid], axis=0))
        decay = jnp.exp(jnp.where(causal, acx - arow, -jnp.inf))
        cbt = []
        for g in range(2):
            cb = lax.dot_general(cms[g].astype(BF16), bms[g].astype(BF16),
                                 (((1,), (1,)), ((), ())), preferred_element_type=F32)
            cbt.append(_dot01x2(cb, tile8))
        wts = (jnp.concatenate(cbt, axis=1) * decay * dtrow).astype(BF16)
        ys = []
        for p in range(SSD_HEADS // 2):
            xp = xs[:, p * LANES:(p + 1) * LANES]
            xbd = jnp.where(pair_mask, jnp.concatenate([xp, xp], axis=0), 0.0).astype(BF16)
            ys.append(_dot(wts[:, p * LANES:(p + 1) * LANES], xbd))
        y = jnp.concatenate(ys, axis=1)
        hT = hT_ref[...]
        hTb = hT.astype(BF16)
        yi = [_dot(cms[g].astype(BF16), hTb[:, g * half:(g + 1) * half]) for g in range(2)]
        y = y + jnp.concatenate(yi, axis=1) * jnp.exp(acx)
        xsc = (jnp.exp(alast - acx) * dtx * xs).astype(BF16)
        st = [lax.dot_general(bms[g].astype(BF16), xsc[:, g * half:(g + 1) * half],
                              (((0,), (0,)), ((), ())), preferred_element_type=F32) for g in range(2)]
        hT_ref[...] = hT * jnp.exp(alast) + jnp.concatenate(st, axis=1)
        y = y + dsk_ref[...] * xs
        y = y * _silu(z_ref[pl.ds(r0, C), :])
        outs = []
        for g in range(2):
            yg = y[:, g * half:(g + 1) * half]
            ms = jnp.mean(yg * yg, axis=-1, keepdims=True)
            outs.append(yg * lax.rsqrt(ms + EPS))
        y_ref[pl.ds(r0, C), :] = jnp.concatenate(outs, axis=1) * nw_ref[...]
        return carry

    lax.fori_loop(0, Lb // C, chunk, 0, unroll=4)

    @pl.when(j == nblk - 1)
    def _():
        ssmn_ref[...] = hT_ref[...].T


def _ssd_seq(xbc, sm, z, P, i, B, L):
    Lb = SEQ_BLOCK
    nblk = L // Lb
    T = B * L
    tok = lambda n: pl.BlockSpec((Lb, n), lambda b, j: (b * nblk + j, 0))
    par = lambda r, n: pl.BlockSpec((None, r, n), lambda b, j: (i, 0, 0))
    y, convn, ssmn = pl.pallas_call(
        functools.partial(_ssd_seq_body, nblk=nblk),
        grid=(B, nblk),
        in_specs=[tok(SSD_CONV_DIM), tok(SMALL), tok(SSD_INNER),
                  par(SSD_CONV, SSD_CONV_DIM), par(1, SSD_CONV_DIM), par(1, SMALL), par(1, SMALL),
                  par(1, SSD_INNER), par(1, SSD_INNER)],
        out_specs=[tok(SSD_INNER),
                   pl.BlockSpec((None, SSD_CONV - 1, SSD_CONV_DIM), lambda b, j: (b, 0, 0)),
                   pl.BlockSpec((None, SSD_INNER, SSD_D_STATE), lambda b, j: (b, 0, 0))],
        out_shape=[jax.ShapeDtypeStruct((T, SSD_INNER), F32),
                   jax.ShapeDtypeStruct((B, SSD_CONV - 1, SSD_CONV_DIM), F32),
                   jax.ShapeDtypeStruct((B, SSD_INNER, SSD_D_STATE), F32)],
        scratch_shapes=[pltpu.VMEM((16, SSD_CONV_DIM), F32),
                        pltpu.VMEM((Lb, SSD_CONV_DIM), F32),
                        pltpu.VMEM((Lb, SSD_INNER), F32),
                        pltpu.VMEM((Lb, SSD_INNER), F32),
                        pltpu.VMEM((SSD_D_STATE, SSD_INNER), F32)],
        compiler_params=_cp(("parallel", "arbitrary")),
        name="ssd_seq",
    )(xbc, sm, z, P["conv_w"], P["conv_b"], P["dt_bias"], P["a_log"], P["d_skip_x"], P["ssd_norm"])
    return y, convn, ssmn.reshape(B, SSD_HEADS, SSD_HEAD_DIM, SSD_D_STATE)


def _gla_seq_body(q_ref, k_ref, v_ref, og_ref, sm_ref, w2_ref, gb_ref, nw_ref,
                  o_ref, glan_ref,
                  st_ref, dec_ref, qg_ref, kd_ref, oi_ref, *, nblk):
    Lb = SEQ_BLOCK
    j = pl.program_id(1)

    @pl.when(j == 0)
    def _():
        st_ref[...] = jnp.zeros_like(st_ref)

    gate = _dot(sm_ref[...].astype(BF16), w2_ref[...]) + gb_ref[...]
    lg = -_softplus_abs(-gate) / GLA_GATE_NORM
    ii = _iota((Lb, Lb), 0)
    jj = _iota((Lb, Lb), 1)
    same = (ii >> GLA_CHUNK_SHIFT) == (jj >> GLA_CHUNK_SHIFT)
    causal = same & (ii >= jj)
    lg_hi = lg.astype(BF16)
    lg2 = jnp.concatenate([lg_hi, (lg - lg_hi.astype(F32)).astype(BF16)], axis=0)
    sel_c = _onehot(causal)
    sel_s = _onehot(same)
    b = _dot(jnp.concatenate([sel_c, sel_c], axis=1), lg2)
    blast = _dot(jnp.concatenate([sel_s, sel_s], axis=1), lg2)
    qg = q_ref[...] * (GLA_DK ** -0.5) * jnp.exp(b)
    kk = k_ref[...]
    kg = kk * jnp.exp(-b)
    kd = kk * jnp.exp(blast - b)
    dec_ref[...] = jnp.exp(blast)
    qg_ref[...] = qg.astype(BF16)
    kd_ref[...] = kd.astype(BF16)

    def chunk(c, carry):
        r0 = pl.multiple_of(c * GLA_CHUNK, GLA_CHUNK)
        for h in range(GLA_HEADS):
            ks = slice(h * GLA_DK, (h + 1) * GLA_DK)
            vs = slice(h * GLA_DV, (h + 1) * GLA_DV)
            st = st_ref[h]
            oi_ref[pl.ds(r0, GLA_CHUNK), vs] = lax.dot_general(
                qg_ref[pl.ds(r0, GLA_CHUNK), ks], st.astype(BF16),
                (((1,), (1,)), ((), ())), preferred_element_type=F32)
            upd = lax.dot_general(v_ref[pl.ds(r0, GLA_CHUNK), vs].astype(BF16),
                                  kd_ref[pl.ds(r0, GLA_CHUNK), ks],
                                  (((0,), (0,)), ((), ())), preferred_element_type=F32)
            st_ref[h] = st * dec_ref[pl.ds(r0, 8), ks][0:1, :] + upd
        return carry

    lax.fori_loop(0, Lb // GLA_CHUNK, chunk, 0, unroll=16)

    for h in range(GLA_HEADS):
        ks = slice(h * GLA_DK, (h + 1) * GLA_DK)
        vs = slice(h * GLA_DV, (h + 1) * GLA_DV)
        att = lax.dot_general(qg[:, ks].astype(BF16), kg[:, ks].astype(BF16),
                              (((1,), (1,)), ((), ())), preferred_element_type=F32)
        att = jnp.where(causal, att, 0.0)
        o_h = _dot(att.astype(BF16), v_ref[:, vs].astype(BF16)) + oi_ref[:, vs]
        ms = jnp.mean(o_h * o_h, axis=-1, keepdims=True)
        o_h = o_h * lax.rsqrt(ms + EPS) * nw_ref[...]
        o_ref[:, vs] = o_h * _silu(og_ref[:, vs])

    @pl.when(j == nblk - 1)
    def _():
        for h in range(GLA_HEADS):
            glan_ref[h] = st_ref[h].T


def _gla_seq(q, k, v, og, sm, P, i, B, L):
    Lb = SEQ_BLOCK
    nblk = L // Lb
    T = B * L
    tok = lambda n: pl.BlockSpec((Lb, n), lambda b, j: (b * nblk + j, 0))
    par = lambda r, n: pl.BlockSpec((None, r, n), lambda b, j: (i, 0, 0))
    return pl.pallas_call(
        functools.partial(_gla_seq_body, nblk=nblk),
        grid=(B, nblk),
        in_specs=[tok(GLA_KEY), tok(GLA_KEY), tok(GLA_VAL), tok(GLA_VAL), tok(SMALL),
                  par(SMALL, GLA_KEY), par(1, GLA_KEY), par(1, GLA_DV)],
        out_specs=[tok(GLA_VAL),
                   pl.BlockSpec((None, GLA_HEADS, GLA_DK, GLA_DV), lambda b, j: (b, 0, 0, 0))],
        out_shape=[jax.ShapeDtypeStruct((T, GLA_VAL), F32),
                   jax.ShapeDtypeStruct((B, GLA_HEADS, GLA_DK, GLA_DV), F32)],
        scratch_shapes=[pltpu.VMEM((GLA_HEADS, GLA_DV, GLA_DK), F32),
                        pltpu.VMEM((Lb, GLA_KEY), F32),
                        pltpu.VMEM((Lb, GLA_KEY), BF16),
                        pltpu.VMEM((Lb, GLA_KEY), BF16),
                        pltpu.VMEM((Lb, GLA_VAL), F32)],
        compiler_params=_cp(("parallel", "arbitrary")),
        name="gla_seq",
    )(q, k, v, og, sm, P["gate_w2"], P["gate_b"], P["gla_norm"])


def _even_step_pre_body(xbc_ref, c0_ref, sm_ref, cw_ref, cb_ref, dtb_ref, alog_ref, dsk_ref,
                        w2_ref, gb_ref,
                        xact_ref, convn_ref, dax_ref, c2_ref, ybase_ref, eg_ref):
    u = xbc_ref[...]
    n = SSD_CONV_DIM
    acc = cb_ref[...] + c0_ref[:, 0:n] * cw_ref[0:1, :]
    acc = acc + c0_ref[:, n:2 * n] * cw_ref[1:2, :]
    acc = acc + c0_ref[:, 2 * n:3 * n] * cw_ref[2:3, :]
    acc = acc + u * cw_ref[3:4, :]
    xa = _silu(acc)
    xact_ref[...] = xa
    convn_ref[:, 0:n] = c0_ref[:, n:2 * n]
    convn_ref[:, n:2 * n] = c0_ref[:, 2 * n:3 * n]
    convn_ref[:, 2 * n:3 * n] = u
    dt = _softplus(sm_ref[...] + dtb_ref[...])
    dta = dt * (-jnp.exp(alog_ref[...]))
    expand = _head_expand()
    dtx = _dot01(dt, expand)
    dax_ref[...] = jnp.exp(_dot01(dta, expand))
    xs = xa[:, :SSD_INNER]
    c2_ref[...] = dtx * xs
    half = SSD_INNER // SSD_GROUPS
    cbs = []
    for g in range(2):
        bm = xa[:, SSD_INNER + g * SSD_D_STATE:SSD_INNER + (g + 1) * SSD_D_STATE]
        cm = xa[:, SSD_INNER + (2 + g) * SSD_D_STATE:SSD_INNER + (3 + g) * SSD_D_STATE]
        cb = jnp.sum(cm * bm, axis=-1, keepdims=True)
        cbs.append(jnp.broadcast_to(cb, (cb.shape[0], half)))
    ybase_ref[...] = jnp.concatenate(cbs, axis=1) * dtx * xs + dsk_ref[...] * xs
    gate = _dot(sm_ref[...].astype(BF16), w2_ref[...]) + gb_ref[...]
    eg_ref[...] = jnp.exp(-_softplus(-gate) / GLA_GATE_NORM)


def _even_step_pre(xbc, conv0, sm, P, i):
    Bn = xbc.shape[0]
    full = lambda n: pl.BlockSpec((Bn, n), lambda t: (0, 0))
    par = lambda r, n: pl.BlockSpec((None, r, n), lambda t: (i, 0, 0))
    outs = (SSD_CONV_DIM, 3 * SSD_CONV_DIM, SSD_INNER, SSD_INNER, SSD_INNER, GLA_KEY)
    return pl.pallas_call(
        _even_step_pre_body,
        grid=(1,),
        in_specs=[full(SSD_CONV_DIM), full(3 * SSD_CONV_DIM), full(SMALL),
                  par(SSD_CONV, SSD_CONV_DIM), par(1, SSD_CONV_DIM), par(1, SMALL), par(1, SMALL),
                  par(1, SSD_INNER), par(SMALL, GLA_KEY), par(1, GLA_KEY)],
        out_specs=[full(n) for n in outs],
        out_shape=[jax.ShapeDtypeStruct((Bn, n), F32) for n in outs],
        compiler_params=_cp(("arbitrary",)),
        name="even_step_pre",
    )(xbc, conv0, sm, P["conv_w"], P["conv_b"], P["dt_bias"], P["a_log"], P["d_skip_x"],
      P["gate_w2"], P["gate_b"])


def _to_cols(x):
    Bn, N = x.shape
    return x.reshape(Bn // STEP_BATCH, STEP_BATCH, N).transpose(0, 2, 1)


def _from_cols(x):
    nb, N, bb = x.shape
    return x.transpose(0, 2, 1).reshape(nb * bb, N)


def _layer_state_call(body, name, state_all, acc, i, rows, width, args, specs, out_spec, out_shape):
    nl, Bn = state_all.shape[:2]
    st = pl.BlockSpec((None, STEP_BATCH, rows, width), lambda t: (i, t, 0, 0))
    args = [state_all.reshape(nl, Bn, rows, width)] + list(args)
    specs = [st] + list(specs)
    aliases = {}
    if acc is not None:
        aliases = {len(args): 0}
        args.append(acc)
        specs.append(pl.BlockSpec(memory_space=pl.ANY))
    return pl.pallas_call(
        body,
        grid=(Bn // STEP_BATCH,),
        in_specs=specs,
        out_specs=[st, out_spec],
        out_shape=[jax.ShapeDtypeStruct((nl, Bn, rows, width), F32), out_shape],
        input_output_aliases=aliases,
        compiler_params=_cp(("parallel",)),
        name=name,
    )(*args)


def _ssd_step_body(h_ref, c1_ref, c2_ref, bm_ref, cm_ref, *rest):
    hn_ref, y_ref = rest[-2:]
    half = SSD_INNER // SSD_GROUPS
    lane = _iota((SSD_INNER, STEP_BATCH), 1)
    ycols = jnp.zeros((SSD_INNER, STEP_BATCH), F32)
    c1 = c1_ref[...]
    c2 = c2_ref[...]
    for b in range(STEP_BATCH):
        h0 = h_ref[b]
        bmat = jnp.concatenate(
            [jnp.broadcast_to(bm_ref[b:b + 1, g * SSD_D_STATE:(g + 1) * SSD_D_STATE],
                              (half, SSD_D_STATE)) for g in range(2)], axis=0)
        cmat = jnp.concatenate(
            [jnp.broadcast_to(cm_ref[b:b + 1, g * SSD_D_STATE:(g + 1) * SSD_D_STATE],
                              (half, SSD_D_STATE)) for g in range(2)], axis=0)
        ycol = jnp.sum(h0 * cmat, axis=1, keepdims=True)
        ycols = jnp.where(lane == b, ycol, ycols)
        hn_ref[b] = h0 * c1[:, b:b + 1] + c2[:, b:b + 1] * bmat
    y_ref[...] = ycols


def _ssd_step(state_all, acc, i, dax, c2, xact):
    Bn = state_all.shape[1]
    bb = STEP_BATCH
    bm = xact[:, SSD_INNER:SSD_INNER + 2 * SSD_D_STATE]
    cm = xact[:, SSD_INNER + 2 * SSD_D_STATE:]
    col = pl.BlockSpec((None, SSD_INNER, bb), lambda t: (t, 0, 0))
    row = pl.BlockSpec((bb, 2 * SSD_D_STATE), lambda t: (t, 0))
    acc, ycols = _layer_state_call(
        _ssd_step_body, "ssd_step", state_all, acc, i, SSD_INNER, SSD_D_STATE,
        [_to_cols(dax), _to_cols(c2), bm, cm], [col, col, row, row],
        col, jax.ShapeDtypeStruct((Bn // bb, SSD_INNER, bb), F32))
    return acc, _from_cols(ycols)


def _gla_step_body(s_ref, eg_ref, kc_ref, qc_ref, v_ref, *rest):
    sn_ref, o_ref = rest[-2:]
    eg = eg_ref[...]
    kc = kc_ref[...]
    qc = qc_ref[...] * (GLA_DK ** -0.5)
    for b in range(STEP_BATCH):
        s0 = s_ref[b]
        vmat = jnp.concatenate(
            [jnp.broadcast_to(v_ref[b:b + 1, h * GLA_DV:(h + 1) * GLA_DV], (GLA_DK, GLA_DV))
             for h in range(GLA_HEADS)], axis=0)
        sn = s0 * eg[:, b:b + 1] + kc[:, b:b + 1] * vmat
        sn_ref[b] = sn
        t = sn * qc[:, b:b + 1]
        o_ref[b:b + 1, :] = jnp.concatenate(
            [jnp.sum(t[h * GLA_DK:(h + 1) * GLA_DK], axis=0, keepdims=True)
             for h in range(GLA_HEADS)], axis=1)


def _gla_step(state_all, acc, i, eg, k, q, v):
    Bn = state_all.shape[1]
    bb = STEP_BATCH
    col = pl.BlockSpec((None, GLA_KEY, bb), lambda t: (t, 0, 0))
    row = pl.BlockSpec((bb, GLA_VAL), lambda t: (t, 0))
    return _layer_state_call(
        _gla_step_body, "gla_step", state_all, acc, i, GLA_KEY, GLA_DV,
        [_to_cols(eg), _to_cols(k), _to_cols(q), v], [col, col, col, row],
        row, jax.ShapeDtypeStruct((Bn, GLA_VAL), F32))


def _even_step_post_body(x_ref, ybase_ref, yi_ref, dax_ref, z_ref, o_ref, og_ref,
                         snw_ref, gnw_ref, w_ref, out_ref):
    y = (ybase_ref[...] + yi_ref[...] * dax_ref[...]) * _silu(z_ref[...])
    half = SSD_INNER // SSD_GROUPS
    acc = None
    for g in range(2):
        yg = y[:, g * half:(g + 1) * half]
        ms = jnp.mean(yg * yg, axis=-1, keepdims=True)
        yn = yg * lax.rsqrt(ms + EPS) * snw_ref[:, g * half:(g + 1) * half]
        part = _dot(yn.astype(BF16), w_ref[g * half:(g + 1) * half, :])
        acc = part if acc is None else acc + part
    for h in range(GLA_HEADS):
        vs = slice(h * GLA_DV, (h + 1) * GLA_DV)
        oh = o_ref[:, vs]
        ms = jnp.mean(oh * oh, axis=-1, keepdims=True)
        on = oh * lax.rsqrt(ms + EPS) * gnw_ref[...] * _silu(og_ref[:, vs])
        acc = acc + _dot(on.astype(BF16), w_ref[SSD_INNER + h * GLA_DV:SSD_INNER + (h + 1) * GLA_DV, :])
    out_ref[...] = x_ref[...] + acc


def _even_step_post(x, ybase, yi, dax, z, o, og, P, w_out, i):
    Bn = x.shape[0]
    full = pl.BlockSpec((Bn, D_MODEL), lambda t: (0, 0))
    return pl.pallas_call(
        _even_step_post_body,
        grid=(1,),
        in_specs=[full] * 7 + [pl.BlockSpec((None, 1, SSD_INNER), lambda t: (i, 0, 0)),
                               pl.BlockSpec((None, 1, GLA_DV), lambda t: (i, 0, 0)),
                               pl.BlockSpec((None, SSD_INNER + GLA_VAL, D_MODEL), lambda t: (i, 0, 0))],
        out_specs=full,
        out_shape=jax.ShapeDtypeStruct((Bn, D_MODEL), F32),
        compiler_params=_cp(("arbitrary",)),
        name="even_step_post",
    )(x, ybase, yi, dax, z, o, og, P["ssd_norm"], P["gla_norm"], w_out)


_ODD_OUTS = 8


def _odd_pre_body(*refs, step, has_vfirst, nblk_seq, tm):
    it = iter(refs)
    x_ref = next(it)
    if step:
        prev_ref = next(it)
    else:
        xp_ref = next(it)
        sh0_ref = next(it)
    g_ref, mix_ref = next(it), next(it)
    wr_ref, wk_ref, wv_ref = next(it), next(it), next(it)
    w0_ref, w1_ref, w2_ref = next(it), next(it), next(it)
    a0_ref, a1_ref, a2_ref = next(it), next(it), next(it)
    if has_vfirst:
        vf_ref, v0_ref, v1_ref, v2_ref = next(it), next(it), next(it), next(it)
    g1_ref, g2_ref = next(it), next(it)
    kkw_ref, kaw_ref = next(it), next(it)
    h_ref, r_ref, dec_ref, k_ref, v_ref, an_ref, b_ref, gg_ref = (next(it) for _ in range(_ODD_OUTS))

    h = _rms(x_ref[...], g_ref[...])
    if step:
        prev = prev_ref[...]
    else:
        i = pl.program_id(0)
        hp = _rms(xp_ref[...], g_ref[...])[7:8, :]
        prow = jnp.where(i % nblk_seq == 0, sh0_ref[...], hp)
        prev = jnp.where(_iota((tm, D_MODEL), 0) == 0, prow, pltpu.roll(h, 1, axis=0))
    h_ref[...] = h
    xx = prev - h
    mixed = lambda r: (h + xx * mix_ref[r:r + 1, :]).astype(BF16)
    xr, xw, xk, xv, xa, xg = (mixed(r) for r in range(6))
    r_ref[...] = _dot(xr, wr_ref[...])
    wl = w0_ref[...] + _dot(jnp.tanh(_dot(xw, w1_ref[...])).astype(BF16), w2_ref[...])
    w = -_softplus_abs(-wl) - 0.5
    dec_ref[...] = jnp.exp(-jnp.exp(w))
    k = _dot(xk, wk_ref[...])
    v = _dot(xv, wv_ref[...])
    if has_vfirst:
        gate = jax.nn.sigmoid(v0_ref[...] + _dot(_dot(xv, v1_ref[...]).astype(BF16), v2_ref[...]))
        v = v + (vf_ref[...] - v) * gate
    v_ref[...] = v
    a = jax.nn.sigmoid(a0_ref[...] + _dot(_dot(xa, a1_ref[...]).astype(BF16), a2_ref[...]))
    gg_ref[...] = _dot(jax.nn.sigmoid(_dot(xg, g1_ref[...])).astype(BF16), g2_ref[...])
    kkf = k * kkw_ref[...]
    red = _head_reduce()
    expand = _head_expand()
    ss = _dot01x2(kkf * kkf, jnp.concatenate([red, red], axis=0))
    kk = kkf * _dot01x2(lax.rsqrt(jnp.maximum(ss, 1e-24)), jnp.concatenate([expand, expand], axis=0))
    k_ref[...] = k * (1.0 + (a - 1.0) * kaw_ref[...])
    an_ref[...] = -kk
    b_ref[...] = kk * a


def _odd_pre(x, prev_or_shift0, vfirst, P, i, tm, L):
    T = x.shape[0]
    step = L == 1
    has_vfirst = vfirst is not None
    nblk_seq = max(L // tm, 1)
    tok = pl.BlockSpec((tm, D_MODEL), lambda t: (t, 0))
    par = lambda r, n, idx=i: pl.BlockSpec((None, r, n), lambda t: (idx, 0, 0), pipeline_mode=pl.Buffered(1))
    args = [x]
    specs = [tok]
    if step:
        args.append(prev_or_shift0)
        specs.append(tok)
    else:
        args += [x, prev_or_shift0]
        specs += [pl.BlockSpec((8, D_MODEL), lambda t: (jnp.maximum(t * (tm // 8) - 1, 0), 0)),
                  pl.BlockSpec((None, 1, D_MODEL), lambda t: (t // nblk_seq, 0, 0))]
    args += [P["mix_norm"], P["mix"], P["w_r"], P["w_k"], P["w_v"], P["w0"], P["w1"], P["w2"],
             P["a0"], P["a1"], P["a2"]]
    specs += [pl.BlockSpec((None, 1, D_MODEL), lambda t: (2 * i + 1, 0, 0)),
              par(8, D_MODEL), par(D_MODEL, D_MODEL), par(D_MODEL, D_MODEL), par(D_MODEL, D_MODEL),
              par(1, D_MODEL), par(D_MODEL, LANES), par(LANES, D_MODEL),
              par(1, D_MODEL), par(D_MODEL, LANES), par(LANES, D_MODEL)]
    if has_vfirst:
        args += [vfirst, P["v0"], P["v1"], P["v2"]]
        specs += [tok, par(1, D_MODEL, i - 1), par(D_MODEL, LANES, i - 1), par(LANES, D_MODEL, i - 1)]
    args += [P["g1"], P["g2"], P["k_k"], P["k_a"]]
    specs += [par(D_MODEL, 2 * LANES), par(2 * LANES, D_MODEL), par(1, D_MODEL), par(1, D_MODEL)]
    return pl.pallas_call(
        functools.partial(_odd_pre_body, step=step, has_vfirst=has_vfirst, nblk_seq=nblk_seq, tm=tm),
        grid=(T // tm,),
        in_specs=specs,
        out_specs=[tok] * _ODD_OUTS,
        out_shape=[jax.ShapeDtypeStruct((T, D_MODEL), F32)] * _ODD_OUTS,
        compiler_params=_cp(("parallel",)),
        name="odd_pre",
    )(*args)


def _dot01x2(x, e2):
    hi = x.astype(BF16)
    mid = (x - hi.astype(F32)).astype(BF16)
    return _dot(jnp.concatenate([hi, mid], axis=1), e2)


def _odd_post_body(x_ref, o_ref, r_ref, k_ref, v_ref, g_ref, gnw_ref, gnb_ref, rk_ref, wo_ref, out_ref):
    red = _head_reduce()
    red = jnp.concatenate([red, red], axis=0)
    expand = _head_expand()
    expand = jnp.concatenate([expand, expand], axis=0)
    for c0 in range(0, x_ref.shape[0], LANES):
        rows = slice(c0, c0 + LANES)
        o = o_ref[rows, :]
        mu = _dot01x2(_dot01x2(o, red) * (1.0 / RWKV_HEAD), expand)
        d = o - mu
        var = _dot01x2(d * d, red) * (1.0 / RWKV_HEAD)
        on = d * _dot01x2(lax.rsqrt(var + RWKV_GN_EPS), expand) * gnw_ref[...] + gnb_ref[...]
        bonus = _dot01x2(_dot01x2(r_ref[rows, :] * k_ref[rows, :] * rk_ref[...], red), expand) * v_ref[rows, :]
        out_ref[rows, :] = x_ref[rows, :] + _dot(((on + bonus) * g_ref[rows, :]).astype(BF16), wo_ref[...])


def _odd_post(x, o, r, k, v, g, P, i, tm):
    T = x.shape[0]
    tok = pl.BlockSpec((tm, D_MODEL), lambda t: (t, 0))
    par = lambda r_, n: pl.BlockSpec((None, r_, n), lambda t: (i, 0, 0))
    return pl.pallas_call(
        _odd_post_body,
        grid=(T // tm,),
        in_specs=[tok] * 6 + [par(1, D_MODEL), par(1, D_MODEL), par(1, D_MODEL), par(D_MODEL, D_MODEL)],
        out_specs=tok,
        out_shape=jax.ShapeDtypeStruct((T, D_MODEL), F32),
        compiler_params=_cp(("parallel",)),
        name="odd_post",
    )(x, o, r, k, v, g, P["gn_w"], P["gn_b"], P["r_k"], P["w_o"])


RWKV_TB = 64
RWKV_NPAIR = RWKV_HEADS // 2


RWKV_SEQS = 4
RWKV_SPLIT = 1
RWKV_UNROLL = 8


def _rwkv_seq_body(w_ref, k_ref, a_ref, b_ref, r_ref, v_ref, o_ref, sn_ref,
                   s_ref, vt_ref, oacc_ref, ap_ref, bp_ref, kp_ref, rp_ref, gl_ref, *, ntb):
    N = RWKV_HEAD
    R = RWKV_NPAIR * N
    tbi = pl.program_id(1)

    @pl.when(tbi == 0)
    def _():
        s_ref[...] = jnp.zeros_like(s_ref)

    seg = _onehot((_iota((2 * LANES, 2 * LANES), 0) >> HEAD_SHIFT) == (_iota((2 * LANES, 2 * LANES), 1) >> HEAD_SHIFT))

    def seg_sum(xb):
        h = xb.shape[0] // 2
        both = _dot(jnp.concatenate([xb[:h], xb[h:]], axis=1), seg)
        return jnp.concatenate([both[:, :LANES], both[:, LANES:]], axis=0)
    incl = _onehot(_iota((RWKV_TB, RWKV_TB), 0) >= _iota((RWKV_TB, RWKV_TB), 1))
    for c in range(RWKV_SEQS):
        logw = jnp.log(w_ref[c])
        cum = _dot01_l(incl, logw)
        g_inc = jnp.exp(cum)
        g_inv = jnp.exp(-cum)
        ap_ref[c] = a_ref[c] * jnp.exp(cum - logw)
        bp_ref[c] = b_ref[c] * g_inv
        kp_ref[c] = k_ref[c] * g_inv
        rp_ref[c] = r_ref[c] * g_inc
        gl_ref[c] = g_inc[RWKV_TB - 8:RWKV_TB, :]
        for p in range(RWKV_NPAIR):
            bt = v_ref[c, :, p * LANES:(p + 1) * LANES].T
            vt_ref[c, p * N:(p + 1) * N, :] = jnp.concatenate([bt[0:N], bt[N:2 * N]], axis=1)
    oacc_ref[...] = jnp.zeros_like(oacc_ref)
    G = R // RWKV_SPLIT
    OSTEPS = LANES // 4
    seg_out = _onehot((_iota((2 * LANES, LANES), 0) >> HEAD_SHIFT) == (_iota((2 * LANES, LANES), 1) >> (OSTEPS.bit_length() - 1)))
    lane_copy = _iota((R // 2, LANES), 1) & (OSTEPS - 1)
    lane_base = (_iota((G, LANES), 1) >> HEAD_SHIFT) * N

    def steps(tu, carry):
        row0 = pl.multiple_of(tu * RWKV_UNROLL, RWKV_UNROLL)
        for j in range(RWKV_UNROLL):
            t = tu * RWKV_UNROLL + j
            for c in range(RWKV_SEQS):
                for g in range(RWKV_SPLIT):
                    rs = slice(g * G, (g + 1) * G)

                    def rows(ref):
                        return jnp.concatenate(
                            [jnp.broadcast_to(
                                ref[c, pl.ds(row0, RWKV_UNROLL), p * LANES:(p + 1) * LANES][j:j + 1, :], (N, LANES))
                             for p in range(g * G // N, (g + 1) * G // N)], axis=0)

                    s = s_ref[c, rs, :]
                    sa = seg_sum((s * rows(ap_ref)).astype(BF16))
                    vcol = jnp.take_along_axis(vt_ref[c, rs, :], lane_base + t, axis=1)
                    s = (s + vcol * rows(kp_ref)) + sa * rows(bp_ref)
                    s_ref[c, rs, :] = s
                    xr = (s * rows(rp_ref)).astype(BF16)
                    ob = _dot(jnp.concatenate([xr[:R // 2], xr[R // 2:]], axis=1), seg_out)
                    hb = tu // (OSTEPS // RWKV_UNROLL)
                    oacc_ref[c, hb] = jnp.where(lane_copy == (t & (OSTEPS - 1)), ob, oacc_ref[c, hb])
        return carry

    lax.fori_loop(0, RWKV_TB // RWKV_UNROLL, steps, 0)

    for c in range(RWKV_SEQS):
        s_ref[c] = s_ref[c] * jnp.concatenate(
            [jnp.broadcast_to(gl_ref[c, 7:8, p * LANES:(p + 1) * LANES], (N, LANES)) for p in range(RWKV_NPAIR)],
            axis=0)
        for hb in range(RWKV_TB // OSTEPS):
            ot = oacc_ref[c, hb].T
            for gh in range(4):
                g, hh = divmod(gh, 2)
                for q in range(RWKV_NPAIR // 2):
                    lane0 = (g * (RWKV_NPAIR // 2) + q) * LANES + hh * N
                    o_ref[c, hb * OSTEPS:(hb + 1) * OSTEPS, lane0:lane0 + N] = (
                        ot[gh * OSTEPS:(gh + 1) * OSTEPS, q * N:(q + 1) * N])

    @pl.when(tbi == ntb - 1)
    def _():
        for c in range(RWKV_SEQS):
            for p in range(RWKV_NPAIR):
                for hh in range(2):
                    sn_ref[c, (2 * p + hh) * N:(2 * p + hh + 1) * N, :] = (
                        s_ref[c, p * N:(p + 1) * N, hh * N:(hh + 1) * N])


def _rwkv_seq(w, k, a, b, r, v, B, L):
    ntb = L // RWKV_TB
    nc = RWKV_SEQS
    as_seqs = lambda x: x.reshape(B, L, D_MODEL)
    tok = pl.BlockSpec((nc, RWKV_TB, D_MODEL), lambda bi, t: (bi, t, 0))
    state_rows = RWKV_NPAIR * RWKV_HEAD
    o, sn = pl.pallas_call(
        functools.partial(_rwkv_seq_body, ntb=ntb),
        grid=(B // nc, ntb),
        in_specs=[tok] * 6,
        out_specs=[tok, pl.BlockSpec((nc, D_MODEL, RWKV_HEAD), lambda bi, t: (bi, 0, 0))],
        out_shape=[jax.ShapeDtypeStruct((B, L, D_MODEL), F32),
                   jax.ShapeDtypeStruct((B, D_MODEL, RWKV_HEAD), F32)],
        scratch_shapes=[pltpu.VMEM((nc, state_rows, LANES), F32)] * 2
        + [pltpu.VMEM((nc, 2, state_rows // 2, LANES), F32)]
        + [pltpu.VMEM((nc, RWKV_TB, D_MODEL), F32)] * 4
        + [pltpu.VMEM((nc, 8, D_MODEL), F32)],
        compiler_params=_cp(("parallel", "arbitrary")),
        name="rwkv_seq",
    )(as_seqs(w), as_seqs(k), as_seqs(a), as_seqs(b), as_seqs(r), as_seqs(v))
    return o.reshape(B * L, D_MODEL), sn.reshape(B, RWKV_HEADS, RWKV_HEAD, RWKV_HEAD)


def _rwkv_step_body(s_ref, w_ref, k_ref, a_ref, b_ref, r_ref, vc_ref, *rest):
    sn_ref, o_ref = rest[-2:]
    N = RWKV_HEAD
    lane = _iota((D_MODEL, STEP_BATCH), 1)
    ocols = jnp.zeros((D_MODEL, STEP_BATCH), F32)
    vc = vc_ref[...]

    def rows(ref, b):
        return jnp.concatenate(
            [jnp.broadcast_to(ref[b:b + 1, h * N:(h + 1) * N], (N, N)) for h in range(RWKV_HEADS)], axis=0)

    for b in range(STEP_BATCH):
        s = s_ref[b]
        sa = jnp.sum(s * rows(a_ref, b), axis=1, keepdims=True)
        s = s * rows(w_ref, b) + sa * rows(b_ref, b) + vc[:, b:b + 1] * rows(k_ref, b)
        sn_ref[b] = s
        ocol = jnp.sum(s * rows(r_ref, b), axis=1, keepdims=True)
        ocols = jnp.where(lane == b, ocol, ocols)
    o_ref[...] = ocols


def _rwkv_step(state_all, acc, i, w, k, a, b, r, v):
    Bn = state_all.shape[1]
    bb = STEP_BATCH
    col = pl.BlockSpec((None, D_MODEL, bb), lambda t: (t, 0, 0))
    row = pl.BlockSpec((bb, D_MODEL), lambda t: (t, 0))
    acc, ocols = _layer_state_call(
        _rwkv_step_body, "rwkv_step", state_all, acc, i, D_MODEL, RWKV_HEAD,
        [w, k, a, b, r, _to_cols(v)], [row] * 5 + [col],
        col, jax.ShapeDtypeStruct((Bn // bb, D_MODEL, bb), F32))
    return _from_cols(ocols), acc


def _pad_to(w, axis, n):
    pad = [(0, 0)] * w.ndim
    pad[axis] = (0, n - w.shape[axis])
    return jnp.pad(w, pad)


def _prepare(W):
    bf = lambda w: w.astype(BF16)
    row = lambda w: w[:, None, :]
    sizes = [SSD_INNER, SSD_CONV_DIM, SSD_HEADS, GLA_KEY, GLA_KEY, GLA_VAL, GLA_VAL, GLA_GATE_RANK]
    offs = [0]
    for s in sizes:
        offs.append(offs[-1] + s)
    piece = lambda n: W["ev_w_in"][:, :, offs[n]:offs[n + 1]]
    w_in = jnp.concatenate([piece(0), piece(1), piece(3), piece(4), piece(5), piece(6),
                            _pad_to(jnp.concatenate([piece(2), piece(7)], axis=-1), 2, SMALL)], axis=-1)
    gate_w2 = jnp.pad(W["ev_gla_gate_w2"], ((0, 0), (SSD_HEADS, SMALL - SSD_HEADS - GLA_GATE_RANK), (0, 0)))
    even = dict(
        w_in=bf(w_in), w_out=bf(W["ev_w_out"]),
        conv_w=W["ev_conv_w"], conv_b=row(W["ev_conv_b"]),
        dt_bias=row(_pad_to(W["ev_dt_bias"], 1, SMALL)), a_log=row(_pad_to(W["ev_a_log"], 1, SMALL)),
        d_skip_x=row(jnp.repeat(W["ev_d_skip"], SSD_HEAD_DIM, axis=1)),
        ssd_norm=row(W["ev_ssd_norm"]), gate_w2=bf(gate_w2), gate_b=row(W["ev_gla_gate_b"]),
        gla_norm=row(W["ev_gla_norm"]),
    )
    odd = dict(
        mix_norm=row(W["mix_norm"]), mix=_pad_to(W["od_mix"], 1, 8),
        w_r=bf(W["od_w_r"]), w_k=bf(W["od_w_k"]), w_v=bf(W["od_w_v"]), w_o=bf(W["od_w_o"]),
        w0=row(W["od_w0"]), w1=bf(_pad_to(W["od_w1"], 2, LANES)), w2=bf(_pad_to(W["od_w2"], 1, LANES)),
        a0=row(W["od_a0"]), a1=bf(_pad_to(W["od_a1"], 2, LANES)), a2=bf(_pad_to(W["od_a2"], 1, LANES)),
        v0=row(W["od_v0"]), v1=bf(_pad_to(W["od_v1"], 2, LANES)), v2=bf(_pad_to(W["od_v2"], 1, LANES)),
        g1=bf(_pad_to(W["od_g1"], 2, 2 * LANES)), g2=bf(_pad_to(W["od_g2"], 1, 2 * LANES)),
        k_k=row(W["od_k_k"]), k_a=row(W["od_k_a"]), r_k=row(W["od_r_k"]),
        gn_w=row(W["od_gn_w"]), gn_b=row(W["od_gn_b"]),
    )
    ffn = dict(
        norm=W["ffn_norm"].reshape(DEPTH * 2, 1, D_MODEL),
        w_gu=bf(W["ffn_w_gu"]).reshape(DEPTH * 2, D_MODEL, 2 * D_FF),
        w_down=bf(W["ffn_w_down"]).reshape(DEPTH * 2, D_FF, D_MODEL),
    )
    return dict(even=even, odd=odd, ffn=ffn, mix_norm=row(W["mix_norm"]),
                final_norm=W["final_norm"][None, :])


def _run_group(x3, states, Wp):
    B, L, _ = x3.shape
    T = B * L
    step = L == 1
    x = x3.reshape(T, D_MODEL)
    tm_ffn = min(512, T)
    tm_proj = min(256, T)
    ffn, even, odd = Wp["ffn"], Wp["even"], Wp["odd"]
    convs, ssms, glas, shifts, wkvs = [], [], [], [], []
    ssm_acc = gla_acc = wkv_acc = None
    v_first = None
    for layer in range(DEPTH):
        i = layer // 2
        mix = None
        x = _ffn(x, ffn["norm"], ffn["w_gu"], ffn["w_down"], 2 * layer, tm_ffn)
        if layer % 2 == 0:
            z, xbc, q, k, v, og, sm = _even_in(x, Wp["mix_norm"], even["w_in"], i, tm_ffn)
            if step:
                xact, convn, dax, c2, ybase, eg = _even_step_pre(
                    xbc, states[0][i].reshape(B, 3 * SSD_CONV_DIM), sm, even, i)
                ssm_acc, yi = _ssd_step(states[1], ssm_acc, i, dax, c2, xact)
                gla_acc, o = _gla_step(states[2], gla_acc, i, eg, k, q, v)
                x = _even_step_post(x, ybase, yi, dax, z, o, og, even, even["w_out"], i)
                convn = convn.reshape(B, SSD_CONV - 1, SSD_CONV_DIM)
            else:
                y, convn, ssmn = _ssd_seq(xbc, sm, z, even, i, B, L)
                o, glan = _gla_seq(q, k, v, og, sm, even, i, B, L)
                mix = (y, o, even["w_out"], i)
                ssms.append(ssmn)
                glas.append(glan)
            convs.append(convn)
        else:
            prev = states[3][i] if step else jnp.zeros((B, 1, D_MODEL), F32)
            h, r, dec, k, v, an, bb, gg = _odd_pre(x, prev, v_first, odd, i, tm_proj, L)
            if v_first is None:
                v_first = v
            if step:
                o, wkv_acc = _rwkv_step(states[4], wkv_acc, i, dec, k, an, bb, r, v)
            else:
                o, wkvn = _rwkv_seq(dec, k, an, bb, r, v, B, L)
                wkvs.append(wkvn)
            x = _odd_post(x, o, r, k, v, gg, odd, i, tm_ffn)
            shifts.append(h.reshape(B, L, D_MODEL)[:, -1])
        x = _ffn(x, ffn["norm"], ffn["w_gu"], ffn["w_down"], 2 * layer + 1, tm_ffn,
                 final_g=Wp["final_norm"] if layer == DEPTH - 1 else None, mix=mix)
    y = x.reshape(B, L, D_MODEL)
    if step:
        ssm_out = ssm_acc.reshape(N_EVEN, B, SSD_HEADS, SSD_HEAD_DIM, SSD_D_STATE)
        gla_out = gla_acc.reshape(N_EVEN, B, GLA_HEADS, GLA_DK, GLA_DV)
        wkv_out = wkv_acc.reshape(N_ODD, B, RWKV_HEADS, RWKV_HEAD, RWKV_HEAD)
    else:
        ssm_out, gla_out, wkv_out = jnp.stack(ssms), jnp.stack(glas), jnp.stack(wkvs)
    return y, jnp.stack(convs), ssm_out, gla_out, jnp.stack(shifts), wkv_out


def kernel(x_prompt, x_sample, state_conv, state_ssm, state_gla, state_shift, state_wkv, ffn_norm, ffn_w_gu, ffn_w_down, mix_norm, final_norm, ev_w_in, ev_conv_w, ev_conv_b, ev_dt_bias, ev_a_log, ev_d_skip, ev_ssd_norm, ev_gla_gate_w2, ev_gla_gate_b, ev_gla_norm, ev_w_out, od_mix, od_w0, od_w1, od_w2, od_a0, od_a1, od_a2, od_v0, od_v1, od_v2, od_g1, od_g2, od_k_k, od_k_a, od_r_k, od_w_r, od_w_k, od_w_v, od_w_o, od_gn_w, od_gn_b):
    W = dict(ffn_norm=ffn_norm, ffn_w_gu=ffn_w_gu, ffn_w_down=ffn_w_down, mix_norm=mix_norm,
             final_norm=final_norm, ev_w_in=ev_w_in, ev_conv_w=ev_conv_w, ev_conv_b=ev_conv_b,
             ev_dt_bias=ev_dt_bias, ev_a_log=ev_a_log, ev_d_skip=ev_d_skip, ev_ssd_norm=ev_ssd_norm,
             ev_gla_gate_w2=ev_gla_gate_w2, ev_gla_gate_b=ev_gla_gate_b, ev_gla_norm=ev_gla_norm,
             ev_w_out=ev_w_out, od_mix=od_mix, od_w0=od_w0, od_w1=od_w1, od_w2=od_w2,
             od_a0=od_a0, od_a1=od_a1, od_a2=od_a2, od_v0=od_v0, od_v1=od_v1, od_v2=od_v2,
             od_g1=od_g1, od_g2=od_g2, od_k_k=od_k_k, od_k_a=od_k_a, od_r_k=od_r_k,
             od_w_r=od_w_r, od_w_k=od_w_k, od_w_v=od_w_v, od_w_o=od_w_o,
             od_gn_w=od_gn_w, od_gn_b=od_gn_b)
    Wp = _prepare(W)
    prompt = _run_group(x_prompt, None, Wp)
    sample = _run_group(x_sample, (state_conv, state_ssm, state_gla, state_shift, state_wkv), Wp)
    return (prompt[0], sample[0]) + prompt[1:] + sample[1:]
```

```python
import functools

import jax
import jax.numpy as jnp
from jax import lax
from jax.experimental import pallas as pl
from jax.experimental.pallas import tpu as pltpu

F32 = jnp.float32
BF16 = jnp.bfloat16

D_MODEL = 1024
DEPTH = 4
N_EVEN = 2
N_ODD = 2
EPS = 1e-5
D_FF = 2816

SSD_HEADS = 16
SSD_HEAD_DIM = 64
SSD_INNER = 1024
SSD_GROUPS = 2
SSD_D_STATE = 128
SSD_CONV = 4
SSD_CONV_DIM = 1536
SSD_CHUNK = 64

GLA_HEADS = 4
GLA_DK = 128
GLA_DV = 256
GLA_KEY = 512
GLA_VAL = 1024
GLA_GATE_RANK = 16
GLA_GATE_NORM = 16.0
GLA_CHUNK = 16

RWKV_HEAD = 64
RWKV_HEADS = 16
RWKV_GN_EPS = 64e-5

HEAD_SHIFT = 6
GLA_CHUNK_SHIFT = 4
assert 1 << HEAD_SHIFT == SSD_HEAD_DIM == RWKV_HEAD == SSD_CHUNK and 1 << GLA_CHUNK_SHIFT == GLA_CHUNK

LANES = 128
SMALL = LANES
IN_PERM = SSD_INNER + SSD_CONV_DIM + 2 * GLA_KEY + 2 * GLA_VAL + SMALL
VMEM_LIMIT = 56 * 1024 * 1024
FF_TILE = 1408
SEQ_BLOCK = 256
STEP_BATCH = 8


def _cp(sem):
    return pltpu.CompilerParams(dimension_semantics=sem, vmem_limit_bytes=VMEM_LIMIT)


def _iota(shape, axis):
    return lax.broadcasted_iota(jnp.int32, shape, axis)


def _onehot(mask):
    return jnp.where(mask, 1.0, 0.0).astype(BF16)


def _dot(a, b):
    return jnp.dot(a, b, preferred_element_type=F32)


def _split3(x):
    hi = x.astype(BF16)
    r = x - hi.astype(F32)
    mid = r.astype(BF16)
    lo = (r - mid.astype(F32)).astype(BF16)
    return hi, mid, lo


def _dot01(x, e):
    hi, mid, lo = _split3(x)
    return _dot(hi, e) + _dot(mid, e) + _dot(lo, e)


def _dot01_l(e, x):
    hi, mid, lo = _split3(x)
    return _dot(e, hi) + _dot(e, mid) + _dot(e, lo)


def _rms(x, g):
    ms = jnp.mean(x * x, axis=-1, keepdims=True)
    return x * lax.rsqrt(ms + EPS) * g


def _silu(x):
    return x * jax.nn.sigmoid(x)


def _softplus(x):
    return jnp.maximum(x, 0.0) + jnp.log1p(jnp.exp(-jnp.abs(x)))


def _softplus_abs(x):
    return jnp.maximum(x, 0.0) + jnp.log(1.0 + jnp.exp(-jnp.abs(x)))


def _head_expand():
    return _onehot((_iota((LANES, D_MODEL), 1) >> HEAD_SHIFT) == _iota((LANES, D_MODEL), 0))


def _head_reduce():
    return _onehot((_iota((D_MODEL, LANES), 0) >> HEAD_SHIFT) == _iota((D_MODEL, LANES), 1))


def _ffn_body(x_ref, g_ref, wgu_ref, wd_ref, *rest, mix_out, final_norm):
    o_ref = rest[-1]
    x = x_ref[...]
    if mix_out:
        y_ref, mo_ref, wo_ref = rest[:3]
        x = x + _dot(y_ref[...].astype(BF16), wo_ref[0:SSD_INNER, :])
        x = x + _dot(mo_ref[...].astype(BF16), wo_ref[SSD_INNER:, :])
    xn = _rms(x, g_ref[...]).astype(BF16)
    acc = None
    for j in range(D_FF // FF_TILE):
        lo = j * FF_TILE
        gate = _dot(xn, wgu_ref[:, lo:lo + FF_TILE])
        up = _dot(xn, wgu_ref[:, D_FF + lo:D_FF + lo + FF_TILE])
        part = _dot((_silu(gate) * up).astype(BF16), wd_ref[lo:lo + FF_TILE, :])
        acc = part if acc is None else acc + part
    y = x + 0.5 * acc
    o_ref[...] = _rms(y, rest[-2][...]) if final_norm else y


def _ffn(x, nrm, wgu, wd, ls, tm, final_g=None, mix=None):
    T = x.shape[0]
    resident = pl.Buffered(1)
    tok = pl.BlockSpec((tm, D_MODEL), lambda i: (i, 0))
    args = [x, nrm, wgu, wd]
    specs = [
        tok,
        pl.BlockSpec((None, 1, D_MODEL), lambda i: (ls, 0, 0)),
        pl.BlockSpec((None, D_MODEL, 2 * D_FF), lambda i: (ls, 0, 0), pipeline_mode=resident),
        pl.BlockSpec((None, D_FF, D_MODEL), lambda i: (ls, 0, 0), pipeline_mode=resident),
    ]
    if mix is not None:
        y, o, w_out, li = mix
        args += [y, o, w_out]
        specs += [tok, tok, pl.BlockSpec((None, SSD_INNER + GLA_VAL, D_MODEL), lambda i: (li, 0, 0),
                                         pipeline_mode=resident)]
    if final_g is not None:
        args.append(final_g)
        specs.append(pl.BlockSpec((1, D_MODEL), lambda i: (0, 0)))
    return pl.pallas_call(
        functools.partial(_ffn_body, mix_out=mix is not None, final_norm=final_g is not None),
        grid=(T // tm,),
        in_specs=specs,
        out_specs=pl.BlockSpec((tm, D_MODEL), lambda i: (i, 0)),
        out_shape=jax.ShapeDtypeStruct((T, D_MODEL), F32),
        compiler_params=_cp(("parallel",)),
        name="ffn",
    )(*args)


_EVEN_PIECES = (SSD_INNER, SSD_CONV_DIM, GLA_KEY, GLA_KEY, GLA_VAL, GLA_VAL, SMALL)


def _even_in_body(x_ref, g_ref, w_ref, *out_refs):
    h = _rms(x_ref[...], g_ref[...]).astype(BF16)
    off = 0
    for ref, n in zip(out_refs, _EVEN_PIECES):
        ref[...] = _dot(h, w_ref[:, off:off + n])
        off += n


def _even_in(x, nrm, w, i, tm):
    T = x.shape[0]
    return pl.pallas_call(
        _even_in_body,
        grid=(T // tm,),
        in_specs=[
            pl.BlockSpec((tm, D_MODEL), lambda t: (t, 0)),
            pl.BlockSpec((None, 1, D_MODEL), lambda t: (2 * i, 0, 0)),
            pl.BlockSpec((None, D_MODEL, IN_PERM), lambda t: (i, 0, 0), pipeline_mode=pl.Buffered(1)),
        ],
        out_specs=[pl.BlockSpec((tm, n), lambda t: (t, 0)) for n in _EVEN_PIECES],
        out_shape=[jax.ShapeDtypeStruct((T, n), F32) for n in _EVEN_PIECES],
        compiler_params=_cp(("parallel",)),
        name="even_in",
    )(x, nrm, w)


def _ssd_seq_body(xbc_ref, sm_ref, z_ref, cw_ref, cb_ref, dtb_ref, alog_ref, dsk_ref, nw_ref,
                  y_ref, convn_ref, ssmn_ref,
                  cbuf_ref, xact_ref, acx_ref, dtx_ref, hT_ref, *, nblk):
    Lb = SEQ_BLOCK
    C = SSD_CHUNK
    j = pl.program_id(1)

    @pl.when(j == 0)
    def _():
        cbuf_ref[0:8, :] = jnp.zeros((8, SSD_CONV_DIM), F32)
        hT_ref[...] = jnp.zeros_like(hT_ref)

    x = xbc_ref[...]
    acc = cb_ref[...] + pltpu.roll(x, 3, axis=0) * cw_ref[0:1, :]
    acc = acc + pltpu.roll(x, 2, axis=0) * cw_ref[1:2, :]
    acc = acc + pltpu.roll(x, 1, axis=0) * cw_ref[2:3, :]
    acc = acc + x * cw_ref[3:4, :]
    xact_ref[...] = _silu(acc)
    cbuf_ref[8:16, :] = x[0:8, :]
    head = cb_ref[...] + cbuf_ref[5:13, :] * cw_ref[0:1, :]
    head = head + cbuf_ref[6:14, :] * cw_ref[1:2, :]
    head = head + cbuf_ref[7:15, :] * cw_ref[2:3, :]
    head = head + cbuf_ref[8:16, :] * cw_ref[3:4, :]
    xact_ref[0:8, :] = _silu(head)
    tail = x[Lb - 8:Lb, :]
    cbuf_ref[0:8, :] = tail

    @pl.when(j == nblk - 1)
    def _():
        convn_ref[...] = tail[5:8, :]

    dt = _softplus(sm_ref[...] + dtb_ref[...])
    dta = dt * (-jnp.exp(alog_ref[...]))
    ii = _iota((Lb, Lb), 0)
    jj = _iota((Lb, Lb), 1)
    tril = _onehot(((ii >> HEAD_SHIFT) == (jj >> HEAD_SHIFT)) & (ii >= jj))
    acum = _dot01_l(tril, dta)
    expand = _head_expand()
    acx_ref[...] = _dot01(acum, expand)
    dtx_ref[...] = _dot01(dt, expand)

    li = _iota((C, D_MODEL), 0)
    lj = _iota((C, D_MODEL), 1) & (C - 1)
    diag = li == lj
    causal = li >= lj
    ones_c = jnp.ones((C, C), BF16)
    ones_c2 = jnp.ones((C, 2 * C), BF16)
    tile8 = _onehot((_iota((2 * C, 8 * C), 1) & (C - 1)) == (_iota((2 * C, 8 * C), 0) & (C - 1)))
    pair_mask = (_iota((LANES, LANES), 0) >> HEAD_SHIFT) == (_iota((LANES, LANES), 1) >> HEAD_SHIFT)
    half = SSD_INNER // SSD_GROUPS

    def chunk(c, carry):
        r0 = pl.multiple_of(c * C, C)
        xa = xact_ref[pl.ds(r0, C), :]
        xs = xa[:, :SSD_INNER]
        bms = [xa[:, SSD_INNER + g * SSD_D_STATE:SSD_INNER + (g + 1) * SSD_D_STATE] for g in range(2)]
        cms = [xa[:, SSD_INNER + (2 + g) * SSD_D_STATE:SSD_INNER + (3 + g) * SSD_D_STATE]
               for g in range(2)]
        acx = acx_ref[pl.ds(r0, C), :]
        dtx = dtx_ref[pl.ds(r0, C), :]
        alast = acx[C - 1:C, :]
        arow = _dot01_l(ones_c, jnp.where(diag, acx, 0.0))
        dsel = jnp.where(diag, dtx, 0.0)
        dhi = dsel.astype(BF16)
        dmid = (dsel - dhi.astype(F32)).astype(BF16)
        dtrow = _dot(ones_c2, jnp.concatenate([dhi, dmid], axis=0))
        decay = jnp.exp(jnp.where(causal, acx - arow, -jnp.inf))
        cbt = []
        for g in range(2):
            cb = lax.dot_general(cms[g].astype(BF16), bms[g].astype(BF16),
                                 (((1,), (1,)), ((), ())), preferred_element_type=F32)
            cbt.append(_dot01x2(cb, tile8))
        wts = (jnp.concatenate(cbt, axis=1) * decay * dtrow).astype(BF16)
        ys = []
        for p in range(SSD_HEADS // 2):
            xp = xs[:, p * LANES:(p + 1) * LANES]
            xbd = jnp.where(pair_mask, jnp.concatenate([xp, xp], axis=0), 0.0).astype(BF16)
            ys.append(_dot(wts[:, p * LANES:(p + 1) * LANES], xbd))
        y = jnp.concatenate(ys, axis=1)
        hT = hT_ref[...]
        hTb = hT.astype(BF16)
        yi = [_dot(cms[g].astype(BF16), hTb[:, g * half:(g + 1) * half]) for g in range(2)]
        y = y + jnp.concatenate(yi, axis=1) * jnp.exp(acx)
        xsc = (jnp.exp(alast - acx) * dtx * xs).astype(BF16)
        st = [lax.dot_general(bms[g].astype(BF16), xsc[:, g * half:(g + 1) * half],
                              (((0,), (0,)), ((), ())), preferred_element_type=F32) for g in range(2)]
        hT_ref[...] = hT * jnp.exp(alast) + jnp.concatenate(st, axis=1)
        y = y + dsk_ref[...] * xs
        y = y * _silu(z_ref[pl.ds(r0, C), :])
        outs = []
        for g in range(2):
            yg = y[:, g * half:(g + 1) * half]
            ms = jnp.mean(yg * yg, axis=-1, keepdims=True)
            outs.append(yg * lax.rsqrt(ms + EPS))
        y_ref[pl.ds(r0, C), :] = jnp.concatenate(outs, axis=1) * nw_ref[...]
        return carry

    lax.fori_loop(0, Lb // C, chunk, 0, unroll=4)

    @pl.when(j == nblk - 1)
    def _():
        ssmn_ref[...] = hT_ref[...].T


def _ssd_seq(xbc, sm, z, P, i, B, L):
    Lb = SEQ_BLOCK
    nblk = L // Lb
    T = B * L
    tok = lambda n: pl.BlockSpec((Lb, n), lambda b, j: (b * nblk + j, 0))
    par = lambda r, n: pl.BlockSpec((None, r, n), lambda b, j: (i, 0, 0))
    y, convn, ssmn = pl.pallas_call(
        functools.partial(_ssd_seq_body, nblk=nblk),
        grid=(B, nblk),
        in_specs=[tok(SSD_CONV_DIM), tok(SMALL), tok(SSD_INNER),
                  par(SSD_CONV, SSD_CONV_DIM), par(1, SSD_CONV_DIM), par(1, SMALL), par(1, SMALL),
                  par(1, SSD_INNER), par(1, SSD_INNER)],
        out_specs=[tok(SSD_INNER),
                   pl.BlockSpec((None, SSD_CONV - 1, SSD_CONV_DIM), lambda b, j: (b, 0, 0)),
                   pl.BlockSpec((None, SSD_INNER, SSD_D_STATE), lambda b, j: (b, 0, 0))],
        out_shape=[jax.ShapeDtypeStruct((T, SSD_INNER), F32),
                   jax.ShapeDtypeStruct((B, SSD_CONV - 1, SSD_CONV_DIM), F32),
                   jax.ShapeDtypeStruct((B, SSD_INNER, SSD_D_STATE), F32)],
        scratch_shapes=[pltpu.VMEM((16, SSD_CONV_DIM), F32),
                        pltpu.VMEM((Lb, SSD_CONV_DIM), F32),
                        pltpu.VMEM((Lb, SSD_INNER), F32),
                        pltpu.VMEM((Lb, SSD_INNER), F32),
                        pltpu.VMEM((SSD_D_STATE, SSD_INNER), F32)],
        compiler_params=_cp(("parallel", "arbitrary")),
        name="ssd_seq",
    )(xbc, sm, z, P["conv_w"], P["conv_b"], P["dt_bias"], P["a_log"], P["d_skip_x"], P["ssd_norm"])
    return y, convn, ssmn.reshape(B, SSD_HEADS, SSD_HEAD_DIM, SSD_D_STATE)


def _gla_seq_body(q_ref, k_ref, v_ref, og_ref, sm_ref, w2_ref, gb_ref, nw_ref,
                  o_ref, glan_ref,
                  st_ref, dec_ref, qg_ref, kd_ref, oi_ref, *, nblk):
    Lb = SEQ_BLOCK
    j = pl.program_id(1)

    @pl.when(j == 0)
    def _():
        st_ref[...] = jnp.zeros_like(st_ref)

    gate = _dot(sm_ref[...].astype(BF16), w2_ref[...]) + gb_ref[...]
    lg = -_softplus_abs(-gate) / GLA_GATE_NORM
    ii = _iota((Lb, Lb), 0)
    jj = _iota((Lb, Lb), 1)
    same = (ii >> GLA_CHUNK_SHIFT) == (jj >> GLA_CHUNK_SHIFT)
    causal = same & (ii >= jj)
    lg_hi = lg.astype(BF16)
    lg2 = jnp.concatenate([lg_hi, (lg - lg_hi.astype(F32)).astype(BF16)], axis=0)
    sel_c = _onehot(causal)
    sel_s = _onehot(same)
    b = _dot(jnp.concatenate([sel_c, sel_c], axis=1), lg2)
    blast = _dot(jnp.concatenate([sel_s, sel_s], axis=1), lg2)
    qg = q_ref[...] * (GLA_DK ** -0.5) * jnp.exp(b)
    kk = k_ref[...]
    kg = kk * jnp.exp(-b)
    kd = kk * jnp.exp(blast - b)
    dec_ref[...] = jnp.exp(blast)
    qg_ref[...] = qg.astype(BF16)
    kd_ref[...] = kd.astype(BF16)

    def chunk(c, carry):
        r0 = pl.multiple_of(c * GLA_CHUNK, GLA_CHUNK)
        for h in range(GLA_HEADS):
            ks = slice(h * GLA_DK, (h + 1) * GLA_DK)
            vs = slice(h * GLA_DV, (h + 1) * GLA_DV)
            st = st_ref[h]
            oi_ref[pl.ds(r0, GLA_CHUNK), vs] = lax.dot_general(
                qg_ref[pl.ds(r0, GLA_CHUNK), ks], st.astype(BF16),
                (((1,), (1,)), ((), ())), preferred_element_type=F32)
            upd = lax.dot_general(v_ref[pl.ds(r0, GLA_CHUNK), vs].astype(BF16),
                                  kd_ref[pl.ds(r0, GLA_CHUNK), ks],
                                  (((0,), (0,)), ((), ())), preferred_element_type=F32)
            st_ref[h] = st * dec_ref[pl.ds(r0, 8), ks][0:1, :] + upd
        return carry

    lax.fori_loop(0, Lb // GLA_CHUNK, chunk, 0, unroll=16)

    for h in range(GLA_HEADS):
        ks = slice(h * GLA_DK, (h + 1) * GLA_DK)
        vs = slice(h * GLA_DV, (h + 1) * GLA_DV)
        att = lax.dot_general(qg[:, ks].astype(BF16), kg[:, ks].astype(BF16),
                              (((1,), (1,)), ((), ())), preferred_element_type=F32)
        att = jnp.where(causal, att, 0.0)
        o_h = _dot(att.astype(BF16), v_ref[:, vs].astype(BF16)) + oi_ref[:, vs]
        ms = jnp.mean(o_h * o_h, axis=-1, keepdims=True)
        o_h = o_h * lax.rsqrt(ms + EPS) * nw_ref[...]
        o_ref[:, vs] = o_h * _silu(og_ref[:, vs])

    @pl.when(j == nblk - 1)
    def _():
        for h in range(GLA_HEADS):
            glan_ref[h] = st_ref[h].T


def _gla_seq(q, k, v, og, sm, P, i, B, L):
    Lb = SEQ_BLOCK
    nblk = L // Lb
    T = B * L
    tok = lambda n: pl.BlockSpec((Lb, n), lambda b, j: (b * nblk + j, 0))
    par = lambda r, n: pl.BlockSpec((None, r, n), lambda b, j: (i, 0, 0))
    return pl.pallas_call(
        functools.partial(_gla_seq_body, nblk=nblk),
        grid=(B, nblk),
        in_specs=[tok(GLA_KEY), tok(GLA_KEY), tok(GLA_VAL), tok(GLA_VAL), tok(SMALL),
                  par(SMALL, GLA_KEY), par(1, GLA_KEY), par(1, GLA_DV)],
        out_specs=[tok(GLA_VAL),
                   pl.BlockSpec((None, GLA_HEADS, GLA_DK, GLA_DV), lambda b, j: (b, 0, 0, 0))],
        out_shape=[jax.ShapeDtypeStruct((T, GLA_VAL), F32),
                   jax.ShapeDtypeStruct((B, GLA_HEADS, GLA_DK, GLA_DV), F32)],
        scratch_shapes=[pltpu.VMEM((GLA_HEADS, GLA_DV, GLA_DK), F32),
                        pltpu.VMEM((Lb, GLA_KEY), F32),
                        pltpu.VMEM((Lb, GLA_KEY), BF16),
                        pltpu.VMEM((Lb, GLA_KEY), BF16),
                        pltpu.VMEM((Lb, GLA_VAL), F32)],
        compiler_params=_cp(("parallel", "arbitrary")),
        name="gla_seq",
    )(q, k, v, og, sm, P["gate_w2"], P["gate_b"], P["gla_norm"])


def _even_step_pre_body(xbc_ref, c0_ref, sm_ref, cw_ref, cb_ref, dtb_ref, alog_ref, dsk_ref,
                        w2_ref, gb_ref,
                        xact_ref, convn_ref, dax_ref, c2_ref, ybase_ref, eg_ref):
    u = xbc_ref[...]
    n = SSD_CONV_DIM
    acc = cb_ref[...] + c0_ref[:, 0:n] * cw_ref[0:1, :]
    acc = acc + c0_ref[:, n:2 * n] * cw_ref[1:2, :]
    acc = acc + c0_ref[:, 2 * n:3 * n] * cw_ref[2:3, :]
    acc = acc + u * cw_ref[3:4, :]
    xa = _silu(acc)
    xact_ref[...] = xa
    convn_ref[:, 0:n] = c0_ref[:, n:2 * n]
    convn_ref[:, n:2 * n] = c0_ref[:, 2 * n:3 * n]
    convn_ref[:, 2 * n:3 * n] = u
    dt = _softplus(sm_ref[...] + dtb_ref[...])
    dta = dt * (-jnp.exp(alog_ref[...]))
    expand = _head_expand()
    dtx = _dot01(dt, expand)
    dax_ref[...] = jnp.exp(_dot01(dta, expand))
    xs = xa[:, :SSD_INNER]
    c2_ref[...] = dtx * xs
    half = SSD_INNER // SSD_GROUPS
    cbs = []
    for g in range(2):
        bm = xa[:, SSD_INNER + g * SSD_D_STATE:SSD_INNER + (g + 1) * SSD_D_STATE]
        cm = xa[:, SSD_INNER + (2 + g) * SSD_D_STATE:SSD_INNER + (3 + g) * SSD_D_STATE]
        cb = jnp.sum(cm * bm, axis=-1, keepdims=True)
        cbs.append(jnp.broadcast_to(cb, (cb.shape[0], half)))
    ybase_ref[...] = jnp.concatenate(cbs, axis=1) * dtx * xs + dsk_ref[...] * xs
    gate = _dot(sm_ref[...].astype(BF16), w2_ref[...]) + gb_ref[...]
    eg_ref[...] = jnp.exp(-_softplus(-gate) / GLA_GATE_NORM)


def _even_step_pre(xbc, conv0, sm, P, i):
    Bn = xbc.shape[0]
    full = lambda n: pl.BlockSpec((Bn, n), lambda t: (0, 0))
    par = lambda r, n: pl.BlockSpec((None, r, n), lambda t: (i, 0, 0))
    outs = (SSD_CONV_DIM, 3 * SSD_CONV_DIM, SSD_INNER, SSD_INNER, SSD_INNER, GLA_KEY)
    return pl.pallas_call(
        _even_step_pre_body,
        grid=(1,),
        in_specs=[full(SSD_CONV_DIM), full(3 * SSD_CONV_DIM), full(SMALL),
                  par(SSD_CONV, SSD_CONV_DIM), par(1, SSD_CONV_DIM), par(1, SMALL), par(1, SMALL),
                  par(1, SSD_INNER), par(SMALL, GLA_KEY), par(1, GLA_KEY)],
        out_specs=[full(n) for n in outs],
        out_shape=[jax.ShapeDtypeStruct((Bn, n), F32) for n in outs],
        compiler_params=_cp(("arbitrary",)),
        name="even_step_pre",
    )(xbc, conv0, sm, P["conv_w"], P["conv_b"], P["dt_bias"], P["a_log"], P["d_skip_x"],
      P["gate_w2"], P["gate_b"])


def _to_cols(x):
    Bn, N = x.shape
    return x.reshape(Bn // STEP_BATCH, STEP_BATCH, N).transpose(0, 2, 1)


def _from_cols(x):
    nb, N, bb = x.shape
    return x.transpose(0, 2, 1).reshape(nb * bb, N)


def _layer_state_call(body, name, state_all, acc, i, rows, width, args, specs, out_spec, out_shape):
    nl, Bn = state_all.shape[:2]
    st = pl.BlockSpec((None, STEP_BATCH, rows, width), lambda t: (i, t, 0, 0))
    args = [state_all.reshape(nl, Bn, rows, width)] + list(args)
    specs = [st] + list(specs)
    aliases = {}
    if acc is not None:
        aliases = {len(args): 0}
        args.append(acc)
        specs.append(pl.BlockSpec(memory_space=pl.ANY))
    return pl.pallas_call(
        body,
        grid=(Bn // STEP_BATCH,),
        in_specs=specs,
        out_specs=[st, out_spec],
        out_shape=[jax.ShapeDtypeStruct((nl, Bn, rows, width), F32), out_shape],
        input_output_aliases=aliases,
        compiler_params=_cp(("parallel",)),
        name=name,
    )(*args)


def _ssd_step_body(h_ref, c1_ref, c2_ref, bm_ref, cm_ref, *rest):
    hn_ref, y_ref = rest[-2:]
    half = SSD_INNER // SSD_GROUPS
    lane = _iota((SSD_INNER, STEP_BATCH), 1)
    ycols = jnp.zeros((SSD_INNER, STEP_BATCH), F32)
    c1 = c1_ref[...]
    c2 = c2_ref[...]
    for b in range(STEP_BATCH):
        h0 = h_ref[b]
        bmat = jnp.concatenate(
            [jnp.broadcast_to(bm_ref[b:b + 1, g * SSD_D_STATE:(g + 1) * SSD_D_STATE],
                              (half, SSD_D_STATE)) for g in range(2)], axis=0)
        cmat = jnp.concatenate(
            [jnp.broadcast_to(cm_ref[b:b + 1, g * SSD_D_STATE:(g + 1) * SSD_D_STATE],
                              (half, SSD_D_STATE)) for g in range(2)], axis=0)
        ycol = jnp.sum(h0 * cmat, axis=1, keepdims=True)
        ycols = jnp.where(lane == b, ycol, ycols)
        hn_ref[b] = h0 * c1[:, b:b + 1] + c2[:, b:b + 1] * bmat
    y_ref[...] = ycols


def _ssd_step(state_all, acc, i, dax, c2, xact):
    Bn = state_all.shape[1]
    bb = STEP_BATCH
    bm = xact[:, SSD_INNER:SSD_INNER + 2 * SSD_D_STATE]
    cm = xact[:, SSD_INNER + 2 * SSD_D_STATE:]
    col = pl.BlockSpec((None, SSD_INNER, bb), lambda t: (t, 0, 0))
    row = pl.BlockSpec((bb, 2 * SSD_D_STATE), lambda t: (t, 0))
    acc, ycols = _layer_state_call(
        _ssd_step_body, "ssd_step", state_all, acc, i, SSD_INNER, SSD_D_STATE,
        [_to_cols(dax), _to_cols(c2), bm, cm], [col, col, row, row],
        col, jax.ShapeDtypeStruct((Bn // bb, SSD_INNER, bb), F32))
    return acc, _from_cols(ycols)


def _gla_step_body(s_ref, eg_ref, kc_ref, qc_ref, v_ref, *rest):
    sn_ref, o_ref = rest[-2:]
    eg = eg_ref[...]
    kc = kc_ref[...]
    qc = qc_ref[...] * (GLA_DK ** -0.5)
    for b in range(STEP_BATCH):
        s0 = s_ref[b]
        vmat = jnp.concatenate(
            [jnp.broadcast_to(v_ref[b:b + 1, h * GLA_DV:(h + 1) * GLA_DV], (GLA_DK, GLA_DV))
             for h in range(GLA_HEADS)], axis=0)
        sn = s0 * eg[:, b:b + 1] + kc[:, b:b + 1] * vmat
        sn_ref[b] = sn
        t = sn * qc[:, b:b + 1]
        o_ref[b:b + 1, :] = jnp.concatenate(
            [jnp.sum(t[h * GLA_DK:(h + 1) * GLA_DK], axis=0, keepdims=True)
             for h in range(GLA_HEADS)], axis=1)


def _gla_step(state_all, acc, i, eg, k, q, v):
    Bn = state_all.shape[1]
    bb = STEP_BATCH
    col = pl.BlockSpec((None, GLA_KEY, bb), lambda t: (t, 0, 0))
    row = pl.BlockSpec((bb, GLA_VAL), lambda t: (t, 0))
    return _layer_state_call(
        _gla_step_body, "gla_step", state_all, acc, i, GLA_KEY, GLA_DV,
        [_to_cols(eg), _to_cols(k), _to_cols(q), v], [col, col, col, row],
        row, jax.ShapeDtypeStruct((Bn, GLA_VAL), F32))


def _even_step_post_body(x_ref, ybase_ref, yi_ref, dax_ref, z_ref, o_ref, og_ref,
                         snw_ref, gnw_ref, w_ref, out_ref):
    y = (ybase_ref[...] + yi_ref[...] * dax_ref[...]) * _silu(z_ref[...])
    half = SSD_INNER // SSD_GROUPS
    acc = None
    for g in range(2):
        yg = y[:, g * half:(g + 1) * half]
        ms = jnp.mean(yg * yg, axis=-1, keepdims=True)
        yn = yg * lax.rsqrt(ms + EPS) * snw_ref[:, g * half:(g + 1) * half]
        part = _dot(yn.astype(BF16), w_ref[g * half:(g + 1) * half, :])
        acc = part if acc is None else acc + part
    for h in range(GLA_HEADS):
        vs = slice(h * GLA_DV, (h + 1) * GLA_DV)
        oh = o_ref[:, vs]
        ms = jnp.mean(oh * oh, axis=-1, keepdims=True)
        on = oh * lax.rsqrt(ms + EPS) * gnw_ref[...] * _silu(og_ref[:, vs])
        acc = acc + _dot(on.astype(BF16), w_ref[SSD_INNER + h * GLA_DV:SSD_INNER + (h + 1) * GLA_DV, :])
    out_ref[...] = x_ref[...] + acc


def _even_step_post(x, ybase, yi, dax, z, o, og, P, w_out, i):
    Bn = x.shape[0]
    full = pl.BlockSpec((Bn, D_MODEL), lambda t: (0, 0))
    return pl.pallas_call(
        _even_step_post_body,
        grid=(1,),
        in_specs=[full] * 7 + [pl.BlockSpec((None, 1, SSD_INNER), lambda t: (i, 0, 0)),
                               pl.BlockSpec((None, 1, GLA_DV), lambda t: (i, 0, 0)),
                               pl.BlockSpec((None, SSD_INNER + GLA_VAL, D_MODEL), lambda t: (i, 0, 0))],
        out_specs=full,
        out_shape=jax.ShapeDtypeStruct((Bn, D_MODEL), F32),
        compiler_params=_cp(("arbitrary",)),
        name="even_step_post",
    )(x, ybase, yi, dax, z, o, og, P["ssd_norm"], P["gla_norm"], w_out)


_ODD_OUTS = 8


def _odd_pre_body(*refs, step, has_vfirst, nblk_seq, tm):
    it = iter(refs)
    x_ref = next(it)
    if step:
        prev_ref = next(it)
    else:
        xp_ref = next(it)
        sh0_ref = next(it)
    g_ref, mix_ref = next(it), next(it)
    wr_ref, wk_ref, wv_ref = next(it), next(it), next(it)
    w0_ref, w1_ref, w2_ref = next(it), next(it), next(it)
    a0_ref, a1_ref, a2_ref = next(it), next(it), next(it)
    if has_vfirst:
        vf_ref, v0_ref, v1_ref, v2_ref = next(it), next(it), next(it), next(it)
    g1_ref, g2_ref = next(it), next(it)
    kkw_ref, kaw_ref = next(it), next(it)
    h_ref, r_ref, dec_ref, k_ref, v_ref, an_ref, b_ref, gg_ref = (next(it) for _ in range(_ODD_OUTS))

    h = _rms(x_ref[...], g_ref[...])
    if step:
        prev = prev_ref[...]
    else:
        i = pl.program_id(0)
        hp = _rms(xp_ref[...], g_ref[...])[7:8, :]
        prow = jnp.where(i % nblk_seq == 0, sh0_ref[...], hp)
        prev = jnp.where(_iota((tm, D_MODEL), 0) == 0, prow, pltpu.roll(h, 1, axis=0))
    h_ref[...] = h
    xx = prev - h
    mixed = lambda r: (h + xx * mix_ref[r:r + 1, :]).astype(BF16)
    xr, xw, xk, xv, xa, xg = (mixed(r) for r in range(6))
    r_ref[...] = _dot(xr, wr_ref[...])
    wl = w0_ref[...] + _dot(jnp.tanh(_dot(xw, w1_ref[...])).astype(BF16), w2_ref[...])
    w = -_softplus_abs(-wl) - 0.5
    dec_ref[...] = jnp.exp(-jnp.exp(w))
    k = _dot(xk, wk_ref[...])
    v = _dot(xv, wv_ref[...])
    if has_vfirst:
        gate = jax.nn.sigmoid(v0_ref[...] + _dot(_dot(xv, v1_ref[...]).astype(BF16), v2_ref[...]))
        v = v + (vf_ref[...] - v) * gate
    v_ref[...] = v
    a = jax.nn.sigmoid(a0_ref[...] + _dot(_dot(xa, a1_ref[...]).astype(BF16), a2_ref[...]))
    gg_ref[...] = _dot(jax.nn.sigmoid(_dot(xg, g1_ref[...])).astype(BF16), g2_ref[...])
    kkf = k * kkw_ref[...]
    red = _head_reduce()
    expand = _head_expand()
    ss = _dot01x2(kkf * kkf, jnp.concatenate([red, red], axis=0))
    kk = kkf * _dot01x2(lax.rsqrt(jnp.maximum(ss, 1e-24)), jnp.concatenate([expand, expand], axis=0))
    k_ref[...] = k * (1.0 + (a - 1.0) * kaw_ref[...])
    an_ref[...] = -kk
    b_ref[...] = kk * a


def _odd_pre(x, prev_or_shift0, vfirst, P, i, tm, L):
    T = x.shape[0]
    step = L == 1
    has_vfirst = vfirst is not None
    nblk_seq = max(L // tm, 1)
    tok = pl.BlockSpec((tm, D_MODEL), lambda t: (t, 0))
    par = lambda r, n, idx=i: pl.BlockSpec((None, r, n), lambda t: (idx, 0, 0), pipeline_mode=pl.Buffered(1))
    args = [x]
    specs = [tok]
    if step:
        args.append(prev_or_shift0)
        specs.append(tok)
    else:
        args += [x, prev_or_shift0]
        specs += [pl.BlockSpec((8, D_MODEL), lambda t: (jnp.maximum(t * (tm // 8) - 1, 0), 0)),
                  pl.BlockSpec((None, 1, D_MODEL), lambda t: (t // nblk_seq, 0, 0))]
    args += [P["mix_norm"], P["mix"], P["w_r"], P["w_k"], P["w_v"], P["w0"], P["w1"], P["w2"],
             P["a0"], P["a1"], P["a2"]]
    specs += [pl.BlockSpec((None, 1, D_MODEL), lambda t: (2 * i + 1, 0, 0)),
              par(8, D_MODEL), par(D_MODEL, D_MODEL), par(D_MODEL, D_MODEL), par(D_MODEL, D_MODEL),
              par(1, D_MODEL), par(D_MODEL, LANES), par(LANES, D_MODEL),
              par(1, D_MODEL), par(D_MODEL, LANES), par(LANES, D_MODEL)]
    if has_vfirst:
        args += [vfirst, P["v0"], P["v1"], P["v2"]]
        specs += [tok, par(1, D_MODEL, i - 1), par(D_MODEL, LANES, i - 1), par(LANES, D_MODEL, i - 1)]
    args += [P["g1"], P["g2"], P["k_k"], P["k_a"]]
    specs += [par(D_MODEL, 2 * LANES), par(2 * LANES, D_MODEL), par(1, D_MODEL), par(1, D_MODEL)]
    return pl.pallas_call(
        functools.partial(_odd_pre_body, step=step, has_vfirst=has_vfirst, nblk_seq=nblk_seq, tm=tm),
        grid=(T // tm,),
        in_specs=specs,
        out_specs=[tok] * _ODD_OUTS,
        out_shape=[jax.ShapeDtypeStruct((T, D_MODEL), F32)] * _ODD_OUTS,
        compiler_params=_cp(("parallel",)),
        name="odd_pre",
    )(*args)


def _dot01x2(x, e2):
    hi = x.astype(BF16)
    mid = (x - hi.astype(F32)).astype(BF16)
    return _dot(jnp.concatenate([hi, mid], axis=1), e2)


def _odd_post_body(x_ref, o_ref, r_ref, k_ref, v_ref, g_ref, gnw_ref, gnb_ref, rk_ref, wo_ref, out_ref):
    red = _head_reduce()
    red = jnp.concatenate([red, red], axis=0)
    expand = _head_expand()
    expand = jnp.concatenate([expand, expand], axis=0)
    for c0 in range(0, x_ref.shape[0], LANES):
        rows = slice(c0, c0 + LANES)
        o = o_ref[rows, :]
        mu = _dot01x2(_dot01x2(o, red) * (1.0 / RWKV_HEAD), expand)
        d = o - mu
        var = _dot01x2(d * d, red) * (1.0 / RWKV_HEAD)
        on = d * _dot01x2(lax.rsqrt(var + RWKV_GN_EPS), expand) * gnw_ref[...] + gnb_ref[...]
        bonus = _dot01x2(_dot01x2(r_ref[rows, :] * k_ref[rows, :] * rk_ref[...], red), expand) * v_ref[rows, :]
        out_ref[rows, :] = x_ref[rows, :] + _dot(((on + bonus) * g_ref[rows, :]).astype(BF16), wo_ref[...])


def _odd_post(x, o, r, k, v, g, P, i, tm):
    T = x.shape[0]
    tok = pl.BlockSpec((tm, D_MODEL), lambda t: (t, 0))
    par = lambda r_, n: pl.BlockSpec((None, r_, n), lambda t: (i, 0, 0))
    return pl.pallas_call(
        _odd_post_body,
        grid=(T // tm,),
        in_specs=[tok] * 6 + [par(1, D_MODEL), par(1, D_MODEL), par(1, D_MODEL), par(D_MODEL, D_MODEL)],
        out_specs=tok,
        out_shape=jax.ShapeDtypeStruct((T, D_MODEL), F32),
        compiler_params=_cp(("parallel",)),
        name="odd_post",
    )(x, o, r, k, v, g, P["gn_w"], P["gn_b"], P["r_k"], P["w_o"])


RWKV_TB = 64
RWKV_NPAIR = RWKV_HEADS // 2


RWKV_SEQS = 8
RWKV_SPLIT = 1
RWKV_UNROLL = 8


def _rwkv_seq_body(w_ref, k_ref, a_ref, b_ref, r_ref, v_ref, o_ref, sn_ref,
                   s_ref, vt_ref, oacc_ref, ap_ref, bp_ref, kp_ref, rp_ref, gl_ref, *, ntb):
    N = RWKV_HEAD
    R = RWKV_NPAIR * N
    tbi = pl.program_id(1)

    @pl.when(tbi == 0)
    def _():
        s_ref[...] = jnp.zeros_like(s_ref)

    seg = _onehot((_iota((2 * LANES, 2 * LANES), 0) >> HEAD_SHIFT) == (_iota((2 * LANES, 2 * LANES), 1) >> HEAD_SHIFT))

    def seg_sum(xb):
        h = xb.shape[0] // 2
        both = _dot(jnp.concatenate([xb[:h], xb[h:]], axis=1), seg)
        return jnp.concatenate([both[:, :LANES], both[:, LANES:]], axis=0)
    incl = _onehot(_iota((RWKV_TB, RWKV_TB), 0) >= _iota((RWKV_TB, RWKV_TB), 1))
    for c in range(RWKV_SEQS):
        logw = jnp.log(w_ref[c])
        cum = _dot01_l(incl, logw)
        g_inc = jnp.exp(cum)
        g_inv = jnp.exp(-cum)
        ap_ref[c] = a_ref[c] * jnp.exp(cum - logw)
        bp_ref[c] = b_ref[c] * g_inv
        kp_ref[c] = k_ref[c] * g_inv
        rp_ref[c] = r_ref[c] * g_inc
        gl_ref[c] = g_inc[RWKV_TB - 8:RWKV_TB, :]
        for p in range(RWKV_NPAIR):
            bt = v_ref[c, :, p * LANES:(p + 1) * LANES].T
            vt_ref[c, p * N:(p + 1) * N, :] = jnp.concatenate([bt[0:N], bt[N:2 * N]], axis=1)
    oacc_ref[...] = jnp.zeros_like(oacc_ref)
    G = R // RWKV_SPLIT
    OSTEPS = LANES // 4
    seg_out = _onehot((_iota((2 * LANES, LANES), 0) >> HEAD_SHIFT) == (_iota((2 * LANES, LANES), 1) >> (OSTEPS.bit_length() - 1)))
    lane_copy = _iota((R // 2, LANES), 1) & (OSTEPS - 1)
    lane_base = (_iota((G, LANES), 1) >> HEAD_SHIFT) * N

    def steps(tu, carry):
        row0 = pl.multiple_of(tu * RWKV_UNROLL, RWKV_UNROLL)
        for j in range(RWKV_UNROLL):
            t = tu * RWKV_UNROLL + j
            for c in range(RWKV_SEQS):
                for g in range(RWKV_SPLIT):
                    rs = slice(g * G, (g + 1) * G)

                    def rows(ref):
                        return jnp.concatenate(
                            [jnp.broadcast_to(
                                ref[c, pl.ds(row0, RWKV_UNROLL), p * LANES:(p + 1) * LANES][j:j + 1, :], (N, LANES))
                             for p in range(g * G // N, (g + 1) * G // N)], axis=0)

                    s = s_ref[c, rs, :]
                    sa = seg_sum((s * rows(ap_ref)).astype(BF16))
                    vcol = jnp.take_along_axis(vt_ref[c, rs, :], lane_base + t, axis=1)
                    s = (s + vcol * rows(kp_ref)) + sa * rows(bp_ref)
                    s_ref[c, rs, :] = s
                    xr = (s * rows(rp_ref)).astype(BF16)
                    ob = _dot(jnp.concatenate([xr[:R // 2], xr[R // 2:]], axis=1), seg_out)
                    hb = tu // (OSTEPS // RWKV_UNROLL)
                    oacc_ref[c, hb] = jnp.where(lane_copy == (t & (OSTEPS - 1)), ob, oacc_ref[c, hb])
        return carry

    lax.fori_loop(0, RWKV_TB // RWKV_UNROLL, steps, 0)

    for c in range(RWKV_SEQS):
        s_ref[c] = s_ref[c] * jnp.concatenate(
            [jnp.broadcast_to(gl_ref[c, 7:8, p * LANES:(p + 1) * LANES], (N, LANES)) for p in range(RWKV_NPAIR)],
            axis=0)
        for hb in range(RWKV_TB // OSTEPS):
            ot = oacc_ref[c, hb].T
            for gh in range(4):
                g, hh = divmod(gh, 2)
                for q in range(RWKV_NPAIR // 2):
                    lane0 = (g * (RWKV_NPAIR // 2) + q) * LANES + hh * N
                    o_ref[c, hb * OSTEPS:(hb + 1) * OSTEPS, lane0:lane0 + N] = (
                        ot[gh * OSTEPS:(gh + 1) * OSTEPS, q * N:(q + 1) * N])

    @pl.when(tbi == ntb - 1)
    def _():
        for c in range(RWKV_SEQS):
            for p in range(RWKV_NPAIR):
                for hh in range(2):
                    sn_ref[c, (2 * p + hh) * N:(2 * p + hh + 1) * N, :] = (
                        s_ref[c, p * N:(p + 1) * N, hh * N:(hh + 1) * N])


def _rwkv_seq(w, k, a, b, r, v, B, L):
    ntb = L // RWKV_TB
    nc = RWKV_SEQS
    as_seqs = lambda x: x.reshape(B, L, D_MODEL)
    tok = pl.BlockSpec((nc, RWKV_TB, D_MODEL), lambda bi, t: (bi, t, 0))
    state_rows = RWKV_NPAIR * RWKV_HEAD
    o, sn = pl.pallas_call(
        functools.partial(_rwkv_seq_body, ntb=ntb),
        grid=(B // nc, ntb),
        in_specs=[tok] * 6,
        out_specs=[tok, pl.BlockSpec((nc, D_MODEL, RWKV_HEAD), lambda bi, t: (bi, 0, 0))],
        out_shape=[jax.ShapeDtypeStruct((B, L, D_MODEL), F32),
                   jax.ShapeDtypeStruct((B, D_MODEL, RWKV_HEAD), F32)],
        scratch_shapes=[pltpu.VMEM((nc, state_rows, LANES), F32)] * 2
        + [pltpu.VMEM((nc, 2, state_rows // 2, LANES), F32)]
        + [pltpu.VMEM((nc, RWKV_TB, D_MODEL), F32)] * 4
        + [pltpu.VMEM((nc, 8, D_MODEL), F32)],
        compiler_params=_cp(("parallel", "arbitrary")),
        name="rwkv_seq",
    )(as_seqs(w), as_seqs(k), as_seqs(a), as_seqs(b), as_seqs(r), as_seqs(v))
    return o.reshape(B * L, D_MODEL), sn.reshape(B, RWKV_HEADS, RWKV_HEAD, RWKV_HEAD)


def _rwkv_step_body(s_ref, w_ref, k_ref, a_ref, b_ref, r_ref, vc_ref, *rest):
    sn_ref, o_ref = rest[-2:]
    N = RWKV_HEAD
    lane = _iota((D_MODEL, STEP_BATCH), 1)
    ocols = jnp.zeros((D_MODEL, STEP_BATCH), F32)
    vc = vc_ref[...]

    def rows(ref, b):
        return jnp.concatenate(
            [jnp.broadcast_to(ref[b:b + 1, h * N:(h + 1) * N], (N, N)) for h in range(RWKV_HEADS)], axis=0)

    for b in range(STEP_BATCH):
        s = s_ref[b]
        sa = jnp.sum(s * rows(a_ref, b), axis=1, keepdims=True)
        s = s * rows(w_ref, b) + sa * rows(b_ref, b) + vc[:, b:b + 1] * rows(k_ref, b)
        sn_ref[b] = s
        ocol = jnp.sum(s * rows(r_ref, b), axis=1, keepdims=True)
        ocols = jnp.where(lane == b, ocol, ocols)
    o_ref[...] = ocols


def _rwkv_step(state_all, acc, i, w, k, a, b, r, v):
    Bn = state_all.shape[1]
    bb = STEP_BATCH
    col = pl.BlockSpec((None, D_MODEL, bb), lambda t: (t, 0, 0))
    row = pl.BlockSpec((bb, D_MODEL), lambda t: (t, 0))
    acc, ocols = _layer_state_call(
        _rwkv_step_body, "rwkv_step", state_all, acc, i, D_MODEL, RWKV_HEAD,
        [w, k, a, b, r, _to_cols(v)], [row] * 5 + [col],
        col, jax.ShapeDtypeStruct((Bn // bb, D_MODEL, bb), F32))
    return _from_cols(ocols), acc


def _pad_to(w, axis, n):
    pad = [(0, 0)] * w.ndim
    pad[axis] = (0, n - w.shape[axis])
    return jnp.pad(w, pad)


def _prepare(W):
    bf = lambda w: w.astype(BF16)
    row = lambda w: w[:, None, :]
    sizes = [SSD_INNER, SSD_CONV_DIM, SSD_HEADS, GLA_KEY, GLA_KEY, GLA_VAL, GLA_VAL, GLA_GATE_RANK]
    offs = [0]
    for s in sizes:
        offs.append(offs[-1] + s)
    piece = lambda n: W["ev_w_in"][:, :, offs[n]:offs[n + 1]]
    w_in = jnp.concatenate([piece(0), piece(1), piece(3), piece(4), piece(5), piece(6),
                            _pad_to(jnp.concatenate([piece(2), piece(7)], axis=-1), 2, SMALL)], axis=-1)
    gate_w2 = jnp.pad(W["ev_gla_gate_w2"], ((0, 0), (SSD_HEADS, SMALL - SSD_HEADS - GLA_GATE_RANK), (0, 0)))
    even = dict(
        w_in=bf(w_in), w_out=bf(W["ev_w_out"]),
        conv_w=W["ev_conv_w"], conv_b=row(W["ev_conv_b"]),
        dt_bias=row(_pad_to(W["ev_dt_bias"], 1, SMALL)), a_log=row(_pad_to(W["ev_a_log"], 1, SMALL)),
        d_skip_x=row(jnp.repeat(W["ev_d_skip"], SSD_HEAD_DIM, axis=1)),
        ssd_norm=row(W["ev_ssd_norm"]), gate_w2=bf(gate_w2), gate_b=row(W["ev_gla_gate_b"]),
        gla_norm=row(W["ev_gla_norm"]),
    )
    odd = dict(
        mix_norm=row(W["mix_norm"]), mix=_pad_to(W["od_mix"], 1, 8),
        w_r=bf(W["od_w_r"]), w_k=bf(W["od_w_k"]), w_v=bf(W["od_w_v"]), w_o=bf(W["od_w_o"]),
        w0=row(W["od_w0"]), w1=bf(_pad_to(W["od_w1"], 2, LANES)), w2=bf(_pad_to(W["od_w2"], 1, LANES)),
        a0=row(W["od_a0"]), a1=bf(_pad_to(W["od_a1"], 2, LANES)), a2=bf(_pad_to(W["od_a2"], 1, LANES)),
        v0=row(W["od_v0"]), v1=bf(_pad_to(W["od_v1"], 2, LANES)), v2=bf(_pad_to(W["od_v2"], 1, LANES)),
        g1=bf(_pad_to(W["od_g1"], 2, 2 * LANES)), g2=bf(_pad_to(W["od_g2"], 1, 2 * LANES)),
        k_k=row(W["od_k_k"]), k_a=row(W["od_k_a"]), r_k=row(W["od_r_k"]),
        gn_w=row(W["od_gn_w"]), gn_b=row(W["od_gn_b"]),
    )
    ffn = dict(
        norm=W["ffn_norm"].reshape(DEPTH * 2, 1, D_MODEL),
        w_gu=bf(W["ffn_w_gu"]).reshape(DEPTH * 2, D_MODEL, 2 * D_FF),
        w_down=bf(W["ffn_w_down"]).reshape(DEPTH * 2, D_FF, D_MODEL),
    )
    return dict(even=even, odd=odd, ffn=ffn, mix_norm=row(W["mix_norm"]),
                final_norm=W["final_norm"][None, :])


def _run_group(x3, states, Wp):
    B, L, _ = x3.shape
    T = B * L
    step = L == 1
    x = x3.reshape(T, D_MODEL)
    tm_ffn = min(512, T)
    tm_proj = min(256, T)
    ffn, even, odd = Wp["ffn"], Wp["even"], Wp["odd"]
    convs, ssms, glas, shifts, wkvs = [], [], [], [], []
    ssm_acc = gla_acc = wkv_acc = None
    v_first = None
    for layer in range(DEPTH):
        i = layer // 2
        mix = None
        x = _ffn(x, ffn["norm"], ffn["w_gu"], ffn["w_down"], 2 * layer, tm_ffn)
        if layer % 2 == 0:
            z, xbc, q, k, v, og, sm = _even_in(x, Wp["mix_norm"], even["w_in"], i, tm_ffn)
            if step:
                xact, convn, dax, c2, ybase, eg = _even_step_pre(
                    xbc, states[0][i].reshape(B, 3 * SSD_CONV_DIM), sm, even, i)
                ssm_acc, yi = _ssd_step(states[1], ssm_acc, i, dax, c2, xact)
                gla_acc, o = _gla_step(states[2], gla_acc, i, eg, k, q, v)
                x = _even_step_post(x, ybase, yi, dax, z, o, og, even, even["w_out"], i)
                convn = convn.reshape(B, SSD_CONV - 1, SSD_CONV_DIM)
            else:
                y, convn, ssmn = _ssd_seq(xbc, sm, z, even, i, B, L)
                o, glan = _gla_seq(q, k, v, og, sm, even, i, B, L)
                mix = (y, o, even["w_out"], i)
                ssms.append(ssmn)
                glas.append(glan)
            convs.append(convn)
        else:
            prev = states[3][i] if step else jnp.zeros((B, 1, D_MODEL), F32)
            h, r, dec, k, v, an, bb, gg = _odd_pre(x, prev, v_first, odd, i, tm_proj, L)
            if v_first is None:
                v_first = v
            if step:
                o, wkv_acc = _rwkv_step(states[4], wkv_acc, i, dec, k, an, bb, r, v)
            else:
                o, wkvn = _rwkv_seq(dec, k, an, bb, r, v, B, L)
                wkvs.append(wkvn)
            x = _odd_post(x, o, r, k, v, gg, odd, i, tm_ffn)
            shifts.append(h.reshape(B, L, D_MODEL)[:, -1])
        x = _ffn(x, ffn["norm"], ffn["w_gu"], ffn["w_down"], 2 * layer + 1, tm_ffn,
                 final_g=Wp["final_norm"] if layer == DEPTH - 1 else None, mix=mix)
    y = x.reshape(B, L, D_MODEL)
    if step:
        ssm_out = ssm_acc.reshape(N_EVEN, B, SSD_HEADS, SSD_HEAD_DIM, SSD_D_STATE)
        gla_out = gla_acc.reshape(N_EVEN, B, GLA_HEADS, GLA_DK, GLA_DV)
        wkv_out = wkv_acc.reshape(N_ODD, B, RWKV_HEADS, RWKV_HEAD, RWKV_HEAD)
    else:
        ssm_out, gla_out, wkv_out = jnp.stack(ssms), jnp.stack(glas), jnp.stack(wkvs)
    return y, jnp.stack(convs), ssm_out, gla_out, jnp.stack(shifts), wkv_out


def kernel(x_prompt, x_sample, state_conv, state_ssm, state_gla, state_shift, state_wkv, ffn_norm, ffn_w_gu, ffn_w_down, mix_norm, final_norm, ev_w_in, ev_conv_w, ev_conv_b, ev_dt_bias, ev_a_log, ev_d_skip, ev_ssd_norm, ev_gla_gate_w2, ev_gla_gate_b, ev_gla_norm, ev_w_out, od_mix, od_w0, od_w1, od_w2, od_a0, od_a1, od_a2, od_v0, od_v1, od_v2, od_g1, od_g2, od_k_k, od_k_a, od_r_k, od_w_r, od_w_k, od_w_v, od_w_o, od_gn_w, od_gn_b):
    W = dict(ffn_norm=ffn_norm, ffn_w_gu=ffn_w_gu, ffn_w_down=ffn_w_down, mix_norm=mix_norm,
             final_norm=final_norm, ev_w_in=ev_w_in, ev_conv_w=ev_conv_w, ev_conv_b=ev_conv_b,
             ev_dt_bias=ev_dt_bias, ev_a_log=ev_a_log, ev_d_skip=ev_d_skip, ev_ssd_norm=ev_ssd_norm,
             ev_gla_gate_w2=ev_gla_gate_w2, ev_gla_gate_b=ev_gla_gate_b, ev_gla_norm=ev_gla_norm,
             ev_w_out=ev_w_out, od_mix=od_mix, od_w0=od_w0, od_w1=od_w1, od_w2=od_w2,
             od_a0=od_a0, od_a1=od_a1, od_a2=od_a2, od_v0=od_v0, od_v1=od_v1, od_v2=od_v2,
             od_g1=od_g1, od_g2=od_g2, od_k_k=od_k_k, od_k_a=od_k_a, od_r_k=od_r_k,
             od_w_r=od_w_r, od_w_k=od_w_k, od_w_v=od_w_v, od_w_o=od_w_o,
             od_gn_w=od_gn_w, od_gn_b=od_gn_b)
    Wp = _prepare(W)
    prompt = _run_group(x_prompt, None, Wp)
    sample = _run_group(x_sample, (state_conv, state_ssm, state_gla, state_shift, state_wkv), Wp)
    return (prompt[0], sample[0]) + prompt[1:] + sample[1:]
```
